```python
import math
import jax
import jax.numpy as jnp
from jax import lax
import numpy as np

D_MODEL = 1024
BATCH = 8
SEQ = 2048
DEPTH = 4
DEC_BATCH = 128
DEC_SEQ = 8
PAST_LEN = 16384
PAGE_SIZE = 128

N_MIXERS = 3
MIX_WIDTH = 3 * D_MODEL // 4
XA_HEADS = 4
XA_HEAD_DIM = 64
XA_WIDTH = XA_HEADS * XA_HEAD_DIM
N_MEM = 256
S5_GROUP = 16
S5_GROUPS = MIX_WIDTH // S5_GROUP
S5_STATE = 64
S5_DT_MIN = 1e-3
S5_DT_MAX = 1e-1
RET_HEADS = 4
RET_DK = 128
RET_DV = MIX_WIDTH // RET_HEADS
RET_CHUNK = 128
ROPE_BASE = 10000.0
GLA_HEADS = 4
GLA_DK = MIX_WIDTH // (2 * GLA_HEADS)
GLA_DV = MIX_WIDTH // GLA_HEADS
GLA_RANK = 16
GLA_TAU = 16.0
GLA_CHUNK = 64
MOE_GROUPS = 4
MOE_PER_GROUP = 4
MOE_EXPERTS = MOE_GROUPS * MOE_PER_GROUP
MOE_TOP_K = 2
MOE_HIDDEN = 256
DN_ALPHA = (2 * DEPTH) ** 0.25
DN_BETA = (8 * DEPTH) ** -0.25
NORM_EPS = 1e-5
N_S5_LAYERS = len(range(0, DEPTH, N_MIXERS))
N_RET_LAYERS = len(range(1, DEPTH, N_MIXERS))
N_GLA_LAYERS = len(range(2, DEPTH, N_MIXERS))
S5_IN = MIX_WIDTH + XA_WIDTH
RET_IN = 2 * RET_HEADS * RET_DK + 2 * MIX_WIDTH + XA_WIDTH
GLA_IN = 2 * GLA_HEADS * GLA_DK + 2 * MIX_WIDTH + GLA_RANK + XA_WIDTH

kernel_name = 'hybrid_s5_ret_gla_hmoe_step'


def layer_norm(x, g, b):
    xf = x.astype(jnp.float32)
    mu = xf.mean(-1, keepdims=True)
    var = jnp.mean(jnp.square(xf - mu), -1, keepdims=True)
    y = (xf - mu) * lax.rsqrt(var + NORM_EPS) * g.astype(jnp.float32) + b.astype(jnp.float32)
    return y.astype(x.dtype)


def head_norm(o, gain):
    mu = o.mean(-1, keepdims=True)
    var = jnp.mean(jnp.square(o - mu), -1, keepdims=True)
    y = (o - mu) * lax.rsqrt(var + NORM_EPS)
    return y.reshape(*o.shape[:-2], -1) * gain.astype(jnp.float32)


def rotary(x, pos):
    half = x.shape[-1] // 2
    inv_freq = ROPE_BASE ** (-jnp.arange(half, dtype=jnp.float32) / half)
    ang = pos.astype(jnp.float32)[:, None] * inv_freq
    cos = jnp.cos(ang)[:, None, :]
    sin = jnp.sin(ang)[:, None, :]
    x1, x2 = x[..., :half], x[..., half:]
    return jnp.concatenate([x1 * cos - x2 * sin, x1 * sin + x2 * cos], axis=-1)


def _linear_combine(e1, e2):
    a1, b1 = e1
    a2, b2 = e2
    return a1 * a2, a2 * b1 + b2


def s5_mixer(u, h0_re, h0_im, lam_re, lam_im, log_dt, b_re, b_im, c_re, c_im, d_skip, w_glu, b_glu):
    f32 = jnp.float32
    n, l, _ = u.shape
    uf = u.astype(f32).reshape(n, l, S5_GROUPS, S5_GROUP)
    lam = lax.complex(lam_re.astype(f32), lam_im.astype(f32))
    lam_bar = jnp.exp(lam * jnp.exp(log_dt.astype(f32))[:, None])
    b_bar = ((lam_bar - 1.0) / lam)[..., None] * lax.complex(b_re.astype(f32), b_im.astype(f32))
    bu = jnp.einsum('gpc,nlgc->nlgp', b_bar, uf.astype(jnp.complex64))
    h0 = lax.complex(h0_re.astype(f32), h0_im.astype(f32))
    bu = bu.at[:, 0].add(lam_bar * h0)
    a = jnp.broadcast_to(lam_bar, bu.shape)
    _, h = lax.associative_scan(_linear_combine, (a, bu), axis=1)
    c = lax.complex(c_re.astype(f32), c_im.astype(f32))
    y = jnp.einsum('gcp,nlgp->nlgc', c, h).real + d_skip.astype(f32).reshape(S5_GROUPS, S5_GROUP) * uf
    y = jax.nn.gelu(y.reshape(n, l, MIX_WIDTH))
    y = y * jax.nn.sigmoid(y @ w_glu.astype(f32) + b_glu.astype(f32))
    h_last = h[:, -1]
    return y, jnp.real(h_last), jnp.imag(h_last)


def retention_chunkwise(q, k, v, s0):
    n, l, h, dk = q.shape
    dv = v.shape[-1]
    c = math.gcd(l, RET_CHUNK)
    nc = l // c
    log_g = jnp.log1p(-jnp.exp2(-5.0 - jnp.arange(h, dtype=jnp.float32)))
    idx = jnp.arange(c, dtype=jnp.float32)
    rel = idx[:, None] - idx[None, :]
    dmat = jnp.where(rel >= 0, jnp.exp(log_g[:, None, None] * jnp.maximum(rel, 0.0)), 0.0)
    inner = jnp.exp(log_g[:, None] * (idx + 1.0))[..., None]
    zeta = jnp.exp(log_g[:, None] * (c - 1.0 - idx))[..., None]
    gamma_c = jnp.exp(log_g * c)[:, None, None]
    qc = q.reshape(n, nc, c, h, dk).transpose(1, 0, 3, 2, 4)
    kc = k.reshape(n, nc, c, h, dk).transpose(1, 0, 3, 2, 4)
    vc = v.reshape(n, nc, c, h, dv).transpose(1, 0, 3, 2, 4)

    def step(s, inp):
        qi, ki, vi = inp
        att = jnp.einsum('nhid,nhjd->nhij', qi, ki) * dmat
        o = jnp.einsum('nhij,nhje->nhie', att, vi) + jnp.einsum('nhid,nhde->nhie', qi, s) * inner
        s = s * gamma_c + jnp.einsum('nhjd,nhje->nhde', ki * zeta, vi)
        return s, o

    s, o = lax.scan(step, s0, (qc, kc, vc))
    return o.transpose(1, 0, 3, 2, 4).reshape(n, l, h, dv), s


def retention_mixer(z, s0, pos, gn):
    n, l, _ = z.shape
    f32 = jnp.float32
    qk = RET_HEADS * RET_DK
    q = rotary(z[..., :qk].astype(f32).reshape(n, l, RET_HEADS, RET_DK), pos)
    k = rotary(z[..., qk:2 * qk].astype(f32).reshape(n, l, RET_HEADS, RET_DK), pos) * RET_DK ** -0.5
    v = z[..., 2 * qk:2 * qk + MIX_WIDTH].astype(f32).reshape(n, l, RET_HEADS, RET_DV)
    g = z[..., 2 * qk + MIX_WIDTH:2 * qk + 2 * MIX_WIDTH].astype(f32)
    o, s = retention_chunkwise(q, k, v, s0.astype(f32))
    return head_norm(o, gn) * jax.nn.silu(g), s


def gla_chunkwise(q, k, v, log_a, s0):
    n, l, h, dk = q.shape
    dv = v.shape[-1]
    c = math.gcd(l, GLA_CHUNK)
    nc = l // c
    causal = jnp.tril(jnp.ones((c, c), dtype=bool))
    qc = q.reshape(n, nc, c, h, dk).transpose(1, 0, 3, 2, 4)
    kc = k.reshape(n, nc, c, h, dk).transpose(1, 0, 3, 2, 4)
    ac = log_a.reshape(n, nc, c, h, dk).transpose(1, 0, 3, 2, 4)
    vc = v.reshape(n, nc, c, h, dv).transpose(1, 0, 3, 2, 4)

    def step(s, inp):
        qi, ki, vi, ai = inp
        b = jnp.cumsum(ai, axis=-2)
        q_t = qi * jnp.exp(b)
        k_t = ki * jnp.exp(-b)
        att = jnp.where(causal, jnp.einsum('nhid,nhjd->nhij', q_t, k_t), 0.0)
        o = jnp.einsum('nhij,nhje->nhie', att, vi) + jnp.einsum('nhid,nhde->nhie', q_t, s)
        b_last = b[..., -1:, :]
        s = s * jnp.exp(b_last[..., 0, :])[..., None] + jnp.einsum('nhjd,nhje->nhde', ki * jnp.exp(b_last - b), vi)
        return s, o

    s, o = lax.scan(step, s0, (qc, kc, vc, ac))
    return o.transpose(1, 0, 3, 2, 4).reshape(n, l, h, dv), s


def gla_mixer(z, s0, w_gate2, b_gate2, gn):
    n, l, _ = z.shape
    f32 = jnp.float32
    qk = GLA_HEADS * GLA_DK
    q = z[..., :qk].astype(f32).reshape(n, l, GLA_HEADS, GLA_DK) * GLA_DK ** -0.5
    k = z[..., qk:2 * qk].astype(f32).reshape(n, l, GLA_HEADS, GLA_DK)
    v = z[..., 2 * qk:2 * qk + MIX_WIDTH].astype(f32).reshape(n, l, GLA_HEADS, GLA_DV)
    r = z[..., 2 * qk + MIX_WIDTH:2 * qk + 2 * MIX_WIDTH].astype(f32)
    z_low = z[..., 2 * qk + 2 * MIX_WIDTH:].astype(f32)
    log_a = jax.nn.log_sigmoid(z_low @ w_gate2.astype(f32) + b_gate2.astype(f32)) / GLA_TAU
    o, s = gla_chunkwise(q, k, v, log_a.reshape(n, l, GLA_HEADS, GLA_DK), s0.astype(f32))
    return head_norm(o, gn) * jax.nn.silu(r), s


def mem_attention(xq, mk, mv):
    n, l, _ = xq.shape
    q = xq.reshape(n, l, XA_HEADS, XA_HEAD_DIM)
    s = jnp.einsum('nlhd,nmhd->nhlm', q, mk).astype(jnp.float32) * XA_HEAD_DIM ** -0.5
    p = jax.nn.softmax(s, axis=-1).astype(mv.dtype)
    return jnp.einsum('nhlm,nmhd->nlhd', p, mv).reshape(n, l, XA_WIDTH)


def hier_moe(x, w_grp, b_grp, w_exp, b_exp, w_gate, w_up, w_down):
    n, l, d = x.shape
    f32 = jnp.float32
    t = x.reshape(n * l, d)
    grp_prob = jax.nn.softmax((t @ w_grp + b_grp).astype(f32), axis=-1)
    grp_p, grp_i = lax.top_k(grp_prob, 1)
    exp_logits = (t @ w_exp + b_exp).astype(f32).reshape(-1, MOE_GROUPS, MOE_PER_GROUP)
    in_grp = jnp.einsum('tg,tge->te', jax.nn.one_hot(grp_i[:, 0], MOE_GROUPS, dtype=f32), exp_logits)
    top_v, top_i = lax.top_k(in_grp, MOE_TOP_K)
    w = jax.nn.softmax(top_v, axis=-1) * grp_p
    expert = grp_i * MOE_PER_GROUP + top_i
    gates = jnp.einsum('tk,tke->te', w, jax.nn.one_hot(expert, MOE_EXPERTS, dtype=f32))
    h = jax.nn.silu(jnp.einsum('td,edf->tef', t, w_gate)) * jnp.einsum('td,edf->tef', t, w_up)
    y = jnp.einsum('tef,efd->td', h * gates[..., None].astype(h.dtype), w_down)
    return y.reshape(n, l, d)


def setup_inputs(seed: int = 0) -> dict:
    key = jax.random.key(seed)
    keys = iter(jax.random.split(key, 64))
    f32 = jnp.float32

    def nrm(shape, scale=1.0):
        return jax.random.normal(next(keys), shape, f32) * scale

    s_in = D_MODEL ** -0.5
    ret_scale = jnp.concatenate([
        jnp.full((2 * RET_HEADS * RET_DK,), s_in, f32),
        jnp.full((MIX_WIDTH,), DN_BETA * s_in, f32),
        jnp.full((MIX_WIDTH + XA_WIDTH,), s_in, f32)])
    gla_scale = jnp.concatenate([
        jnp.full((2 * GLA_HEADS * GLA_DK,), s_in, f32),
        jnp.full((MIX_WIDTH,), DN_BETA * s_in, f32),
        jnp.full((MIX_WIDTH + GLA_RANK + XA_WIDTH,), s_in, f32)])
    lam_im0 = jnp.pi * jnp.arange(S5_STATE, dtype=f32)
    mem_shape = (DEPTH, DEC_BATCH, N_MEM, XA_HEADS, XA_HEAD_DIM)
    return {
        'x_prompt': nrm((BATCH, SEQ, D_MODEL)),
        'x_sample': nrm((DEC_BATCH, DEC_SEQ, D_MODEL)),
        'mem_prompt': nrm((BATCH, N_MEM, D_MODEL)),
        'state_s5_re': nrm((N_S5_LAYERS, DEC_BATCH, S5_GROUPS, S5_STATE), 0.3),
        'state_s5_im': nrm((N_S5_LAYERS, DEC_BATCH, S5_GROUPS, S5_STATE), 0.3),
        'state_ret': nrm((N_RET_LAYERS, DEC_BATCH, RET_HEADS, RET_DK, RET_DV), 0.5),
        'state_gla': nrm((N_GLA_LAYERS, DEC_BATCH, GLA_HEADS, GLA_DK, GLA_DV), 0.5),
        'cache_mem_k': nrm(mem_shape),
        'cache_mem_v': nrm(mem_shape, DN_BETA),
        'w_in_s5': nrm((N_S5_LAYERS, D_MODEL, S5_IN), s_in),
        's5_lam_re': -0.5 + nrm((N_S5_LAYERS, S5_GROUPS, S5_STATE), 0.01),
        's5_lam_im': lam_im0 + nrm((N_S5_LAYERS, S5_GROUPS, S5_STATE), 0.01),
        's5_log_dt': jax.random.uniform(next(keys), (N_S5_LAYERS, S5_GROUPS), f32,
                                        minval=math.log(S5_DT_MIN), maxval=math.log(S5_DT_MAX)),
        's5_b_re': nrm((N_S5_LAYERS, S5_GROUPS, S5_STATE, S5_GROUP), (2 * S5_GROUP) ** -0.5),
        's5_b_im': nrm((N_S5_LAYERS, S5_GROUPS, S5_STATE, S5_GROUP), (2 * S5_GROUP) ** -0.5),
        's5_c_re': nrm((N_S5_LAYERS, S5_GROUPS, S5_GROUP, S5_STATE), (2 * S5_STATE) ** -0.5),
        's5_c_im': nrm((N_S5_LAYERS, S5_GROUPS, S5_GROUP, S5_STATE), (2 * S5_STATE) ** -0.5),
        's5_d': nrm((N_S5_LAYERS, MIX_WIDTH)),
        's5_w_glu': nrm((N_S5_LAYERS, MIX_WIDTH, MIX_WIDTH), MIX_WIDTH ** -0.5),
        's5_b_glu': nrm((N_S5_LAYERS, MIX_WIDTH), 0.01),
        'w_in_ret': nrm((N_RET_LAYERS, D_MODEL, RET_IN)) * ret_scale,
        'ret_gn': 1.0 + nrm((N_RET_LAYERS, MIX_WIDTH), 0.01),
        'w_in_gla': nrm((N_GLA_LAYERS, D_MODEL, GLA_IN)) * gla_scale,
        'gla_w_gate2': nrm((N_GLA_LAYERS, GLA_RANK, GLA_HEADS * GLA_DK), GLA_RANK ** -0.5),
        'gla_b_gate2': nrm((N_GLA_LAYERS, GLA_HEADS * GLA_DK), 0.01),
        'gla_gn': 1.0 + nrm((N_GLA_LAYERS, MIX_WIDTH), 0.01),
        'w_mem_k': nrm((DEPTH, D_MODEL, XA_WIDTH), s_in),
        'w_mem_v': nrm((DEPTH, D_MODEL, XA_WIDTH), s_in * DN_BETA),
        'w_out': nrm((DEPTH, MIX_WIDTH + XA_WIDTH, D_MODEL), (MIX_WIDTH + XA_WIDTH) ** -0.5 * DN_BETA),
        'ln_g': 1.0 + nrm((DEPTH, 2, D_MODEL), 0.01),
        'ln_b': nrm((DEPTH, 2, D_MODEL), 0.01),
        'moe_w_grp': nrm((DEPTH, D_MODEL, MOE_GROUPS), s_in),
        'moe_b_grp': nrm((DEPTH, MOE_GROUPS), 0.01),
        'moe_w_exp': nrm((DEPTH, D_MODEL, MOE_EXPERTS), s_in),
        'moe_b_exp': nrm((DEPTH, MOE_EXPERTS), 0.01),
        'moe_w_gate': nrm((DEPTH, MOE_EXPERTS, D_MODEL, MOE_HIDDEN), s_in),
        'moe_w_up': nrm((DEPTH, MOE_EXPERTS, D_MODEL, MOE_HIDDEN), s_in),
        'moe_w_down': nrm((DEPTH, MOE_EXPERTS, MOE_HIDDEN, D_MODEL), MOE_HIDDEN ** -0.5 * DN_BETA),
    }


def reference(x_prompt, x_sample, mem_prompt, state_s5_re, state_s5_im, state_ret, state_gla,
              cache_mem_k, cache_mem_v,
              w_in_s5, s5_lam_re, s5_lam_im, s5_log_dt, s5_b_re, s5_b_im, s5_c_re, s5_c_im,
              s5_d, s5_w_glu, s5_b_glu,
              w_in_ret, ret_gn,
              w_in_gla, gla_w_gate2, gla_b_gate2, gla_gn,
              w_mem_k, w_mem_v, w_out, ln_g, ln_b,
              moe_w_grp, moe_b_grp, moe_w_exp, moe_b_exp, moe_w_gate, moe_w_up, moe_w_down):

    def layer(i, x, st, mk, mv, pos):
        kind, j = i % N_MIXERS, i // N_MIXERS
        if kind == 0:
            z = x @ w_in_s5[j]
            mix, h_re, h_im = s5_mixer(z[..., :MIX_WIDTH], st[0], st[1], s5_lam_re[j], s5_lam_im[j],
                                       s5_log_dt[j], s5_b_re[j], s5_b_im[j], s5_c_re[j], s5_c_im[j],
                                       s5_d[j], s5_w_glu[j], s5_b_glu[j])
            new_st = (h_re, h_im)
        elif kind == 1:
            z = x @ w_in_ret[j]
            mix, new_st = retention_mixer(z[..., :-XA_WIDTH], st, pos, ret_gn[j])
        else:
            z = x @ w_in_gla[j]
            mix, new_st = gla_mixer(z[..., :-XA_WIDTH], st, gla_w_gate2[j], gla_b_gate2[j], gla_gn[j])
        att = mem_attention(z[..., -XA_WIDTH:], mk, mv)
        h = jnp.concatenate([mix.astype(x.dtype), att.astype(x.dtype)], axis=-1) @ w_out[i]
        x = layer_norm(DN_ALPHA * x + h, ln_g[i, 0], ln_b[i, 0])
        y = hier_moe(x, moe_w_grp[i], moe_b_grp[i], moe_w_exp[i], moe_b_exp[i],
                     moe_w_gate[i], moe_w_up[i], moe_w_down[i])
        x = layer_norm(DN_ALPHA * x + y, ln_g[i, 1], ln_b[i, 1])
        return x, new_st

    bp = x_prompt.shape[0]
    pos_p = jnp.arange(x_prompt.shape[1], dtype=jnp.int32)
    pos_s = PAST_LEN + jnp.arange(x_sample.shape[1], dtype=jnp.int32)
    xp, xs = x_prompt, x_sample
    p_re, p_im, p_ret, p_gla, p_mk, p_mv = [], [], [], [], [], []
    s_re, s_im, s_ret, s_gla = [], [], [], []
    for i in range(DEPTH):
        kind, j = i % N_MIXERS, i // N_MIXERS
        mk_p = (mem_prompt @ w_mem_k[i]).reshape(bp, N_MEM, XA_HEADS, XA_HEAD_DIM)
        mv_p = (mem_prompt @ w_mem_v[i]).reshape(bp, N_MEM, XA_HEADS, XA_HEAD_DIM)
        p_mk.append(mk_p)
        p_mv.append(mv_p)
        mk_s, mv_s = cache_mem_k[i], cache_mem_v[i]
        if kind == 0:
            z0 = jnp.zeros((bp, S5_GROUPS, S5_STATE), jnp.float32)
            xp, (a_re, a_im) = layer(i, xp, (z0, z0), mk_p, mv_p, pos_p)
            xs, (b_re, b_im) = layer(i, xs, (state_s5_re[j], state_s5_im[j]), mk_s, mv_s, pos_s)
            p_re.append(a_re.astype(state_s5_re.dtype))
            p_im.append(a_im.astype(state_s5_im.dtype))
            s_re.append(b_re.astype(state_s5_re.dtype))
            s_im.append(b_im.astype(state_s5_im.dtype))
        elif kind == 1:
            z0 = jnp.zeros((bp, RET_HEADS, RET_DK, RET_DV), jnp.float32)
            xp, a_s = layer(i, xp, z0, mk_p, mv_p, pos_p)
            xs, b_s = layer(i, xs, state_ret[j], mk_s, mv_s, pos_s)
            p_ret.append(a_s.astype(state_ret.dtype))
            s_ret.append(b_s.astype(state_ret.dtype))
        else:
            z0 = jnp.zeros((bp, GLA_HEADS, GLA_DK, GLA_DV), jnp.float32)
            xp, a_s = layer(i, xp, z0, mk_p, mv_p, pos_p)
            xs, b_s = layer(i, xs, state_gla[j], mk_s, mv_s, pos_s)
            p_gla.append(a_s.astype(state_gla.dtype))
            s_gla.append(b_s.astype(state_gla.dtype))
    return (xp, xs,
            jnp.stack(p_re), jnp.stack(p_im), jnp.stack(p_ret), jnp.stack(p_gla),
            jnp.stack(p_mk), jnp.stack(p_mv),
            jnp.stack(s_re), jnp.stack(s_im), jnp.stack(s_ret), jnp.stack(s_gla))
```

```python
import functools
import math

import jax
import jax.numpy as jnp
from jax import lax
from jax.experimental import pallas as pl
from jax.experimental.pallas import tpu as pltpu

F32 = jnp.float32
BF16 = jnp.bfloat16
HI = lax.Precision.HIGHEST

D_MODEL = 1024
DEPTH = 4
PAST_LEN = 16384
N_MIXERS = 3
MIX_WIDTH = 768
XA_HEADS = 4
XA_HEAD_DIM = 64
XA_WIDTH = 256
N_MEM = 256
S5_GROUP = 16
S5_GROUPS = 48
S5_STATE = 64
S5_SLABS = 6
S5_SLAB_STATE = 512
RET_HEADS = 4
RET_DK = 128
RET_DV = 192
RET_CHUNK = 128
ROPE_BASE = 10000.0
GLA_HEADS = 4
GLA_DK = 96
GLA_DV = 192
GLA_RANK = 16
GLA_TAU = 16.0
GLA_CHUNK = 64
MOE_GROUPS = 4
MOE_PER_GROUP = 4
MOE_EXPERTS = 16
MOE_HIDDEN = 256
DN_ALPHA = (2 * DEPTH) ** 0.25
NORM_EPS = 1e-5

DK_PAD = 128
DV_PAD = 256
HEADS = 4
COL_Q, COL_K, COL_V, COL_G, COL_XA, COL_LOW = 0, 512, 1024, 2048, 3072, 3328
RET_IN_PAD = 3328
GLA_IN_PAD = 3456
VMEM_LIMIT = 52 * 1024 * 1024


def _cparams(*sem):
    return pltpu.CompilerParams(dimension_semantics=sem, vmem_limit_bytes=VMEM_LIMIT)


def _dot(a, b):
    return jnp.dot(a, b, preferred_element_type=F32)


def _dot_nt(a, b):
    return lax.dot_general(a, b, (((1,), (1,)), ((), ())), preferred_element_type=F32)


def _dot_hi(a, b):
    return jnp.dot(a, b, preferred_element_type=F32, precision=HI)


def _layer_norm(y, g, b):
    mu = jnp.mean(y, axis=-1, keepdims=True)
    d = y - mu
    var = jnp.mean(d * d, axis=-1, keepdims=True)
    return d * lax.rsqrt(var + NORM_EPS) * g + b


def _sigmoid(x):
    return 1.0 / (1.0 + jnp.exp(-x))


def _silu(x):
    return x * _sigmoid(x)


def _gelu_tanh(x):
    c = math.sqrt(2.0 / math.pi)
    return 0.5 * x * (1.0 + jnp.tanh(c * (x + 0.044715 * (x * x * x))))


def _matmul_kernel(x_ref, w_ref, o_ref, *, precise):
    if precise:
        o_ref[...] = _dot_hi(x_ref[...], w_ref[...])
    else:
        o_ref[...] = _dot(x_ref[...].astype(BF16), w_ref[...])


def _matmul(x, w, tm, precise=False):
    t, k = x.shape
    tm = min(tm, t)
    n = w.shape[1]
    return pl.pallas_call(
        functools.partial(_matmul_kernel, precise=precise),
        grid=(t // tm,),
        in_specs=[pl.BlockSpec((tm, k), lambda i: (i, 0)),
                  pl.BlockSpec((k, n), lambda i: (0, 0))],
        out_specs=pl.BlockSpec((tm, n), lambda i: (i, 0)),
        out_shape=jax.ShapeDtypeStruct((t, n), F32),
        compiler_params=_cparams("parallel"),
        name="in_proj",
    )(x, w)


def _memkv_kernel(x_ref, wk_ref, wv_ref, ok_ref, ov_ref):
    xb = x_ref[...].astype(BF16)
    ok_ref[0] = _dot(xb, wk_ref[0])
    ov_ref[0] = _dot(xb, wv_ref[0])


def _mem_kv(mem, wk, wv):
    r = mem.shape[0]
    spec_w = pl.BlockSpec((1, D_MODEL, XA_WIDTH), lambda i: (i, 0, 0))
    spec_o = pl.BlockSpec((1, r, XA_WIDTH), lambda i: (i, 0, 0))
    return pl.pallas_call(
        _memkv_kernel,
        grid=(DEPTH,),
        in_specs=[pl.BlockSpec((r, D_MODEL), lambda i: (0, 0)), spec_w, spec_w],
        out_specs=[spec_o, spec_o],
        out_shape=[jax.ShapeDtypeStruct((DEPTH, r, XA_WIDTH), F32)] * 2,
        compiler_params=_cparams("parallel"),
        name="mem_kv",
    )(mem, wk, wv)


def _attn_kernel(q_ref, k_ref, v_ref, o_ref, *, nseq, rl):
    head = lax.broadcasted_iota(jnp.int32, (rl, XA_WIDTH), 1) // XA_HEAD_DIM
    for s in range(nseq):
        q = q_ref[s * rl:(s + 1) * rl, :] * (XA_HEAD_DIM ** -0.5)
        qs = jnp.concatenate([jnp.where(head == h, q, 0.0) for h in range(XA_HEADS)], axis=0)
        sc = _dot_nt(qs.astype(BF16), k_ref[s].astype(BF16))
        sc = sc - jnp.max(sc, axis=-1, keepdims=True)
        p = jnp.exp(sc)
        p = p / jnp.sum(p, axis=-1, keepdims=True)
        pv = _dot(p.astype(BF16), v_ref[s].astype(BF16))
        o = jnp.where(head == 0, pv[0:rl], 0.0)
        for h in range(1, XA_HEADS):
            o = o + jnp.where(head == h, pv[h * rl:(h + 1) * rl], 0.0)
        o_ref[s * rl:(s + 1) * rl, :] = o


def _mem_attention(z, xa_col, mk, mv, seq_len, rl, nseq):
    t = z.shape[0]
    rows = rl * nseq
    cb = xa_col // XA_WIDTH
    if nseq == 1:
        per_seq = seq_len // rl
        kv_map = lambda i: (i // per_seq, 0, 0)
    else:
        kv_map = lambda i: (i, 0, 0)
    return pl.pallas_call(
        functools.partial(_attn_kernel, nseq=nseq, rl=rl),
        grid=(t // rows,),
        in_specs=[pl.BlockSpec((rows, XA_WIDTH), lambda i: (i, cb)),
                  pl.BlockSpec((nseq, N_MEM, XA_WIDTH), kv_map),
                  pl.BlockSpec((nseq, N_MEM, XA_WIDTH), kv_map)],
        out_specs=pl.BlockSpec((rows, XA_WIDTH), lambda i: (i, 0)),
        out_shape=jax.ShapeDtypeStruct((t, XA_WIDTH), F32),
        compiler_params=_cparams("parallel"),
        name="mem_attention",
    )(z, mk, mv)


def _outproj_kernel(*refs, glu):
    if glu:
        mix_ref, att_ref, x_ref, wm_ref, wa_ref, g_ref, b_ref, wg_ref, bg_ref, o_ref = refs
        y = _gelu_tanh(mix_ref[...])
        mix = y * _sigmoid(_dot(y.astype(BF16), wg_ref[...]) + bg_ref[...])
    else:
        mix_ref, att_ref, x_ref, wm_ref, wa_ref, g_ref, b_ref, o_ref = refs
        mix = mix_ref[...]
    h = _dot(mix.astype(BF16), wm_ref[...]) + _dot(att_ref[...].astype(BF16), wa_ref[...])
    o_ref[...] = _layer_norm(DN_ALPHA * x_ref[...] + h, g_ref[...], b_ref[...])


def _out_proj_ln(mix, att, x, wm, wa, g, b, glu_w=None, glu_b=None, tm=512):
    t = x.shape[0]
    tm = min(tm, t)
    km = mix.shape[1]
    row = lambda w: pl.BlockSpec((tm, w), lambda i: (i, 0))
    full = lambda a: pl.BlockSpec(a.shape, lambda i: (0,) * a.ndim)
    args = [mix, att, x, wm, wa, g, b]
    specs = [row(km), row(XA_WIDTH), row(D_MODEL), full(wm), full(wa), full(g), full(b)]
    if glu_w is not None:
        args += [glu_w, glu_b]
        specs += [full(glu_w), full(glu_b)]
    return pl.pallas_call(
        functools.partial(_outproj_kernel, glu=glu_w is not None),
        grid=(t // tm,),
        in_specs=specs,
        out_specs=row(D_MODEL),
        out_shape=jax.ShapeDtypeStruct((t, D_MODEL), F32),
        compiler_params=_cparams("parallel"),
        name="out_proj_ln",
    )(*args)


def _router_gates(x, wr, br):
    lg = _dot_hi(x, wr) + br
    lane = lax.broadcasted_iota(jnp.int32, lg.shape, 1).astype(F32)
    ninf = jnp.float32(-jnp.inf)
    gm = (lane >= MOE_EXPERTS) & (lane < MOE_EXPERTS + MOE_GROUPS)
    gl = jnp.where(gm, lg, ninf)
    ge = jnp.exp(gl - jnp.max(gl, axis=-1, keepdims=True))
    gp = ge / jnp.sum(ge, axis=-1, keepdims=True)
    pmax = jnp.max(gp, axis=-1, keepdims=True)
    gi = jnp.min(jnp.where((gp == pmax) & gm, lane, 1e4), axis=-1, keepdims=True) - MOE_EXPERTS
    lo = gi * MOE_PER_GROUP
    em = (lane >= lo) & (lane < lo + MOE_PER_GROUP)
    el = jnp.where(em, lg, ninf)
    v1 = jnp.max(el, axis=-1, keepdims=True)
    i1 = jnp.min(jnp.where((el == v1) & em, lane, 1e4), axis=-1, keepdims=True)
    em2 = em & (lane != i1)
    el2 = jnp.where(em2, lg, ninf)
    v2 = jnp.max(el2, axis=-1, keepdims=True)
    i2 = jnp.min(jnp.where((el2 == v2) & em2, lane, 1e4), axis=-1, keepdims=True)
    t = jnp.exp(v2 - v1)
    w1 = 1.0 / (1.0 + t)
    w2 = t / (1.0 + t)
    return jnp.where(lane == i1, w1 * pmax, 0.0) + jnp.where(lane == i2, w2 * pmax, 0.0)


def _moe_kernel(x_ref, wr_ref, br_ref, wgu_ref, wd_ref, g_ref, b_ref, o_ref, acc_ref, gates_ref, xb_ref):
    grp = pl.program_id(1)

    @pl.when(grp == 0)
    def _():
        x = x_ref[...]
        gates_ref[...] = _router_gates(x, wr_ref[...], br_ref[...])
        xb_ref[...] = x.astype(BF16)
        acc_ref[...] = jnp.zeros_like(acc_ref)

    gates = gates_ref[...]
    lane = lax.broadcasted_iota(jnp.int32, gates.shape, 1)
    gu = _dot(xb_ref[...], wgu_ref[0])
    half = MOE_PER_GROUP * MOE_HIDDEN
    hs = []
    for j in range(MOE_PER_GROUP):
        ge = jnp.sum(jnp.where(lane == grp * MOE_PER_GROUP + j, gates, 0.0), axis=-1, keepdims=True)
        a = gu[:, j * MOE_HIDDEN:(j + 1) * MOE_HIDDEN]
        u = gu[:, half + j * MOE_HIDDEN:half + (j + 1) * MOE_HIDDEN]
        hs.append((_silu(a) * u * ge).astype(BF16))
    acc_ref[...] += _dot(jnp.concatenate(hs, axis=-1), wd_ref[0])

    @pl.when(grp == MOE_GROUPS - 1)
    def _():
        o_ref[...] = _layer_norm(DN_ALPHA * x_ref[...] + acc_ref[...], g_ref[...], b_ref[...])


def _moe_ln(x, wr, br, wgu, wd, g, b, tm=512):
    t = x.shape[0]
    tm = min(tm, t)
    row = pl.BlockSpec((tm, D_MODEL), lambda i, e: (i, 0))
    full = lambda a: pl.BlockSpec(a.shape, lambda i, e: (0,) * a.ndim)
    return pl.pallas_call(
        _moe_kernel,
        grid=(t // tm, MOE_GROUPS),
        in_specs=[row, full(wr), full(br),
                  pl.BlockSpec((1,) + wgu.shape[1:], lambda i, e: (e, 0, 0)),
                  pl.BlockSpec((1,) + wd.shape[1:], lambda i, e: (e, 0, 0)),
                  full(g), full(b)],
        out_specs=row,
        out_shape=jax.ShapeDtypeStruct((t, D_MODEL), F32),
        scratch_shapes=[pltpu.VMEM((tm, D_MODEL), F32), pltpu.VMEM((tm, 128), F32),
                        pltpu.VMEM((tm, D_MODEL), BF16)],
        compiler_params=_cparams("parallel", "arbitrary"),
        name="moe_ln",
    )(x, wr, br, wgu, wd, g, b)


def _s5_scan_block(xr, xi, tab_ref, cr, ci):
    for k in range(3):
        sh = 1 << k
        ar = tab_ref[0, 2 * k]
        ai = tab_ref[0, 2 * k + 1]
        sr = pltpu.roll(xr, sh, axis=0)
        si = pltpu.roll(xi, sh, axis=0)
        xr, xi = xr + ar * sr - ai * si, xi + ar * si + ai * sr
    pr = tab_ref[0, 6]
    pi = tab_ref[0, 7]
    return xr + pr * cr - pi * ci, xi + pr * ci + pi * cr


def _s5_kernel(*refs, tl, chained, tiles_per_seq, precise):
    if chained:
        u_ref, wb_ref, wc_ref, tab_ref, d_ref, y_ref, ore_ref, oim_ref, bu_ref, h_ref, cr_ref, ci_ref = refs
    else:
        (u_ref, h0r_ref, h0i_ref, wb_ref, wc_ref, tab_ref, d_ref,
         y_ref, ore_ref, oim_ref, bu_ref, h_ref) = refs
    u = u_ref[...]
    if precise:
        bu_ref[...] = _dot_hi(u, wb_ref[0])
    else:
        bu_ref[...] = _dot(u.astype(BF16), wb_ref[0].astype(BF16))
    ns = S5_SLAB_STATE
    bcast = lambda row: jnp.broadcast_to(row, (8, ns))

    if chained:
        @pl.when(pl.program_id(1) % tiles_per_seq == 0)
        def _():
            cr_ref[...] = jnp.zeros_like(cr_ref)
            ci_ref[...] = jnp.zeros_like(ci_ref)

        def body(b, carry):
            cr, ci = carry
            r0 = pl.multiple_of(b * 8, 8)
            hr, hi = _s5_scan_block(bu_ref[pl.ds(r0, 8), 0:ns], bu_ref[pl.ds(r0, 8), ns:2 * ns], tab_ref, cr, ci)
            h_ref[pl.ds(r0, 8), 0:ns] = hr
            h_ref[pl.ds(r0, 8), ns:2 * ns] = hi
            return bcast(hr[7:8, :]), bcast(hi[7:8, :])

        cr, ci = lax.fori_loop(0, tl // 8, body, (cr_ref[...], ci_ref[...]))
        cr_ref[...] = cr
        ci_ref[...] = ci
        ore_ref[0] = cr[0:1, :]
        oim_ref[0] = ci[0:1, :]
    else:
        def body(b, _):
            r0 = pl.multiple_of(b * 8, 8)
            cr = bcast(h0r_ref[pl.ds(b, 1), :])
            ci = bcast(h0i_ref[pl.ds(b, 1), :])
            hr, hi = _s5_scan_block(bu_ref[pl.ds(r0, 8), 0:ns], bu_ref[pl.ds(r0, 8), ns:2 * ns], tab_ref, cr, ci)
            h_ref[pl.ds(r0, 8), 0:ns] = hr
            h_ref[pl.ds(r0, 8), ns:2 * ns] = hi
            ore_ref[pl.ds(b, 1), :] = hr[7:8, :]
            oim_ref[pl.ds(b, 1), :] = hi[7:8, :]
            return 0

        lax.fori_loop(0, tl // 8, body, 0)
    y_ref[...] = _dot(h_ref[...].astype(BF16), wc_ref[0].astype(BF16)) + d_ref[0] * u


def _s5_mixer(z, params, seq_len, h0=None, tl=256):
    t = z.shape[0]
    nseq = t // seq_len
    wb, wc, tab, dsk = params
    chained = h0 is None
    ns = S5_SLAB_STATE
    u_spec = pl.BlockSpec((tl, 128), lambda j, i: (i, j))
    slab3 = lambda a: pl.BlockSpec((1,) + a.shape[1:], lambda j, i: (j,) + (0,) * (a.ndim - 1))
    w_specs = [slab3(wb), slab3(wc), slab3(tab), slab3(dsk)]
    scratch = [pltpu.VMEM((tl, 2 * ns), F32), pltpu.VMEM((tl, 2 * ns), F32)]
    if chained:
        tiles_per_seq = seq_len // tl
        st_spec = pl.BlockSpec((1, 1, ns), lambda j, i: (i // tiles_per_seq, 0, j))
        st_shape = jax.ShapeDtypeStruct((nseq, 1, S5_SLABS * ns), F32)
        in_specs = [u_spec] + w_specs
        args = (z,) + tuple(params)
        scratch += [pltpu.VMEM((8, ns), F32), pltpu.VMEM((8, ns), F32)]
    else:
        assert seq_len == 8
        tiles_per_seq = 1
        st_spec = pl.BlockSpec((tl // 8, ns), lambda j, i: (i, j))
        st_shape = jax.ShapeDtypeStruct((nseq, S5_SLABS * ns), F32)
        in_specs = [u_spec, st_spec, st_spec] + w_specs
        args = (z, h0[0], h0[1]) + tuple(params)
    y, hre, him = pl.pallas_call(
        functools.partial(_s5_kernel, tl=tl, chained=chained, tiles_per_seq=tiles_per_seq,
                          precise=not chained),
        grid=(S5_SLABS, t // tl),
        in_specs=in_specs,
        out_specs=[pl.BlockSpec((tl, 128), lambda j, i: (i, j)), st_spec, st_spec],
        out_shape=[jax.ShapeDtypeStruct((t, MIX_WIDTH), F32), st_shape, st_shape],
        scratch_shapes=scratch,
        compiler_params=_cparams("parallel", "arbitrary"),
        name="s5_mixer",
    )(*args)
    shape = (nseq, S5_GROUPS, S5_STATE)
    return y, hre.reshape(shape), him.reshape(shape)


def _s5_params(lam_re, lam_im, log_dt, b_re, b_im, c_re, c_im, d_skip):
    lam = lax.complex(lam_re.astype(F32), lam_im.astype(F32))
    ldt = lam * jnp.exp(log_dt.astype(F32))[:, None]
    lam_bar = jnp.exp(ldt)
    b_bar = ((lam_bar - 1.0) / lam)[..., None] * lax.complex(b_re.astype(F32), b_im.astype(F32))
    eye = jnp.eye(8, dtype=F32)
    gps = 8

    def blockdiag_in(m):
        m = m.reshape(S5_SLABS, gps, S5_STATE, S5_GROUP).transpose(0, 1, 3, 2)
        return jnp.einsum('jgcp,gh->jgchp', m, eye).reshape(S5_SLABS, 128, S5_SLAB_STATE)

    def blockdiag_out(m):
        m = m.reshape(S5_SLABS, gps, S5_GROUP, S5_STATE).transpose(0, 1, 3, 2)
        return jnp.einsum('jgpc,gh->jgphc', m, eye).reshape(S5_SLABS, S5_SLAB_STATE, 128)

    wb = jnp.concatenate([blockdiag_in(jnp.real(b_bar)), blockdiag_in(jnp.imag(b_bar))], axis=2)
    wc = jnp.concatenate([blockdiag_out(c_re.astype(F32)), blockdiag_out(-c_im.astype(F32))], axis=1)
    ldt_s = ldt.reshape(S5_SLABS, S5_SLAB_STATE)
    row = jnp.arange(8)
    tabs = []
    for k in range(3):
        a = jnp.exp(ldt_s * float(1 << k))
        m = (row >= (1 << k)).astype(F32)[None, :, None]
        tabs += [jnp.real(a)[:, None, :] * m, jnp.imag(a)[:, None, :] * m]
    p = jnp.exp(ldt_s[:, None, :] * (row + 1).astype(F32)[None, :, None])
    tabs += [jnp.real(p), jnp.imag(p)]
    tab = jnp.stack(tabs, axis=1)
    dsk = d_skip.astype(F32).reshape(S5_SLABS, 1, 128)
    return wb, wc, tab, dsk


def _head_norm_gate(o, gn, gate):
    col = lax.broadcasted_iota(jnp.int32, o.shape, 1)
    valid = col < RET_DV
    mu = jnp.sum(o, axis=-1, keepdims=True) * (1.0 / RET_DV)
    d = jnp.where(valid, o - mu, 0.0)
    var = jnp.sum(d * d, axis=-1, keepdims=True) * (1.0 / RET_DV)
    return d * lax.rsqrt(var + NORM_EPS) * gn * _silu(gate)


def _load_states(s_in_ref, spad_ref, nseg, dk):
    spad_ref[...] = jnp.zeros_like(spad_ref)
    for s in range(nseg):
        for h in range(HEADS):
            spad_ref[h, s * DK_PAD:s * DK_PAD + dk, 0:RET_DV] = s_in_ref[s, h]


def _ret_kernel(*refs, nseg, gammas):
    if nseg == 1:
        (q_ref, k_ref, v_ref, g_ref, cos_ref, sin_ref, dmat_ref, inner_ref, zeta_ref, gn_ref,
         o_ref, so_ref, spad_ref) = refs

        @pl.when(pl.program_id(1) == 0)
        def _():
            spad_ref[...] = jnp.zeros_like(spad_ref)
    else:
        (q_ref, k_ref, v_ref, g_ref, cos_ref, sin_ref, dmat_ref, inner_ref, zeta_ref, gn_ref, s_in_ref,
         o_ref, so_ref, spad_ref) = refs
        _load_states(s_in_ref, spad_ref, nseg, RET_DK)
    r = q_ref.shape[0]
    seg_len = r // nseg
    cos = cos_ref[...]
    sin = sin_ref[...]
    rot = lambda x: x * cos + pltpu.roll(x, RET_DK // 2, axis=1) * sin
    rowseg = lax.broadcasted_iota(jnp.int32, (r, DK_PAD), 0) // seg_len
    colseg = lax.broadcasted_iota(jnp.int32, (DK_PAD, r), 1) // seg_len
    for h in range(HEADS):
        q = rot(q_ref[:, h * DK_PAD:(h + 1) * DK_PAD])
        k = rot(k_ref[:, h * DK_PAD:(h + 1) * DK_PAD]) * (RET_DK ** -0.5)
        vb = v_ref[:, h * DV_PAD:(h + 1) * DV_PAD].astype(BF16)
        qb = q.astype(BF16)
        att = _dot_nt(qb, k.astype(BF16)) * dmat_ref[h]
        o = _dot(att.astype(BF16), vb)
        s_all = spad_ref[h]
        if nseg == 1:
            qbd = qb
        else:
            qbd = jnp.concatenate([jnp.where(rowseg == s, q, 0.0) for s in range(nseg)], axis=1).astype(BF16)
        o = o + _dot(qbd, s_all.astype(BF16)) * inner_ref[h]
        kzt = jnp.transpose(k * zeta_ref[h])
        for s in range(nseg):
            kzs = kzt if nseg == 1 else jnp.where(colseg == s, kzt, 0.0)
            s_new = s_all[s * DK_PAD:(s + 1) * DK_PAD] * gammas[h] + _dot(kzs.astype(BF16), vb)
            if nseg == 1:
                spad_ref[h] = s_new
                so_ref[0, h] = s_new[:, 0:RET_DV]
            else:
                so_ref[s, h] = s_new[:, 0:RET_DV]
        o_ref[:, h * DV_PAD:(h + 1) * DV_PAD] = _head_norm_gate(
            o, gn_ref[:, h * DV_PAD:(h + 1) * DV_PAD], g_ref[:, h * DV_PAD:(h + 1) * DV_PAD])


def _ret_consts(seg_len, nseg, pos):
    hh = jnp.arange(RET_HEADS, dtype=F32)
    log_g = jnp.log1p(-jnp.exp2(-5.0 - hh))
    idx = jnp.arange(seg_len, dtype=F32)
    rel = idx[:, None] - idx[None, :]
    dmat = jnp.where(rel >= 0, jnp.exp(log_g[:, None, None] * jnp.maximum(rel, 0.0)), 0.0)
    inner = jnp.exp(log_g[:, None] * (idx + 1.0))
    zeta = jnp.exp(log_g[:, None] * (seg_len - 1.0 - idx))
    r = seg_len * nseg
    seg = jnp.arange(r) // seg_len
    same = (seg[:, None] == seg[None, :]).astype(F32)
    dmat = jnp.tile(dmat, (1, nseg, nseg)) * same[None]
    inner = jnp.broadcast_to(jnp.tile(inner, (1, nseg))[:, :, None], (RET_HEADS, r, DV_PAD))
    zeta = jnp.broadcast_to(jnp.tile(zeta, (1, nseg))[:, :, None], (RET_HEADS, r, DK_PAD))
    half = RET_DK // 2
    inv_freq = ROPE_BASE ** (-jnp.arange(half, dtype=F32) / half)
    ang = pos.astype(F32)[:, None] * inv_freq
    cos2 = jnp.concatenate([jnp.cos(ang), jnp.cos(ang)], axis=-1)
    sin2 = jnp.concatenate([-jnp.sin(ang), jnp.sin(ang)], axis=-1)
    return dmat, inner, zeta, cos2, sin2


def _ret_gammas(seg_len):
    return tuple(float((1.0 - 2.0 ** (-5.0 - h)) ** seg_len) for h in range(RET_HEADS))


def _ret_mixer(z, gn, seq_len, pos, s0=None):
    t = z.shape[0]
    nseq = t // seq_len
    if s0 is None:
        r, nseg, seg_len = RET_CHUNK, 1, RET_CHUNK
        steps = seq_len // r
        grid = (nseq, steps)
        rowmap = lambda c: (lambda n, i: (n * steps + i, c))
        tabmap = lambda n, i: (i, 0)
        stmap = lambda n, i: (n, 0, 0, 0)
        st_rows = 1
        sem = ("parallel", "arbitrary")
    else:
        seg_len = seq_len
        nseg = 128 // seg_len
        r = 128
        grid = (t // r,)
        rowmap = lambda c: (lambda i: (i, c))
        tabmap = lambda i: (0, 0)
        stmap = lambda i: (i, 0, 0, 0)
        st_rows = nseg
        sem = ("arbitrary",)
    dmat, inner, zeta, cos2, sin2 = _ret_consts(seg_len, nseg, pos)
    if s0 is not None:
        cos2 = jnp.tile(cos2, (nseg, 1))
        sin2 = jnp.tile(sin2, (nseg, 1))
    full = lambda a: pl.BlockSpec(a.shape, lambda *_: (0,) * a.ndim)
    st_spec = pl.BlockSpec((st_rows, RET_HEADS, RET_DK, RET_DV), stmap)
    in_specs = [pl.BlockSpec((r, 512), rowmap(0)), pl.BlockSpec((r, 512), rowmap(1)),
                pl.BlockSpec((r, 1024), rowmap(1)), pl.BlockSpec((r, 1024), rowmap(2)),
                pl.BlockSpec((r, DK_PAD), tabmap), pl.BlockSpec((r, DK_PAD), tabmap),
                full(dmat), full(inner), full(zeta), full(gn)]
    args = [z, z, z, z, cos2, sin2, dmat, inner, zeta, gn]
    if s0 is not None:
        in_specs.append(st_spec)
        args.append(s0)
    mix, s_new = pl.pallas_call(
        functools.partial(_ret_kernel, nseg=nseg, gammas=_ret_gammas(seg_len)),
        grid=grid,
        in_specs=in_specs,
        out_specs=[pl.BlockSpec((r, HEADS * DV_PAD), rowmap(0)), st_spec],
        out_shape=[jax.ShapeDtypeStruct((t, HEADS * DV_PAD), F32),
                   jax.ShapeDtypeStruct((nseq, RET_HEADS, RET_DK, RET_DV), F32)],
        scratch_shapes=[pltpu.VMEM((HEADS, nseg * DK_PAD, DV_PAD), F32)],
        compiler_params=_cparams(*sem),
        name="retention",
    )(*args)
    return mix, s_new


def _log_sigmoid(x):
    return jnp.minimum(x, 0.0) - jnp.log1p(jnp.exp(-jnp.abs(x)))


def _gla_kernel(*refs, nseg):
    if nseg == 1:
        (q_ref, k_ref, v_ref, r_ref, low_ref, tril_ref, same_ref, wg_ref, bg_ref, gn_ref,
         o_ref, so_ref, spad_ref) = refs

        @pl.when(pl.program_id(1) == 0)
        def _():
            spad_ref[...] = jnp.zeros_like(spad_ref)
    else:
        (q_ref, k_ref, v_ref, r_ref, low_ref, tril_ref, same_ref, wg_ref, bg_ref, gn_ref, s_in_ref,
         o_ref, so_ref, spad_ref) = refs
        _load_states(s_in_ref, spad_ref, nseg, GLA_DK)
    r = q_ref.shape[0]
    seg_len = r // nseg
    rp = max(r, 128)
    tril = tril_ref[...]
    causal = tril > 0.0
    la = _log_sigmoid(_dot_hi(low_ref[...], wg_ref[...]) + bg_ref[...]) * (1.0 / GLA_TAU)
    b = _dot_hi(tril, la)
    b_tot = _dot_hi(same_ref[...], la)
    rowseg = lax.broadcasted_iota(jnp.int32, (r, DK_PAD), 0) // seg_len
    colseg = lax.broadcasted_iota(jnp.int32, (DK_PAD, rp), 1) // seg_len

    def pad_rows(x):
        if rp == r:
            return x
        return jnp.concatenate([x, jnp.zeros((rp - r, x.shape[1]), x.dtype)], axis=0)

    for h in range(HEADS):
        sl = slice(h * DK_PAD, (h + 1) * DK_PAD)
        bh = b[:, sl]
        lah = la[:, sl]
        q = q_ref[:, sl] * (GLA_DK ** -0.5)
        k = k_ref[:, sl]
        vb = v_ref[:, h * DV_PAD:(h + 1) * DV_PAD].astype(BF16)
        q_t = q * jnp.exp(bh)
        k_t = k * jnp.exp(-bh)
        att = jnp.where(causal, _dot_nt(q_t.astype(BF16), k_t.astype(BF16)), 0.0)
        o = _dot(att.astype(BF16), vb)
        s_all = spad_ref[h]
        if nseg == 1:
            qbd = q_t.astype(BF16)
        else:
            qbd = jnp.concatenate([jnp.where(rowseg == s, q_t, 0.0) for s in range(nseg)], axis=1).astype(BF16)
        o = o + _dot(qbd, s_all.astype(BF16))
        k_dec = k * jnp.exp(b_tot[:, sl] - bh)
        kdt = jnp.transpose(pad_rows(k_dec))
        lat = jnp.transpose(pad_rows(lah))
        vbp = pad_rows(vb)
        for s in range(nseg):
            if nseg == 1:
                kds, las = kdt, lat
            else:
                kds = jnp.where(colseg == s, kdt, 0.0)
                las = jnp.where(colseg == s, lat, 0.0)
            decay = jnp.exp(jnp.sum(las, axis=-1, keepdims=True))
            s_new = s_all[s * DK_PAD:(s + 1) * DK_PAD] * decay + _dot(kds.astype(BF16), vbp)
            if nseg == 1:
                spad_ref[h] = s_new
                so_ref[0, h] = s_new[0:GLA_DK, 0:GLA_DV]
            else:
                so_ref[s, h] = s_new[0:GLA_DK, 0:GLA_DV]
        o_ref[:, h * DV_PAD:(h + 1) * DV_PAD] = _head_norm_gate(
            o, gn_ref[:, h * DV_PAD:(h + 1) * DV_PAD], r_ref[:, h * DV_PAD:(h + 1) * DV_PAD])


def _gla_mixer(z, wg, bg, gn, seq_len, s0=None):
    t = z.shape[0]
    nseq = t // seq_len
    if s0 is None:
        r, nseg, seg_len = GLA_CHUNK, 1, GLA_CHUNK
        steps = seq_len // r
        grid = (nseq, steps)
        rowmap = lambda c: (lambda n, i: (n * steps + i, c))
        stmap = lambda n, i: (n, 0, 0, 0)
        st_rows = 1
        sem = ("parallel", "arbitrary")
    else:
        seg_len = seq_len
        nseg = 128 // seg_len
        r = 128
        grid = (t // r,)
        rowmap = lambda c: (lambda i: (i, c))
        stmap = lambda i: (i, 0, 0, 0)
        st_rows = nseg
        sem = ("arbitrary",)
    seg = jnp.arange(r) // seg_len
    idx = jnp.arange(r)
    same = (seg[:, None] == seg[None, :]).astype(F32)
    tril = same * (idx[:, None] >= idx[None, :]).astype(F32)
    full = lambda a: pl.BlockSpec(a.shape, lambda *_: (0,) * a.ndim)
    st_spec = pl.BlockSpec((st_rows, GLA_HEADS, GLA_DK, GLA_DV), stmap)
    in_specs = [pl.BlockSpec((r, 512), rowmap(0)), pl.BlockSpec((r, 512), rowmap(1)),
                pl.BlockSpec((r, 1024), rowmap(1)), pl.BlockSpec((r, 1024), rowmap(2)),
                pl.BlockSpec((r, 128), rowmap(COL_LOW // 128)),
                full(tril), full(same), full(wg), full(bg), full(gn)]
    args = [z, z, z, z, z, tril, same, wg, bg, gn]
    if s0 is not None:
        in_specs.append(st_spec)
        args.append(s0)
    mix, s_new = pl.pallas_call(
        functools.partial(_gla_kernel, nseg=nseg),
        grid=grid,
        in_specs=in_specs,
        out_specs=[pl.BlockSpec((r, HEADS * DV_PAD), rowmap(0)), st_spec],
        out_shape=[jax.ShapeDtypeStruct((t, HEADS * DV_PAD), F32),
                   jax.ShapeDtypeStruct((nseq, GLA_HEADS, GLA_DK, GLA_DV), F32)],
        scratch_shapes=[pltpu.VMEM((HEADS, nseg * DK_PAD, DV_PAD), F32)],
        compiler_params=_cparams(*sem),
        name="gla",
    )(*args)
    return mix, s_new


def _pad_heads(w, width, pad):
    lead = w.shape[:-1]
    w = w.reshape(lead + (HEADS, width))
    w = jnp.pad(w, [(0, 0)] * len(lead) + [(0, 0), (0, pad - width)])
    return w.reshape(lead + (HEADS * pad,))


def _pad_head_rows(w, width, pad):
    return jnp.swapaxes(_pad_heads(jnp.swapaxes(w, 0, 1), width, pad), 0, 1)


def _ret_in_weight(w):
    qk = RET_HEADS * RET_DK
    v = _pad_heads(w[:, 2 * qk:2 * qk + MIX_WIDTH], RET_DV, DV_PAD)
    g = _pad_heads(w[:, 2 * qk + MIX_WIDTH:2 * qk + 2 * MIX_WIDTH], RET_DV, DV_PAD)
    return jnp.concatenate([w[:, :2 * qk], v, g, w[:, -XA_WIDTH:]], axis=1).astype(BF16)


def _gla_in_weight(w):
    qk = GLA_HEADS * GLA_DK
    q = _pad_heads(w[:, :qk], GLA_DK, DK_PAD)
    k = _pad_heads(w[:, qk:2 * qk], GLA_DK, DK_PAD)
    v = _pad_heads(w[:, 2 * qk:2 * qk + MIX_WIDTH], GLA_DV, DV_PAD)
    r = _pad_heads(w[:, 2 * qk + MIX_WIDTH:2 * qk + 2 * MIX_WIDTH], GLA_DV, DV_PAD)
    low = jnp.pad(w[:, 2 * qk + 2 * MIX_WIDTH:2 * qk + 2 * MIX_WIDTH + GLA_RANK], [(0, 0), (0, 128 - GLA_RANK)])
    return jnp.concatenate([q, k, v, r, w[:, -XA_WIDTH:], low], axis=1).astype(BF16)


def kernel(x_prompt, x_sample, mem_prompt, state_s5_re, state_s5_im, state_ret, state_gla,
           cache_mem_k, cache_mem_v,
           w_in_s5, s5_lam_re, s5_lam_im, s5_log_dt, s5_b_re, s5_b_im, s5_c_re, s5_c_im,
           s5_d, s5_w_glu, s5_b_glu,
           w_in_ret, ret_gn,
           w_in_gla, gla_w_gate2, gla_b_gate2, gla_gn,
           w_mem_k, w_mem_v, w_out, ln_g, ln_b,
           moe_w_grp, moe_b_grp, moe_w_exp, moe_b_exp, moe_w_gate, moe_w_up, moe_w_down):
    bp, lp, _ = x_prompt.shape
    bs, ls, _ = x_sample.shape
    xp = x_prompt.reshape(bp * lp, D_MODEL)
    xs = x_sample.reshape(bs * ls, D_MODEL)
    pos_p = jnp.arange(lp, dtype=jnp.int32)
    pos_s = PAST_LEN + jnp.arange(ls, dtype=jnp.int32)

    mk_all, mv_all = _mem_kv(mem_prompt.reshape(bp * N_MEM, D_MODEL), w_mem_k.astype(BF16), w_mem_v.astype(BF16))
    mk_all = mk_all.reshape(DEPTH, bp, N_MEM, XA_WIDTH)
    mv_all = mv_all.reshape(DEPTH, bp, N_MEM, XA_WIDTH)
    cache_k = cache_mem_k.reshape(DEPTH, bs, N_MEM, XA_WIDTH)
    cache_v = cache_mem_v.reshape(DEPTH, bs, N_MEM, XA_WIDTH)

    p_re, p_im, p_ret, p_gla = [], [], [], []
    s_re, s_im, s_ret, s_gla = [], [], [], []
    for i in range(DEPTH):
        kind, j = i % N_MIXERS, i // N_MIXERS
        ln1 = (ln_g[i, 0].reshape(1, D_MODEL), ln_b[i, 0].reshape(1, D_MODEL))
        ln2 = (ln_g[i, 1].reshape(1, D_MODEL), ln_b[i, 1].reshape(1, D_MODEL))
        wo = w_out[i]
        wo_att = wo[MIX_WIDTH:].astype(BF16)
        glu = {}
        if kind == 0:
            params = _s5_params(s5_lam_re[j], s5_lam_im[j], s5_log_dt[j], s5_b_re[j], s5_b_im[j],
                                s5_c_re[j], s5_c_im[j], s5_d[j])
            zp = _matmul(xp, w_in_s5[j].astype(BF16), 512)
            zs = _matmul(xs, w_in_s5[j], 512, precise=True)
            mix_p, a_re, a_im = _s5_mixer(zp, params, lp)
            h0 = (state_s5_re[j].reshape(bs, -1), state_s5_im[j].reshape(bs, -1))
            mix_s, b_re, b_im = _s5_mixer(zs, params, ls, h0=h0)
            p_re.append(a_re), p_im.append(a_im), s_re.append(b_re), s_im.append(b_im)
            xa_col = MIX_WIDTH
            wo_mix = wo[:MIX_WIDTH].astype(BF16)
            glu = dict(glu_w=s5_w_glu[j].astype(BF16), glu_b=s5_b_glu[j].reshape(1, MIX_WIDTH))
        elif kind == 1:
            w = _ret_in_weight(w_in_ret[j])
            gn = _pad_heads(ret_gn[j], RET_DV, DV_PAD).reshape(1, HEADS * DV_PAD)
            zp = _matmul(xp, w, 512)
            zs = _matmul(xs, w, 512)
            mix_p, a_s = _ret_mixer(zp, gn, lp, pos_p)
            mix_s, b_s = _ret_mixer(zs, gn, ls, pos_s, s0=state_ret[j])
            p_ret.append(a_s), s_ret.append(b_s)
            xa_col = COL_XA
            wo_mix = _pad_head_rows(wo[:MIX_WIDTH], RET_DV, DV_PAD).astype(BF16)
        else:
            w = _gla_in_weight(w_in_gla[j])
            gn = _pad_heads(gla_gn[j], GLA_DV, DV_PAD).reshape(1, HEADS * DV_PAD)
            wg = jnp.pad(_pad_heads(gla_w_gate2[j], GLA_DK, DK_PAD), [(0, 128 - GLA_RANK), (0, 0)])
            bg = _pad_heads(gla_b_gate2[j], GLA_DK, DK_PAD).reshape(1, HEADS * DK_PAD)
            zp = _matmul(xp, w, 512)
            zs = _matmul(xs, w, 512)
            mix_p, a_s = _gla_mixer(zp, wg, bg, gn, lp)
            mix_s, b_s = _gla_mixer(zs, wg, bg, gn, ls, s0=state_gla[j])
            p_gla.append(a_s), s_gla.append(b_s)
            xa_col = COL_XA
            wo_mix = _pad_head_rows(wo[:MIX_WIDTH], GLA_DV, DV_PAD).astype(BF16)
        att_p = _mem_attention(zp, xa_col, mk_all[i], mv_all[i], lp, rl=512, nseq=1)
        att_s = _mem_attention(zs, xa_col, cache_k[i], cache_v[i], ls, rl=ls, nseq=128 // ls)
        xp = _out_proj_ln(mix_p, att_p, xp, wo_mix, wo_att, *ln1, **glu)
        xs = _out_proj_ln(mix_s, att_s, xs, wo_mix, wo_att, *ln1, **glu)

        wr = jnp.pad(jnp.concatenate([moe_w_exp[i], moe_w_grp[i]], axis=1),
                     [(0, 0), (0, 128 - MOE_EXPERTS - MOE_GROUPS)]).astype(F32)
        br = jnp.pad(jnp.concatenate([moe_b_exp[i], moe_b_grp[i]]),
                     [(0, 128 - MOE_EXPERTS - MOE_GROUPS)]).reshape(1, 128).astype(F32)
        to_cols = lambda w: w.reshape(MOE_GROUPS, MOE_PER_GROUP, D_MODEL, MOE_HIDDEN).transpose(0, 2, 1, 3).reshape(
            MOE_GROUPS, D_MODEL, MOE_PER_GROUP * MOE_HIDDEN)
        wgu = jnp.concatenate([to_cols(moe_w_gate[i]), to_cols(moe_w_up[i])], axis=2).astype(BF16)
        wd = moe_w_down[i].reshape(MOE_GROUPS, MOE_PER_GROUP * MOE_HIDDEN, D_MODEL).astype(BF16)
        xp = _moe_ln(xp, wr, br, wgu, wd, *ln2)
        xs = _moe_ln(xs, wr, br, wgu, wd, *ln2)

    st = lambda xs_: jnp.stack(xs_)
    mem_shape = (DEPTH, bp, N_MEM, XA_HEADS, XA_HEAD_DIM)
    return (xp.reshape(bp, lp, D_MODEL), xs.reshape(bs, ls, D_MODEL),
            st(p_re), st(p_im), st(p_ret), st(p_gla),
            mk_all.reshape(mem_shape), mv_all.reshape(mem_shape),
            st(s_re), st(s_im), st(s_ret), st(s_gla))
```

```python
import functools
import math

import jax
import jax.numpy as jnp
from jax import lax
from jax.experimental import pallas as pl
from jax.experimental.pallas import tpu as pltpu

F32 = jnp.float32
BF16 = jnp.bfloat16
HI = lax.Precision.HIGHEST

D_MODEL = 1024
DEPTH = 4
PAST_LEN = 16384
N_MIXERS = 3
MIX_WIDTH = 768
XA_HEADS = 4
XA_HEAD_DIM = 64
XA_WIDTH = 256
N_MEM = 256
S5_GROUP = 16
S5_GROUPS = 48
S5_STATE = 64
S5_SLABS = 6
S5_SLAB_STATE = 512
RET_HEADS = 4
RET_DK = 128
RET_DV = 192
RET_CHUNK = 128
ROPE_BASE = 10000.0
GLA_HEADS = 4
GLA_DK = 96
GLA_DV = 192
GLA_RANK = 16
GLA_TAU = 16.0
GLA_CHUNK = 64
MOE_GROUPS = 4
MOE_PER_GROUP = 4
MOE_EXPERTS = 16
MOE_HIDDEN = 256
DN_ALPHA = (2 * DEPTH) ** 0.25
NORM_EPS = 1e-5

DK_PAD = 128
DV_PAD = 256
HEADS = 4
COL_Q, COL_K, COL_V, COL_G, COL_XA, COL_LOW = 0, 512, 1024, 2048, 3072, 3328
RET_IN_PAD = 3328
GLA_IN_PAD = 3456
VMEM_LIMIT = 52 * 1024 * 1024


def _cparams(*sem):
    return pltpu.CompilerParams(dimension_semantics=sem, vmem_limit_bytes=VMEM_LIMIT)


def _dot(a, b):
    return jnp.dot(a, b, preferred_element_type=F32)


def _dot_nt(a, b):
    return lax.dot_general(a, b, (((1,), (1,)), ((), ())), preferred_element_type=F32)


def _dot_hi(a, b):
    return jnp.dot(a, b, preferred_element_type=F32, precision=HI)


def _layer_norm(y, g, b):
    mu = jnp.mean(y, axis=-1, keepdims=True)
    d = y - mu
    var = jnp.mean(d * d, axis=-1, keepdims=True)
    return d * lax.rsqrt(var + NORM_EPS) * g + b


def _sigmoid(x):
    return 1.0 / (1.0 + jnp.exp(-x))


def _silu(x):
    return x * _sigmoid(x)


def _gelu_tanh(x):
    c = math.sqrt(2.0 / math.pi)
    return 0.5 * x * (1.0 + jnp.tanh(c * (x + 0.044715 * (x * x * x))))


def _matmul_kernel(x_ref, w_ref, o_ref, *, precise):
    if precise:
        o_ref[...] = _dot_hi(x_ref[...], w_ref[...])
    else:
        o_ref[...] = _dot(x_ref[...].astype(BF16), w_ref[...])


def _matmul(x, w, tm, precise=False):
    t, k = x.shape
    tm = min(tm, t)
    n = w.shape[1]
    return pl.pallas_call(
        functools.partial(_matmul_kernel, precise=precise),
        grid=(t // tm,),
        in_specs=[pl.BlockSpec((tm, k), lambda i: (i, 0)),
                  pl.BlockSpec((k, n), lambda i: (0, 0))],
        out_specs=pl.BlockSpec((tm, n), lambda i: (i, 0)),
        out_shape=jax.ShapeDtypeStruct((t, n), F32),
        compiler_params=_cparams("parallel"),
        name="in_proj",
    )(x, w)


def _memkv_kernel(x_ref, wk_ref, wv_ref, ok_ref, ov_ref):
    xb = x_ref[0].astype(BF16)
    ok_ref[0, 0] = _dot_nt(wk_ref[0], xb)
    ov_ref[0, 0] = _dot_nt(wv_ref[0], xb)


def _mem_kv(mem, wkt, wvt):
    n = mem.shape[0]
    spec_w = pl.BlockSpec((1, XA_WIDTH, D_MODEL), lambda i, b: (i, 0, 0))
    spec_o = pl.BlockSpec((1, 1, XA_WIDTH, N_MEM), lambda i, b: (i, b, 0, 0))
    return pl.pallas_call(
        _memkv_kernel,
        grid=(DEPTH, n),
        in_specs=[pl.BlockSpec((1, N_MEM, D_MODEL), lambda i, b: (b, 0, 0)), spec_w, spec_w],
        out_specs=[spec_o, spec_o],
        out_shape=[jax.ShapeDtypeStruct((DEPTH, n, XA_WIDTH, N_MEM), F32)] * 2,
        compiler_params=_cparams("parallel", "parallel"),
        name="mem_kv",
    )(mem, wkt, wvt)


def _attn_kernel(q_ref, kt_ref, vt_ref, o_ref, *, nseq, rl):
    head = lax.broadcasted_iota(jnp.int32, (rl, XA_WIDTH), 1) // XA_HEAD_DIM
    for s in range(nseq):
        q = q_ref[s * rl:(s + 1) * rl, :] * (XA_HEAD_DIM ** -0.5)
        qs = jnp.concatenate([jnp.where(head == h, q, 0.0) for h in range(XA_HEADS)], axis=0)
        sc = _dot(qs.astype(BF16), kt_ref[0, s].astype(BF16))
        sc = sc - jnp.max(sc, axis=-1, keepdims=True)
        p = jnp.exp(sc)
        p = p / jnp.sum(p, axis=-1, keepdims=True)
        pv = _dot_nt(p.astype(BF16), vt_ref[0, s].astype(BF16))
        o = jnp.where(head == 0, pv[0:rl], 0.0)
        for h in range(1, XA_HEADS):
            o = o + jnp.where(head == h, pv[h * rl:(h + 1) * rl], 0.0)
        o_ref[s * rl:(s + 1) * rl, :] = o


def _mem_attention(z, xa_col, mkt, mvt, layer, seq_len, rl, nseq):
    t = z.shape[0]
    rows = rl * nseq
    cb = xa_col // XA_WIDTH
    if nseq == 1:
        per_seq = seq_len // rl
        kv_map = lambda i: (layer, i // per_seq, 0, 0)
    else:
        kv_map = lambda i: (layer, i, 0, 0)
    return pl.pallas_call(
        functools.partial(_attn_kernel, nseq=nseq, rl=rl),
        grid=(t // rows,),
        in_specs=[pl.BlockSpec((rows, XA_WIDTH), lambda i: (i, cb)),
                  pl.BlockSpec((1, nseq, XA_WIDTH, N_MEM), kv_map),
                  pl.BlockSpec((1, nseq, XA_WIDTH, N_MEM), kv_map)],
        out_specs=pl.BlockSpec((rows, XA_WIDTH), lambda i: (i, 0)),
        out_shape=jax.ShapeDtypeStruct((t, XA_WIDTH), F32),
        compiler_params=_cparams("parallel"),
        name="mem_attention",
    )(z, mkt, mvt)


def _outproj_kernel(*refs, glu):
    if glu:
        mix_ref, att_ref, x_ref, wm_ref, wa_ref, g_ref, b_ref, wg_ref, bg_ref, o_ref = refs
        y = _gelu_tanh(mix_ref[...])
        mix = y * _sigmoid(_dot(y.astype(BF16), wg_ref[...]) + bg_ref[...])
    else:
        mix_ref, att_ref, x_ref, wm_ref, wa_ref, g_ref, b_ref, o_ref = refs
        mix = mix_ref[...]
    h = _dot(mix.astype(BF16), wm_ref[...]) + _dot(att_ref[...].astype(BF16), wa_ref[...])
    o_ref[...] = _layer_norm(DN_ALPHA * x_ref[...] + h, g_ref[...], b_ref[...])


def _out_proj_ln(mix, att, x, wm, wa, g, b, glu_w=None, glu_b=None, tm=512):
    t = x.shape[0]
    tm = min(tm, t)
    km = mix.shape[1]
    row = lambda w: pl.BlockSpec((tm, w), lambda i: (i, 0))
    full = lambda a: pl.BlockSpec(a.shape, lambda i: (0,) * a.ndim)
    args = [mix, att, x, wm, wa, g, b]
    specs = [row(km), row(XA_WIDTH), row(D_MODEL), full(wm), full(wa), full(g), full(b)]
    if glu_w is not None:
        args += [glu_w, glu_b]
        specs += [full(glu_w), full(glu_b)]
    return pl.pallas_call(
        functools.partial(_outproj_kernel, glu=glu_w is not None),
        grid=(t // tm,),
        in_specs=specs,
        out_specs=row(D_MODEL),
        out_shape=jax.ShapeDtypeStruct((t, D_MODEL), F32),
        compiler_params=_cparams("parallel"),
        name="out_proj_ln",
    )(*args)


def _router_gates(x, wr, br):
    lg = _dot_hi(x, wr) + br
    lane = lax.broadcasted_iota(jnp.int32, lg.shape, 1).astype(F32)
    ninf = jnp.float32(-jnp.inf)
    gm = (lane >= MOE_EXPERTS) & (lane < MOE_EXPERTS + MOE_GROUPS)
    gl = jnp.where(gm, lg, ninf)
    ge = jnp.exp(gl - jnp.max(gl, axis=-1, keepdims=True))
    gp = ge / jnp.sum(ge, axis=-1, keepdims=True)
    pmax = jnp.max(gp, axis=-1, keepdims=True)
    gi = jnp.min(jnp.where((gp == pmax) & gm, lane, 1e4), axis=-1, keepdims=True) - MOE_EXPERTS
    lo = gi * MOE_PER_GROUP
    em = (lane >= lo) & (lane < lo + MOE_PER_GROUP)
    el = jnp.where(em, lg, ninf)
    v1 = jnp.max(el, axis=-1, keepdims=True)
    i1 = jnp.min(jnp.where((el == v1) & em, lane, 1e4), axis=-1, keepdims=True)
    em2 = em & (lane != i1)
    el2 = jnp.where(em2, lg, ninf)
    v2 = jnp.max(el2, axis=-1, keepdims=True)
    i2 = jnp.min(jnp.where((el2 == v2) & em2, lane, 1e4), axis=-1, keepdims=True)
    t = jnp.exp(v2 - v1)
    w1 = 1.0 / (1.0 + t)
    w2 = t / (1.0 + t)
    return jnp.where(lane == i1, w1 * pmax, 0.0) + jnp.where(lane == i2, w2 * pmax, 0.0)


def _moe_kernel(x_ref, wr_ref, br_ref, wg_ref, wu_ref, wd_ref, g_ref, b_ref, o_ref, acc_ref, gates_ref, xb_ref):
    grp = pl.program_id(1)

    @pl.when(grp == 0)
    def _():
        x = x_ref[...]
        gates_ref[...] = _router_gates(x, wr_ref[...], br_ref[...])
        xb_ref[...] = x.astype(BF16)
        acc_ref[...] = jnp.zeros_like(acc_ref)

    gates = gates_ref[...]
    lane = lax.broadcasted_iota(jnp.int32, gates.shape, 1)
    xb = xb_ref[...]
    hs = []
    for j in range(MOE_PER_GROUP):
        ge = jnp.sum(jnp.where(lane == grp * MOE_PER_GROUP + j, gates, 0.0), axis=-1, keepdims=True)
        a = _dot(xb, wg_ref[0, j])
        u = _dot(xb, wu_ref[0, j])
        hs.append((_silu(a) * u * ge).astype(BF16))
    wd = wd_ref[0].reshape(MOE_PER_GROUP * MOE_HIDDEN, D_MODEL)
    acc_ref[...] += _dot(jnp.concatenate(hs, axis=-1), wd)

    @pl.when(grp == MOE_GROUPS - 1)
    def _():
        o_ref[...] = _layer_norm(DN_ALPHA * x_ref[...] + acc_ref[...], g_ref[...], b_ref[...])


def _moe_ln(x, wr, br, wg, wu, wd, layer, g, b, tm=512):
    t = x.shape[0]
    tm = min(tm, t)
    row = pl.BlockSpec((tm, D_MODEL), lambda i, e: (i, 0))
    full = lambda a: pl.BlockSpec(a.shape, lambda i, e: (0,) * a.ndim)
    grp_w = lambda a: pl.BlockSpec((1, MOE_PER_GROUP) + a.shape[2:], lambda i, e: (layer, e, 0, 0))
    return pl.pallas_call(
        _moe_kernel,
        grid=(t // tm, MOE_GROUPS),
        in_specs=[row, full(wr), full(br), grp_w(wg), grp_w(wu), grp_w(wd), full(g), full(b)],
        out_specs=row,
        out_shape=jax.ShapeDtypeStruct((t, D_MODEL), F32),
        scratch_shapes=[pltpu.VMEM((tm, D_MODEL), F32), pltpu.VMEM((tm, 128), F32),
                        pltpu.VMEM((tm, D_MODEL), BF16)],
        compiler_params=_cparams("parallel", "arbitrary"),
        name="moe_ln",
    )(x, wr, br, wg, wu, wd, g, b)


def _s5_kernel(*refs, tlen, ngrp, chained, precise):
    if chained:
        u_ref, wb_ref, wc_ref, lam_ref, d_ref, y_ref, ore_ref, oim_ref, bu_ref, h_ref, cr_ref, ci_ref = refs
        u = u_ref[...].reshape(8 * tlen, 128)
    else:
        (u_ref, h0r_ref, h0i_ref, wb_ref, wc_ref, lam_ref, d_ref,
         y_ref, ore_ref, oim_ref, bu_ref, h_ref, cr_ref, ci_ref) = refs
        u = u_ref[...]
    if precise:
        bu = _dot_hi(u, wb_ref[0])
    else:
        bu = _dot(u.astype(BF16), wb_ref[0].astype(BF16))
    ns = S5_SLAB_STATE
    nblk = 2 * ns // 128
    for c in range(nblk):
        bu_ref[c] = bu[:, c * 128:(c + 1) * 128]
    lr = jnp.broadcast_to(lam_ref[0, 0:1, :], (8, ns))
    li = jnp.broadcast_to(lam_ref[0, 1:2, :], (8, ns))

    def step(row, hr, hi):
        rows = pl.ds(row, 8, stride=tlen)
        xr = jnp.concatenate([bu_ref[c, rows, :] for c in range(nblk // 2)], axis=1)
        xi = jnp.concatenate([bu_ref[c, rows, :] for c in range(nblk // 2, nblk)], axis=1)
        nr = lr * hr - li * hi + xr
        ni = lr * hi + li * hr + xi
        for c in range(nblk // 2):
            h_ref[c, rows, :] = nr[:, c * 128:(c + 1) * 128]
            h_ref[nblk // 2 + c, rows, :] = ni[:, c * 128:(c + 1) * 128]
        return nr, ni

    if chained:
        @pl.when(pl.program_id(2) == 0)
        def _():
            cr_ref[...] = jnp.zeros_like(cr_ref)
            ci_ref[...] = jnp.zeros_like(ci_ref)

        hr, hi = lax.fori_loop(0, tlen, lambda t, c: step(t, *c), (cr_ref[...], ci_ref[...]), unroll=8)
        cr_ref[...] = hr
        ci_ref[...] = hi
        ore_ref[...] = hr
        oim_ref[...] = hi
    else:
        cr_ref[...] = jnp.transpose(h0r_ref[0])
        ci_ref[...] = jnp.transpose(h0i_ref[0])

        def group(g, _):
            s0 = pl.multiple_of(g * 8, 8)
            hr, hi = cr_ref[pl.ds(s0, 8), :], ci_ref[pl.ds(s0, 8), :]
            for t in range(tlen):
                hr, hi = step(g * (8 * tlen) + t, hr, hi)
            cr_ref[pl.ds(s0, 8), :] = hr
            ci_ref[pl.ds(s0, 8), :] = hi
            return 0

        lax.fori_loop(0, ngrp, group, 0)
        ore_ref[...] = jnp.transpose(cr_ref[...])
        oim_ref[...] = jnp.transpose(ci_ref[...])
    h = jnp.concatenate([h_ref[c].astype(BF16) for c in range(nblk)], axis=1)
    y = _dot(h, wc_ref[0].astype(BF16)) + d_ref[0] * u
    y_ref[...] = y.reshape(y_ref.shape)


def _s5_mixer_prompt(z, params, seq_len, tlen=128):
    t = z.shape[0]
    nseq = t // seq_len
    ns = S5_SLAB_STATE
    z3 = z.reshape(nseq, seq_len, z.shape[1])
    slab = lambda a: pl.BlockSpec((1,) + a.shape[1:], lambda j, b, i: (j,) + (0,) * (a.ndim - 1))
    u_spec = pl.BlockSpec((8, tlen, 128), lambda j, b, i: (b, i, j))
    st_spec = pl.BlockSpec((8, ns), lambda j, b, i: (b, j))
    st_shape = jax.ShapeDtypeStruct((nseq, S5_SLABS * ns), F32)
    rows = 8 * tlen
    y, hre, him = pl.pallas_call(
        functools.partial(_s5_kernel, tlen=tlen, ngrp=1, chained=True, precise=False),
        grid=(S5_SLABS, nseq // 8, seq_len // tlen),
        in_specs=[u_spec] + [slab(a) for a in params],
        out_specs=[u_spec, st_spec, st_spec],
        out_shape=[jax.ShapeDtypeStruct((nseq, seq_len, MIX_WIDTH), F32), st_shape, st_shape],
        scratch_shapes=[pltpu.VMEM((2 * ns // 128, rows, 128), F32), pltpu.VMEM((2 * ns // 128, rows, 128), F32),
                        pltpu.VMEM((8, ns), F32), pltpu.VMEM((8, ns), F32)],
        compiler_params=_cparams("parallel", "parallel", "arbitrary"),
        name="s5_mixer",
    )(z3, *params)
    return y.reshape(t, MIX_WIDTH), hre, him


def _s5_mixer_sample(z, params, seq_len, h0r, h0i, layer):
    t = z.shape[0]
    nseq = t // seq_len
    ns = S5_SLAB_STATE
    seqs = 128
    rows = seqs * seq_len
    slab = lambda a: pl.BlockSpec((1,) + a.shape[1:], lambda j, i: (j,) + (0,) * (a.ndim - 1))
    u_spec = pl.BlockSpec((rows, 128), lambda j, i: (i, j))
    h0_spec = pl.BlockSpec((1, ns, seqs), lambda j, i: (layer, j, i))
    st_spec = pl.BlockSpec((ns, seqs), lambda j, i: (j, i))
    st_shape = jax.ShapeDtypeStruct((S5_SLABS * ns, nseq), F32)
    return pl.pallas_call(
        functools.partial(_s5_kernel, tlen=seq_len, ngrp=seqs // 8, chained=False, precise=True),
        grid=(S5_SLABS, nseq // seqs),
        in_specs=[u_spec, h0_spec, h0_spec] + [slab(a) for a in params],
        out_specs=[u_spec, st_spec, st_spec],
        out_shape=[jax.ShapeDtypeStruct((t, MIX_WIDTH), F32), st_shape, st_shape],
        scratch_shapes=[pltpu.VMEM((2 * ns // 128, rows, 128), F32), pltpu.VMEM((2 * ns // 128, rows, 128), F32),
                        pltpu.VMEM((seqs, ns), F32), pltpu.VMEM((seqs, ns), F32)],
        compiler_params=_cparams("parallel", "parallel"),
        name="s5_mixer",
    )(z, h0r, h0i, *params)


def _s5_params(lam_re, lam_im, log_dt, b_re, b_im, c_re, c_im, d_skip):
    lre, lim = lam_re.astype(F32), lam_im.astype(F32)
    dt = jnp.exp(log_dt.astype(F32))[:, None]
    mag = jnp.exp(lre * dt)
    bar_re, bar_im = mag * jnp.cos(lim * dt), mag * jnp.sin(lim * dt)
    nre, nim = bar_re - 1.0, bar_im
    den = lre * lre + lim * lim
    fre, fim = (nre * lre + nim * lim) / den, (nim * lre - nre * lim) / den
    bre, bim = b_re.astype(F32), b_im.astype(F32)
    bb_re = fre[..., None] * bre - fim[..., None] * bim
    bb_im = fre[..., None] * bim + fim[..., None] * bre
    eye = jnp.eye(8, dtype=F32)
    gps = 8

    def blockdiag_in(m):
        m = m.reshape(S5_SLABS, gps, S5_STATE, S5_GROUP).transpose(0, 1, 3, 2)
        return jnp.einsum('jgcp,gh->jgchp', m, eye).reshape(S5_SLABS, 128, S5_SLAB_STATE)

    def blockdiag_out(m):
        m = m.reshape(S5_SLABS, gps, S5_GROUP, S5_STATE).transpose(0, 1, 3, 2)
        return jnp.einsum('jgpc,gh->jgphc', m, eye).reshape(S5_SLABS, S5_SLAB_STATE, 128)

    wb = jnp.concatenate([blockdiag_in(bb_re), blockdiag_in(bb_im)], axis=2)
    wc = jnp.concatenate([blockdiag_out(c_re.astype(F32)), blockdiag_out(-c_im.astype(F32))], axis=1)
    lam = jnp.stack([bar_re.reshape(S5_SLABS, S5_SLAB_STATE), bar_im.reshape(S5_SLABS, S5_SLAB_STATE)], axis=1)
    dsk = d_skip.astype(F32).reshape(S5_SLABS, 1, 128)
    return wb, wc, lam, dsk


def _head_norm_gate(o, gn, gate):
    col = lax.broadcasted_iota(jnp.int32, o.shape, 1)
    valid = col < RET_DV
    mu = jnp.sum(o, axis=-1, keepdims=True) * (1.0 / RET_DV)
    d = jnp.where(valid, o - mu, 0.0)
    var = jnp.sum(d * d, axis=-1, keepdims=True) * (1.0 / RET_DV)
    return d * lax.rsqrt(var + NORM_EPS) * gn * _silu(gate)


def _load_states(s_in_ref, spad_ref, nseg, dk):
    spad_ref[...] = jnp.zeros_like(spad_ref)
    for s in range(nseg):
        for h in range(HEADS):
            spad_ref[h, s * DK_PAD:s * DK_PAD + dk, 0:RET_DV] = s_in_ref[s, h]


def _ret_kernel(*refs, nseg, gammas):
    if nseg == 1:
        (q_ref, k_ref, v_ref, g_ref, cos_ref, sin_ref, dmat_ref, inner_ref, zeta_ref, gn_ref,
         o_ref, so_ref, st_ref) = refs

        @pl.when(pl.program_id(1) == 0)
        def _():
            st_ref[...] = jnp.zeros_like(st_ref)
    else:
        (q_ref, k_ref, v_ref, g_ref, cos_ref, sin_ref, dmat_ref, inner_ref, zeta_ref, gn_ref, s_in_ref,
         o_ref, so_ref, st_ref) = refs
        st_ref[:, RET_DV:DV_PAD, :] = jnp.zeros((HEADS, DV_PAD - RET_DV, nseg * RET_DK), F32)
        for s in range(nseg):
            for h in range(HEADS):
                st_ref[h, 0:RET_DV, s * RET_DK:(s + 1) * RET_DK] = s_in_ref[s, h]
    r = q_ref.shape[0]
    seg_len = r // nseg
    cos = cos_ref[...]
    sin = sin_ref[...]
    rot = lambda x: x * cos + pltpu.roll(x, RET_DK // 2, axis=1) * sin
    rowseg = lax.broadcasted_iota(jnp.int32, (r, DK_PAD), 0) // seg_len
    colseg = lax.broadcasted_iota(jnp.int32, (DV_PAD, r), 1) // seg_len
    for h in range(HEADS):
        q = rot(q_ref[:, h * DK_PAD:(h + 1) * DK_PAD])
        k = rot(k_ref[:, h * DK_PAD:(h + 1) * DK_PAD]) * (RET_DK ** -0.5)
        v = v_ref[:, h * DV_PAD:(h + 1) * DV_PAD]
        vb = v.astype(BF16)
        qb = q.astype(BF16)
        att = _dot_nt(qb, k.astype(BF16)) * dmat_ref[h]
        o = _dot(att.astype(BF16), vb)
        st_all = st_ref[h]
        if nseg == 1:
            qbd = qb
        else:
            qbd = jnp.concatenate([jnp.where(rowseg == s, q, 0.0) for s in range(nseg)], axis=1).astype(BF16)
        o = o + _dot_nt(qbd, st_all.astype(BF16)) * inner_ref[h]
        kzb = (k * zeta_ref[h]).astype(BF16)
        vt = jnp.transpose(v)
        for s in range(nseg):
            vts = vt if nseg == 1 else jnp.where(colseg == s, vt, 0.0)
            st_new = st_all[:, s * RET_DK:(s + 1) * RET_DK] * gammas[h] + _dot(vts.astype(BF16), kzb)
            if nseg == 1:
                st_ref[h] = st_new
                so_ref[0, h] = st_new[0:RET_DV, :]
            else:
                so_ref[s, h] = st_new[0:RET_DV, :]
        o_ref[:, h * DV_PAD:(h + 1) * DV_PAD] = _head_norm_gate(
            o, gn_ref[:, h * DV_PAD:(h + 1) * DV_PAD], g_ref[:, h * DV_PAD:(h + 1) * DV_PAD])


def _ret_consts(seg_len, nseg, pos):
    hh = jnp.arange(RET_HEADS, dtype=F32)
    log_g = jnp.log1p(-jnp.exp2(-5.0 - hh))
    idx = jnp.arange(seg_len, dtype=F32)
    rel = idx[:, None] - idx[None, :]
    dmat = jnp.where(rel >= 0, jnp.exp(log_g[:, None, None] * jnp.maximum(rel, 0.0)), 0.0)
    inner = jnp.exp(log_g[:, None] * (idx + 1.0))
    zeta = jnp.exp(log_g[:, None] * (seg_len - 1.0 - idx))
    r = seg_len * nseg
    seg = jnp.arange(r) // seg_len
    same = (seg[:, None] == seg[None, :]).astype(F32)
    dmat = jnp.tile(dmat, (1, nseg, nseg)) * same[None]
    inner = jnp.broadcast_to(jnp.tile(inner, (1, nseg))[:, :, None], (RET_HEADS, r, DV_PAD))
    zeta = jnp.broadcast_to(jnp.tile(zeta, (1, nseg))[:, :, None], (RET_HEADS, r, DK_PAD))
    half = RET_DK // 2
    inv_freq = ROPE_BASE ** (-jnp.arange(half, dtype=F32) / half)
    ang = pos.astype(F32)[:, None] * inv_freq
    cos2 = jnp.concatenate([jnp.cos(ang), jnp.cos(ang)], axis=-1)
    sin2 = jnp.concatenate([-jnp.sin(ang), jnp.sin(ang)], axis=-1)
    return dmat, inner, zeta, cos2, sin2


def _ret_gammas(seg_len):
    return tuple(float((1.0 - 2.0 ** (-5.0 - h)) ** seg_len) for h in range(RET_HEADS))


def _ret_mixer(z, gn, seq_len, pos, s0=None):
    t = z.shape[0]
    nseq = t // seq_len
    if s0 is None:
        r, nseg, seg_len = RET_CHUNK, 1, RET_CHUNK
        steps = seq_len // r
        grid = (nseq, steps)
        rowmap = lambda c: (lambda n, i: (n * steps + i, c))
        tabmap = lambda n, i: (i, 0)
        stmap = lambda n, i: (n, 0, 0, 0)
        st_rows = 1
        sem = ("parallel", "arbitrary")
    else:
        seg_len = seq_len
        nseg = 128 // seg_len
        r = 128
        grid = (t // r,)
        rowmap = lambda c: (lambda i: (i, c))
        tabmap = lambda i: (0, 0)
        stmap = lambda i: (i, 0, 0, 0)
        st_rows = nseg
        sem = ("arbitrary",)
    dmat, inner, zeta, cos2, sin2 = _ret_consts(seg_len, nseg, pos)
    if s0 is not None:
        cos2 = jnp.tile(cos2, (nseg, 1))
        sin2 = jnp.tile(sin2, (nseg, 1))
    full = lambda a: pl.BlockSpec(a.shape, lambda *_: (0,) * a.ndim)
    st_spec = pl.BlockSpec((st_rows, RET_HEADS, RET_DV, RET_DK), stmap)
    in_specs = [pl.BlockSpec((r, 512), rowmap(0)), pl.BlockSpec((r, 512), rowmap(1)),
                pl.BlockSpec((r, 1024), rowmap(1)), pl.BlockSpec((r, 1024), rowmap(2)),
                pl.BlockSpec((r, DK_PAD), tabmap), pl.BlockSpec((r, DK_PAD), tabmap),
                full(dmat), full(inner), full(zeta), full(gn)]
    args = [z, z, z, z, cos2, sin2, dmat, inner, zeta, gn]
    if s0 is not None:
        in_specs.append(st_spec)
        args.append(s0)
    mix, s_new = pl.pallas_call(
        functools.partial(_ret_kernel, nseg=nseg, gammas=_ret_gammas(seg_len)),
        grid=grid,
        in_specs=in_specs,
        out_specs=[pl.BlockSpec((r, HEADS * DV_PAD), rowmap(0)), st_spec],
        out_shape=[jax.ShapeDtypeStruct((t, HEADS * DV_PAD), F32),
                   jax.ShapeDtypeStruct((nseq, RET_HEADS, RET_DV, RET_DK), F32)],
        scratch_shapes=[pltpu.VMEM((HEADS, DV_PAD, nseg * RET_DK), F32)],
        compiler_params=_cparams(*sem),
        name="retention",
    )(*args)
    return mix, s_new


def _log_sigmoid(x):
    return jnp.minimum(x, 0.0) - jnp.log1p(jnp.exp(-jnp.abs(x)))


def _gla_kernel(*refs, nseg):
    if nseg == 1:
        (q_ref, k_ref, v_ref, r_ref, low_ref, tril_ref, same_ref, wg_ref, bg_ref, gn_ref,
         o_ref, so_ref, spad_ref) = refs

        @pl.when(pl.program_id(1) == 0)
        def _():
            spad_ref[...] = jnp.zeros_like(spad_ref)
    else:
        (q_ref, k_ref, v_ref, r_ref, low_ref, tril_ref, same_ref, wg_ref, bg_ref, gn_ref, s_in_ref,
         o_ref, so_ref, spad_ref) = refs
        _load_states(s_in_ref, spad_ref, nseg, GLA_DK)
    r = q_ref.shape[0]
    seg_len = r // nseg
    rp = max(r, 128)
    tril = tril_ref[...]
    causal = tril > 0.0
    la = _log_sigmoid(_dot_hi(low_ref[...], wg_ref[...]) + bg_ref[...]) * (1.0 / GLA_TAU)
    b = _dot_hi(tril, la)
    b_tot = _dot_hi(same_ref[...], la)
    rowseg = lax.broadcasted_iota(jnp.int32, (r, DK_PAD), 0) // seg_len
    colseg = lax.broadcasted_iota(jnp.int32, (DK_PAD, rp), 1) // seg_len

    def pad_rows(x):
        if rp == r:
            return x
        return jnp.concatenate([x, jnp.zeros((rp - r, x.shape[1]), x.dtype)], axis=0)

    for h in range(HEADS):
        sl = slice(h * DK_PAD, (h + 1) * DK_PAD)
        bh = b[:, sl]
        lah = la[:, sl]
        q = q_ref[:, sl] * (GLA_DK ** -0.5)
        k = k_ref[:, sl]
        vb = v_ref[:, h * DV_PAD:(h + 1) * DV_PAD].astype(BF16)
        q_t = q * jnp.exp(bh)
        k_t = k * jnp.exp(-bh)
        att = jnp.where(causal, _dot_nt(q_t.astype(BF16), k_t.astype(BF16)), 0.0)
        o = _dot(att.astype(BF16), vb)
        s_all = spad_ref[h]
        if nseg == 1:
            qbd = q_t.astype(BF16)
        else:
            qbd = jnp.concatenate([jnp.where(rowseg == s, q_t, 0.0) for s in range(nseg)], axis=1).astype(BF16)
        o = o + _dot(qbd, s_all.astype(BF16))
        k_dec = k * jnp.exp(b_tot[:, sl] - bh)
        kdt = jnp.transpose(pad_rows(k_dec))
        lat = jnp.transpose(pad_rows(lah))
        vbp = pad_rows(vb)
        for s in range(nseg):
            if nseg == 1:
                kds, las = kdt, lat
            else:
                kds = jnp.where(colseg == s, kdt, 0.0)
                las = jnp.where(colseg == s, lat, 0.0)
            decay = jnp.exp(jnp.sum(las, axis=-1, keepdims=True))
            s_new = s_all[s * DK_PAD:(s + 1) * DK_PAD] * decay + _dot(kds.astype(BF16), vbp)
            if nseg == 1:
                spad_ref[h] = s_new
                so_ref[0, h] = s_new[0:GLA_DK, 0:GLA_DV]
            else:
                so_ref[s, h] = s_new[0:GLA_DK, 0:GLA_DV]
        o_ref[:, h * DV_PAD:(h + 1) * DV_PAD] = _head_norm_gate(
            o, gn_ref[:, h * DV_PAD:(h + 1) * DV_PAD], r_ref[:, h * DV_PAD:(h + 1) * DV_PAD])


def _gla_mixer(z, wg, bg, gn, seq_len, s0=None):
    t = z.shape[0]
    nseq = t // seq_len
    if s0 is None:
        r, nseg, seg_len = GLA_CHUNK, 1, GLA_CHUNK
        steps = seq_len // r
        grid = (nseq, steps)
        rowmap = lambda c: (lambda n, i: (n * steps + i, c))
        stmap = lambda n, i: (n, 0, 0, 0)
        st_rows = 1
        sem = ("parallel", "arbitrary")
    else:
        seg_len = seq_len
        nseg = 128 // seg_len
        r = 128
        grid = (t // r,)
        rowmap = lambda c: (lambda i: (i, c))
        stmap = lambda i: (i, 0, 0, 0)
        st_rows = nseg
        sem = ("arbitrary",)
    seg = jnp.arange(r) // seg_len
    idx = jnp.arange(r)
    same = (seg[:, None] == seg[None, :]).astype(F32)
    tril = same * (idx[:, None] >= idx[None, :]).astype(F32)
    full = lambda a: pl.BlockSpec(a.shape, lambda *_: (0,) * a.ndim)
    st_spec = pl.BlockSpec((st_rows, GLA_HEADS, GLA_DK, GLA_DV), stmap)
    in_specs = [pl.BlockSpec((r, 512), rowmap(0)), pl.BlockSpec((r, 512), rowmap(1)),
                pl.BlockSpec((r, 1024), rowmap(1)), pl.BlockSpec((r, 1024), rowmap(2)),
                pl.BlockSpec((r, 128), rowmap(COL_LOW // 128)),
                full(tril), full(same), full(wg), full(bg), full(gn)]
    args = [z, z, z, z, z, tril, same, wg, bg, gn]
    if s0 is not None:
        in_specs.append(st_spec)
        args.append(s0)
    mix, s_new = pl.pallas_call(
        functools.partial(_gla_kernel, nseg=nseg),
        grid=grid,
        in_specs=in_specs,
        out_specs=[pl.BlockSpec((r, HEADS * DV_PAD), rowmap(0)), st_spec],
        out_shape=[jax.ShapeDtypeStruct((t, HEADS * DV_PAD), F32),
                   jax.ShapeDtypeStruct((nseq, GLA_HEADS, GLA_DK, GLA_DV), F32)],
        scratch_shapes=[pltpu.VMEM((HEADS, nseg * DK_PAD, DV_PAD), F32)],
        compiler_params=_cparams(*sem),
        name="gla",
    )(*args)
    return mix, s_new


def _pad_heads(w, width, pad):
    lead = w.shape[:-1]
    w = w.reshape(lead + (HEADS, width))
    w = jnp.pad(w, [(0, 0)] * len(lead) + [(0, 0), (0, pad - width)])
    return w.reshape(lead + (HEADS * pad,))


def _pad_head_rows(w, width, pad):
    return jnp.swapaxes(_pad_heads(jnp.swapaxes(w, 0, 1), width, pad), 0, 1)


def _ret_in_weight(w):
    qk = RET_HEADS * RET_DK
    v = _pad_heads(w[:, 2 * qk:2 * qk + MIX_WIDTH], RET_DV, DV_PAD)
    g = _pad_heads(w[:, 2 * qk + MIX_WIDTH:2 * qk + 2 * MIX_WIDTH], RET_DV, DV_PAD)
    return jnp.concatenate([w[:, :2 * qk], v, g, w[:, -XA_WIDTH:]], axis=1).astype(BF16)


def _gla_in_weight(w):
    qk = GLA_HEADS * GLA_DK
    q = _pad_heads(w[:, :qk], GLA_DK, DK_PAD)
    k = _pad_heads(w[:, qk:2 * qk], GLA_DK, DK_PAD)
    v = _pad_heads(w[:, 2 * qk:2 * qk + MIX_WIDTH], GLA_DV, DV_PAD)
    r = _pad_heads(w[:, 2 * qk + MIX_WIDTH:2 * qk + 2 * MIX_WIDTH], GLA_DV, DV_PAD)
    low = jnp.pad(w[:, 2 * qk + 2 * MIX_WIDTH:2 * qk + 2 * MIX_WIDTH + GLA_RANK], [(0, 0), (0, 128 - GLA_RANK)])
    return jnp.concatenate([q, k, v, r, w[:, -XA_WIDTH:], low], axis=1).astype(BF16)


def kernel(x_prompt, x_sample, mem_prompt, state_s5_re, state_s5_im, state_ret, state_gla,
           cache_mem_k, cache_mem_v,
           w_in_s5, s5_lam_re, s5_lam_im, s5_log_dt, s5_b_re, s5_b_im, s5_c_re, s5_c_im,
           s5_d, s5_w_glu, s5_b_glu,
           w_in_ret, ret_gn,
           w_in_gla, gla_w_gate2, gla_b_gate2, gla_gn,
           w_mem_k, w_mem_v, w_out, ln_g, ln_b,
           moe_w_grp, moe_b_grp, moe_w_exp, moe_b_exp, moe_w_gate, moe_w_up, moe_w_down):
    bp, lp, _ = x_prompt.shape
    bs, ls, _ = x_sample.shape
    xp = x_prompt.reshape(bp * lp, D_MODEL)
    xs = x_sample.reshape(bs * ls, D_MODEL)
    pos_p = jnp.arange(lp, dtype=jnp.int32)
    pos_s = PAST_LEN + jnp.arange(ls, dtype=jnp.int32)

    to_t = lambda c: jnp.transpose(c, (0, 1, 3, 4, 2)).reshape(DEPTH, c.shape[1], XA_WIDTH, N_MEM)
    from_t = lambda c: jnp.transpose(c.reshape(DEPTH, c.shape[1], XA_HEADS, XA_HEAD_DIM, N_MEM), (0, 1, 4, 2, 3))
    mkt_all, mvt_all = _mem_kv(mem_prompt, jnp.swapaxes(w_mem_k, 1, 2).astype(BF16),
                               jnp.swapaxes(w_mem_v, 1, 2).astype(BF16))
    cache_kt = to_t(cache_mem_k)
    cache_vt = to_t(cache_mem_v)
    ret_t = jnp.swapaxes(state_ret, 3, 4)
    s5_t = lambda s: jnp.transpose(s, (0, 2, 3, 1)).reshape(s.shape[0], S5_GROUPS * S5_STATE, bs)
    s5_re_t, s5_im_t = s5_t(state_s5_re), s5_t(state_s5_im)
    moe_wg, moe_wu, moe_wd = moe_w_gate.astype(BF16), moe_w_up.astype(BF16), moe_w_down.astype(BF16)

    p_re, p_im, p_ret, p_gla = [], [], [], []
    s_re, s_im, s_ret, s_gla = [], [], [], []
    for i in range(DEPTH):
        kind, j = i % N_MIXERS, i // N_MIXERS
        ln1 = (ln_g[i, 0].reshape(1, D_MODEL), ln_b[i, 0].reshape(1, D_MODEL))
        ln2 = (ln_g[i, 1].reshape(1, D_MODEL), ln_b[i, 1].reshape(1, D_MODEL))
        wo = w_out[i]
        wo_att = wo[MIX_WIDTH:].astype(BF16)
        glu = {}
        if kind == 0:
            params = _s5_params(s5_lam_re[j], s5_lam_im[j], s5_log_dt[j], s5_b_re[j], s5_b_im[j],
                                s5_c_re[j], s5_c_im[j], s5_d[j])
            zp = _matmul(xp, w_in_s5[j].astype(BF16), 512)
            zs = _matmul(xs, w_in_s5[j], 512, precise=True)
            mix_p, a_re, a_im = _s5_mixer_prompt(zp, params, lp)
            mix_s, b_re, b_im = _s5_mixer_sample(zs, params, ls, s5_re_t, s5_im_t, j)
            p_re.append(a_re), p_im.append(a_im), s_re.append(b_re), s_im.append(b_im)
            xa_col = MIX_WIDTH
            wo_mix = wo[:MIX_WIDTH].astype(BF16)
            glu = dict(glu_w=s5_w_glu[j].astype(BF16), glu_b=s5_b_glu[j].reshape(1, MIX_WIDTH))
        elif kind == 1:
            w = _ret_in_weight(w_in_ret[j])
            gn = _pad_heads(ret_gn[j], RET_DV, DV_PAD).reshape(1, HEADS * DV_PAD)
            zp = _matmul(xp, w, 512)
            zs = _matmul(xs, w, 512)
            mix_p, a_s = _ret_mixer(zp, gn, lp, pos_p)
            mix_s, b_s = _ret_mixer(zs, gn, ls, pos_s, s0=ret_t[j])
            p_ret.append(a_s), s_ret.append(b_s)
            xa_col = COL_XA
            wo_mix = _pad_head_rows(wo[:MIX_WIDTH], RET_DV, DV_PAD).astype(BF16)
        else:
            w = _gla_in_weight(w_in_gla[j])
            gn = _pad_heads(gla_gn[j], GLA_DV, DV_PAD).reshape(1, HEADS * DV_PAD)
            wg = jnp.pad(_pad_heads(gla_w_gate2[j], GLA_DK, DK_PAD), [(0, 128 - GLA_RANK), (0, 0)])
            bg = _pad_heads(gla_b_gate2[j], GLA_DK, DK_PAD).reshape(1, HEADS * DK_PAD)
            zp = _matmul(xp, w, 512)
            zs = _matmul(xs, w, 512)
            mix_p, a_s = _gla_mixer(zp, wg, bg, gn, lp)
            mix_s, b_s = _gla_mixer(zs, wg, bg, gn, ls, s0=state_gla[j])
            p_gla.append(a_s), s_gla.append(b_s)
            xa_col = COL_XA
            wo_mix = _pad_head_rows(wo[:MIX_WIDTH], GLA_DV, DV_PAD).astype(BF16)
        att_p = _mem_attention(zp, xa_col, mkt_all, mvt_all, i, lp, rl=min(512, lp), nseq=1)
        att_s = _mem_attention(zs, xa_col, cache_kt, cache_vt, i, ls, rl=ls, nseq=128 // ls)
        xp = _out_proj_ln(mix_p, att_p, xp, wo_mix, wo_att, *ln1, **glu)
        xs = _out_proj_ln(mix_s, att_s, xs, wo_mix, wo_att, *ln1, **glu)

        wr = jnp.pad(jnp.concatenate([moe_w_exp[i], moe_w_grp[i]], axis=1),
                     [(0, 0), (0, 128 - MOE_EXPERTS - MOE_GROUPS)]).astype(F32)
        br = jnp.pad(jnp.concatenate([moe_b_exp[i], moe_b_grp[i]]),
                     [(0, 128 - MOE_EXPERTS - MOE_GROUPS)]).reshape(1, 128).astype(F32)
        xp = _moe_ln(xp, wr, br, moe_wg, moe_wu, moe_wd, i, *ln2)
        xs = _moe_ln(xs, wr, br, moe_wg, moe_wu, moe_wd, i, *ln2)

    st = lambda xs_: jnp.stack(xs_)
    s5_p = lambda xs_: st(xs_).reshape(len(xs_), bp, S5_GROUPS, S5_STATE)
    s5_s = lambda xs_: jnp.transpose(st(xs_).reshape(len(xs_), S5_GROUPS, S5_STATE, bs), (0, 3, 1, 2))
    ret_out = lambda xs_: jnp.swapaxes(st(xs_), 3, 4)
    return (xp.reshape(bp, lp, D_MODEL), xs.reshape(bs, ls, D_MODEL),
            s5_p(p_re), s5_p(p_im), ret_out(p_ret), st(p_gla),
            from_t(mkt_all), from_t(mvt_all),
            s5_s(s_re), s5_s(s_im), ret_out(s_ret), st(s_gla))
```

```python
import functools
import math

import jax
import jax.numpy as jnp
from jax import lax
from jax.experimental import pallas as pl
from jax.experimental.pallas import tpu as pltpu

F32 = jnp.float32
BF16 = jnp.bfloat16
HI = lax.Precision.HIGHEST

D_MODEL = 1024
DEPTH = 4
PAST_LEN = 16384
N_MIXERS = 3
MIX_WIDTH = 768
XA_HEADS = 4
XA_HEAD_DIM = 64
XA_WIDTH = 256
N_MEM = 256
S5_GROUP = 16
S5_GROUPS = 48
S5_STATE = 64
S5_SLABS = 6
S5_SLAB_STATE = 512
RET_HEADS = 4
RET_DK = 128
RET_DV = 192
RET_CHUNK = 128
ROPE_BASE = 10000.0
GLA_HEADS = 4
GLA_DK = 96
GLA_DV = 192
GLA_RANK = 16
GLA_TAU = 16.0
GLA_CHUNK = 64
MOE_GROUPS = 4
MOE_PER_GROUP = 4
MOE_EXPERTS = 16
MOE_HIDDEN = 256
DN_ALPHA = (2 * DEPTH) ** 0.25
NORM_EPS = 1e-5

DK_PAD = 128
DV_PAD = 256
HEADS = 4
COL_Q, COL_K, COL_V, COL_G, COL_XA, COL_LOW = 0, 512, 1024, 2048, 3072, 3328
RET_IN_PAD = 3328
GLA_IN_PAD = 3456
VMEM_LIMIT = 52 * 1024 * 1024


def _cparams(*sem):
    return pltpu.CompilerParams(dimension_semantics=sem, vmem_limit_bytes=VMEM_LIMIT)


def _dot(a, b):
    return jnp.dot(a, b, preferred_element_type=F32)


def _dot_nt(a, b):
    return lax.dot_general(a, b, (((1,), (1,)), ((), ())), preferred_element_type=F32)


def _dot_hi(a, b):
    return jnp.dot(a, b, preferred_element_type=F32, precision=HI)


def _layer_norm(y, g, b):
    mu = jnp.mean(y, axis=-1, keepdims=True)
    d = y - mu
    var = jnp.mean(d * d, axis=-1, keepdims=True)
    return d * lax.rsqrt(var + NORM_EPS) * g + b


def _sigmoid(x):
    return 1.0 / (1.0 + jnp.exp(-x))


def _silu(x):
    return x * _sigmoid(x)


def _gelu_tanh(x):
    c = math.sqrt(2.0 / math.pi)
    return 0.5 * x * (1.0 + jnp.tanh(c * (x + 0.044715 * (x * x * x))))


def _matmul_kernel(x_ref, w_ref, o_ref, *, precise):
    if precise:
        o_ref[...] = _dot_hi(x_ref[...], w_ref[...])
    else:
        o_ref[...] = _dot(x_ref[...].astype(BF16), w_ref[...])


def _matmul(x, w, tm, precise=False):
    t, k = x.shape
    tm = min(tm, t)
    n = w.shape[1]
    return pl.pallas_call(
        functools.partial(_matmul_kernel, precise=precise),
        grid=(t // tm,),
        in_specs=[pl.BlockSpec((tm, k), lambda i: (i, 0)),
                  pl.BlockSpec((k, n), lambda i: (0, 0))],
        out_specs=pl.BlockSpec((tm, n), lambda i: (i, 0)),
        out_shape=jax.ShapeDtypeStruct((t, n), F32),
        compiler_params=_cparams("parallel"),
        name="in_proj",
    )(x, w)


def _memkv_kernel(x_ref, wk_ref, wv_ref, ok_ref, ov_ref):
    xb = x_ref[0].astype(BF16)
    ok_ref[0, 0] = _dot_nt(wk_ref[0], xb)
    ov_ref[0, 0] = _dot_nt(wv_ref[0], xb)


def _mem_kv(mem, wkt, wvt):
    n = mem.shape[0]
    spec_w = pl.BlockSpec((1, XA_WIDTH, D_MODEL), lambda i, b: (i, 0, 0))
    spec_o = pl.BlockSpec((1, 1, XA_WIDTH, N_MEM), lambda i, b: (i, b, 0, 0))
    return pl.pallas_call(
        _memkv_kernel,
        grid=(DEPTH, n),
        in_specs=[pl.BlockSpec((1, N_MEM, D_MODEL), lambda i, b: (b, 0, 0)), spec_w, spec_w],
        out_specs=[spec_o, spec_o],
        out_shape=[jax.ShapeDtypeStruct((DEPTH, n, XA_WIDTH, N_MEM), F32)] * 2,
        compiler_params=_cparams("parallel", "parallel"),
        name="mem_kv",
    )(mem, wkt, wvt)


def _attn_kernel(q_ref, kt_ref, vt_ref, o_ref, *, nseq, rl):
    head = lax.broadcasted_iota(jnp.int32, (rl, XA_WIDTH), 1) // XA_HEAD_DIM
    for s in range(nseq):
        q = q_ref[s * rl:(s + 1) * rl, :] * (XA_HEAD_DIM ** -0.5)
        qs = jnp.concatenate([jnp.where(head == h, q, 0.0) for h in range(XA_HEADS)], axis=0)
        sc = _dot(qs.astype(BF16), kt_ref[0, s].astype(BF16))
        sc = sc - jnp.max(sc, axis=-1, keepdims=True)
        p = jnp.exp(sc)
        p = p / jnp.sum(p, axis=-1, keepdims=True)
        pv = _dot_nt(p.astype(BF16), vt_ref[0, s].astype(BF16))
        o = jnp.where(head == 0, pv[0:rl], 0.0)
        for h in range(1, XA_HEADS):
            o = o + jnp.where(head == h, pv[h * rl:(h + 1) * rl], 0.0)
        o_ref[s * rl:(s + 1) * rl, :] = o


def _mem_attention(z, xa_col, mkt, mvt, layer, seq_len, rl, nseq):
    t = z.shape[0]
    rows = rl * nseq
    cb = xa_col // XA_WIDTH
    if nseq == 1:
        per_seq = seq_len // rl
        kv_map = lambda i: (layer, i // per_seq, 0, 0)
    else:
        kv_map = lambda i: (layer, i, 0, 0)
    return pl.pallas_call(
        functools.partial(_attn_kernel, nseq=nseq, rl=rl),
        grid=(t // rows,),
        in_specs=[pl.BlockSpec((rows, XA_WIDTH), lambda i: (i, cb)),
                  pl.BlockSpec((1, nseq, XA_WIDTH, N_MEM), kv_map),
                  pl.BlockSpec((1, nseq, XA_WIDTH, N_MEM), kv_map)],
        out_specs=pl.BlockSpec((rows, XA_WIDTH), lambda i: (i, 0)),
        out_shape=jax.ShapeDtypeStruct((t, XA_WIDTH), F32),
        compiler_params=_cparams("parallel"),
        name="mem_attention",
    )(z, mkt, mvt)


def _outproj_kernel(*refs, glu):
    if glu:
        mix_ref, att_ref, x_ref, wm_ref, wa_ref, g_ref, b_ref, wg_ref, bg_ref, o_ref = refs
        y = _gelu_tanh(mix_ref[...])
        mix = y * _sigmoid(_dot(y.astype(BF16), wg_ref[...]) + bg_ref[...])
    else:
        mix_ref, att_ref, x_ref, wm_ref, wa_ref, g_ref, b_ref, o_ref = refs
        mix = mix_ref[...]
    h = _dot(mix.astype(BF16), wm_ref[...]) + _dot(att_ref[...].astype(BF16), wa_ref[...])
    o_ref[...] = _layer_norm(DN_ALPHA * x_ref[...] + h, g_ref[...], b_ref[...])


def _out_proj_ln(mix, att, x, wm, wa, g, b, glu_w=None, glu_b=None, tm=512):
    t = x.shape[0]
    tm = min(tm, t)
    km = mix.shape[1]
    row = lambda w: pl.BlockSpec((tm, w), lambda i: (i, 0))
    full = lambda a: pl.BlockSpec(a.shape, lambda i: (0,) * a.ndim)
    args = [mix, att, x, wm, wa, g, b]
    specs = [row(km), row(XA_WIDTH), row(D_MODEL), full(wm), full(wa), full(g), full(b)]
    if glu_w is not None:
        args += [glu_w, glu_b]
        specs += [full(glu_w), full(glu_b)]
    return pl.pallas_call(
        functools.partial(_outproj_kernel, glu=glu_w is not None),
        grid=(t // tm,),
        in_specs=specs,
        out_specs=row(D_MODEL),
        out_shape=jax.ShapeDtypeStruct((t, D_MODEL), F32),
        compiler_params=_cparams("parallel"),
        name="out_proj_ln",
    )(*args)


def _router_gates(x, wr, br):
    lg = _dot_hi(x, wr) + br
    lane = lax.broadcasted_iota(jnp.int32, lg.shape, 1).astype(F32)
    ninf = jnp.float32(-jnp.inf)
    gm = (lane >= MOE_EXPERTS) & (lane < MOE_EXPERTS + MOE_GROUPS)
    gl = jnp.where(gm, lg, ninf)
    ge = jnp.exp(gl - jnp.max(gl, axis=-1, keepdims=True))
    gp = ge / jnp.sum(ge, axis=-1, keepdims=True)
    pmax = jnp.max(gp, axis=-1, keepdims=True)
    gi = jnp.min(jnp.where((gp == pmax) & gm, lane, 1e4), axis=-1, keepdims=True) - MOE_EXPERTS
    lo = gi * MOE_PER_GROUP
    em = (lane >= lo) & (lane < lo + MOE_PER_GROUP)
    el = jnp.where(em, lg, ninf)
    v1 = jnp.max(el, axis=-1, keepdims=True)
    i1 = jnp.min(jnp.where((el == v1) & em, lane, 1e4), axis=-1, keepdims=True)
    em2 = em & (lane != i1)
    el2 = jnp.where(em2, lg, ninf)
    v2 = jnp.max(el2, axis=-1, keepdims=True)
    i2 = jnp.min(jnp.where((el2 == v2) & em2, lane, 1e4), axis=-1, keepdims=True)
    t = jnp.exp(v2 - v1)
    w1 = 1.0 / (1.0 + t)
    w2 = t / (1.0 + t)
    return jnp.where(lane == i1, w1 * pmax, 0.0) + jnp.where(lane == i2, w2 * pmax, 0.0)


def _moe_kernel(x_ref, wr_ref, br_ref, wg_ref, wu_ref, wd_ref, g_ref, b_ref, o_ref, acc_ref, gates_ref, xb_ref):
    grp = pl.program_id(1)

    @pl.when(grp == 0)
    def _():
        x = x_ref[...]
        gates_ref[...] = _router_gates(x, wr_ref[...], br_ref[...])
        xb_ref[...] = x.astype(BF16)
        acc_ref[...] = jnp.zeros_like(acc_ref)

    gates = gates_ref[...]
    lane = lax.broadcasted_iota(jnp.int32, gates.shape, 1)
    xb = xb_ref[...]
    hs = []
    for j in range(MOE_PER_GROUP):
        ge = jnp.sum(jnp.where(lane == grp * MOE_PER_GROUP + j, gates, 0.0), axis=-1, keepdims=True)
        a = _dot(xb, wg_ref[0, j])
        u = _dot(xb, wu_ref[0, j])
        hs.append((_silu(a) * u * ge).astype(BF16))
    wd = wd_ref[0].reshape(MOE_PER_GROUP * MOE_HIDDEN, D_MODEL)
    acc_ref[...] += _dot(jnp.concatenate(hs, axis=-1), wd)

    @pl.when(grp == MOE_GROUPS - 1)
    def _():
        o_ref[...] = _layer_norm(DN_ALPHA * x_ref[...] + acc_ref[...], g_ref[...], b_ref[...])


def _moe_ln(x, wr, br, wg, wu, wd, layer, g, b, tm=512):
    t = x.shape[0]
    tm = min(tm, t)
    row = pl.BlockSpec((tm, D_MODEL), lambda i, e: (i, 0))
    full = lambda a: pl.BlockSpec(a.shape, lambda i, e: (0,) * a.ndim)
    grp_w = lambda a: pl.BlockSpec((1, MOE_PER_GROUP) + a.shape[2:], lambda i, e: (layer, e, 0, 0))
    return pl.pallas_call(
        _moe_kernel,
        grid=(t // tm, MOE_GROUPS),
        in_specs=[row, full(wr), full(br), grp_w(wg), grp_w(wu), grp_w(wd), full(g), full(b)],
        out_specs=row,
        out_shape=jax.ShapeDtypeStruct((t, D_MODEL), F32),
        scratch_shapes=[pltpu.VMEM((tm, D_MODEL), F32), pltpu.VMEM((tm, 128), F32),
                        pltpu.VMEM((tm, D_MODEL), BF16)],
        compiler_params=_cparams("parallel", "arbitrary"),
        name="moe_ln",
    )(x, wr, br, wg, wu, wd, g, b)


def _s5_kernel(*refs, tlen, ngrp, chained, precise):
    if chained:
        u_ref, wb_ref, wc_ref, lam_ref, d_ref, y_ref, ore_ref, oim_ref, bu_ref, h_ref, cr_ref, ci_ref = refs
        u = jnp.swapaxes(u_ref[...], 0, 1).reshape(8 * tlen, 128)
    else:
        (u_ref, h0r_ref, h0i_ref, wb_ref, wc_ref, lam_ref, d_ref,
         y_ref, ore_ref, oim_ref, bu_ref, h_ref, cr_ref, ci_ref) = refs
        gr = 8 * tlen
        u = jnp.concatenate(
            [jnp.swapaxes(u_ref[g * gr:(g + 1) * gr, :].reshape(8, tlen, 128), 0, 1).reshape(gr, 128)
             for g in range(ngrp)], axis=0)
    if precise:
        bu_ref[...] = _dot_hi(u, wb_ref[0])
    else:
        bu_ref[...] = _dot(u.astype(BF16), wb_ref[0].astype(BF16))
    ns = S5_SLAB_STATE
    lr = jnp.broadcast_to(lam_ref[0, 0:1, :], (8, ns))
    li = jnp.broadcast_to(lam_ref[0, 1:2, :], (8, ns))

    def step(tile, hr, hi):
        rows = pl.ds(pl.multiple_of(tile * 8, 8), 8)
        nr = lr * hr - li * hi + bu_ref[rows, 0:ns]
        ni = lr * hi + li * hr + bu_ref[rows, ns:2 * ns]
        h_ref[rows, 0:ns] = nr
        h_ref[rows, ns:2 * ns] = ni
        return nr, ni

    if chained:
        @pl.when(pl.program_id(2) == 0)
        def _():
            cr_ref[...] = jnp.zeros_like(cr_ref)
            ci_ref[...] = jnp.zeros_like(ci_ref)

        hr, hi = lax.fori_loop(0, tlen, lambda t, c: step(t, *c), (cr_ref[...], ci_ref[...]), unroll=8)
        cr_ref[...] = hr
        ci_ref[...] = hi
        ore_ref[...] = hr
        oim_ref[...] = hi
    else:
        cr_ref[...] = jnp.transpose(h0r_ref[0])
        ci_ref[...] = jnp.transpose(h0i_ref[0])

        def group(g, _):
            s0 = pl.multiple_of(g * 8, 8)
            hr, hi = cr_ref[pl.ds(s0, 8), :], ci_ref[pl.ds(s0, 8), :]
            for t in range(tlen):
                hr, hi = step(g * tlen + t, hr, hi)
            cr_ref[pl.ds(s0, 8), :] = hr
            ci_ref[pl.ds(s0, 8), :] = hi
            return 0

        lax.fori_loop(0, ngrp, group, 0)
        ore_ref[...] = jnp.transpose(cr_ref[...])
        oim_ref[...] = jnp.transpose(ci_ref[...])
    y = _dot(h_ref[...].astype(BF16), wc_ref[0].astype(BF16)) + d_ref[0] * u
    if chained:
        y_ref[...] = jnp.swapaxes(y.reshape(tlen, 8, 128), 0, 1)
    else:
        gr = 8 * tlen
        for g in range(ngrp):
            y_ref[g * gr:(g + 1) * gr, :] = jnp.swapaxes(
                y[g * gr:(g + 1) * gr, :].reshape(tlen, 8, 128), 0, 1).reshape(gr, 128)


def _s5_mixer_prompt(z, params, seq_len, tlen=128):
    t = z.shape[0]
    nseq = t // seq_len
    ns = S5_SLAB_STATE
    z3 = z.reshape(nseq, seq_len, z.shape[1])
    slab = lambda a: pl.BlockSpec((1,) + a.shape[1:], lambda j, b, i: (j,) + (0,) * (a.ndim - 1))
    u_spec = pl.BlockSpec((8, tlen, 128), lambda j, b, i: (b, i, j))
    st_spec = pl.BlockSpec((8, ns), lambda j, b, i: (b, j))
    st_shape = jax.ShapeDtypeStruct((nseq, S5_SLABS * ns), F32)
    rows = 8 * tlen
    y, hre, him = pl.pallas_call(
        functools.partial(_s5_kernel, tlen=tlen, ngrp=1, chained=True, precise=False),
        grid=(S5_SLABS, nseq // 8, seq_len // tlen),
        in_specs=[u_spec] + [slab(a) for a in params],
        out_specs=[u_spec, st_spec, st_spec],
        out_shape=[jax.ShapeDtypeStruct((nseq, seq_len, MIX_WIDTH), F32), st_shape, st_shape],
        scratch_shapes=[pltpu.VMEM((rows, 2 * ns), F32), pltpu.VMEM((rows, 2 * ns), F32),
                        pltpu.VMEM((8, ns), F32), pltpu.VMEM((8, ns), F32)],
        compiler_params=_cparams("parallel", "parallel", "arbitrary"),
        name="s5_mixer",
    )(z3, *params)
    return y.reshape(t, MIX_WIDTH), hre, him


def _s5_mixer_sample(z, params, seq_len, h0r, h0i, layer):
    t = z.shape[0]
    nseq = t // seq_len
    ns = S5_SLAB_STATE
    seqs = 128
    rows = seqs * seq_len
    slab = lambda a: pl.BlockSpec((1,) + a.shape[1:], lambda j, i: (j,) + (0,) * (a.ndim - 1))
    u_spec = pl.BlockSpec((rows, 128), lambda j, i: (i, j))
    h0_spec = pl.BlockSpec((1, ns, seqs), lambda j, i: (layer, j, i))
    st_spec = pl.BlockSpec((ns, seqs), lambda j, i: (j, i))
    st_shape = jax.ShapeDtypeStruct((S5_SLABS * ns, nseq), F32)
    return pl.pallas_call(
        functools.partial(_s5_kernel, tlen=seq_len, ngrp=seqs // 8, chained=False, precise=True),
        grid=(S5_SLABS, nseq // seqs),
        in_specs=[u_spec, h0_spec, h0_spec] + [slab(a) for a in params],
        out_specs=[u_spec, st_spec, st_spec],
        out_shape=[jax.ShapeDtypeStruct((t, MIX_WIDTH), F32), st_shape, st_shape],
        scratch_shapes=[pltpu.VMEM((rows, 2 * ns), F32), pltpu.VMEM((rows, 2 * ns), F32),
                        pltpu.VMEM((seqs, ns), F32), pltpu.VMEM((seqs, ns), F32)],
        compiler_params=_cparams("parallel", "parallel"),
        name="s5_mixer",
    )(z, h0r, h0i, *params)


def _s5_params(lam_re, lam_im, log_dt, b_re, b_im, c_re, c_im, d_skip):
    lre, lim = lam_re.astype(F32), lam_im.astype(F32)
    dt = jnp.exp(log_dt.astype(F32))[:, None]
    mag = jnp.exp(lre * dt)
    bar_re, bar_im = mag * jnp.cos(lim * dt), mag * jnp.sin(lim * dt)
    nre, nim = bar_re - 1.0, bar_im
    den = lre * lre + lim * lim
    fre, fim = (nre * lre + nim * lim) / den, (nim * lre - nre * lim) / den
    bre, bim = b_re.astype(F32), b_im.astype(F32)
    bb_re = fre[..., None] * bre - fim[..., None] * bim
    bb_im = fre[..., None] * bim + fim[..., None] * bre
    eye = jnp.eye(8, dtype=F32)
    gps = 8

    def blockdiag_in(m):
        m = m.reshape(S5_SLABS, gps, S5_STATE, S5_GROUP).transpose(0, 1, 3, 2)
        return jnp.einsum('jgcp,gh->jgchp', m, eye).reshape(S5_SLABS, 128, S5_SLAB_STATE)

    def blockdiag_out(m):
        m = m.reshape(S5_SLABS, gps, S5_GROUP, S5_STATE).transpose(0, 1, 3, 2)
        return jnp.einsum('jgpc,gh->jgphc', m, eye).reshape(S5_SLABS, S5_SLAB_STATE, 128)

    wb = jnp.concatenate([blockdiag_in(bb_re), blockdiag_in(bb_im)], axis=2)
    wc = jnp.concatenate([blockdiag_out(c_re.astype(F32)), blockdiag_out(-c_im.astype(F32))], axis=1)
    lam = jnp.stack([bar_re.reshape(S5_SLABS, S5_SLAB_STATE), bar_im.reshape(S5_SLABS, S5_SLAB_STATE)], axis=1)
    dsk = d_skip.astype(F32).reshape(S5_SLABS, 1, 128)
    return wb, wc, lam, dsk


def _head_norm_gate(o, gn, gate):
    col = lax.broadcasted_iota(jnp.int32, o.shape, 1)
    valid = col < RET_DV
    mu = jnp.sum(o, axis=-1, keepdims=True) * (1.0 / RET_DV)
    d = jnp.where(valid, o - mu, 0.0)
    var = jnp.sum(d * d, axis=-1, keepdims=True) * (1.0 / RET_DV)
    return d * lax.rsqrt(var + NORM_EPS) * gn * _silu(gate)


def _load_states(s_in_ref, spad_ref, nseg, dk):
    spad_ref[...] = jnp.zeros_like(spad_ref)
    for s in range(nseg):
        for h in range(HEADS):
            spad_ref[h, s * DK_PAD:s * DK_PAD + dk, 0:RET_DV] = s_in_ref[s, h]


def _ret_kernel(*refs, nseg, gammas):
    if nseg == 1:
        (q_ref, k_ref, v_ref, g_ref, cos_ref, sin_ref, dmat_ref, inner_ref, zeta_ref, gn_ref,
         o_ref, so_ref, st_ref) = refs

        @pl.when(pl.program_id(1) == 0)
        def _():
            st_ref[...] = jnp.zeros_like(st_ref)
    else:
        (q_ref, k_ref, v_ref, g_ref, cos_ref, sin_ref, dmat_ref, inner_ref, zeta_ref, gn_ref, s_in_ref,
         o_ref, so_ref, st_ref) = refs
        st_ref[:, RET_DV:DV_PAD, :] = jnp.zeros((HEADS, DV_PAD - RET_DV, nseg * RET_DK), F32)
        for s in range(nseg):
            for h in range(HEADS):
                st_ref[h, 0:RET_DV, s * RET_DK:(s + 1) * RET_DK] = s_in_ref[s, h]
    r = q_ref.shape[0]
    seg_len = r // nseg
    cos = cos_ref[...]
    sin = sin_ref[...]
    rot = lambda x: x * cos + pltpu.roll(x, RET_DK // 2, axis=1) * sin
    rowseg = lax.broadcasted_iota(jnp.int32, (r, DK_PAD), 0) // seg_len
    colseg = lax.broadcasted_iota(jnp.int32, (DV_PAD, r), 1) // seg_len
    for h in range(HEADS):
        q = rot(q_ref[:, h * DK_PAD:(h + 1) * DK_PAD])
        k = rot(k_ref[:, h * DK_PAD:(h + 1) * DK_PAD]) * (RET_DK ** -0.5)
        v = v_ref[:, h * DV_PAD:(h + 1) * DV_PAD]
        vb = v.astype(BF16)
        qb = q.astype(BF16)
        att = _dot_nt(qb, k.astype(BF16)) * dmat_ref[h]
        o = _dot(att.astype(BF16), vb)
        st_all = st_ref[h]
        if nseg == 1:
            qbd = qb
        else:
            qbd = jnp.concatenate([jnp.where(rowseg == s, q, 0.0) for s in range(nseg)], axis=1).astype(BF16)
        o = o + _dot_nt(qbd, st_all.astype(BF16)) * inner_ref[h]
        kzb = (k * zeta_ref[h]).astype(BF16)
        vt = jnp.transpose(v)
        for s in range(nseg):
            vts = vt if nseg == 1 else jnp.where(colseg == s, vt, 0.0)
            st_new = st_all[:, s * RET_DK:(s + 1) * RET_DK] * gammas[h] + _dot(vts.astype(BF16), kzb)
            if nseg == 1:
                st_ref[h] = st_new
                so_ref[0, h] = st_new[0:RET_DV, :]
            else:
                so_ref[s, h] = st_new[0:RET_DV, :]
        o_ref[:, h * DV_PAD:(h + 1) * DV_PAD] = _head_norm_gate(
            o, gn_ref[:, h * DV_PAD:(h + 1) * DV_PAD], g_ref[:, h * DV_PAD:(h + 1) * DV_PAD])


def _ret_consts(seg_len, nseg, pos):
    hh = jnp.arange(RET_HEADS, dtype=F32)
    log_g = jnp.log1p(-jnp.exp2(-5.0 - hh))
    idx = jnp.arange(seg_len, dtype=F32)
    rel = idx[:, None] - idx[None, :]
    dmat = jnp.where(rel >= 0, jnp.exp(log_g[:, None, None] * jnp.maximum(rel, 0.0)), 0.0)
    inner = jnp.exp(log_g[:, None] * (idx + 1.0))
    zeta = jnp.exp(log_g[:, None] * (seg_len - 1.0 - idx))
    r = seg_len * nseg
    seg = jnp.arange(r) // seg_len
    same = (seg[:, None] == seg[None, :]).astype(F32)
    dmat = jnp.tile(dmat, (1, nseg, nseg)) * same[None]
    inner = jnp.broadcast_to(jnp.tile(inner, (1, nseg))[:, :, None], (RET_HEADS, r, DV_PAD))
    zeta = jnp.broadcast_to(jnp.tile(zeta, (1, nseg))[:, :, None], (RET_HEADS, r, DK_PAD))
    half = RET_DK // 2
    inv_freq = ROPE_BASE ** (-jnp.arange(half, dtype=F32) / half)
    ang = pos.astype(F32)[:, None] * inv_freq
    cos2 = jnp.concatenate([jnp.cos(ang), jnp.cos(ang)], axis=-1)
    sin2 = jnp.concatenate([-jnp.sin(ang), jnp.sin(ang)], axis=-1)
    return dmat, inner, zeta, cos2, sin2


def _ret_gammas(seg_len):
    return tuple(float((1.0 - 2.0 ** (-5.0 - h)) ** seg_len) for h in range(RET_HEADS))


def _ret_mixer(z, gn, seq_len, pos, s0=None):
    t = z.shape[0]
    nseq = t // seq_len
    if s0 is None:
        r, nseg, seg_len = RET_CHUNK, 1, RET_CHUNK
        steps = seq_len // r
        grid = (nseq, steps)
        rowmap = lambda c: (lambda n, i: (n * steps + i, c))
        tabmap = lambda n, i: (i, 0)
        stmap = lambda n, i: (n, 0, 0, 0)
        st_rows = 1
        sem = ("parallel", "arbitrary")
    else:
        seg_len = seq_len
        nseg = 128 // seg_len
        r = 128
        grid = (t // r,)
        rowmap = lambda c: (lambda i: (i, c))
        tabmap = lambda i: (0, 0)
        stmap = lambda i: (i, 0, 0, 0)
        st_rows = nseg
        sem = ("arbitrary",)
    dmat, inner, zeta, cos2, sin2 = _ret_consts(seg_len, nseg, pos)
    if s0 is not None:
        cos2 = jnp.tile(cos2, (nseg, 1))
        sin2 = jnp.tile(sin2, (nseg, 1))
    full = lambda a: pl.BlockSpec(a.shape, lambda *_: (0,) * a.ndim)
    st_spec = pl.BlockSpec((st_rows, RET_HEADS, RET_DV, RET_DK), stmap)
    in_specs = [pl.BlockSpec((r, 512), rowmap(0)), pl.BlockSpec((r, 512), rowmap(1)),
                pl.BlockSpec((r, 1024), rowmap(1)), pl.BlockSpec((r, 1024), rowmap(2)),
                pl.BlockSpec((r, DK_PAD), tabmap), pl.BlockSpec((r, DK_PAD), tabmap),
                full(dmat), full(inner), full(zeta), full(gn)]
    args = [z, z, z, z, cos2, sin2, dmat, inner, zeta, gn]
    if s0 is not None:
        in_specs.append(st_spec)
        args.append(s0)
    mix, s_new = pl.pallas_call(
        functools.partial(_ret_kernel, nseg=nseg, gammas=_ret_gammas(seg_len)),
        grid=grid,
        in_specs=in_specs,
        out_specs=[pl.BlockSpec((r, HEADS * DV_PAD), rowmap(0)), st_spec],
        out_shape=[jax.ShapeDtypeStruct((t, HEADS * DV_PAD), F32),
                   jax.ShapeDtypeStruct((nseq, RET_HEADS, RET_DV, RET_DK), F32)],
        scratch_shapes=[pltpu.VMEM((HEADS, DV_PAD, nseg * RET_DK), F32)],
        compiler_params=_cparams(*sem),
        name="retention",
    )(*args)
    return mix, s_new


def _log_sigmoid(x):
    return jnp.minimum(x, 0.0) - jnp.log1p(jnp.exp(-jnp.abs(x)))


def _gla_kernel(*refs, nseg):
    if nseg == 1:
        (q_ref, k_ref, v_ref, r_ref, low_ref, tril_ref, same_ref, wg_ref, bg_ref, gn_ref,
         o_ref, so_ref, spad_ref) = refs

        @pl.when(pl.program_id(1) == 0)
        def _():
            spad_ref[...] = jnp.zeros_like(spad_ref)
    else:
        (q_ref, k_ref, v_ref, r_ref, low_ref, tril_ref, same_ref, wg_ref, bg_ref, gn_ref, s_in_ref,
         o_ref, so_ref, spad_ref) = refs
        _load_states(s_in_ref, spad_ref, nseg, GLA_DK)
    r = q_ref.shape[0]
    seg_len = r // nseg
    rp = max(r, 128)
    tril = tril_ref[...]
    causal = tril > 0.0
    la = _log_sigmoid(_dot_hi(low_ref[...], wg_ref[...]) + bg_ref[...]) * (1.0 / GLA_TAU)
    b = _dot_hi(tril, la)
    b_tot = _dot_hi(same_ref[...], la)
    rowseg = lax.broadcasted_iota(jnp.int32, (r, DK_PAD), 0) // seg_len
    colseg = lax.broadcasted_iota(jnp.int32, (DK_PAD, rp), 1) // seg_len

    def pad_rows(x):
        if rp == r:
            return x
        return jnp.concatenate([x, jnp.zeros((rp - r, x.shape[1]), x.dtype)], axis=0)

    for h in range(HEADS):
        sl = slice(h * DK_PAD, (h + 1) * DK_PAD)
        bh = b[:, sl]
        lah = la[:, sl]
        q = q_ref[:, sl] * (GLA_DK ** -0.5)
        k = k_ref[:, sl]
        vb = v_ref[:, h * DV_PAD:(h + 1) * DV_PAD].astype(BF16)
        q_t = q * jnp.exp(bh)
        k_t = k * jnp.exp(-bh)
        att = jnp.where(causal, _dot_nt(q_t.astype(BF16), k_t.astype(BF16)), 0.0)
        o = _dot(att.astype(BF16), vb)
        s_all = spad_ref[h]
        if nseg == 1:
            qbd = q_t.astype(BF16)
        else:
            qbd = jnp.concatenate([jnp.where(rowseg == s, q_t, 0.0) for s in range(nseg)], axis=1).astype(BF16)
        o = o + _dot(qbd, s_all.astype(BF16))
        k_dec = k * jnp.exp(b_tot[:, sl] - bh)
        kdt = jnp.transpose(pad_rows(k_dec))
        lat = jnp.transpose(pad_rows(lah))
        vbp = pad_rows(vb)
        for s in range(nseg):
            if nseg == 1:
                kds, las = kdt, lat
            else:
                kds = jnp.where(colseg == s, kdt, 0.0)
                las = jnp.where(colseg == s, lat, 0.0)
            decay = jnp.exp(jnp.sum(las, axis=-1, keepdims=True))
            s_new = s_all[s * DK_PAD:(s + 1) * DK_PAD] * decay + _dot(kds.astype(BF16), vbp)
            if nseg == 1:
                spad_ref[h] = s_new
                so_ref[0, h] = s_new[0:GLA_DK, 0:GLA_DV]
            else:
                so_ref[s, h] = s_new[0:GLA_DK, 0:GLA_DV]
        o_ref[:, h * DV_PAD:(h + 1) * DV_PAD] = _head_norm_gate(
            o, gn_ref[:, h * DV_PAD:(h + 1) * DV_PAD], r_ref[:, h * DV_PAD:(h + 1) * DV_PAD])


def _gla_mixer(z, wg, bg, gn, seq_len, s0=None):
    t = z.shape[0]
    nseq = t // seq_len
    if s0 is None:
        r, nseg, seg_len = GLA_CHUNK, 1, GLA_CHUNK
        steps = seq_len // r
        grid = (nseq, steps)
        rowmap = lambda c: (lambda n, i: (n * steps + i, c))
        stmap = lambda n, i: (n, 0, 0, 0)
        st_rows = 1
        sem = ("parallel", "arbitrary")
    else:
        seg_len = seq_len
        nseg = 128 // seg_len
        r = 128
        grid = (t // r,)
        rowmap = lambda c: (lambda i: (i, c))
        stmap = lambda i: (i, 0, 0, 0)
        st_rows = nseg
        sem = ("arbitrary",)
    seg = jnp.arange(r) // seg_len
    idx = jnp.arange(r)
    same = (seg[:, None] == seg[None, :]).astype(F32)
    tril = same * (idx[:, None] >= idx[None, :]).astype(F32)
    full = lambda a: pl.BlockSpec(a.shape, lambda *_: (0,) * a.ndim)
    st_spec = pl.BlockSpec((st_rows, GLA_HEADS, GLA_DK, GLA_DV), stmap)
    in_specs = [pl.BlockSpec((r, 512), rowmap(0)), pl.BlockSpec((r, 512), rowmap(1)),
                pl.BlockSpec((r, 1024), rowmap(1)), pl.BlockSpec((r, 1024), rowmap(2)),
                pl.BlockSpec((r, 128), rowmap(COL_LOW // 128)),
                full(tril), full(same), full(wg), full(bg), full(gn)]
    args = [z, z, z, z, z, tril, same, wg, bg, gn]
    if s0 is not None:
        in_specs.append(st_spec)
        args.append(s0)
    mix, s_new = pl.pallas_call(
        functools.partial(_gla_kernel, nseg=nseg),
        grid=grid,
        in_specs=in_specs,
        out_specs=[pl.BlockSpec((r, HEADS * DV_PAD), rowmap(0)), st_spec],
        out_shape=[jax.ShapeDtypeStruct((t, HEADS * DV_PAD), F32),
                   jax.ShapeDtypeStruct((nseq, GLA_HEADS, GLA_DK, GLA_DV), F32)],
        scratch_shapes=[pltpu.VMEM((HEADS, nseg * DK_PAD, DV_PAD), F32)],
        compiler_params=_cparams(*sem),
        name="gla",
    )(*args)
    return mix, s_new


def _pad_heads(w, width, pad):
    lead = w.shape[:-1]
    w = w.reshape(lead + (HEADS, width))
    w = jnp.pad(w, [(0, 0)] * len(lead) + [(0, 0), (0, pad - width)])
    return w.reshape(lead + (HEADS * pad,))


def _pad_head_rows(w, width, pad):
    return jnp.swapaxes(_pad_heads(jnp.swapaxes(w, 0, 1), width, pad), 0, 1)


def _ret_in_weight(w):
    qk = RET_HEADS * RET_DK
    v = _pad_heads(w[:, 2 * qk:2 * qk + MIX_WIDTH], RET_DV, DV_PAD)
    g = _pad_heads(w[:, 2 * qk + MIX_WIDTH:2 * qk + 2 * MIX_WIDTH], RET_DV, DV_PAD)
    return jnp.concatenate([w[:, :2 * qk], v, g, w[:, -XA_WIDTH:]], axis=1).astype(BF16)


def _gla_in_weight(w):
    qk = GLA_HEADS * GLA_DK
    q = _pad_heads(w[:, :qk], GLA_DK, DK_PAD)
    k = _pad_heads(w[:, qk:2 * qk], GLA_DK, DK_PAD)
    v = _pad_heads(w[:, 2 * qk:2 * qk + MIX_WIDTH], GLA_DV, DV_PAD)
    r = _pad_heads(w[:, 2 * qk + MIX_WIDTH:2 * qk + 2 * MIX_WIDTH], GLA_DV, DV_PAD)
    low = jnp.pad(w[:, 2 * qk + 2 * MIX_WIDTH:2 * qk + 2 * MIX_WIDTH + GLA_RANK], [(0, 0), (0, 128 - GLA_RANK)])
    return jnp.concatenate([q, k, v, r, w[:, -XA_WIDTH:], low], axis=1).astype(BF16)


def kernel(x_prompt, x_sample, mem_prompt, state_s5_re, state_s5_im, state_ret, state_gla,
           cache_mem_k, cache_mem_v,
           w_in_s5, s5_lam_re, s5_lam_im, s5_log_dt, s5_b_re, s5_b_im, s5_c_re, s5_c_im,
           s5_d, s5_w_glu, s5_b_glu,
           w_in_ret, ret_gn,
           w_in_gla, gla_w_gate2, gla_b_gate2, gla_gn,
           w_mem_k, w_mem_v, w_out, ln_g, ln_b,
           moe_w_grp, moe_b_grp, moe_w_exp, moe_b_exp, moe_w_gate, moe_w_up, moe_w_down):
    bp, lp, _ = x_prompt.shape
    bs, ls, _ = x_sample.shape
    xp = x_prompt.reshape(bp * lp, D_MODEL)
    xs = x_sample.reshape(bs * ls, D_MODEL)
    pos_p = jnp.arange(lp, dtype=jnp.int32)
    pos_s = PAST_LEN + jnp.arange(ls, dtype=jnp.int32)

    to_t = lambda c: jnp.transpose(c, (0, 1, 3, 4, 2)).reshape(DEPTH, c.shape[1], XA_WIDTH, N_MEM)
    from_t = lambda c: jnp.transpose(c.reshape(DEPTH, c.shape[1], XA_HEADS, XA_HEAD_DIM, N_MEM), (0, 1, 4, 2, 3))
    mkt_all, mvt_all = _mem_kv(mem_prompt, jnp.swapaxes(w_mem_k, 1, 2).astype(BF16),
                               jnp.swapaxes(w_mem_v, 1, 2).astype(BF16))
    cache_kt = to_t(cache_mem_k)
    cache_vt = to_t(cache_mem_v)
    ret_t = jnp.swapaxes(state_ret, 3, 4)
    s5_t = lambda s: jnp.transpose(s, (0, 2, 3, 1)).reshape(s.shape[0], S5_GROUPS * S5_STATE, bs)
    s5_re_t, s5_im_t = s5_t(state_s5_re), s5_t(state_s5_im)
    moe_wg, moe_wu, moe_wd = moe_w_gate.astype(BF16), moe_w_up.astype(BF16), moe_w_down.astype(BF16)

    p_re, p_im, p_ret, p_gla = [], [], [], []
    s_re, s_im, s_ret, s_gla = [], [], [], []
    for i in range(DEPTH):
        kind, j = i % N_MIXERS, i // N_MIXERS
        ln1 = (ln_g[i, 0].reshape(1, D_MODEL), ln_b[i, 0].reshape(1, D_MODEL))
        ln2 = (ln_g[i, 1].reshape(1, D_MODEL), ln_b[i, 1].reshape(1, D_MODEL))
        wo = w_out[i]
        wo_att = wo[MIX_WIDTH:].astype(BF16)
        glu = {}
        if kind == 0:
            params = _s5_params(s5_lam_re[j], s5_lam_im[j], s5_log_dt[j], s5_b_re[j], s5_b_im[j],
                                s5_c_re[j], s5_c_im[j], s5_d[j])
            zp = _matmul(xp, w_in_s5[j].astype(BF16), 512)
            zs = _matmul(xs, w_in_s5[j], 512, precise=True)
            mix_p, a_re, a_im = _s5_mixer_prompt(zp, params, lp)
            mix_s, b_re, b_im = _s5_mixer_sample(zs, params, ls, s5_re_t, s5_im_t, j)
            p_re.append(a_re), p_im.append(a_im), s_re.append(b_re), s_im.append(b_im)
            xa_col = MIX_WIDTH
            wo_mix = wo[:MIX_WIDTH].astype(BF16)
            glu = dict(glu_w=s5_w_glu[j].astype(BF16), glu_b=s5_b_glu[j].reshape(1, MIX_WIDTH))
        elif kind == 1:
            w = _ret_in_weight(w_in_ret[j])
            gn = _pad_heads(ret_gn[j], RET_DV, DV_PAD).reshape(1, HEADS * DV_PAD)
            zp = _matmul(xp, w, 512)
            zs = _matmul(xs, w, 512)
            mix_p, a_s = _ret_mixer(zp, gn, lp, pos_p)
            mix_s, b_s = _ret_mixer(zs, gn, ls, pos_s, s0=ret_t[j])
            p_ret.append(a_s), s_ret.append(b_s)
            xa_col = COL_XA
            wo_mix = _pad_head_rows(wo[:MIX_WIDTH], RET_DV, DV_PAD).astype(BF16)
        else:
            w = _gla_in_weight(w_in_gla[j])
            gn = _pad_heads(gla_gn[j], GLA_DV, DV_PAD).reshape(1, HEADS * DV_PAD)
            wg = jnp.pad(_pad_heads(gla_w_gate2[j], GLA_DK, DK_PAD), [(0, 128 - GLA_RANK), (0, 0)])
            bg = _pad_heads(gla_b_gate2[j], GLA_DK, DK_PAD).reshape(1, HEADS * DK_PAD)
            zp = _matmul(xp, w, 512)
            zs = _matmul(xs, w, 512)
            mix_p, a_s = _gla_mixer(zp, wg, bg, gn, lp)
            mix_s, b_s = _gla_mixer(zs, wg, bg, gn, ls, s0=state_gla[j])
            p_gla.append(a_s), s_gla.append(b_s)
            xa_col = COL_XA
            wo_mix = _pad_head_rows(wo[:MIX_WIDTH], GLA_DV, DV_PAD).astype(BF16)
        att_p = _mem_attention(zp, xa_col, mkt_all, mvt_all, i, lp, rl=min(512, lp), nseq=1)
        att_s = _mem_attention(zs, xa_col, cache_kt, cache_vt, i, ls, rl=ls, nseq=128 // ls)
        xp = _out_proj_ln(mix_p, att_p, xp, wo_mix, wo_att, *ln1, **glu)
        xs = _out_proj_ln(mix_s, att_s, xs, wo_mix, wo_att, *ln1, **glu)

        wr = jnp.pad(jnp.concatenate([moe_w_exp[i], moe_w_grp[i]], axis=1),
                     [(0, 0), (0, 128 - MOE_EXPERTS - MOE_GROUPS)]).astype(F32)
        br = jnp.pad(jnp.concatenate([moe_b_exp[i], moe_b_grp[i]]),
                     [(0, 128 - MOE_EXPERTS - MOE_GROUPS)]).reshape(1, 128).astype(F32)
        xp = _moe_ln(xp, wr, br, moe_wg, moe_wu, moe_wd, i, *ln2)
        xs = _moe_ln(xs, wr, br, moe_wg, moe_wu, moe_wd, i, *ln2)

    st = lambda xs_: jnp.stack(xs_)
    s5_p = lambda xs_: st(xs_).reshape(len(xs_), bp, S5_GROUPS, S5_STATE)
    s5_s = lambda xs_: jnp.transpose(st(xs_).reshape(len(xs_), S5_GROUPS, S5_STATE, bs), (0, 3, 1, 2))
    ret_out = lambda xs_: jnp.swapaxes(st(xs_), 3, 4)
    return (xp.reshape(bp, lp, D_MODEL), xs.reshape(bs, ls, D_MODEL),
            s5_p(p_re), s5_p(p_im), ret_out(p_ret), st(p_gla),
            from_t(mkt_all), from_t(mvt_all),
            s5_s(s_re), s5_s(s_im), ret_out(s_ret), st(s_gla))
```

```python
import functools
import math

import jax
import jax.numpy as jnp
from jax import lax
from jax.experimental import pallas as pl
from jax.experimental.pallas import tpu as pltpu

F32 = jnp.float32
BF16 = jnp.bfloat16
HI = lax.Precision.HIGHEST

D_MODEL = 1024
DEPTH = 4
PAST_LEN = 16384
N_MIXERS = 3
MIX_WIDTH = 768
XA_HEADS = 4
XA_HEAD_DIM = 64
XA_WIDTH = 256
N_MEM = 256
S5_GROUP = 16
S5_GROUPS = 48
S5_STATE = 64
S5_SLABS = 6
S5_SLAB_STATE = 512
RET_HEADS = 4
RET_DK = 128
RET_DV = 192
RET_CHUNK = 128
ROPE_BASE = 10000.0
GLA_HEADS = 4
GLA_DK = 96
GLA_DV = 192
GLA_RANK = 16
GLA_TAU = 16.0
GLA_CHUNK = 64
MOE_GROUPS = 4
MOE_PER_GROUP = 4
MOE_EXPERTS = 16
MOE_HIDDEN = 256
DN_ALPHA = (2 * DEPTH) ** 0.25
NORM_EPS = 1e-5

DK_PAD = 128
DV_PAD = 256
HEADS = 4
COL_Q, COL_K, COL_V, COL_G, COL_XA, COL_LOW = 0, 512, 1024, 2048, 3072, 3328
RET_IN_PAD = 3328
GLA_IN_PAD = 3456
VMEM_LIMIT = 52 * 1024 * 1024


def _cparams(*sem):
    return pltpu.CompilerParams(dimension_semantics=sem, vmem_limit_bytes=VMEM_LIMIT)


def _dot(a, b):
    return jnp.dot(a, b, preferred_element_type=F32)


def _dot_nt(a, b):
    return lax.dot_general(a, b, (((1,), (1,)), ((), ())), preferred_element_type=F32)


def _dot_hi(a, b):
    return jnp.dot(a, b, preferred_element_type=F32, precision=HI)


def _layer_norm(y, g, b):
    mu = jnp.mean(y, axis=-1, keepdims=True)
    d = y - mu
    var = jnp.mean(d * d, axis=-1, keepdims=True)
    return d * lax.rsqrt(var + NORM_EPS) * g + b


def _sigmoid(x):
    return 1.0 / (1.0 + jnp.exp(-x))


def _silu(x):
    return x * _sigmoid(x)


def _gelu_tanh(x):
    c = math.sqrt(2.0 / math.pi)
    return 0.5 * x * (1.0 + jnp.tanh(c * (x + 0.044715 * (x * x * x))))


def _matmul_kernel(x_ref, w_ref, o_ref, *, precise):
    if precise:
        o_ref[...] = _dot_hi(x_ref[...], w_ref[...])
    else:
        o_ref[...] = _dot(x_ref[...].astype(BF16), w_ref[...])


def _matmul(x, w, tm, precise=False):
    t, k = x.shape
    tm = min(tm, t)
    n = w.shape[1]
    return pl.pallas_call(
        functools.partial(_matmul_kernel, precise=precise),
        grid=(t // tm,),
        in_specs=[pl.BlockSpec((tm, k), lambda i: (i, 0)),
                  pl.BlockSpec((k, n), lambda i: (0, 0))],
        out_specs=pl.BlockSpec((tm, n), lambda i: (i, 0)),
        out_shape=jax.ShapeDtypeStruct((t, n), F32),
        compiler_params=_cparams("parallel"),
        name="in_proj",
    )(x, w)


def _memkv_kernel(x_ref, wk_ref, wv_ref, ok_ref, ov_ref):
    xb = x_ref[0].astype(BF16)
    ok_ref[0, 0] = _dot_nt(wk_ref[0], xb)
    ov_ref[0, 0] = _dot_nt(wv_ref[0], xb)


def _mem_kv(mem, wkt, wvt):
    n = mem.shape[0]
    spec_w = pl.BlockSpec((1, XA_WIDTH, D_MODEL), lambda i, b: (i, 0, 0))
    spec_o = pl.BlockSpec((1, 1, XA_WIDTH, N_MEM), lambda i, b: (i, b, 0, 0))
    return pl.pallas_call(
        _memkv_kernel,
        grid=(DEPTH, n),
        in_specs=[pl.BlockSpec((1, N_MEM, D_MODEL), lambda i, b: (b, 0, 0)), spec_w, spec_w],
        out_specs=[spec_o, spec_o],
        out_shape=[jax.ShapeDtypeStruct((DEPTH, n, XA_WIDTH, N_MEM), F32)] * 2,
        compiler_params=_cparams("parallel", "parallel"),
        name="mem_kv",
    )(mem, wkt, wvt)


def _attn_kernel(q_ref, kt_ref, vt_ref, o_ref, *, nseq, rl):
    head = lax.broadcasted_iota(jnp.int32, (rl, XA_WIDTH), 1) // XA_HEAD_DIM
    for s in range(nseq):
        q = q_ref[s * rl:(s + 1) * rl, :] * (XA_HEAD_DIM ** -0.5)
        qs = jnp.concatenate([jnp.where(head == h, q, 0.0) for h in range(XA_HEADS)], axis=0)
        sc = _dot(qs.astype(BF16), kt_ref[0, s].astype(BF16))
        sc = sc - jnp.max(sc, axis=-1, keepdims=True)
        p = jnp.exp(sc)
        p = p / jnp.sum(p, axis=-1, keepdims=True)
        pv = _dot_nt(p.astype(BF16), vt_ref[0, s].astype(BF16))
        o = jnp.where(head == 0, pv[0:rl], 0.0)
        for h in range(1, XA_HEADS):
            o = o + jnp.where(head == h, pv[h * rl:(h + 1) * rl], 0.0)
        o_ref[s * rl:(s + 1) * rl, :] = o


def _mem_attention(z, xa_col, mkt, mvt, layer, seq_len, rl, nseq):
    t = z.shape[0]
    rows = rl * nseq
    cb = xa_col // XA_WIDTH
    if nseq == 1:
        per_seq = seq_len // rl
        kv_map = lambda i: (layer, i // per_seq, 0, 0)
    else:
        kv_map = lambda i: (layer, i, 0, 0)
    return pl.pallas_call(
        functools.partial(_attn_kernel, nseq=nseq, rl=rl),
        grid=(t // rows,),
        in_specs=[pl.BlockSpec((rows, XA_WIDTH), lambda i: (i, cb)),
                  pl.BlockSpec((1, nseq, XA_WIDTH, N_MEM), kv_map),
                  pl.BlockSpec((1, nseq, XA_WIDTH, N_MEM), kv_map)],
        out_specs=pl.BlockSpec((rows, XA_WIDTH), lambda i: (i, 0)),
        out_shape=jax.ShapeDtypeStruct((t, XA_WIDTH), F32),
        compiler_params=_cparams("parallel"),
        name="mem_attention",
    )(z, mkt, mvt)


def _outproj_kernel(*refs, glu):
    if glu:
        mix_ref, att_ref, x_ref, wm_ref, wa_ref, g_ref, b_ref, wg_ref, bg_ref, o_ref = refs
        y = _gelu_tanh(mix_ref[...])
        mix = y * _sigmoid(_dot(y.astype(BF16), wg_ref[...]) + bg_ref[...])
    else:
        mix_ref, att_ref, x_ref, wm_ref, wa_ref, g_ref, b_ref, o_ref = refs
        mix = mix_ref[...]
    h = _dot(mix.astype(BF16), wm_ref[...]) + _dot(att_ref[...].astype(BF16), wa_ref[...])
    o_ref[...] = _layer_norm(DN_ALPHA * x_ref[...] + h, g_ref[...], b_ref[...])


def _out_proj_ln(mix, att, x, wm, wa, g, b, glu_w=None, glu_b=None, tm=512):
    t = x.shape[0]
    tm = min(tm, t)
    km = mix.shape[1]
    row = lambda w: pl.BlockSpec((tm, w), lambda i: (i, 0))
    full = lambda a: pl.BlockSpec(a.shape, lambda i: (0,) * a.ndim)
    args = [mix, att, x, wm, wa, g, b]
    specs = [row(km), row(XA_WIDTH), row(D_MODEL), full(wm), full(wa), full(g), full(b)]
    if glu_w is not None:
        args += [glu_w, glu_b]
        specs += [full(glu_w), full(glu_b)]
    return pl.pallas_call(
        functools.partial(_outproj_kernel, glu=glu_w is not None),
        grid=(t // tm,),
        in_specs=specs,
        out_specs=row(D_MODEL),
        out_shape=jax.ShapeDtypeStruct((t, D_MODEL), F32),
        compiler_params=_cparams("parallel"),
        name="out_proj_ln",
    )(*args)


def _router_gates(x, wr, br):
    lg = _dot_hi(x, wr) + br
    lane = lax.broadcasted_iota(jnp.int32, lg.shape, 1).astype(F32)
    ninf = jnp.float32(-jnp.inf)
    gm = (lane >= MOE_EXPERTS) & (lane < MOE_EXPERTS + MOE_GROUPS)
    gl = jnp.where(gm, lg, ninf)
    ge = jnp.exp(gl - jnp.max(gl, axis=-1, keepdims=True))
    gp = ge / jnp.sum(ge, axis=-1, keepdims=True)
    pmax = jnp.max(gp, axis=-1, keepdims=True)
    gi = jnp.min(jnp.where((gp == pmax) & gm, lane, 1e4), axis=-1, keepdims=True) - MOE_EXPERTS
    lo = gi * MOE_PER_GROUP
    em = (lane >= lo) & (lane < lo + MOE_PER_GROUP)
    el = jnp.where(em, lg, ninf)
    v1 = jnp.max(el, axis=-1, keepdims=True)
    i1 = jnp.min(jnp.where((el == v1) & em, lane, 1e4), axis=-1, keepdims=True)
    em2 = em & (lane != i1)
    el2 = jnp.where(em2, lg, ninf)
    v2 = jnp.max(el2, axis=-1, keepdims=True)
    i2 = jnp.min(jnp.where((el2 == v2) & em2, lane, 1e4), axis=-1, keepdims=True)
    t = jnp.exp(v2 - v1)
    w1 = 1.0 / (1.0 + t)
    w2 = t / (1.0 + t)
    gates = jnp.where(lane == i1, w1 * pmax, 0.0) + jnp.where(lane == i2, w2 * pmax, 0.0)
    return gates, jnp.where(lane == gi, 1.0, 0.0)


MOE_CAP = 160
MOE_OV = 128


def _moe_kernel(x_ref, wr_ref, br_ref, ltri_ref, wg_ref, wu_ref, wd_ref, g_ref, b_ref, o_ref,
                acc_ref, xb_ref, gs_ref):
    tm = x_ref.shape[0]
    x = x_ref[...]
    gates, ohc = _router_gates(x, wr_ref[...], br_ref[...])
    g_hi = gates.astype(BF16)
    g_r1 = gates - g_hi.astype(F32)
    g_mid = g_r1.astype(BF16)
    g_lo = (g_r1 - g_mid.astype(F32)).astype(BF16)
    gs_ref[...] = jnp.concatenate([g_hi, g_mid, g_lo], axis=1)
    xb_ref[...] = x.astype(BF16)
    acc_ref[...] = jnp.zeros_like(acc_ref)
    rkc = _dot(ltri_ref[...], ohc.astype(BF16))
    rkc = jnp.where(ohc > 0.0, rkc, -1.0)
    rkr = jnp.transpose(rkc)
    cnt = jnp.sum(ohc, axis=0, keepdims=True)
    lane = lax.broadcasted_iota(jnp.int32, (tm, 128), 1)

    for grp in range(MOE_GROUPS):
        rk_col = jnp.sum(jnp.where(lane == grp, rkc, 0.0), axis=-1, keepdims=True)
        rk_row = rkr[grp:grp + 1, :]

        def process(base, rows, kpad, grp=grp, rk_col=rk_col, rk_row=rk_row):
            r_io = lax.broadcasted_iota(jnp.int32, (rows, tm), 0).astype(F32) + base
            p = jnp.where(rk_row == r_io, 1.0, 0.0).astype(BF16)
            xg = _dot(p, xb_ref[...]).astype(BF16)
            gg = _dot(p, gs_ref[...])
            gg = gg[:, 0:128] + gg[:, 128:256] + gg[:, 256:384]
            glane = lax.broadcasted_iota(jnp.int32, (rows, 128), 1)
            hs = []
            for j in range(MOE_PER_GROUP):
                e = grp * MOE_PER_GROUP + j
                ge = jnp.sum(jnp.where(glane == e, gg, 0.0), axis=-1, keepdims=True)
                a = _dot(xg, wg_ref[0, e])
                u = _dot(xg, wu_ref[0, e])
                hs.append((_silu(a) * u * ge).astype(BF16))
            wd = wd_ref[0, grp * MOE_PER_GROUP:(grp + 1) * MOE_PER_GROUP].reshape(
                MOE_PER_GROUP * MOE_HIDDEN, D_MODEL)
            out = _dot(jnp.concatenate(hs, axis=-1), wd).astype(BF16)
            if kpad > rows:
                out = jnp.concatenate([out, jnp.zeros((kpad - rows, D_MODEL), BF16)], axis=0)
            c_io = lax.broadcasted_iota(jnp.int32, (tm, kpad), 1)
            hit = (rk_col == c_io.astype(F32) + base) & (c_io < rows)
            acc_ref[...] += _dot(jnp.where(hit, 1.0, 0.0).astype(BF16), out)

        process(0.0, MOE_CAP, 256)
        n_g = cnt[0, grp].astype(jnp.int32)
        n_ov = jnp.maximum(n_g - MOE_CAP + MOE_OV - 1, 0) // MOE_OV

        def overflow(i, _, process=process):
            process((MOE_CAP + i * MOE_OV).astype(F32), MOE_OV, MOE_OV)
            return 0

        lax.fori_loop(0, n_ov, overflow, 0)

    o_ref[...] = _layer_norm(DN_ALPHA * x + acc_ref[...], g_ref[...], b_ref[...])


def _moe_ln(x, wr, br, wg, wu, wd, layer, g, b, tm=512):
    t = x.shape[0]
    tm = min(tm, t)
    idx = jnp.arange(tm)
    ltri = (idx[:, None] > idx[None, :]).astype(BF16)
    row = pl.BlockSpec((tm, D_MODEL), lambda i: (i, 0))
    full = lambda a: pl.BlockSpec(a.shape, lambda i: (0,) * a.ndim)
    layer_w = lambda a: pl.BlockSpec((1,) + a.shape[1:], lambda i: (layer, 0, 0, 0),
                                     pipeline_mode=pl.Buffered(1))
    return pl.pallas_call(
        _moe_kernel,
        grid=(t // tm,),
        in_specs=[row, full(wr), full(br), full(ltri), layer_w(wg), layer_w(wu), layer_w(wd), full(g), full(b)],
        out_specs=row,
        out_shape=jax.ShapeDtypeStruct((t, D_MODEL), F32),
        scratch_shapes=[pltpu.VMEM((tm, D_MODEL), F32), pltpu.VMEM((tm, D_MODEL), BF16),
                        pltpu.VMEM((tm, 384), BF16)],
        compiler_params=_cparams("arbitrary"),
        name="moe_ln",
    )(x, wr, br, ltri, wg, wu, wd, g, b)


def _s5_kernel(*refs, tlen, ngrp, chained, precise):
    if chained:
        u_ref, wb_ref, wc_ref, lam_ref, d_ref, y_ref, ore_ref, oim_ref, bu_ref, h_ref, cr_ref, ci_ref = refs
        u = jnp.swapaxes(u_ref[...], 0, 1).reshape(8 * tlen, 128)
    else:
        (u_ref, h0r_ref, h0i_ref, wb_ref, wc_ref, lam_ref, d_ref,
         y_ref, ore_ref, oim_ref, bu_ref, h_ref, cr_ref, ci_ref) = refs
        gr = 8 * tlen
        u = jnp.concatenate(
            [jnp.swapaxes(u_ref[g * gr:(g + 1) * gr, :].reshape(8, tlen, 128), 0, 1).reshape(gr, 128)
             for g in range(ngrp)], axis=0)
    if precise:
        bu_ref[...] = _dot_hi(u, wb_ref[0])
    else:
        bu_ref[...] = _dot(u.astype(BF16), wb_ref[0].astype(BF16))
    ns = S5_SLAB_STATE
    lr = jnp.broadcast_to(lam_ref[0, 0:1, :], (8, ns))
    li = jnp.broadcast_to(lam_ref[0, 1:2, :], (8, ns))

    def step(tile, hr, hi):
        rows = pl.ds(pl.multiple_of(tile * 8, 8), 8)
        nr = lr * hr - li * hi + bu_ref[rows, 0:ns]
        ni = lr * hi + li * hr + bu_ref[rows, ns:2 * ns]
        h_ref[rows, 0:ns] = nr
        h_ref[rows, ns:2 * ns] = ni
        return nr, ni

    if chained:
        @pl.when(pl.program_id(2) == 0)
        def _():
            cr_ref[...] = jnp.zeros_like(cr_ref)
            ci_ref[...] = jnp.zeros_like(ci_ref)

        hr, hi = lax.fori_loop(0, tlen, lambda t, c: step(t, *c), (cr_ref[...], ci_ref[...]), unroll=8)
        cr_ref[...] = hr
        ci_ref[...] = hi
        ore_ref[...] = hr
        oim_ref[...] = hi
    else:
        cr_ref[...] = jnp.transpose(h0r_ref[0])
        ci_ref[...] = jnp.transpose(h0i_ref[0])

        def group(g, _):
            s0 = pl.multiple_of(g * 8, 8)
            hr, hi = cr_ref[pl.ds(s0, 8), :], ci_ref[pl.ds(s0, 8), :]
            for t in range(tlen):
                hr, hi = step(g * tlen + t, hr, hi)
            cr_ref[pl.ds(s0, 8), :] = hr
            ci_ref[pl.ds(s0, 8), :] = hi
            return 0

        lax.fori_loop(0, ngrp, group, 0)
        ore_ref[...] = jnp.transpose(cr_ref[...])
        oim_ref[...] = jnp.transpose(ci_ref[...])
    y = _dot(h_ref[...].astype(BF16), wc_ref[0].astype(BF16)) + d_ref[0] * u
    if chained:
        y_ref[...] = jnp.swapaxes(y.reshape(tlen, 8, 128), 0, 1)
    else:
        gr = 8 * tlen
        for g in range(ngrp):
            y_ref[g * gr:(g + 1) * gr, :] = jnp.swapaxes(
                y[g * gr:(g + 1) * gr, :].reshape(tlen, 8, 128), 0, 1).reshape(gr, 128)


def _s5_mixer_prompt(z, params, seq_len, tlen=128):
    t = z.shape[0]
    nseq = t // seq_len
    ns = S5_SLAB_STATE
    z3 = z.reshape(nseq, seq_len, z.shape[1])
    slab = lambda a: pl.BlockSpec((1,) + a.shape[1:], lambda j, b, i: (j,) + (0,) * (a.ndim - 1))
    u_spec = pl.BlockSpec((8, tlen, 128), lambda j, b, i: (b, i, j))
    st_spec = pl.BlockSpec((8, ns), lambda j, b, i: (b, j))
    st_shape = jax.ShapeDtypeStruct((nseq, S5_SLABS * ns), F32)
    rows = 8 * tlen
    y, hre, him = pl.pallas_call(
        functools.partial(_s5_kernel, tlen=tlen, ngrp=1, chained=True, precise=False),
        grid=(S5_SLABS, nseq // 8, seq_len // tlen),
        in_specs=[u_spec] + [slab(a) for a in params],
        out_specs=[u_spec, st_spec, st_spec],
        out_shape=[jax.ShapeDtypeStruct((nseq, seq_len, MIX_WIDTH), F32), st_shape, st_shape],
        scratch_shapes=[pltpu.VMEM((rows, 2 * ns), F32), pltpu.VMEM((rows, 2 * ns), F32),
                        pltpu.VMEM((8, ns), F32), pltpu.VMEM((8, ns), F32)],
        compiler_params=_cparams("parallel", "parallel", "arbitrary"),
        name="s5_mixer",
    )(z3, *params)
    return y.reshape(t, MIX_WIDTH), hre, him


def _s5_mixer_sample(z, params, seq_len, h0r, h0i, layer):
    t = z.shape[0]
    nseq = t // seq_len
    ns = S5_SLAB_STATE
    seqs = 128
    rows = seqs * seq_len
    slab = lambda a: pl.BlockSpec((1,) + a.shape[1:], lambda j, i: (j,) + (0,) * (a.ndim - 1))
    u_spec = pl.BlockSpec((rows, 128), lambda j, i: (i, j))
    h0_spec = pl.BlockSpec((1, ns, seqs), lambda j, i: (layer, j, i))
    st_spec = pl.BlockSpec((ns, seqs), lambda j, i: (j, i))
    st_shape = jax.ShapeDtypeStruct((S5_SLABS * ns, nseq), F32)
    return pl.pallas_call(
        functools.partial(_s5_kernel, tlen=seq_len, ngrp=seqs // 8, chained=False, precise=True),
        grid=(S5_SLABS, nseq // seqs),
        in_specs=[u_spec, h0_spec, h0_spec] + [slab(a) for a in params],
        out_specs=[u_spec, st_spec, st_spec],
        out_shape=[jax.ShapeDtypeStruct((t, MIX_WIDTH), F32), st_shape, st_shape],
        scratch_shapes=[pltpu.VMEM((rows, 2 * ns), F32), pltpu.VMEM((rows, 2 * ns), F32),
                        pltpu.VMEM((seqs, ns), F32), pltpu.VMEM((seqs, ns), F32)],
        compiler_params=_cparams("parallel", "parallel"),
        name="s5_mixer",
    )(z, h0r, h0i, *params)


def _s5_params(lam_re, lam_im, log_dt, b_re, b_im, c_re, c_im, d_skip):
    lre, lim = lam_re.astype(F32), lam_im.astype(F32)
    dt = jnp.exp(log_dt.astype(F32))[:, None]
    mag = jnp.exp(lre * dt)
    bar_re, bar_im = mag * jnp.cos(lim * dt), mag * jnp.sin(lim * dt)
    nre, nim = bar_re - 1.0, bar_im
    den = lre * lre + lim * lim
    fre, fim = (nre * lre + nim * lim) / den, (nim * lre - nre * lim) / den
    bre, bim = b_re.astype(F32), b_im.astype(F32)
    bb_re = fre[..., None] * bre - fim[..., None] * bim
    bb_im = fre[..., None] * bim + fim[..., None] * bre
    eye = jnp.eye(8, dtype=F32)
    gps = 8

    def blockdiag_in(m):
        m = m.reshape(S5_SLABS, gps, S5_STATE, S5_GROUP).transpose(0, 1, 3, 2)
        return jnp.einsum('jgcp,gh->jgchp', m, eye).reshape(S5_SLABS, 128, S5_SLAB_STATE)

    def blockdiag_out(m):
        m = m.reshape(S5_SLABS, gps, S5_GROUP, S5_STATE).transpose(0, 1, 3, 2)
        return jnp.einsum('jgpc,gh->jgphc', m, eye).reshape(S5_SLABS, S5_SLAB_STATE, 128)

    wb = jnp.concatenate([blockdiag_in(bb_re), blockdiag_in(bb_im)], axis=2)
    wc = jnp.concatenate([blockdiag_out(c_re.astype(F32)), blockdiag_out(-c_im.astype(F32))], axis=1)
    lam = jnp.stack([bar_re.reshape(S5_SLABS, S5_SLAB_STATE), bar_im.reshape(S5_SLABS, S5_SLAB_STATE)], axis=1)
    dsk = d_skip.astype(F32).reshape(S5_SLABS, 1, 128)
    return wb, wc, lam, dsk


def _head_norm_gate(o, gn, gate):
    col = lax.broadcasted_iota(jnp.int32, o.shape, 1)
    valid = col < RET_DV
    mu = jnp.sum(o, axis=-1, keepdims=True) * (1.0 / RET_DV)
    d = jnp.where(valid, o - mu, 0.0)
    var = jnp.sum(d * d, axis=-1, keepdims=True) * (1.0 / RET_DV)
    return d * lax.rsqrt(var + NORM_EPS) * gn * _silu(gate)


def _load_states(s_in_ref, spad_ref, nseg, dk):
    spad_ref[...] = jnp.zeros_like(spad_ref)
    for s in range(nseg):
        for h in range(HEADS):
            spad_ref[h, s * DK_PAD:s * DK_PAD + dk, 0:RET_DV] = s_in_ref[s, h]


def _ret_kernel(*refs, nseg, gammas):
    if nseg == 1:
        (q_ref, k_ref, v_ref, g_ref, cos_ref, sin_ref, dmat_ref, inner_ref, zeta_ref, gn_ref,
         o_ref, so_ref, st_ref) = refs

        @pl.when(pl.program_id(1) == 0)
        def _():
            st_ref[...] = jnp.zeros_like(st_ref)
    else:
        (q_ref, k_ref, v_ref, g_ref, cos_ref, sin_ref, dmat_ref, inner_ref, zeta_ref, gn_ref, s_in_ref,
         o_ref, so_ref, st_ref) = refs
        st_ref[:, RET_DV:DV_PAD, :] = jnp.zeros((HEADS, DV_PAD - RET_DV, nseg * RET_DK), F32)
        for s in range(nseg):
            for h in range(HEADS):
                st_ref[h, 0:RET_DV, s * RET_DK:(s + 1) * RET_DK] = s_in_ref[s, h]
    r = q_ref.shape[0]
    seg_len = r // nseg
    cos = cos_ref[...]
    sin = sin_ref[...]
    rot = lambda x: x * cos + pltpu.roll(x, RET_DK // 2, axis=1) * sin
    rowseg = lax.broadcasted_iota(jnp.int32, (r, DK_PAD), 0) // seg_len
    colseg = lax.broadcasted_iota(jnp.int32, (DV_PAD, r), 1) // seg_len
    for h in range(HEADS):
        q = rot(q_ref[:, h * DK_PAD:(h + 1) * DK_PAD])
        k = rot(k_ref[:, h * DK_PAD:(h + 1) * DK_PAD]) * (RET_DK ** -0.5)
        v = v_ref[:, h * DV_PAD:(h + 1) * DV_PAD]
        vb = v.astype(BF16)
        qb = q.astype(BF16)
        att = _dot_nt(qb, k.astype(BF16)) * dmat_ref[h]
        o = _dot(att.astype(BF16), vb)
        st_all = st_ref[h]
        if nseg == 1:
            qbd = qb
        else:
            qbd = jnp.concatenate([jnp.where(rowseg == s, q, 0.0) for s in range(nseg)], axis=1).astype(BF16)
        o = o + _dot_nt(qbd, st_all.astype(BF16)) * inner_ref[h]
        kzb = (k * zeta_ref[h]).astype(BF16)
        vt = jnp.transpose(v)
        for s in range(nseg):
            vts = vt if nseg == 1 else jnp.where(colseg == s, vt, 0.0)
            st_new = st_all[:, s * RET_DK:(s + 1) * RET_DK] * gammas[h] + _dot(vts.astype(BF16), kzb)
            if nseg == 1:
                st_ref[h] = st_new
                so_ref[0, h] = st_new[0:RET_DV, :]
            else:
                so_ref[s, h] = st_new[0:RET_DV, :]
        o_ref[:, h * DV_PAD:(h + 1) * DV_PAD] = _head_norm_gate(
            o, gn_ref[:, h * DV_PAD:(h + 1) * DV_PAD], g_ref[:, h * DV_PAD:(h + 1) * DV_PAD])


def _ret_consts(seg_len, nseg, pos):
    hh = jnp.arange(RET_HEADS, dtype=F32)
    log_g = jnp.log1p(-jnp.exp2(-5.0 - hh))
    idx = jnp.arange(seg_len, dtype=F32)
    rel = idx[:, None] - idx[None, :]
    dmat = jnp.where(rel >= 0, jnp.exp(log_g[:, None, None] * jnp.maximum(rel, 0.0)), 0.0)
    inner = jnp.exp(log_g[:, None] * (idx + 1.0))
    zeta = jnp.exp(log_g[:, None] * (seg_len - 1.0 - idx))
    r = seg_len * nseg
    seg = jnp.arange(r) // seg_len
    same = (seg[:, None] == seg[None, :]).astype(F32)
    dmat = jnp.tile(dmat, (1, nseg, nseg)) * same[None]
    inner = jnp.broadcast_to(jnp.tile(inner, (1, nseg))[:, :, None], (RET_HEADS, r, DV_PAD))
    zeta = jnp.broadcast_to(jnp.tile(zeta, (1, nseg))[:, :, None], (RET_HEADS, r, DK_PAD))
    half = RET_DK // 2
    inv_freq = ROPE_BASE ** (-jnp.arange(half, dtype=F32) / half)
    ang = pos.astype(F32)[:, None] * inv_freq
    cos2 = jnp.concatenate([jnp.cos(ang), jnp.cos(ang)], axis=-1)
    sin2 = jnp.concatenate([-jnp.sin(ang), jnp.sin(ang)], axis=-1)
    return dmat, inner, zeta, cos2, sin2


def _ret_gammas(seg_len):
    return tuple(float((1.0 - 2.0 ** (-5.0 - h)) ** seg_len) for h in range(RET_HEADS))


def _ret_mixer(z, gn, seq_len, pos, s0=None):
    t = z.shape[0]
    nseq = t // seq_len
    if s0 is None:
        r, nseg, seg_len = RET_CHUNK, 1, RET_CHUNK
        steps = seq_len // r
        grid = (nseq, steps)
        rowmap = lambda c: (lambda n, i: (n * steps + i, c))
        tabmap = lambda n, i: (i, 0)
        stmap = lambda n, i: (n, 0, 0, 0)
        st_rows = 1
        sem = ("parallel", "arbitrary")
    else:
        seg_len = seq_len
        nseg = 128 // seg_len
        r = 128
        grid = (t // r,)
        rowmap = lambda c: (lambda i: (i, c))
        tabmap = lambda i: (0, 0)
        stmap = lambda i: (i, 0, 0, 0)
        st_rows = nseg
        sem = ("arbitrary",)
    dmat, inner, zeta, cos2, sin2 = _ret_consts(seg_len, nseg, pos)
    if s0 is not None:
        cos2 = jnp.tile(cos2, (nseg, 1))
        sin2 = jnp.tile(sin2, (nseg, 1))
    full = lambda a: pl.BlockSpec(a.shape, lambda *_: (0,) * a.ndim)
    st_spec = pl.BlockSpec((st_rows, RET_HEADS, RET_DV, RET_DK), stmap)
    in_specs = [pl.BlockSpec((r, 512), rowmap(0)), pl.BlockSpec((r, 512), rowmap(1)),
                pl.BlockSpec((r, 1024), rowmap(1)), pl.BlockSpec((r, 1024), rowmap(2)),
                pl.BlockSpec((r, DK_PAD), tabmap), pl.BlockSpec((r, DK_PAD), tabmap),
                full(dmat), full(inner), full(zeta), full(gn)]
    args = [z, z, z, z, cos2, sin2, dmat, inner, zeta, gn]
    if s0 is not None:
        in_specs.append(st_spec)
        args.append(s0)
    mix, s_new = pl.pallas_call(
        functools.partial(_ret_kernel, nseg=nseg, gammas=_ret_gammas(seg_len)),
        grid=grid,
        in_specs=in_specs,
        out_specs=[pl.BlockSpec((r, HEADS * DV_PAD), rowmap(0)), st_spec],
        out_shape=[jax.ShapeDtypeStruct((t, HEADS * DV_PAD), F32),
                   jax.ShapeDtypeStruct((nseq, RET_HEADS, RET_DV, RET_DK), F32)],
        scratch_shapes=[pltpu.VMEM((HEADS, DV_PAD, nseg * RET_DK), F32)],
        compiler_params=_cparams(*sem),
        name="retention",
    )(*args)
    return mix, s_new


def _log_sigmoid(x):
    return jnp.minimum(x, 0.0) - jnp.log1p(jnp.exp(-jnp.abs(x)))


def _gla_kernel(*refs, nseg):
    if nseg == 1:
        (q_ref, k_ref, v_ref, r_ref, low_ref, tril_ref, same_ref, wg_ref, bg_ref, gn_ref,
         o_ref, so_ref, spad_ref) = refs

        @pl.when(pl.program_id(1) == 0)
        def _():
            spad_ref[...] = jnp.zeros_like(spad_ref)
    else:
        (q_ref, k_ref, v_ref, r_ref, low_ref, tril_ref, same_ref, wg_ref, bg_ref, gn_ref, s_in_ref,
         o_ref, so_ref, spad_ref) = refs
        _load_states(s_in_ref, spad_ref, nseg, GLA_DK)
    r = q_ref.shape[0]
    seg_len = r // nseg
    rp = max(r, 128)
    tril = tril_ref[...]
    causal = tril > 0.0
    la = _log_sigmoid(_dot_hi(low_ref[...], wg_ref[...]) + bg_ref[...]) * (1.0 / GLA_TAU)
    b = _dot_hi(tril, la)
    b_tot = _dot_hi(same_ref[...], la)
    rowseg = lax.broadcasted_iota(jnp.int32, (r, DK_PAD), 0) // seg_len
    colseg = lax.broadcasted_iota(jnp.int32, (DK_PAD, rp), 1) // seg_len

    def pad_rows(x):
        if rp == r:
            return x
        return jnp.concatenate([x, jnp.zeros((rp - r, x.shape[1]), x.dtype)], axis=0)

    for h in range(HEADS):
        sl = slice(h * DK_PAD, (h + 1) * DK_PAD)
        bh = b[:, sl]
        lah = la[:, sl]
        q = q_ref[:, sl] * (GLA_DK ** -0.5)
        k = k_ref[:, sl]
        vb = v_ref[:, h * DV_PAD:(h + 1) * DV_PAD].astype(BF16)
        q_t = q * jnp.exp(bh)
        k_t = k * jnp.exp(-bh)
        att = jnp.where(causal, _dot_nt(q_t.astype(BF16), k_t.astype(BF16)), 0.0)
        o = _dot(att.astype(BF16), vb)
        s_all = spad_ref[h]
        if nseg == 1:
            qbd = q_t.astype(BF16)
        else:
            qbd = jnp.concatenate([jnp.where(rowseg == s, q_t, 0.0) for s in range(nseg)], axis=1).astype(BF16)
        o = o + _dot(qbd, s_all.astype(BF16))
        k_dec = k * jnp.exp(b_tot[:, sl] - bh)
        kdt = jnp.transpose(pad_rows(k_dec))
        lat = jnp.transpose(pad_rows(lah))
        vbp = pad_rows(vb)
        for s in range(nseg):
            if nseg == 1:
                kds, las = kdt, lat
            else:
                kds = jnp.where(colseg == s, kdt, 0.0)
                las = jnp.where(colseg == s, lat, 0.0)
            decay = jnp.exp(jnp.sum(las, axis=-1, keepdims=True))
            s_new = s_all[s * DK_PAD:(s + 1) * DK_PAD] * decay + _dot(kds.astype(BF16), vbp)
            if nseg == 1:
                spad_ref[h] = s_new
                so_ref[0, h] = s_new[0:GLA_DK, 0:GLA_DV]
            else:
                so_ref[s, h] = s_new[0:GLA_DK, 0:GLA_DV]
        o_ref[:, h * DV_PAD:(h + 1) * DV_PAD] = _head_norm_gate(
            o, gn_ref[:, h * DV_PAD:(h + 1) * DV_PAD], r_ref[:, h * DV_PAD:(h + 1) * DV_PAD])


def _gla_mixer(z, wg, bg, gn, seq_len, s0=None):
    t = z.shape[0]
    nseq = t // seq_len
    if s0 is None:
        r, nseg, seg_len = GLA_CHUNK, 1, GLA_CHUNK
        steps = seq_len // r
        grid = (nseq, steps)
        rowmap = lambda c: (lambda n, i: (n * steps + i, c))
        stmap = lambda n, i: (n, 0, 0, 0)
        st_rows = 1
        sem = ("parallel", "arbitrary")
    else:
        seg_len = seq_len
        nseg = 128 // seg_len
        r = 128
        grid = (t // r,)
        rowmap = lambda c: (lambda i: (i, c))
        stmap = lambda i: (i, 0, 0, 0)
        st_rows = nseg
        sem = ("arbitrary",)
    seg = jnp.arange(r) // seg_len
    idx = jnp.arange(r)
    same = (seg[:, None] == seg[None, :]).astype(F32)
    tril = same * (idx[:, None] >= idx[None, :]).astype(F32)
    full = lambda a: pl.BlockSpec(a.shape, lambda *_: (0,) * a.ndim)
    st_spec = pl.BlockSpec((st_rows, GLA_HEADS, GLA_DK, GLA_DV), stmap)
    in_specs = [pl.BlockSpec((r, 512), rowmap(0)), pl.BlockSpec((r, 512), rowmap(1)),
                pl.BlockSpec((r, 1024), rowmap(1)), pl.BlockSpec((r, 1024), rowmap(2)),
                pl.BlockSpec((r, 128), rowmap(COL_LOW // 128)),
                full(tril), full(same), full(wg), full(bg), full(gn)]
    args = [z, z, z, z, z, tril, same, wg, bg, gn]
    if s0 is not None:
        in_specs.append(st_spec)
        args.append(s0)
    mix, s_new = pl.pallas_call(
        functools.partial(_gla_kernel, nseg=nseg),
        grid=grid,
        in_specs=in_specs,
        out_specs=[pl.BlockSpec((r, HEADS * DV_PAD), rowmap(0)), st_spec],
        out_shape=[jax.ShapeDtypeStruct((t, HEADS * DV_PAD), F32),
                   jax.ShapeDtypeStruct((nseq, GLA_HEADS, GLA_DK, GLA_DV), F32)],
        scratch_shapes=[pltpu.VMEM((HEADS, nseg * DK_PAD, DV_PAD), F32)],
        compiler_params=_cparams(*sem),
        name="gla",
    )(*args)
    return mix, s_new


def _pad_heads(w, width, pad):
    lead = w.shape[:-1]
    w = w.reshape(lead + (HEADS, width))
    w = jnp.pad(w, [(0, 0)] * len(lead) + [(0, 0), (0, pad - width)])
    return w.reshape(lead + (HEADS * pad,))


def _pad_head_rows(w, width, pad):
    return jnp.swapaxes(_pad_heads(jnp.swapaxes(w, 0, 1), width, pad), 0, 1)


def _ret_in_weight(w):
    qk = RET_HEADS * RET_DK
    v = _pad_heads(w[:, 2 * qk:2 * qk + MIX_WIDTH], RET_DV, DV_PAD)
    g = _pad_heads(w[:, 2 * qk + MIX_WIDTH:2 * qk + 2 * MIX_WIDTH], RET_DV, DV_PAD)
    return jnp.concatenate([w[:, :2 * qk], v, g, w[:, -XA_WIDTH:]], axis=1).astype(BF16)


def _gla_in_weight(w):
    qk = GLA_HEADS * GLA_DK
    q = _pad_heads(w[:, :qk], GLA_DK, DK_PAD)
    k = _pad_heads(w[:, qk:2 * qk], GLA_DK, DK_PAD)
    v = _pad_heads(w[:, 2 * qk:2 * qk + MIX_WIDTH], GLA_DV, DV_PAD)
    r = _pad_heads(w[:, 2 * qk + MIX_WIDTH:2 * qk + 2 * MIX_WIDTH], GLA_DV, DV_PAD)
    low = jnp.pad(w[:, 2 * qk + 2 * MIX_WIDTH:2 * qk + 2 * MIX_WIDTH + GLA_RANK], [(0, 0), (0, 128 - GLA_RANK)])
    return jnp.concatenate([q, k, v, r, w[:, -XA_WIDTH:], low], axis=1).astype(BF16)


def kernel(x_prompt, x_sample, mem_prompt, state_s5_re, state_s5_im, state_ret, state_gla,
           cache_mem_k, cache_mem_v,
           w_in_s5, s5_lam_re, s5_lam_im, s5_log_dt, s5_b_re, s5_b_im, s5_c_re, s5_c_im,
           s5_d, s5_w_glu, s5_b_glu,
           w_in_ret, ret_gn,
           w_in_gla, gla_w_gate2, gla_b_gate2, gla_gn,
           w_mem_k, w_mem_v, w_out, ln_g, ln_b,
           moe_w_grp, moe_b_grp, moe_w_exp, moe_b_exp, moe_w_gate, moe_w_up, moe_w_down):
    bp, lp, _ = x_prompt.shape
    bs, ls, _ = x_sample.shape
    xp = x_prompt.reshape(bp * lp, D_MODEL)
    xs = x_sample.reshape(bs * ls, D_MODEL)
    pos_p = jnp.arange(lp, dtype=jnp.int32)
    pos_s = PAST_LEN + jnp.arange(ls, dtype=jnp.int32)

    to_t = lambda c: jnp.transpose(c, (0, 1, 3, 4, 2)).reshape(DEPTH, c.shape[1], XA_WIDTH, N_MEM)
    from_t = lambda c: jnp.transpose(c.reshape(DEPTH, c.shape[1], XA_HEADS, XA_HEAD_DIM, N_MEM), (0, 1, 4, 2, 3))
    mkt_all, mvt_all = _mem_kv(mem_prompt, jnp.swapaxes(w_mem_k, 1, 2).astype(BF16),
                               jnp.swapaxes(w_mem_v, 1, 2).astype(BF16))
    cache_kt = to_t(cache_mem_k)
    cache_vt = to_t(cache_mem_v)
    ret_t = jnp.swapaxes(state_ret, 3, 4)
    s5_t = lambda s: jnp.transpose(s, (0, 2, 3, 1)).reshape(s.shape[0], S5_GROUPS * S5_STATE, bs)
    s5_re_t, s5_im_t = s5_t(state_s5_re), s5_t(state_s5_im)
    moe_wg, moe_wu, moe_wd = moe_w_gate.astype(BF16), moe_w_up.astype(BF16), moe_w_down.astype(BF16)

    p_re, p_im, p_ret, p_gla = [], [], [], []
    s_re, s_im, s_ret, s_gla = [], [], [], []
    for i in range(DEPTH):
        kind, j = i % N_MIXERS, i // N_MIXERS
        ln1 = (ln_g[i, 0].reshape(1, D_MODEL), ln_b[i, 0].reshape(1, D_MODEL))
        ln2 = (ln_g[i, 1].reshape(1, D_MODEL), ln_b[i, 1].reshape(1, D_MODEL))
        wo = w_out[i]
        wo_att = wo[MIX_WIDTH:].astype(BF16)
        glu = {}
        if kind == 0:
            params = _s5_params(s5_lam_re[j], s5_lam_im[j], s5_log_dt[j], s5_b_re[j], s5_b_im[j],
                                s5_c_re[j], s5_c_im[j], s5_d[j])
            zp = _matmul(xp, w_in_s5[j].astype(BF16), 512)
            zs = _matmul(xs, w_in_s5[j], 512, precise=True)
            mix_p, a_re, a_im = _s5_mixer_prompt(zp, params, lp)
            mix_s, b_re, b_im = _s5_mixer_sample(zs, params, ls, s5_re_t, s5_im_t, j)
            p_re.append(a_re), p_im.append(a_im), s_re.append(b_re), s_im.append(b_im)
            xa_col = MIX_WIDTH
            wo_mix = wo[:MIX_WIDTH].astype(BF16)
            glu = dict(glu_w=s5_w_glu[j].astype(BF16), glu_b=s5_b_glu[j].reshape(1, MIX_WIDTH))
        elif kind == 1:
            w = _ret_in_weight(w_in_ret[j])
            gn = _pad_heads(ret_gn[j], RET_DV, DV_PAD).reshape(1, HEADS * DV_PAD)
            zp = _matmul(xp, w, 512)
            zs = _matmul(xs, w, 512)
            mix_p, a_s = _ret_mixer(zp, gn, lp, pos_p)
            mix_s, b_s = _ret_mixer(zs, gn, ls, pos_s, s0=ret_t[j])
            p_ret.append(a_s), s_ret.append(b_s)
            xa_col = COL_XA
            wo_mix = _pad_head_rows(wo[:MIX_WIDTH], RET_DV, DV_PAD).astype(BF16)
        else:
            w = _gla_in_weight(w_in_gla[j])
            gn = _pad_heads(gla_gn[j], GLA_DV, DV_PAD).reshape(1, HEADS * DV_PAD)
            wg = jnp.pad(_pad_heads(gla_w_gate2[j], GLA_DK, DK_PAD), [(0, 128 - GLA_RANK), (0, 0)])
            bg = _pad_heads(gla_b_gate2[j], GLA_DK, DK_PAD).reshape(1, HEADS * DK_PAD)
            zp = _matmul(xp, w, 512)
            zs = _matmul(xs, w, 512)
            mix_p, a_s = _gla_mixer(zp, wg, bg, gn, lp)
            mix_s, b_s = _gla_mixer(zs, wg, bg, gn, ls, s0=state_gla[j])
            p_gla.append(a_s), s_gla.append(b_s)
            xa_col = COL_XA
            wo_mix = _pad_head_rows(wo[:MIX_WIDTH], GLA_DV, DV_PAD).astype(BF16)
        att_p = _mem_attention(zp, xa_col, mkt_all, mvt_all, i, lp, rl=min(512, lp), nseq=1)
        att_s = _mem_attention(zs, xa_col, cache_kt, cache_vt, i, ls, rl=ls, nseq=128 // ls)
        xp = _out_proj_ln(mix_p, att_p, xp, wo_mix, wo_att, *ln1, **glu)
        xs = _out_proj_ln(mix_s, att_s, xs, wo_mix, wo_att, *ln1, **glu)

        wr = jnp.pad(jnp.concatenate([moe_w_exp[i], moe_w_grp[i]], axis=1),
                     [(0, 0), (0, 128 - MOE_EXPERTS - MOE_GROUPS)]).astype(F32)
        br = jnp.pad(jnp.concatenate([moe_b_exp[i], moe_b_grp[i]]),
                     [(0, 128 - MOE_EXPERTS - MOE_GROUPS)]).reshape(1, 128).astype(F32)
        xp = _moe_ln(xp, wr, br, moe_wg, moe_wu, moe_wd, i, *ln2)
        xs = _moe_ln(xs, wr, br, moe_wg, moe_wu, moe_wd, i, *ln2)

    st = lambda xs_: jnp.stack(xs_)
    s5_p = lambda xs_: st(xs_).reshape(len(xs_), bp, S5_GROUPS, S5_STATE)
    s5_s = lambda xs_: jnp.transpose(st(xs_).reshape(len(xs_), S5_GROUPS, S5_STATE, bs), (0, 3, 1, 2))
    ret_out = lambda xs_: jnp.swapaxes(st(xs_), 3, 4)
    return (xp.reshape(bp, lp, D_MODEL), xs.reshape(bs, ls, D_MODEL),
            s5_p(p_re), s5_p(p_im), ret_out(p_ret), st(p_gla),
            from_t(mkt_all), from_t(mvt_all),
            s5_s(s_re), s5_s(s_im), ret_out(s_ret), st(s_gla))
```

```python
import functools
import math

import jax
import jax.numpy as jnp
from jax import lax
from jax.experimental import pallas as pl
from jax.experimental.pallas import tpu as pltpu

F32 = jnp.float32
BF16 = jnp.bfloat16
HI = lax.Precision.HIGHEST

D_MODEL = 1024
DEPTH = 4
PAST_LEN = 16384
N_MIXERS = 3
MIX_WIDTH = 768
XA_HEADS = 4
XA_HEAD_DIM = 64
XA_WIDTH = 256
N_MEM = 256
S5_GROUP = 16
S5_GROUPS = 48
S5_STATE = 64
S5_SLABS = 6
S5_SLAB_STATE = 512
RET_HEADS = 4
RET_DK = 128
RET_DV = 192
RET_CHUNK = 128
ROPE_BASE = 10000.0
GLA_HEADS = 4
GLA_DK = 96
GLA_DV = 192
GLA_RANK = 16
GLA_TAU = 16.0
GLA_CHUNK = 64
MOE_GROUPS = 4
MOE_PER_GROUP = 4
MOE_EXPERTS = 16
MOE_HIDDEN = 256
DN_ALPHA = (2 * DEPTH) ** 0.25
NORM_EPS = 1e-5

DK_PAD = 128
DV_PAD = 256
HEADS = 4
COL_Q, COL_K, COL_V, COL_G, COL_XA, COL_LOW = 0, 512, 1024, 2048, 3072, 3328
RET_IN_PAD = 3328
GLA_IN_PAD = 3456
VMEM_LIMIT = 52 * 1024 * 1024


def _cparams(*sem):
    return pltpu.CompilerParams(dimension_semantics=sem, vmem_limit_bytes=VMEM_LIMIT)


def _dot(a, b):
    return jnp.dot(a, b, preferred_element_type=F32)


def _dot_nt(a, b):
    return lax.dot_general(a, b, (((1,), (1,)), ((), ())), preferred_element_type=F32)


def _dot_hi(a, b):
    return jnp.dot(a, b, preferred_element_type=F32, precision=HI)


def _layer_norm(y, g, b):
    mu = jnp.mean(y, axis=-1, keepdims=True)
    d = y - mu
    var = jnp.mean(d * d, axis=-1, keepdims=True)
    return d * lax.rsqrt(var + NORM_EPS) * g + b


def _sigmoid(x):
    return 1.0 / (1.0 + jnp.exp(-x))


def _silu(x):
    return x * _sigmoid(x)


def _gelu_tanh(x):
    c = math.sqrt(2.0 / math.pi)
    return 0.5 * x * (1.0 + jnp.tanh(c * (x + 0.044715 * (x * x * x))))


def _matmul_kernel(x_ref, w_ref, o_ref, *, precise):
    if precise:
        o_ref[...] = _dot_hi(x_ref[...], w_ref[...])
    else:
        o_ref[...] = _dot(x_ref[...].astype(BF16), w_ref[...])


def _matmul(x, w, tm, precise=False):
    t, k = x.shape
    tm = min(tm, t)
    n = w.shape[1]
    return pl.pallas_call(
        functools.partial(_matmul_kernel, precise=precise),
        grid=(t // tm,),
        in_specs=[pl.BlockSpec((tm, k), lambda i: (i, 0)),
                  pl.BlockSpec((k, n), lambda i: (0, 0))],
        out_specs=pl.BlockSpec((tm, n), lambda i: (i, 0)),
        out_shape=jax.ShapeDtypeStruct((t, n), F32),
        compiler_params=_cparams("parallel"),
        name="in_proj",
    )(x, w)


def _memkv_kernel(x_ref, wk_ref, wv_ref, ok_ref, ov_ref):
    xb = x_ref[0].astype(BF16)
    ok_ref[0, 0] = _dot_nt(wk_ref[0], xb)
    ov_ref[0, 0] = _dot_nt(wv_ref[0], xb)


def _mem_kv(mem, wkt, wvt):
    n = mem.shape[0]
    spec_w = pl.BlockSpec((1, XA_WIDTH, D_MODEL), lambda i, b: (i, 0, 0))
    spec_o = pl.BlockSpec((1, 1, XA_WIDTH, N_MEM), lambda i, b: (i, b, 0, 0))
    return pl.pallas_call(
        _memkv_kernel,
        grid=(DEPTH, n),
        in_specs=[pl.BlockSpec((1, N_MEM, D_MODEL), lambda i, b: (b, 0, 0)), spec_w, spec_w],
        out_specs=[spec_o, spec_o],
        out_shape=[jax.ShapeDtypeStruct((DEPTH, n, XA_WIDTH, N_MEM), F32)] * 2,
        compiler_params=_cparams("parallel", "parallel"),
        name="mem_kv",
    )(mem, wkt, wvt)


def _attn_kernel(q_ref, kt_ref, vt_ref, o_ref, *, nseq, rl):
    head = lax.broadcasted_iota(jnp.int32, (rl, XA_WIDTH), 1) // XA_HEAD_DIM
    for s in range(nseq):
        q = q_ref[s * rl:(s + 1) * rl, :] * (XA_HEAD_DIM ** -0.5)
        qs = jnp.concatenate([jnp.where(head == h, q, 0.0) for h in range(XA_HEADS)], axis=0)
        sc = _dot(qs.astype(BF16), kt_ref[0, s].astype(BF16))
        sc = sc - jnp.max(sc, axis=-1, keepdims=True)
        p = jnp.exp(sc)
        p = p / jnp.sum(p, axis=-1, keepdims=True)
        pv = _dot_nt(p.astype(BF16), vt_ref[0, s].astype(BF16))
        o = jnp.where(head == 0, pv[0:rl], 0.0)
        for h in range(1, XA_HEADS):
            o = o + jnp.where(head == h, pv[h * rl:(h + 1) * rl], 0.0)
        o_ref[s * rl:(s + 1) * rl, :] = o


def _mem_attention(z, xa_col, mkt, mvt, layer, seq_len, rl, nseq):
    t = z.shape[0]
    rows = rl * nseq
    cb = xa_col // XA_WIDTH
    if nseq == 1:
        per_seq = seq_len // rl
        kv_map = lambda i: (layer, i // per_seq, 0, 0)
    else:
        kv_map = lambda i: (layer, i, 0, 0)
    return pl.pallas_call(
        functools.partial(_attn_kernel, nseq=nseq, rl=rl),
        grid=(t // rows,),
        in_specs=[pl.BlockSpec((rows, XA_WIDTH), lambda i: (i, cb)),
                  pl.BlockSpec((1, nseq, XA_WIDTH, N_MEM), kv_map),
                  pl.BlockSpec((1, nseq, XA_WIDTH, N_MEM), kv_map)],
        out_specs=pl.BlockSpec((rows, XA_WIDTH), lambda i: (i, 0)),
        out_shape=jax.ShapeDtypeStruct((t, XA_WIDTH), F32),
        compiler_params=_cparams("parallel"),
        name="mem_attention",
    )(z, mkt, mvt)


def _outproj_kernel(*refs, glu):
    if glu:
        mix_ref, att_ref, x_ref, wm_ref, wa_ref, g_ref, b_ref, wg_ref, bg_ref, o_ref = refs
        y = _gelu_tanh(mix_ref[...])
        mix = y * _sigmoid(_dot(y.astype(BF16), wg_ref[...]) + bg_ref[...])
    else:
        mix_ref, att_ref, x_ref, wm_ref, wa_ref, g_ref, b_ref, o_ref = refs
        mix = mix_ref[...]
    h = _dot(mix.astype(BF16), wm_ref[...]) + _dot(att_ref[...].astype(BF16), wa_ref[...])
    o_ref[...] = _layer_norm(DN_ALPHA * x_ref[...] + h, g_ref[...], b_ref[...])


def _out_proj_ln(mix, att, x, wm, wa, g, b, glu_w=None, glu_b=None, tm=512):
    t = x.shape[0]
    tm = min(tm, t)
    km = mix.shape[1]
    row = lambda w: pl.BlockSpec((tm, w), lambda i: (i, 0))
    full = lambda a: pl.BlockSpec(a.shape, lambda i: (0,) * a.ndim)
    args = [mix, att, x, wm, wa, g, b]
    specs = [row(km), row(XA_WIDTH), row(D_MODEL), full(wm), full(wa), full(g), full(b)]
    if glu_w is not None:
        args += [glu_w, glu_b]
        specs += [full(glu_w), full(glu_b)]
    return pl.pallas_call(
        functools.partial(_outproj_kernel, glu=glu_w is not None),
        grid=(t // tm,),
        in_specs=specs,
        out_specs=row(D_MODEL),
        out_shape=jax.ShapeDtypeStruct((t, D_MODEL), F32),
        compiler_params=_cparams("parallel"),
        name="out_proj_ln",
    )(*args)


def _router_gates(lg):
    lane = lax.broadcasted_iota(jnp.int32, lg.shape, 1).astype(F32)
    ninf = jnp.float32(-jnp.inf)
    gm = (lane >= MOE_EXPERTS) & (lane < MOE_EXPERTS + MOE_GROUPS)
    gl = jnp.where(gm, lg, ninf)
    ge = jnp.exp(gl - jnp.max(gl, axis=-1, keepdims=True))
    gp = ge / jnp.sum(ge, axis=-1, keepdims=True)
    pmax = jnp.max(gp, axis=-1, keepdims=True)
    gi = jnp.min(jnp.where((gp == pmax) & gm, lane, 1e4), axis=-1, keepdims=True) - MOE_EXPERTS
    lo = gi * MOE_PER_GROUP
    em = (lane >= lo) & (lane < lo + MOE_PER_GROUP)
    el = jnp.where(em, lg, ninf)
    v1 = jnp.max(el, axis=-1, keepdims=True)
    i1 = jnp.min(jnp.where((el == v1) & em, lane, 1e4), axis=-1, keepdims=True)
    em2 = em & (lane != i1)
    el2 = jnp.where(em2, lg, ninf)
    v2 = jnp.max(el2, axis=-1, keepdims=True)
    i2 = jnp.min(jnp.where((el2 == v2) & em2, lane, 1e4), axis=-1, keepdims=True)
    t = jnp.exp(v2 - v1)
    w1 = 1.0 / (1.0 + t)
    w2 = t / (1.0 + t)
    gates = jnp.where(lane == i1, w1 * pmax, 0.0) + jnp.where(lane == i2, w2 * pmax, 0.0)
    return gates, jnp.where(lane == gi, 1.0, 0.0)


MOE_CAP = 160
MOE_OV = 128


def _moe_kernel(x_ref, wr_ref, br_ref, ltri_ref, wg_ref, wu_ref, wd_ref, g_ref, b_ref, o_ref,
                acc_ref, xb_ref, gs_ref):
    tm = x_ref.shape[0]
    x = x_ref[...]
    xh = x.astype(BF16)
    xl = (x - xh.astype(F32)).astype(BF16)
    lg = _dot(xh, wr_ref[...])
    lg = lg[:, 0:128] + lg[:, 128:256] + _dot(xl, wr_ref[:, 0:128]) + br_ref[...]
    gates, ohc = _router_gates(lg)
    g_hi = gates.astype(BF16).astype(F32)
    g_mid = (gates - g_hi).astype(BF16).astype(F32)
    g_lo = gates - g_hi - g_mid
    gs_ref[...] = (g_hi + pltpu.roll(g_mid, MOE_EXPERTS, axis=1) + pltpu.roll(g_lo, 2 * MOE_EXPERTS, axis=1)).astype(BF16)
    xb_ref[...] = xh
    rkc = _dot(ltri_ref[...], ohc.astype(BF16))
    rkc = jnp.where(ohc > 0.0, rkc, -1.0)
    rkr = jnp.transpose(rkc)
    cnt = jnp.sum(ohc, axis=0, keepdims=True)
    lane = lax.broadcasted_iota(jnp.int32, (tm, 128), 1)
    rk_cols = [jnp.sum(jnp.where(lane == g, rkc, 0.0), axis=-1, keepdims=True) for g in range(MOE_GROUPS)]
    rk_rows = [rkr[g:g + 1, :] for g in range(MOE_GROUPS)]

    def gather_mat(g, base, rows):
        r_io = lax.broadcasted_iota(jnp.int32, (rows, tm), 0).astype(F32) + base
        return jnp.where(rk_rows[g] == r_io, 1.0, 0.0).astype(BF16)

    def scatter_mat(g, base, kpad):
        c_io = lax.broadcasted_iota(jnp.int32, (tm, kpad), 1).astype(F32) + base
        return jnp.where(rk_cols[g] == c_io, 1.0, 0.0).astype(BF16)

    def experts(g, xg, gg, kpad):
        rows = xg.shape[0]
        glane = lax.broadcasted_iota(jnp.int32, (rows, 128), 1)
        hs = []
        for j in range(MOE_PER_GROUP):
            e = g * MOE_PER_GROUP + j
            ge = jnp.sum(jnp.where((glane % MOE_EXPERTS == e) & (glane < 3 * MOE_EXPERTS), gg, 0.0),
                         axis=-1, keepdims=True)
            hs.append((_silu(_dot(xg, wg_ref[0, e])) * _dot(xg, wu_ref[0, e]) * ge).astype(BF16))
        wd = wd_ref[0, g * MOE_PER_GROUP:(g + 1) * MOE_PER_GROUP].reshape(MOE_PER_GROUP * MOE_HIDDEN, D_MODEL)
        out = _dot(jnp.concatenate(hs, axis=-1), wd).astype(BF16)
        if kpad > rows:
            out = jnp.concatenate([out, jnp.zeros((kpad - rows, D_MODEL), BF16)], axis=0)
        return out

    kpad = 256
    p_all = jnp.concatenate([gather_mat(g, 0.0, MOE_CAP) for g in range(MOE_GROUPS)], axis=0)
    xg_all = _dot(p_all, xh).astype(BF16)
    gg_all = _dot(p_all, gs_ref[...])
    outs = [experts(g, xg_all[g * MOE_CAP:(g + 1) * MOE_CAP], gg_all[g * MOE_CAP:(g + 1) * MOE_CAP], kpad)
            for g in range(MOE_GROUPS)]
    pt_all = jnp.concatenate([scatter_mat(g, 0.0, kpad) for g in range(MOE_GROUPS)], axis=1)
    acc_ref[...] = _dot(pt_all, jnp.concatenate(outs, axis=0))

    for g in range(MOE_GROUPS):
        n_ov = jnp.maximum(cnt[0, g].astype(jnp.int32) - MOE_CAP + MOE_OV - 1, 0) // MOE_OV

        def overflow(i, _, g=g):
            base = (MOE_CAP + i * MOE_OV).astype(F32)
            p = gather_mat(g, base, MOE_OV)
            out = experts(g, _dot(p, xb_ref[...]).astype(BF16), _dot(p, gs_ref[...]), MOE_OV)
            acc_ref[...] += _dot(scatter_mat(g, base, MOE_OV), out)
            return 0

        lax.fori_loop(0, n_ov, overflow, 0)

    o_ref[...] = _layer_norm(DN_ALPHA * x_ref[...] + acc_ref[...], g_ref[...], b_ref[...])


def _moe_ln(x, wr, br, wg, wu, wd, layer, g, b, tm=512):
    t = x.shape[0]
    tm = min(tm, t)
    idx = jnp.arange(tm)
    ltri = (idx[:, None] > idx[None, :]).astype(BF16)
    row = pl.BlockSpec((tm, D_MODEL), lambda i: (i, 0))
    full = lambda a: pl.BlockSpec(a.shape, lambda i: (0,) * a.ndim)
    layer_w = lambda a: pl.BlockSpec((1,) + a.shape[1:], lambda i: (layer, 0, 0, 0),
                                     pipeline_mode=pl.Buffered(1))
    return pl.pallas_call(
        _moe_kernel,
        grid=(t // tm,),
        in_specs=[row, full(wr), full(br), full(ltri), layer_w(wg), layer_w(wu), layer_w(wd), full(g), full(b)],
        out_specs=row,
        out_shape=jax.ShapeDtypeStruct((t, D_MODEL), F32),
        scratch_shapes=[pltpu.VMEM((tm, D_MODEL), F32), pltpu.VMEM((tm, D_MODEL), BF16),
                        pltpu.VMEM((tm, 128), BF16)],
        compiler_params=_cparams("arbitrary"),
        name="moe_ln",
    )(x, wr, br, ltri, wg, wu, wd, g, b)


def _s5_kernel(*refs, tlen, ngrp, chained, precise):
    if chained:
        u_ref, wb_ref, wc_ref, lam_ref, d_ref, y_ref, ore_ref, oim_ref, bu_ref, h_ref, cr_ref, ci_ref = refs
        u = jnp.swapaxes(u_ref[...], 0, 1).reshape(8 * tlen, 128)
    else:
        (u_ref, h0r_ref, h0i_ref, wb_ref, wc_ref, lam_ref, d_ref,
         y_ref, ore_ref, oim_ref, bu_ref, h_ref, cr_ref, ci_ref) = refs
        gr = 8 * tlen
        u = jnp.concatenate(
            [jnp.swapaxes(u_ref[g * gr:(g + 1) * gr, :].reshape(8, tlen, 128), 0, 1).reshape(gr, 128)
             for g in range(ngrp)], axis=0)
    if precise:
        bu_ref[...] = _dot_hi(u, wb_ref[0])
    else:
        bu_ref[...] = _dot(u.astype(BF16), wb_ref[0].astype(BF16))
    ns = S5_SLAB_STATE
    lr = jnp.broadcast_to(lam_ref[0, 0:1, :], (8, ns))
    li = jnp.broadcast_to(lam_ref[0, 1:2, :], (8, ns))

    def step(tile, hr, hi):
        rows = pl.ds(pl.multiple_of(tile * 8, 8), 8)
        nr = lr * hr - li * hi + bu_ref[rows, 0:ns]
        ni = lr * hi + li * hr + bu_ref[rows, ns:2 * ns]
        h_ref[rows, 0:ns] = nr
        h_ref[rows, ns:2 * ns] = ni
        return nr, ni

    if chained:
        @pl.when(pl.program_id(2) == 0)
        def _():
            cr_ref[...] = jnp.zeros_like(cr_ref)
            ci_ref[...] = jnp.zeros_like(ci_ref)

        hr, hi = lax.fori_loop(0, tlen, lambda t, c: step(t, *c), (cr_ref[...], ci_ref[...]), unroll=8)
        cr_ref[...] = hr
        ci_ref[...] = hi
        ore_ref[...] = hr
        oim_ref[...] = hi
    else:
        cr_ref[...] = jnp.transpose(h0r_ref[0])
        ci_ref[...] = jnp.transpose(h0i_ref[0])

        def group(g, _):
            s0 = pl.multiple_of(g * 8, 8)
            hr, hi = cr_ref[pl.ds(s0, 8), :], ci_ref[pl.ds(s0, 8), :]
            for t in range(tlen):
                hr, hi = step(g * tlen + t, hr, hi)
            cr_ref[pl.ds(s0, 8), :] = hr
            ci_ref[pl.ds(s0, 8), :] = hi
            return 0

        lax.fori_loop(0, ngrp, group, 0)
        ore_ref[...] = jnp.transpose(cr_ref[...])
        oim_ref[...] = jnp.transpose(ci_ref[...])
    y = _dot(h_ref[...].astype(BF16), wc_ref[0].astype(BF16)) + d_ref[0] * u
    if chained:
        y_ref[...] = jnp.swapaxes(y.reshape(tlen, 8, 128), 0, 1)
    else:
        gr = 8 * tlen
        for g in range(ngrp):
            y_ref[g * gr:(g + 1) * gr, :] = jnp.swapaxes(
                y[g * gr:(g + 1) * gr, :].reshape(tlen, 8, 128), 0, 1).reshape(gr, 128)


def _s5_mixer_prompt(z, params, seq_len, tlen=128):
    t = z.shape[0]
    nseq = t // seq_len
    ns = S5_SLAB_STATE
    z3 = z.reshape(nseq, seq_len, z.shape[1])
    slab = lambda a: pl.BlockSpec((1,) + a.shape[1:], lambda j, b, i: (j,) + (0,) * (a.ndim - 1))
    u_spec = pl.BlockSpec((8, tlen, 128), lambda j, b, i: (b, i, j))
    st_spec = pl.BlockSpec((8, ns), lambda j, b, i: (b, j))
    st_shape = jax.ShapeDtypeStruct((nseq, S5_SLABS * ns), F32)
    rows = 8 * tlen
    y, hre, him = pl.pallas_call(
        functools.partial(_s5_kernel, tlen=tlen, ngrp=1, chained=True, precise=False),
        grid=(S5_SLABS, nseq // 8, seq_len // tlen),
        in_specs=[u_spec] + [slab(a) for a in params],
        out_specs=[u_spec, st_spec, st_spec],
        out_shape=[jax.ShapeDtypeStruct((nseq, seq_len, MIX_WIDTH), F32), st_shape, st_shape],
        scratch_shapes=[pltpu.VMEM((rows, 2 * ns), F32), pltpu.VMEM((rows, 2 * ns), F32),
                        pltpu.VMEM((8, ns), F32), pltpu.VMEM((8, ns), F32)],
        compiler_params=_cparams("parallel", "parallel", "arbitrary"),
        name="s5_mixer",
    )(z3, *params)
    return y.reshape(t, MIX_WIDTH), hre, him


def _s5_mixer_sample(z, params, seq_len, h0r, h0i, layer):
    t = z.shape[0]
    nseq = t // seq_len
    ns = S5_SLAB_STATE
    seqs = 128
    rows = seqs * seq_len
    slab = lambda a: pl.BlockSpec((1,) + a.shape[1:], lambda j, i: (j,) + (0,) * (a.ndim - 1))
    u_spec = pl.BlockSpec((rows, 128), lambda j, i: (i, j))
    h0_spec = pl.BlockSpec((1, ns, seqs), lambda j, i: (layer, j, i))
    st_spec = pl.BlockSpec((ns, seqs), lambda j, i: (j, i))
    st_shape = jax.ShapeDtypeStruct((S5_SLABS * ns, nseq), F32)
    return pl.pallas_call(
        functools.partial(_s5_kernel, tlen=seq_len, ngrp=seqs // 8, chained=False, precise=True),
        grid=(S5_SLABS, nseq // seqs),
        in_specs=[u_spec, h0_spec, h0_spec] + [slab(a) for a in params],
        out_specs=[u_spec, st_spec, st_spec],
        out_shape=[jax.ShapeDtypeStruct((t, MIX_WIDTH), F32), st_shape, st_shape],
        scratch_shapes=[pltpu.VMEM((rows, 2 * ns), F32), pltpu.VMEM((rows, 2 * ns), F32),
                        pltpu.VMEM((seqs, ns), F32), pltpu.VMEM((seqs, ns), F32)],
        compiler_params=_cparams("parallel", "parallel"),
        name="s5_mixer",
    )(z, h0r, h0i, *params)


def _s5_params(lam_re, lam_im, log_dt, b_re, b_im, c_re, c_im, d_skip):
    lre, lim = lam_re.astype(F32), lam_im.astype(F32)
    dt = jnp.exp(log_dt.astype(F32))[:, None]
    mag = jnp.exp(lre * dt)
    bar_re, bar_im = mag * jnp.cos(lim * dt), mag * jnp.sin(lim * dt)
    nre, nim = bar_re - 1.0, bar_im
    den = lre * lre + lim * lim
    fre, fim = (nre * lre + nim * lim) / den, (nim * lre - nre * lim) / den
    bre, bim = b_re.astype(F32), b_im.astype(F32)
    bb_re = fre[..., None] * bre - fim[..., None] * bim
    bb_im = fre[..., None] * bim + fim[..., None] * bre
    eye = jnp.eye(8, dtype=F32)
    gps = 8

    def blockdiag_in(m):
        m = m.reshape(S5_SLABS, gps, S5_STATE, S5_GROUP).transpose(0, 1, 3, 2)
        return jnp.einsum('jgcp,gh->jgchp', m, eye).reshape(S5_SLABS, 128, S5_SLAB_STATE)

    def blockdiag_out(m):
        m = m.reshape(S5_SLABS, gps, S5_GROUP, S5_STATE).transpose(0, 1, 3, 2)
        return jnp.einsum('jgpc,gh->jgphc', m, eye).reshape(S5_SLABS, S5_SLAB_STATE, 128)

    wb = jnp.concatenate([blockdiag_in(bb_re), blockdiag_in(bb_im)], axis=2)
    wc = jnp.concatenate([blockdiag_out(c_re.astype(F32)), blockdiag_out(-c_im.astype(F32))], axis=1)
    lam = jnp.stack([bar_re.reshape(S5_SLABS, S5_SLAB_STATE), bar_im.reshape(S5_SLABS, S5_SLAB_STATE)], axis=1)
    dsk = d_skip.astype(F32).reshape(S5_SLABS, 1, 128)
    return wb, wc, lam, dsk


def _head_norm_gate(o, gn, gate):
    col = lax.broadcasted_iota(jnp.int32, o.shape, 1)
    valid = col < RET_DV
    mu = jnp.sum(o, axis=-1, keepdims=True) * (1.0 / RET_DV)
    d = jnp.where(valid, o - mu, 0.0)
    var = jnp.sum(d * d, axis=-1, keepdims=True) * (1.0 / RET_DV)
    return d * lax.rsqrt(var + NORM_EPS) * gn * _silu(gate)


def _load_states(s_in_ref, spad_ref, nseg, dk):
    spad_ref[...] = jnp.zeros_like(spad_ref)
    for s in range(nseg):
        for h in range(HEADS):
            spad_ref[h, s * DK_PAD:s * DK_PAD + dk, 0:RET_DV] = s_in_ref[s, h]


def _ret_kernel(*refs, nseg, gammas):
    if nseg == 1:
        (q_ref, k_ref, v_ref, g_ref, cos_ref, sin_ref, dmat_ref, inner_ref, zeta_ref, gn_ref,
         o_ref, so_ref, st_ref) = refs

        @pl.when(pl.program_id(1) == 0)
        def _():
            st_ref[...] = jnp.zeros_like(st_ref)
    else:
        (q_ref, k_ref, v_ref, g_ref, cos_ref, sin_ref, dmat_ref, inner_ref, zeta_ref, gn_ref, s_in_ref,
         o_ref, so_ref, st_ref) = refs
        st_ref[:, RET_DV:DV_PAD, :] = jnp.zeros((HEADS, DV_PAD - RET_DV, nseg * RET_DK), F32)
        for s in range(nseg):
            for h in range(HEADS):
                st_ref[h, 0:RET_DV, s * RET_DK:(s + 1) * RET_DK] = s_in_ref[s, h]
    r = q_ref.shape[0]
    seg_len = r // nseg
    cos = cos_ref[...]
    sin = sin_ref[...]
    rot = lambda x: x * cos + pltpu.roll(x, RET_DK // 2, axis=1) * sin
    rowseg = lax.broadcasted_iota(jnp.int32, (r, DK_PAD), 0) // seg_len
    colseg = lax.broadcasted_iota(jnp.int32, (DV_PAD, r), 1) // seg_len
    for h in range(HEADS):
        q = rot(q_ref[:, h * DK_PAD:(h + 1) * DK_PAD])
        k = rot(k_ref[:, h * DK_PAD:(h + 1) * DK_PAD]) * (RET_DK ** -0.5)
        v = v_ref[:, h * DV_PAD:(h + 1) * DV_PAD]
        vb = v.astype(BF16)
        qb = q.astype(BF16)
        att = _dot_nt(qb, k.astype(BF16)) * dmat_ref[h]
        o = _dot(att.astype(BF16), vb)
        st_all = st_ref[h]
        if nseg == 1:
            qbd = qb
        else:
            qbd = jnp.concatenate([jnp.where(rowseg == s, q, 0.0) for s in range(nseg)], axis=1).astype(BF16)
        o = o + _dot_nt(qbd, st_all.astype(BF16)) * inner_ref[h]
        kzb = (k * zeta_ref[h]).astype(BF16)
        vt = jnp.transpose(v)
        for s in range(nseg):
            vts = vt if nseg == 1 else jnp.where(colseg == s, vt, 0.0)
            st_new = st_all[:, s * RET_DK:(s + 1) * RET_DK] * gammas[h] + _dot(vts.astype(BF16), kzb)
            if nseg == 1:
                st_ref[h] = st_new
                so_ref[0, h] = st_new[0:RET_DV, :]
            else:
                so_ref[s, h] = st_new[0:RET_DV, :]
        o_ref[:, h * DV_PAD:(h + 1) * DV_PAD] = _head_norm_gate(
            o, gn_ref[:, h * DV_PAD:(h + 1) * DV_PAD], g_ref[:, h * DV_PAD:(h + 1) * DV_PAD])


def _ret_consts(seg_len, nseg, pos):
    hh = jnp.arange(RET_HEADS, dtype=F32)
    log_g = jnp.log1p(-jnp.exp2(-5.0 - hh))
    idx = jnp.arange(seg_len, dtype=F32)
    rel = idx[:, None] - idx[None, :]
    dmat = jnp.where(rel >= 0, jnp.exp(log_g[:, None, None] * jnp.maximum(rel, 0.0)), 0.0)
    inner = jnp.exp(log_g[:, None] * (idx + 1.0))
    zeta = jnp.exp(log_g[:, None] * (seg_len - 1.0 - idx))
    r = seg_len * nseg
    seg = jnp.arange(r) // seg_len
    same = (seg[:, None] == seg[None, :]).astype(F32)
    dmat = jnp.tile(dmat, (1, nseg, nseg)) * same[None]
    inner = jnp.broadcast_to(jnp.tile(inner, (1, nseg))[:, :, None], (RET_HEADS, r, DV_PAD))
    zeta = jnp.broadcast_to(jnp.tile(zeta, (1, nseg))[:, :, None], (RET_HEADS, r, DK_PAD))
    half = RET_DK // 2
    inv_freq = ROPE_BASE ** (-jnp.arange(half, dtype=F32) / half)
    ang = pos.astype(F32)[:, None] * inv_freq
    cos2 = jnp.concatenate([jnp.cos(ang), jnp.cos(ang)], axis=-1)
    sin2 = jnp.concatenate([-jnp.sin(ang), jnp.sin(ang)], axis=-1)
    return dmat, inner, zeta, cos2, sin2


def _ret_gammas(seg_len):
    return tuple(float((1.0 - 2.0 ** (-5.0 - h)) ** seg_len) for h in range(RET_HEADS))


def _ret_mixer(z, gn, seq_len, pos, s0=None):
    t = z.shape[0]
    nseq = t // seq_len
    if s0 is None:
        r, nseg, seg_len = RET_CHUNK, 1, RET_CHUNK
        steps = seq_len // r
        grid = (nseq, steps)
        rowmap = lambda c: (lambda n, i: (n * steps + i, c))
        tabmap = lambda n, i: (i, 0)
        stmap = lambda n, i: (n, 0, 0, 0)
        st_rows = 1
        sem = ("parallel", "arbitrary")
    else:
        seg_len = seq_len
        nseg = 128 // seg_len
        r = 128
        grid = (t // r,)
        rowmap = lambda c: (lambda i: (i, c))
        tabmap = lambda i: (0, 0)
        stmap = lambda i: (i, 0, 0, 0)
        st_rows = nseg
        sem = ("arbitrary",)
    dmat, inner, zeta, cos2, sin2 = _ret_consts(seg_len, nseg, pos)
    if s0 is not None:
        cos2 = jnp.tile(cos2, (nseg, 1))
        sin2 = jnp.tile(sin2, (nseg, 1))
    full = lambda a: pl.BlockSpec(a.shape, lambda *_: (0,) * a.ndim)
    st_spec = pl.BlockSpec((st_rows, RET_HEADS, RET_DV, RET_DK), stmap)
    in_specs = [pl.BlockSpec((r, 512), rowmap(0)), pl.BlockSpec((r, 512), rowmap(1)),
                pl.BlockSpec((r, 1024), rowmap(1)), pl.BlockSpec((r, 1024), rowmap(2)),
                pl.BlockSpec((r, DK_PAD), tabmap), pl.BlockSpec((r, DK_PAD), tabmap),
                full(dmat), full(inner), full(zeta), full(gn)]
    args = [z, z, z, z, cos2, sin2, dmat, inner, zeta, gn]
    if s0 is not None:
        in_specs.append(st_spec)
        args.append(s0)
    mix, s_new = pl.pallas_call(
        functools.partial(_ret_kernel, nseg=nseg, gammas=_ret_gammas(seg_len)),
        grid=grid,
        in_specs=in_specs,
        out_specs=[pl.BlockSpec((r, HEADS * DV_PAD), rowmap(0)), st_spec],
        out_shape=[jax.ShapeDtypeStruct((t, HEADS * DV_PAD), F32),
                   jax.ShapeDtypeStruct((nseq, RET_HEADS, RET_DV, RET_DK), F32)],
        scratch_shapes=[pltpu.VMEM((HEADS, DV_PAD, nseg * RET_DK), F32)],
        compiler_params=_cparams(*sem),
        name="retention",
    )(*args)
    return mix, s_new


def _log_sigmoid(x):
    return jnp.minimum(x, 0.0) - jnp.log1p(jnp.exp(-jnp.abs(x)))


def _gla_kernel(*refs, nseg):
    if nseg == 1:
        (q_ref, k_ref, v_ref, r_ref, low_ref, tril_ref, same_ref, wg_ref, bg_ref, gn_ref,
         o_ref, so_ref, spad_ref) = refs

        @pl.when(pl.program_id(1) == 0)
        def _():
            spad_ref[...] = jnp.zeros_like(spad_ref)
    else:
        (q_ref, k_ref, v_ref, r_ref, low_ref, tril_ref, same_ref, wg_ref, bg_ref, gn_ref, s_in_ref,
         o_ref, so_ref, spad_ref) = refs
        _load_states(s_in_ref, spad_ref, nseg, GLA_DK)
    r = q_ref.shape[0]
    seg_len = r // nseg
    rp = max(r, 128)
    tril = tril_ref[...]
    causal = tril > 0.0
    la = _log_sigmoid(_dot_hi(low_ref[...], wg_ref[...]) + bg_ref[...]) * (1.0 / GLA_TAU)
    b = _dot_hi(tril, la)
    b_tot = _dot_hi(same_ref[...], la)
    rowseg = lax.broadcasted_iota(jnp.int32, (r, DK_PAD), 0) // seg_len
    colseg = lax.broadcasted_iota(jnp.int32, (DK_PAD, rp), 1) // seg_len

    def pad_rows(x):
        if rp == r:
            return x
        return jnp.concatenate([x, jnp.zeros((rp - r, x.shape[1]), x.dtype)], axis=0)

    for h in range(HEADS):
        sl = slice(h * DK_PAD, (h + 1) * DK_PAD)
        bh = b[:, sl]
        lah = la[:, sl]
        q = q_ref[:, sl] * (GLA_DK ** -0.5)
        k = k_ref[:, sl]
        vb = v_ref[:, h * DV_PAD:(h + 1) * DV_PAD].astype(BF16)
        q_t = q * jnp.exp(bh)
        k_t = k * jnp.exp(-bh)
        att = jnp.where(causal, _dot_nt(q_t.astype(BF16), k_t.astype(BF16)), 0.0)
        o = _dot(att.astype(BF16), vb)
        s_all = spad_ref[h]
        if nseg == 1:
            qbd = q_t.astype(BF16)
        else:
            qbd = jnp.concatenate([jnp.where(rowseg == s, q_t, 0.0) for s in range(nseg)], axis=1).astype(BF16)
        o = o + _dot(qbd, s_all.astype(BF16))
        k_dec = k * jnp.exp(b_tot[:, sl] - bh)
        kdt = jnp.transpose(pad_rows(k_dec))
        lat = jnp.transpose(pad_rows(lah))
        vbp = pad_rows(vb)
        for s in range(nseg):
            if nseg == 1:
                kds, las = kdt, lat
            else:
                kds = jnp.where(colseg == s, kdt, 0.0)
                las = jnp.where(colseg == s, lat, 0.0)
            decay = jnp.exp(jnp.sum(las, axis=-1, keepdims=True))
            s_new = s_all[s * DK_PAD:(s + 1) * DK_PAD] * decay + _dot(kds.astype(BF16), vbp)
            if nseg == 1:
                spad_ref[h] = s_new
                so_ref[0, h] = s_new[0:GLA_DK, 0:GLA_DV]
            else:
                so_ref[s, h] = s_new[0:GLA_DK, 0:GLA_DV]
        o_ref[:, h * DV_PAD:(h + 1) * DV_PAD] = _head_norm_gate(
            o, gn_ref[:, h * DV_PAD:(h + 1) * DV_PAD], r_ref[:, h * DV_PAD:(h + 1) * DV_PAD])


def _gla_mixer(z, wg, bg, gn, seq_len, s0=None):
    t = z.shape[0]
    nseq = t // seq_len
    if s0 is None:
        r, nseg, seg_len = GLA_CHUNK, 1, GLA_CHUNK
        steps = seq_len // r
        grid = (nseq, steps)
        rowmap = lambda c: (lambda n, i: (n * steps + i, c))
        stmap = lambda n, i: (n, 0, 0, 0)
        st_rows = 1
        sem = ("parallel", "arbitrary")
    else:
        seg_len = seq_len
        nseg = 128 // seg_len
        r = 128
        grid = (t // r,)
        rowmap = lambda c: (lambda i: (i, c))
        stmap = lambda i: (i, 0, 0, 0)
        st_rows = nseg
        sem = ("arbitrary",)
    seg = jnp.arange(r) // seg_len
    idx = jnp.arange(r)
    same = (seg[:, None] == seg[None, :]).astype(F32)
    tril = same * (idx[:, None] >= idx[None, :]).astype(F32)
    full = lambda a: pl.BlockSpec(a.shape, lambda *_: (0,) * a.ndim)
    st_spec = pl.BlockSpec((st_rows, GLA_HEADS, GLA_DK, GLA_DV), stmap)
    in_specs = [pl.BlockSpec((r, 512), rowmap(0)), pl.BlockSpec((r, 512), rowmap(1)),
                pl.BlockSpec((r, 1024), rowmap(1)), pl.BlockSpec((r, 1024), rowmap(2)),
                pl.BlockSpec((r, 128), rowmap(COL_LOW // 128)),
                full(tril), full(same), full(wg), full(bg), full(gn)]
    args = [z, z, z, z, z, tril, same, wg, bg, gn]
    if s0 is not None:
        in_specs.append(st_spec)
        args.append(s0)
    mix, s_new = pl.pallas_call(
        functools.partial(_gla_kernel, nseg=nseg),
        grid=grid,
        in_specs=in_specs,
        out_specs=[pl.BlockSpec((r, HEADS * DV_PAD), rowmap(0)), st_spec],
        out_shape=[jax.ShapeDtypeStruct((t, HEADS * DV_PAD), F32),
                   jax.ShapeDtypeStruct((nseq, GLA_HEADS, GLA_DK, GLA_DV), F32)],
        scratch_shapes=[pltpu.VMEM((HEADS, nseg * DK_PAD, DV_PAD), F32)],
        compiler_params=_cparams(*sem),
        name="gla",
    )(*args)
    return mix, s_new


def _pad_heads(w, width, pad):
    lead = w.shape[:-1]
    w = w.reshape(lead + (HEADS, width))
    w = jnp.pad(w, [(0, 0)] * len(lead) + [(0, 0), (0, pad - width)])
    return w.reshape(lead + (HEADS * pad,))


def _pad_head_rows(w, width, pad):
    return jnp.swapaxes(_pad_heads(jnp.swapaxes(w, 0, 1), width, pad), 0, 1)


def _ret_in_weight(w):
    qk = RET_HEADS * RET_DK
    v = _pad_heads(w[:, 2 * qk:2 * qk + MIX_WIDTH], RET_DV, DV_PAD)
    g = _pad_heads(w[:, 2 * qk + MIX_WIDTH:2 * qk + 2 * MIX_WIDTH], RET_DV, DV_PAD)
    return jnp.concatenate([w[:, :2 * qk], v, g, w[:, -XA_WIDTH:]], axis=1).astype(BF16)


def _gla_in_weight(w):
    qk = GLA_HEADS * GLA_DK
    q = _pad_heads(w[:, :qk], GLA_DK, DK_PAD)
    k = _pad_heads(w[:, qk:2 * qk], GLA_DK, DK_PAD)
    v = _pad_heads(w[:, 2 * qk:2 * qk + MIX_WIDTH], GLA_DV, DV_PAD)
    r = _pad_heads(w[:, 2 * qk + MIX_WIDTH:2 * qk + 2 * MIX_WIDTH], GLA_DV, DV_PAD)
    low = jnp.pad(w[:, 2 * qk + 2 * MIX_WIDTH:2 * qk + 2 * MIX_WIDTH + GLA_RANK], [(0, 0), (0, 128 - GLA_RANK)])
    return jnp.concatenate([q, k, v, r, w[:, -XA_WIDTH:], low], axis=1).astype(BF16)


def kernel(x_prompt, x_sample, mem_prompt, state_s5_re, state_s5_im, state_ret, state_gla,
           cache_mem_k, cache_mem_v,
           w_in_s5, s5_lam_re, s5_lam_im, s5_log_dt, s5_b_re, s5_b_im, s5_c_re, s5_c_im,
           s5_d, s5_w_glu, s5_b_glu,
           w_in_ret, ret_gn,
           w_in_gla, gla_w_gate2, gla_b_gate2, gla_gn,
           w_mem_k, w_mem_v, w_out, ln_g, ln_b,
           moe_w_grp, moe_b_grp, moe_w_exp, moe_b_exp, moe_w_gate, moe_w_up, moe_w_down):
    bp, lp, _ = x_prompt.shape
    bs, ls, _ = x_sample.shape
    xp = x_prompt.reshape(bp * lp, D_MODEL)
    xs = x_sample.reshape(bs * ls, D_MODEL)
    pos_p = jnp.arange(lp, dtype=jnp.int32)
    pos_s = PAST_LEN + jnp.arange(ls, dtype=jnp.int32)

    to_t = lambda c: jnp.transpose(c, (0, 1, 3, 4, 2)).reshape(DEPTH, c.shape[1], XA_WIDTH, N_MEM)
    from_t = lambda c: jnp.transpose(c.reshape(DEPTH, c.shape[1], XA_HEADS, XA_HEAD_DIM, N_MEM), (0, 1, 4, 2, 3))
    mkt_all, mvt_all = _mem_kv(mem_prompt, jnp.swapaxes(w_mem_k, 1, 2).astype(BF16),
                               jnp.swapaxes(w_mem_v, 1, 2).astype(BF16))
    cache_kt = to_t(cache_mem_k)
    cache_vt = to_t(cache_mem_v)
    ret_t = jnp.swapaxes(state_ret, 3, 4)
    s5_t = lambda s: jnp.transpose(s, (0, 2, 3, 1)).reshape(s.shape[0], S5_GROUPS * S5_STATE, bs)
    s5_re_t, s5_im_t = s5_t(state_s5_re), s5_t(state_s5_im)
    moe_wg, moe_wu, moe_wd = moe_w_gate.astype(BF16), moe_w_up.astype(BF16), moe_w_down.astype(BF16)

    p_re, p_im, p_ret, p_gla = [], [], [], []
    s_re, s_im, s_ret, s_gla = [], [], [], []
    for i in range(DEPTH):
        kind, j = i % N_MIXERS, i // N_MIXERS
        ln1 = (ln_g[i, 0].reshape(1, D_MODEL), ln_b[i, 0].reshape(1, D_MODEL))
        ln2 = (ln_g[i, 1].reshape(1, D_MODEL), ln_b[i, 1].reshape(1, D_MODEL))
        wo = w_out[i]
        wo_att = wo[MIX_WIDTH:].astype(BF16)
        glu = {}
        if kind == 0:
            params = _s5_params(s5_lam_re[j], s5_lam_im[j], s5_log_dt[j], s5_b_re[j], s5_b_im[j],
                                s5_c_re[j], s5_c_im[j], s5_d[j])
            zp = _matmul(xp, w_in_s5[j].astype(BF16), 512)
            zs = _matmul(xs, w_in_s5[j], 512, precise=True)
            mix_p, a_re, a_im = _s5_mixer_prompt(zp, params, lp)
            mix_s, b_re, b_im = _s5_mixer_sample(zs, params, ls, s5_re_t, s5_im_t, j)
            p_re.append(a_re), p_im.append(a_im), s_re.append(b_re), s_im.append(b_im)
            xa_col = MIX_WIDTH
            wo_mix = wo[:MIX_WIDTH].astype(BF16)
            glu = dict(glu_w=s5_w_glu[j].astype(BF16), glu_b=s5_b_glu[j].reshape(1, MIX_WIDTH))
        elif kind == 1:
            w = _ret_in_weight(w_in_ret[j])
            gn = _pad_heads(ret_gn[j], RET_DV, DV_PAD).reshape(1, HEADS * DV_PAD)
            zp = _matmul(xp, w, 512)
            zs = _matmul(xs, w, 512)
            mix_p, a_s = _ret_mixer(zp, gn, lp, pos_p)
            mix_s, b_s = _ret_mixer(zs, gn, ls, pos_s, s0=ret_t[j])
            p_ret.append(a_s), s_ret.append(b_s)
            xa_col = COL_XA
            wo_mix = _pad_head_rows(wo[:MIX_WIDTH], RET_DV, DV_PAD).astype(BF16)
        else:
            w = _gla_in_weight(w_in_gla[j])
            gn = _pad_heads(gla_gn[j], GLA_DV, DV_PAD).reshape(1, HEADS * DV_PAD)
            wg = jnp.pad(_pad_heads(gla_w_gate2[j], GLA_DK, DK_PAD), [(0, 128 - GLA_RANK), (0, 0)])
            bg = _pad_heads(gla_b_gate2[j], GLA_DK, DK_PAD).reshape(1, HEADS * DK_PAD)
            zp = _matmul(xp, w, 512)
            zs = _matmul(xs, w, 512)
            mix_p, a_s = _gla_mixer(zp, wg, bg, gn, lp)
            mix_s, b_s = _gla_mixer(zs, wg, bg, gn, ls, s0=state_gla[j])
            p_gla.append(a_s), s_gla.append(b_s)
            xa_col = COL_XA
            wo_mix = _pad_head_rows(wo[:MIX_WIDTH], GLA_DV, DV_PAD).astype(BF16)
        att_p = _mem_attention(zp, xa_col, mkt_all, mvt_all, i, lp, rl=min(512, lp), nseq=1)
        att_s = _mem_attention(zs, xa_col, cache_kt, cache_vt, i, ls, rl=ls, nseq=128 // ls)
        xp = _out_proj_ln(mix_p, att_p, xp, wo_mix, wo_att, *ln1, **glu)
        xs = _out_proj_ln(mix_s, att_s, xs, wo_mix, wo_att, *ln1, **glu)

        wr = jnp.pad(jnp.concatenate([moe_w_exp[i], moe_w_grp[i]], axis=1),
                     [(0, 0), (0, 128 - MOE_EXPERTS - MOE_GROUPS)]).astype(F32)
        wr_hi = wr.astype(BF16)
        wr = jnp.concatenate([wr_hi, (wr - wr_hi.astype(F32)).astype(BF16)], axis=1)
        br = jnp.pad(jnp.concatenate([moe_b_exp[i], moe_b_grp[i]]),
                     [(0, 128 - MOE_EXPERTS - MOE_GROUPS)]).reshape(1, 128).astype(F32)
        xp = _moe_ln(xp, wr, br, moe_wg, moe_wu, moe_wd, i, *ln2)
        xs = _moe_ln(xs, wr, br, moe_wg, moe_wu, moe_wd, i, *ln2)

    st = lambda xs_: jnp.stack(xs_)
    s5_p = lambda xs_: st(xs_).reshape(len(xs_), bp, S5_GROUPS, S5_STATE)
    s5_s = lambda xs_: jnp.transpose(st(xs_).reshape(len(xs_), S5_GROUPS, S5_STATE, bs), (0, 3, 1, 2))
    ret_out = lambda xs_: jnp.swapaxes(st(xs_), 3, 4)
    return (xp.reshape(bp, lp, D_MODEL), xs.reshape(bs, ls, D_MODEL),
            s5_p(p_re), s5_p(p_im), ret_out(p_ret), st(p_gla),
            from_t(mkt_all), from_t(mvt_all),
            s5_s(s_re), s5_s(s_im), ret_out(s_ret), st(s_gla))
```

```python
import functools
import math

import jax
import jax.numpy as jnp
from jax import lax
from jax.experimental import pallas as pl
from jax.experimental.pallas import tpu as pltpu

F32 = jnp.float32
BF16 = jnp.bfloat16
HI = lax.Precision.HIGHEST

D_MODEL = 1024
DEPTH = 4
PAST_LEN = 16384
N_MIXERS = 3
MIX_WIDTH = 768
XA_HEADS = 4
XA_HEAD_DIM = 64
XA_WIDTH = 256
N_MEM = 256
S5_GROUP = 16
S5_GROUPS = 48
S5_STATE = 64
S5_SLABS = 6
S5_SLAB_STATE = 512
RET_HEADS = 4
RET_DK = 128
RET_DV = 192
RET_CHUNK = 128
ROPE_BASE = 10000.0
GLA_HEADS = 4
GLA_DK = 96
GLA_DV = 192
GLA_RANK = 16
GLA_TAU = 16.0
GLA_CHUNK = 64
MOE_GROUPS = 4
MOE_PER_GROUP = 4
MOE_EXPERTS = 16
MOE_HIDDEN = 256
DN_ALPHA = (2 * DEPTH) ** 0.25
NORM_EPS = 1e-5

DK_PAD = 128
DV_PAD = 256
HEADS = 4
COL_Q, COL_K, COL_V, COL_G, COL_XA, COL_LOW = 0, 512, 1024, 2048, 3072, 3328
RET_IN_PAD = 3328
GLA_IN_PAD = 3456
VMEM_LIMIT = 52 * 1024 * 1024


def _cparams(*sem):
    return pltpu.CompilerParams(dimension_semantics=sem, vmem_limit_bytes=VMEM_LIMIT)


def _dot(a, b):
    return jnp.dot(a, b, preferred_element_type=F32)


def _dot_nt(a, b):
    return lax.dot_general(a, b, (((1,), (1,)), ((), ())), preferred_element_type=F32)


def _dot_hi(a, b):
    return jnp.dot(a, b, preferred_element_type=F32, precision=HI)


def _layer_norm(y, g, b):
    mu = jnp.mean(y, axis=-1, keepdims=True)
    d = y - mu
    var = jnp.mean(d * d, axis=-1, keepdims=True)
    return d * lax.rsqrt(var + NORM_EPS) * g + b


def _sigmoid(x):
    return 1.0 / (1.0 + jnp.exp(-x))


def _silu(x):
    return x * _sigmoid(x)


def _gelu_tanh(x):
    c = math.sqrt(2.0 / math.pi)
    return 0.5 * x * (1.0 + jnp.tanh(c * (x + 0.044715 * (x * x * x))))


def _matmul_kernel(x_ref, w_ref, o_ref, *, precise):
    if precise:
        o_ref[...] = _dot_hi(x_ref[...], w_ref[...])
    else:
        o_ref[...] = _dot(x_ref[...].astype(BF16), w_ref[...])


def _matmul(x, w, tm, precise=False):
    t, k = x.shape
    tm = min(tm, t)
    n = w.shape[1]
    return pl.pallas_call(
        functools.partial(_matmul_kernel, precise=precise),
        grid=(t // tm,),
        in_specs=[pl.BlockSpec((tm, k), lambda i: (i, 0)),
                  pl.BlockSpec((k, n), lambda i: (0, 0))],
        out_specs=pl.BlockSpec((tm, n), lambda i: (i, 0)),
        out_shape=jax.ShapeDtypeStruct((t, n), F32),
        compiler_params=_cparams("parallel"),
        name="in_proj",
    )(x, w)


def _memkv_kernel(x_ref, wk_ref, wv_ref, ok_ref, ov_ref):
    xb = x_ref[0].astype(BF16)
    ok_ref[0, 0] = _dot_nt(wk_ref[0], xb)
    ov_ref[0, 0] = _dot_nt(wv_ref[0], xb)


def _mem_kv(mem, wkt, wvt):
    n = mem.shape[0]
    spec_w = pl.BlockSpec((1, XA_WIDTH, D_MODEL), lambda i, b: (i, 0, 0))
    spec_o = pl.BlockSpec((1, 1, XA_WIDTH, N_MEM), lambda i, b: (i, b, 0, 0))
    return pl.pallas_call(
        _memkv_kernel,
        grid=(DEPTH, n),
        in_specs=[pl.BlockSpec((1, N_MEM, D_MODEL), lambda i, b: (b, 0, 0)), spec_w, spec_w],
        out_specs=[spec_o, spec_o],
        out_shape=[jax.ShapeDtypeStruct((DEPTH, n, XA_WIDTH, N_MEM), F32)] * 2,
        compiler_params=_cparams("parallel", "parallel"),
        name="mem_kv",
    )(mem, wkt, wvt)


def _attn_kernel(q_ref, kt_ref, vt_ref, o_ref, *, nseq, rl):
    head = lax.broadcasted_iota(jnp.int32, (rl, XA_WIDTH), 1) // XA_HEAD_DIM
    outs = []
    for s in range(nseq):
        q = q_ref[s * rl:(s + 1) * rl, :] * (XA_HEAD_DIM ** -0.5)
        qs = jnp.concatenate([jnp.where(head == h, q, 0.0) for h in range(XA_HEADS)], axis=0)
        sc = _dot(qs.astype(BF16), kt_ref[0, s].astype(BF16))
        sc = sc - jnp.max(sc, axis=-1, keepdims=True)
        p = jnp.exp(sc)
        p = p / jnp.sum(p, axis=-1, keepdims=True)
        pv = _dot_nt(p.astype(BF16), vt_ref[0, s].astype(BF16))
        o = jnp.where(head == 0, pv[0:rl], 0.0)
        for h in range(1, XA_HEADS):
            o = o + jnp.where(head == h, pv[h * rl:(h + 1) * rl], 0.0)
        outs.append(o)
    o_ref[...] = jnp.concatenate(outs, axis=0).astype(o_ref.dtype)


def _mem_attention(z, xa_col, mkt, mvt, layer, seq_len, rl, nseq):
    t = z.shape[0]
    rows = rl * nseq
    cb = xa_col // XA_WIDTH
    if nseq == 1:
        per_seq = seq_len // rl
        kv_map = lambda i: (layer, i // per_seq, 0, 0)
    else:
        kv_map = lambda i: (layer, i, 0, 0)
    return pl.pallas_call(
        functools.partial(_attn_kernel, nseq=nseq, rl=rl),
        grid=(t // rows,),
        in_specs=[pl.BlockSpec((rows, XA_WIDTH), lambda i: (i, cb)),
                  pl.BlockSpec((1, nseq, XA_WIDTH, N_MEM), kv_map),
                  pl.BlockSpec((1, nseq, XA_WIDTH, N_MEM), kv_map)],
        out_specs=pl.BlockSpec((rows, XA_WIDTH), lambda i: (i, 0)),
        out_shape=jax.ShapeDtypeStruct((t, XA_WIDTH), BF16),
        compiler_params=_cparams("parallel"),
        name="mem_attention",
    )(z, mkt, mvt)


def _outproj_kernel(*refs, glu):
    if glu:
        mix_ref, att_ref, x_ref, wm_ref, wa_ref, g_ref, b_ref, wg_ref, bg_ref, o_ref = refs
        y = _gelu_tanh(mix_ref[...])
        mix = y * _sigmoid(_dot(y.astype(BF16), wg_ref[...]) + bg_ref[...])
    else:
        mix_ref, att_ref, x_ref, wm_ref, wa_ref, g_ref, b_ref, o_ref = refs
        mix = mix_ref[...]
    h = _dot(mix.astype(BF16), wm_ref[...]) + _dot(att_ref[...].astype(BF16), wa_ref[...])
    o_ref[...] = _layer_norm(DN_ALPHA * x_ref[...] + h, g_ref[...], b_ref[...])


def _out_proj_ln(mix, att, x, wm, wa, g, b, glu_w=None, glu_b=None, tm=512):
    t = x.shape[0]
    tm = min(tm, t)
    km = mix.shape[1]
    row = lambda w: pl.BlockSpec((tm, w), lambda i: (i, 0))
    full = lambda a: pl.BlockSpec(a.shape, lambda i: (0,) * a.ndim)
    args = [mix, att, x, wm, wa, g, b]
    specs = [row(km), row(XA_WIDTH), row(D_MODEL), full(wm), full(wa), full(g), full(b)]
    if glu_w is not None:
        args += [glu_w, glu_b]
        specs += [full(glu_w), full(glu_b)]
    return pl.pallas_call(
        functools.partial(_outproj_kernel, glu=glu_w is not None),
        grid=(t // tm,),
        in_specs=specs,
        out_specs=row(D_MODEL),
        out_shape=jax.ShapeDtypeStruct((t, D_MODEL), F32),
        compiler_params=_cparams("parallel"),
        name="out_proj_ln",
    )(*args)


def _router_gates(lg):
    lane = lax.broadcasted_iota(jnp.int32, lg.shape, 1).astype(F32)
    ninf = jnp.float32(-jnp.inf)
    gm = (lane >= MOE_EXPERTS) & (lane < MOE_EXPERTS + MOE_GROUPS)
    gl = jnp.where(gm, lg, ninf)
    ge = jnp.exp(gl - jnp.max(gl, axis=-1, keepdims=True))
    gp = ge / jnp.sum(ge, axis=-1, keepdims=True)
    pmax = jnp.max(gp, axis=-1, keepdims=True)
    gi = jnp.min(jnp.where((gp == pmax) & gm, lane, 1e4), axis=-1, keepdims=True) - MOE_EXPERTS
    lo = gi * MOE_PER_GROUP
    em = (lane >= lo) & (lane < lo + MOE_PER_GROUP)
    el = jnp.where(em, lg, ninf)
    v1 = jnp.max(el, axis=-1, keepdims=True)
    i1 = jnp.min(jnp.where((el == v1) & em, lane, 1e4), axis=-1, keepdims=True)
    em2 = em & (lane != i1)
    el2 = jnp.where(em2, lg, ninf)
    v2 = jnp.max(el2, axis=-1, keepdims=True)
    i2 = jnp.min(jnp.where((el2 == v2) & em2, lane, 1e4), axis=-1, keepdims=True)
    t = jnp.exp(v2 - v1)
    w1 = 1.0 / (1.0 + t)
    w2 = t / (1.0 + t)
    gates = jnp.where(lane == i1, w1 * pmax, 0.0) + jnp.where(lane == i2, w2 * pmax, 0.0)
    return gates, jnp.where(lane == gi, 1.0, 0.0)


MOE_CAP = 160
MOE_OV = 128


def _moe_kernel(x_ref, wr_ref, br_ref, ltri_ref, wg_ref, wu_ref, wd_ref, g_ref, b_ref, o_ref,
                acc_ref, xb_ref, gs_ref):
    tm = x_ref.shape[0]
    x = x_ref[...]
    xh = x.astype(BF16)
    xl = (x - xh.astype(F32)).astype(BF16)
    lg = _dot(xh, wr_ref[...])
    lg = lg[:, 0:128] + lg[:, 128:256] + _dot(xl, wr_ref[:, 0:128]) + br_ref[...]
    gates, ohc = _router_gates(lg)
    g_hi = gates.astype(BF16).astype(F32)
    g_mid = (gates - g_hi).astype(BF16).astype(F32)
    g_lo = gates - g_hi - g_mid
    gs_ref[...] = (g_hi + pltpu.roll(g_mid, MOE_EXPERTS, axis=1) + pltpu.roll(g_lo, 2 * MOE_EXPERTS, axis=1)).astype(BF16)
    xb_ref[...] = xh
    rkc = _dot(ltri_ref[...], ohc.astype(BF16))
    rkc = jnp.where(ohc > 0.0, rkc, -1.0)
    rkr = jnp.transpose(rkc)
    cnt = jnp.sum(ohc, axis=0, keepdims=True)
    lane = lax.broadcasted_iota(jnp.int32, (tm, 128), 1)
    rk_cols = [jnp.sum(jnp.where(lane == g, rkc, 0.0), axis=-1, keepdims=True) for g in range(MOE_GROUPS)]
    rk_rows = [rkr[g:g + 1, :] for g in range(MOE_GROUPS)]

    def gather_mat(g, base, rows):
        r_io = lax.broadcasted_iota(jnp.int32, (rows, tm), 0).astype(F32) + base
        return jnp.where(rk_rows[g] == r_io, 1.0, 0.0).astype(BF16)

    def scatter_mat(g, base, kpad):
        c_io = lax.broadcasted_iota(jnp.int32, (tm, kpad), 1).astype(F32) + base
        return jnp.where(rk_cols[g] == c_io, 1.0, 0.0).astype(BF16)

    def experts(g, xg, gg, kpad):
        rows = xg.shape[0]
        glane = lax.broadcasted_iota(jnp.int32, (rows, 128), 1)
        hs = []
        for j in range(MOE_PER_GROUP):
            e = g * MOE_PER_GROUP + j
            ge = jnp.sum(jnp.where((glane % MOE_EXPERTS == e) & (glane < 3 * MOE_EXPERTS), gg, 0.0),
                         axis=-1, keepdims=True)
            hs.append((_silu(_dot(xg, wg_ref[0, e])) * _dot(xg, wu_ref[0, e]) * ge).astype(BF16))
        wd = wd_ref[0, g * MOE_PER_GROUP:(g + 1) * MOE_PER_GROUP].reshape(MOE_PER_GROUP * MOE_HIDDEN, D_MODEL)
        out = _dot(jnp.concatenate(hs, axis=-1), wd).astype(BF16)
        if kpad > rows:
            out = jnp.concatenate([out, jnp.zeros((kpad - rows, D_MODEL), BF16)], axis=0)
        return out

    in_cap = lambda rk: (rk >= 0.0) & (rk < MOE_CAP)
    slot_col = sum(jnp.where(in_cap(rk_cols[g]), rk_cols[g] + (g * MOE_CAP + 1.0), 0.0) for g in range(MOE_GROUPS)) - 1.0
    slot_row = sum(jnp.where(in_cap(rk_rows[g]), rk_rows[g] + (g * MOE_CAP + 1.0), 0.0) for g in range(MOE_GROUPS)) - 1.0
    slots = MOE_GROUPS * MOE_CAP
    r_io = lax.broadcasted_iota(jnp.int32, (slots, tm), 0).astype(F32)
    p_all = jnp.where(slot_row == r_io, 1.0, 0.0).astype(BF16)
    xg_all = _dot(p_all, xh).astype(BF16)
    gg_all = _dot(p_all, gs_ref[...])
    outs = [experts(g, xg_all[g * MOE_CAP:(g + 1) * MOE_CAP], gg_all[g * MOE_CAP:(g + 1) * MOE_CAP], MOE_CAP)
            for g in range(MOE_GROUPS)]
    c_io = lax.broadcasted_iota(jnp.int32, (tm, slots), 1).astype(F32)
    pt_all = jnp.where(slot_col == c_io, 1.0, 0.0).astype(BF16)
    acc_ref[...] = _dot(pt_all, jnp.concatenate(outs, axis=0))

    for g in range(MOE_GROUPS):
        n_ov = jnp.maximum(cnt[0, g].astype(jnp.int32) - MOE_CAP + MOE_OV - 1, 0) // MOE_OV

        def overflow(i, _, g=g):
            base = (MOE_CAP + i * MOE_OV).astype(F32)
            p = gather_mat(g, base, MOE_OV)
            out = experts(g, _dot(p, xb_ref[...]).astype(BF16), _dot(p, gs_ref[...]), MOE_OV)
            acc_ref[...] += _dot(scatter_mat(g, base, MOE_OV), out)
            return 0

        lax.fori_loop(0, n_ov, overflow, 0)

    o_ref[...] = _layer_norm(DN_ALPHA * x_ref[...] + acc_ref[...], g_ref[...], b_ref[...])


def _moe_ln(x, wr, br, wg, wu, wd, layer, g, b, tm=512):
    t = x.shape[0]
    tm = min(tm, t)
    idx = jnp.arange(tm)
    ltri = (idx[:, None] > idx[None, :]).astype(BF16)
    row = pl.BlockSpec((tm, D_MODEL), lambda i: (i, 0))
    full = lambda a: pl.BlockSpec(a.shape, lambda i: (0,) * a.ndim)
    layer_w = lambda a: pl.BlockSpec((1,) + a.shape[1:], lambda i: (layer, 0, 0, 0),
                                     pipeline_mode=pl.Buffered(1))
    return pl.pallas_call(
        _moe_kernel,
        grid=(t // tm,),
        in_specs=[row, full(wr), full(br), full(ltri), layer_w(wg), layer_w(wu), layer_w(wd), full(g), full(b)],
        out_specs=row,
        out_shape=jax.ShapeDtypeStruct((t, D_MODEL), F32),
        scratch_shapes=[pltpu.VMEM((tm, D_MODEL), F32), pltpu.VMEM((tm, D_MODEL), BF16),
                        pltpu.VMEM((tm, 128), BF16)],
        compiler_params=_cparams("arbitrary"),
        name="moe_ln",
    )(x, wr, br, ltri, wg, wu, wd, g, b)


def _s5_kernel(*refs, tlen, ngrp, chained, precise):
    if chained:
        u_ref, wb_ref, wc_ref, lam_ref, d_ref, y_ref, ore_ref, oim_ref, bu_ref, h_ref, cr_ref, ci_ref = refs
        u = jnp.swapaxes(u_ref[...], 0, 1).reshape(8 * tlen, 128)
    else:
        (u_ref, h0r_ref, h0i_ref, wb_ref, wc_ref, lam_ref, d_ref,
         y_ref, ore_ref, oim_ref, bu_ref, h_ref, cr_ref, ci_ref) = refs
        gr = 8 * tlen
        u = jnp.concatenate(
            [jnp.swapaxes(u_ref[g * gr:(g + 1) * gr, :].reshape(8, tlen, 128), 0, 1).reshape(gr, 128)
             for g in range(ngrp)], axis=0)
    if precise:
        bu_ref[...] = _dot_hi(u, wb_ref[0])
    else:
        bu_ref[...] = _dot(u.astype(BF16), wb_ref[0].astype(BF16))
    ns = S5_SLAB_STATE
    lr = jnp.broadcast_to(lam_ref[0, 0:1, :], (8, ns))
    li = jnp.broadcast_to(lam_ref[0, 1:2, :], (8, ns))

    def step(tile, hr, hi):
        rows = pl.ds(pl.multiple_of(tile * 8, 8), 8)
        nr = lr * hr - li * hi + bu_ref[rows, 0:ns]
        ni = lr * hi + li * hr + bu_ref[rows, ns:2 * ns]
        h_ref[rows, 0:ns] = nr
        h_ref[rows, ns:2 * ns] = ni
        return nr, ni

    if chained:
        @pl.when(pl.program_id(2) == 0)
        def _():
            cr_ref[...] = jnp.zeros_like(cr_ref)
            ci_ref[...] = jnp.zeros_like(ci_ref)

        hr, hi = lax.fori_loop(0, tlen, lambda t, c: step(t, *c), (cr_ref[...], ci_ref[...]), unroll=8)
        cr_ref[...] = hr
        ci_ref[...] = hi
        ore_ref[...] = hr
        oim_ref[...] = hi
    else:
        cr_ref[...] = jnp.transpose(h0r_ref[0])
        ci_ref[...] = jnp.transpose(h0i_ref[0])

        def group(g, _):
            s0 = pl.multiple_of(g * 8, 8)
            hr, hi = cr_ref[pl.ds(s0, 8), :], ci_ref[pl.ds(s0, 8), :]
            for t in range(tlen):
                hr, hi = step(g * tlen + t, hr, hi)
            cr_ref[pl.ds(s0, 8), :] = hr
            ci_ref[pl.ds(s0, 8), :] = hi
            return 0

        lax.fori_loop(0, ngrp, group, 0)
        ore_ref[...] = jnp.transpose(cr_ref[...])
        oim_ref[...] = jnp.transpose(ci_ref[...])
    y = _dot(h_ref[...].astype(BF16), wc_ref[0].astype(BF16)) + d_ref[0] * u
    if chained:
        y_ref[...] = jnp.swapaxes(y.reshape(tlen, 8, 128), 0, 1)
    else:
        gr = 8 * tlen
        for g in range(ngrp):
            y_ref[g * gr:(g + 1) * gr, :] = jnp.swapaxes(
                y[g * gr:(g + 1) * gr, :].reshape(tlen, 8, 128), 0, 1).reshape(gr, 128)


def _s5_mixer_prompt(z, params, seq_len, tlen=128):
    t = z.shape[0]
    nseq = t // seq_len
    ns = S5_SLAB_STATE
    z3 = z.reshape(nseq, seq_len, z.shape[1])
    slab = lambda a: pl.BlockSpec((1,) + a.shape[1:], lambda j, b, i: (j,) + (0,) * (a.ndim - 1))
    u_spec = pl.BlockSpec((8, tlen, 128), lambda j, b, i: (b, i, j))
    st_spec = pl.BlockSpec((8, ns), lambda j, b, i: (b, j))
    st_shape = jax.ShapeDtypeStruct((nseq, S5_SLABS * ns), F32)
    rows = 8 * tlen
    y, hre, him = pl.pallas_call(
        functools.partial(_s5_kernel, tlen=tlen, ngrp=1, chained=True, precise=False),
        grid=(S5_SLABS, nseq // 8, seq_len // tlen),
        in_specs=[u_spec] + [slab(a) for a in params],
        out_specs=[u_spec, st_spec, st_spec],
        out_shape=[jax.ShapeDtypeStruct((nseq, seq_len, MIX_WIDTH), F32), st_shape, st_shape],
        scratch_shapes=[pltpu.VMEM((rows, 2 * ns), F32), pltpu.VMEM((rows, 2 * ns), F32),
                        pltpu.VMEM((8, ns), F32), pltpu.VMEM((8, ns), F32)],
        compiler_params=_cparams("parallel", "parallel", "arbitrary"),
        name="s5_mixer",
    )(z3, *params)
    return y.reshape(t, MIX_WIDTH), hre, him


def _s5_mixer_sample(z, params, seq_len, h0r, h0i, layer):
    t = z.shape[0]
    nseq = t // seq_len
    ns = S5_SLAB_STATE
    seqs = 128
    rows = seqs * seq_len
    slab = lambda a: pl.BlockSpec((1,) + a.shape[1:], lambda j, i: (j,) + (0,) * (a.ndim - 1))
    u_spec = pl.BlockSpec((rows, 128), lambda j, i: (i, j))
    h0_spec = pl.BlockSpec((1, ns, seqs), lambda j, i: (layer, j, i))
    st_spec = pl.BlockSpec((ns, seqs), lambda j, i: (j, i))
    st_shape = jax.ShapeDtypeStruct((S5_SLABS * ns, nseq), F32)
    return pl.pallas_call(
        functools.partial(_s5_kernel, tlen=seq_len, ngrp=seqs // 8, chained=False, precise=True),
        grid=(S5_SLABS, nseq // seqs),
        in_specs=[u_spec, h0_spec, h0_spec] + [slab(a) for a in params],
        out_specs=[u_spec, st_spec, st_spec],
        out_shape=[jax.ShapeDtypeStruct((t, MIX_WIDTH), F32), st_shape, st_shape],
        scratch_shapes=[pltpu.VMEM((rows, 2 * ns), F32), pltpu.VMEM((rows, 2 * ns), F32),
                        pltpu.VMEM((seqs, ns), F32), pltpu.VMEM((seqs, ns), F32)],
        compiler_params=_cparams("parallel", "parallel"),
        name="s5_mixer",
    )(z, h0r, h0i, *params)


def _s5_params(lam_re, lam_im, log_dt, b_re, b_im, c_re, c_im, d_skip):
    lre, lim = lam_re.astype(F32), lam_im.astype(F32)
    dt = jnp.exp(log_dt.astype(F32))[:, None]
    mag = jnp.exp(lre * dt)
    bar_re, bar_im = mag * jnp.cos(lim * dt), mag * jnp.sin(lim * dt)
    nre, nim = bar_re - 1.0, bar_im
    den = lre * lre + lim * lim
    fre, fim = (nre * lre + nim * lim) / den, (nim * lre - nre * lim) / den
    bre, bim = b_re.astype(F32), b_im.astype(F32)
    bb_re = fre[..., None] * bre - fim[..., None] * bim
    bb_im = fre[..., None] * bim + fim[..., None] * bre
    eye = jnp.eye(8, dtype=F32)
    gps = 8

    def blockdiag_in(m):
        m = m.reshape(S5_SLABS, gps, S5_STATE, S5_GROUP).transpose(0, 1, 3, 2)
        return jnp.einsum('jgcp,gh->jgchp', m, eye).reshape(S5_SLABS, 128, S5_SLAB_STATE)

    def blockdiag_out(m):
        m = m.reshape(S5_SLABS, gps, S5_GROUP, S5_STATE).transpose(0, 1, 3, 2)
        return jnp.einsum('jgpc,gh->jgphc', m, eye).reshape(S5_SLABS, S5_SLAB_STATE, 128)

    wb = jnp.concatenate([blockdiag_in(bb_re), blockdiag_in(bb_im)], axis=2)
    wc = jnp.concatenate([blockdiag_out(c_re.astype(F32)), blockdiag_out(-c_im.astype(F32))], axis=1)
    lam = jnp.stack([bar_re.reshape(S5_SLABS, S5_SLAB_STATE), bar_im.reshape(S5_SLABS, S5_SLAB_STATE)], axis=1)
    dsk = d_skip.astype(F32).reshape(S5_SLABS, 1, 128)
    return wb, wc, lam, dsk


def _head_norm_gate(o, gn, gate):
    col = lax.broadcasted_iota(jnp.int32, o.shape, 1)
    valid = col < RET_DV
    mu = jnp.sum(o, axis=-1, keepdims=True) * (1.0 / RET_DV)
    d = jnp.where(valid, o - mu, 0.0)
    var = jnp.sum(d * d, axis=-1, keepdims=True) * (1.0 / RET_DV)
    return d * lax.rsqrt(var + NORM_EPS) * gn * _silu(gate)


def _load_states(s_in_ref, spad_ref, nseg, dk):
    spad_ref[...] = jnp.zeros_like(spad_ref)
    for s in range(nseg):
        for h in range(HEADS):
            spad_ref[h, s * DK_PAD:s * DK_PAD + dk, 0:RET_DV] = s_in_ref[s, h]


def _ret_kernel(*refs, nseg, gammas):
    if nseg == 1:
        (q_ref, k_ref, v_ref, g_ref, cos_ref, sin_ref, dmat_ref, inner_ref, zeta_ref, gn_ref,
         o_ref, so_ref, st_ref) = refs

        @pl.when(pl.program_id(1) == 0)
        def _():
            st_ref[...] = jnp.zeros_like(st_ref)
    else:
        (q_ref, k_ref, v_ref, g_ref, cos_ref, sin_ref, dmat_ref, inner_ref, zeta_ref, gn_ref, s_in_ref,
         o_ref, so_ref, st_ref) = refs
        st_ref[:, RET_DV:DV_PAD, :] = jnp.zeros((HEADS, DV_PAD - RET_DV, nseg * RET_DK), F32)
        for s in range(nseg):
            for h in range(HEADS):
                st_ref[h, 0:RET_DV, s * RET_DK:(s + 1) * RET_DK] = s_in_ref[s, h]
    r = q_ref.shape[0]
    seg_len = r // nseg
    cos = cos_ref[...]
    sin = sin_ref[...]
    rot = lambda x: x * cos + pltpu.roll(x, RET_DK // 2, axis=1) * sin
    rowseg = lax.broadcasted_iota(jnp.int32, (r, DK_PAD), 0) // seg_len
    colseg = lax.broadcasted_iota(jnp.int32, (DV_PAD, r), 1) // seg_len
    for h in range(HEADS):
        q = rot(q_ref[:, h * DK_PAD:(h + 1) * DK_PAD])
        k = rot(k_ref[:, h * DK_PAD:(h + 1) * DK_PAD]) * (RET_DK ** -0.5)
        v = v_ref[:, h * DV_PAD:(h + 1) * DV_PAD]
        vb = v.astype(BF16)
        qb = q.astype(BF16)
        att = _dot_nt(qb, k.astype(BF16)) * dmat_ref[h]
        o = _dot(att.astype(BF16), vb)
        st_all = st_ref[h]
        if nseg == 1:
            qbd = qb
        else:
            qbd = jnp.concatenate([jnp.where(rowseg == s, q, 0.0) for s in range(nseg)], axis=1).astype(BF16)
        o = o + _dot_nt(qbd, st_all.astype(BF16)) * inner_ref[h]
        kzb = (k * zeta_ref[h]).astype(BF16)
        vt = jnp.transpose(v)
        for s in range(nseg):
            vts = vt if nseg == 1 else jnp.where(colseg == s, vt, 0.0)
            st_new = st_all[:, s * RET_DK:(s + 1) * RET_DK] * gammas[h] + _dot(vts.astype(BF16), kzb)
            if nseg == 1:
                st_ref[h] = st_new
                so_ref[0, h] = st_new[0:RET_DV, :]
            else:
                so_ref[s, h] = st_new[0:RET_DV, :]
        o_ref[:, h * DV_PAD:(h + 1) * DV_PAD] = _head_norm_gate(
            o, gn_ref[:, h * DV_PAD:(h + 1) * DV_PAD], g_ref[:, h * DV_PAD:(h + 1) * DV_PAD]).astype(o_ref.dtype)


def _ret_consts(seg_len, nseg, pos):
    hh = jnp.arange(RET_HEADS, dtype=F32)
    log_g = jnp.log1p(-jnp.exp2(-5.0 - hh))
    idx = jnp.arange(seg_len, dtype=F32)
    rel = idx[:, None] - idx[None, :]
    dmat = jnp.where(rel >= 0, jnp.exp(log_g[:, None, None] * jnp.maximum(rel, 0.0)), 0.0)
    inner = jnp.exp(log_g[:, None] * (idx + 1.0))
    zeta = jnp.exp(log_g[:, None] * (seg_len - 1.0 - idx))
    r = seg_len * nseg
    seg = jnp.arange(r) // seg_len
    same = (seg[:, None] == seg[None, :]).astype(F32)
    dmat = jnp.tile(dmat, (1, nseg, nseg)) * same[None]
    inner = jnp.broadcast_to(jnp.tile(inner, (1, nseg))[:, :, None], (RET_HEADS, r, DV_PAD))
    zeta = jnp.broadcast_to(jnp.tile(zeta, (1, nseg))[:, :, None], (RET_HEADS, r, DK_PAD))
    half = RET_DK // 2
    inv_freq = ROPE_BASE ** (-jnp.arange(half, dtype=F32) / half)
    ang = pos.astype(F32)[:, None] * inv_freq
    cos2 = jnp.concatenate([jnp.cos(ang), jnp.cos(ang)], axis=-1)
    sin2 = jnp.concatenate([-jnp.sin(ang), jnp.sin(ang)], axis=-1)
    return dmat, inner, zeta, cos2, sin2


def _ret_gammas(seg_len):
    return tuple(float((1.0 - 2.0 ** (-5.0 - h)) ** seg_len) for h in range(RET_HEADS))


def _ret_mixer(z, gn, seq_len, pos, s0=None):
    t = z.shape[0]
    nseq = t // seq_len
    if s0 is None:
        r, nseg, seg_len = RET_CHUNK, 1, RET_CHUNK
        steps = seq_len // r
        grid = (nseq, steps)
        rowmap = lambda c: (lambda n, i: (n * steps + i, c))
        tabmap = lambda n, i: (i, 0)
        stmap = lambda n, i: (n, 0, 0, 0)
        st_rows = 1
        sem = ("parallel", "arbitrary")
    else:
        seg_len = seq_len
        nseg = 128 // seg_len
        r = 128
        grid = (t // r,)
        rowmap = lambda c: (lambda i: (i, c))
        tabmap = lambda i: (0, 0)
        stmap = lambda i: (i, 0, 0, 0)
        st_rows = nseg
        sem = ("arbitrary",)
    dmat, inner, zeta, cos2, sin2 = _ret_consts(seg_len, nseg, pos)
    if s0 is not None:
        cos2 = jnp.tile(cos2, (nseg, 1))
        sin2 = jnp.tile(sin2, (nseg, 1))
    full = lambda a: pl.BlockSpec(a.shape, lambda *_: (0,) * a.ndim)
    st_spec = pl.BlockSpec((st_rows, RET_HEADS, RET_DV, RET_DK), stmap)
    in_specs = [pl.BlockSpec((r, 512), rowmap(0)), pl.BlockSpec((r, 512), rowmap(1)),
                pl.BlockSpec((r, 1024), rowmap(1)), pl.BlockSpec((r, 1024), rowmap(2)),
                pl.BlockSpec((r, DK_PAD), tabmap), pl.BlockSpec((r, DK_PAD), tabmap),
                full(dmat), full(inner), full(zeta), full(gn)]
    args = [z, z, z, z, cos2, sin2, dmat, inner, zeta, gn]
    if s0 is not None:
        in_specs.append(st_spec)
        args.append(s0)
    mix, s_new = pl.pallas_call(
        functools.partial(_ret_kernel, nseg=nseg, gammas=_ret_gammas(seg_len)),
        grid=grid,
        in_specs=in_specs,
        out_specs=[pl.BlockSpec((r, HEADS * DV_PAD), rowmap(0)), st_spec],
        out_shape=[jax.ShapeDtypeStruct((t, HEADS * DV_PAD), BF16),
                   jax.ShapeDtypeStruct((nseq, RET_HEADS, RET_DV, RET_DK), F32)],
        scratch_shapes=[pltpu.VMEM((HEADS, DV_PAD, nseg * RET_DK), F32)],
        compiler_params=_cparams(*sem),
        name="retention",
    )(*args)
    return mix, s_new


def _log_sigmoid(x):
    return jnp.minimum(x, 0.0) - jnp.log1p(jnp.exp(-jnp.abs(x)))


def _gla_kernel(*refs, nseg, chained):
    if chained:
        (q_ref, k_ref, v_ref, r_ref, low_ref, tril_ref, same_ref, wg_ref, bg_ref, gn_ref,
         o_ref, so_ref, spad_ref) = refs

        @pl.when(pl.program_id(1) == 0)
        def _():
            spad_ref[...] = jnp.zeros_like(spad_ref)
    else:
        (q_ref, k_ref, v_ref, r_ref, low_ref, tril_ref, same_ref, wg_ref, bg_ref, gn_ref, s_in_ref,
         o_ref, so_ref, spad_ref) = refs
        _load_states(s_in_ref, spad_ref, nseg, GLA_DK)
    r = q_ref.shape[0]
    seg_len = r // nseg
    tril = tril_ref[...]
    causal = tril > 0.0
    la = _log_sigmoid(_dot_hi(low_ref[...], wg_ref[...]) + bg_ref[...]) * (1.0 / GLA_TAU)
    b = _dot_hi(tril, la)
    b_tot = _dot_hi(same_ref[...], la)
    rowseg = lax.broadcasted_iota(jnp.int32, (r, DK_PAD), 0) // seg_len
    colseg = lax.broadcasted_iota(jnp.int32, (DK_PAD, r), 1) // seg_len

    for h in range(HEADS):
        sl = slice(h * DK_PAD, (h + 1) * DK_PAD)
        bh = b[:, sl]
        q = q_ref[:, sl] * (GLA_DK ** -0.5)
        k = k_ref[:, sl]
        vb = v_ref[:, h * DV_PAD:(h + 1) * DV_PAD].astype(BF16)
        q_t = q * jnp.exp(bh)
        k_t = k * jnp.exp(-bh)
        att = jnp.where(causal, _dot_nt(q_t.astype(BF16), k_t.astype(BF16)), 0.0)
        o = _dot(att.astype(BF16), vb)
        kdt = jnp.transpose(k * jnp.exp(b_tot[:, sl] - bh))
        lat = jnp.transpose(la[:, sl])

        def seg_update(s, s_old):
            kds = jnp.where(colseg == s, kdt, 0.0) if nseg > 1 else kdt
            las = jnp.where(colseg == s, lat, 0.0) if nseg > 1 else lat
            decay = jnp.exp(jnp.sum(las, axis=-1, keepdims=True))
            return s_old * decay + _dot(kds.astype(BF16), vb)

        if chained:
            qb = q_t.astype(BF16)
            s_cur = spad_ref[h]
            inter = []
            for s in range(nseg):
                inter.append(_dot(qb[s * seg_len:(s + 1) * seg_len], s_cur.astype(BF16)))
                s_cur = seg_update(s, s_cur)
            o = o + jnp.concatenate(inter, axis=0)
            spad_ref[h] = s_cur
            so_ref[0, h] = s_cur[0:GLA_DK, 0:GLA_DV]
        else:
            s_all = spad_ref[h]
            qbd = jnp.concatenate([jnp.where(rowseg == s, q_t, 0.0) for s in range(nseg)], axis=1).astype(BF16)
            o = o + _dot(qbd, s_all.astype(BF16))
            for s in range(nseg):
                so_ref[s, h] = seg_update(s, s_all[s * DK_PAD:(s + 1) * DK_PAD])[0:GLA_DK, 0:GLA_DV]
        o_ref[:, h * DV_PAD:(h + 1) * DV_PAD] = _head_norm_gate(
            o, gn_ref[:, h * DV_PAD:(h + 1) * DV_PAD], r_ref[:, h * DV_PAD:(h + 1) * DV_PAD]).astype(o_ref.dtype)


def _gla_mixer(z, wg, bg, gn, seq_len, s0=None):
    t = z.shape[0]
    nseq = t // seq_len
    r = 128
    if s0 is None:
        seg_len = GLA_CHUNK
        nseg = r // seg_len
        steps = seq_len // r
        grid = (nseq, steps)
        rowmap = lambda c: (lambda n, i: (n * steps + i, c))
        stmap = lambda n, i: (n, 0, 0, 0)
        st_rows = 1
        scratch_rows = DK_PAD
        sem = ("parallel", "arbitrary")
    else:
        seg_len = seq_len
        nseg = r // seg_len
        grid = (t // r,)
        rowmap = lambda c: (lambda i: (i, c))
        stmap = lambda i: (i, 0, 0, 0)
        st_rows = nseg
        scratch_rows = nseg * DK_PAD
        sem = ("arbitrary",)
    seg = jnp.arange(r) // seg_len
    idx = jnp.arange(r)
    same = (seg[:, None] == seg[None, :]).astype(F32)
    tril = same * (idx[:, None] >= idx[None, :]).astype(F32)
    full = lambda a: pl.BlockSpec(a.shape, lambda *_: (0,) * a.ndim)
    st_spec = pl.BlockSpec((st_rows, GLA_HEADS, GLA_DK, GLA_DV), stmap)
    in_specs = [pl.BlockSpec((r, 512), rowmap(0)), pl.BlockSpec((r, 512), rowmap(1)),
                pl.BlockSpec((r, 1024), rowmap(1)), pl.BlockSpec((r, 1024), rowmap(2)),
                pl.BlockSpec((r, 128), rowmap(COL_LOW // 128)),
                full(tril), full(same), full(wg), full(bg), full(gn)]
    args = [z, z, z, z, z, tril, same, wg, bg, gn]
    if s0 is not None:
        in_specs.append(st_spec)
        args.append(s0)
    mix, s_new = pl.pallas_call(
        functools.partial(_gla_kernel, nseg=nseg, chained=s0 is None),
        grid=grid,
        in_specs=in_specs,
        out_specs=[pl.BlockSpec((r, HEADS * DV_PAD), rowmap(0)), st_spec],
        out_shape=[jax.ShapeDtypeStruct((t, HEADS * DV_PAD), BF16),
                   jax.ShapeDtypeStruct((nseq, GLA_HEADS, GLA_DK, GLA_DV), F32)],
        scratch_shapes=[pltpu.VMEM((HEADS, scratch_rows, DV_PAD), F32)],
        compiler_params=_cparams(*sem),
        name="gla",
    )(*args)
    return mix, s_new


def _pad_heads(w, width, pad):
    lead = w.shape[:-1]
    w = w.reshape(lead + (HEADS, width))
    w = jnp.pad(w, [(0, 0)] * len(lead) + [(0, 0), (0, pad - width)])
    return w.reshape(lead + (HEADS * pad,))


def _pad_head_rows(w, width, pad):
    return jnp.swapaxes(_pad_heads(jnp.swapaxes(w, 0, 1), width, pad), 0, 1)


def _ret_in_weight(w):
    qk = RET_HEADS * RET_DK
    v = _pad_heads(w[:, 2 * qk:2 * qk + MIX_WIDTH], RET_DV, DV_PAD)
    g = _pad_heads(w[:, 2 * qk + MIX_WIDTH:2 * qk + 2 * MIX_WIDTH], RET_DV, DV_PAD)
    return jnp.concatenate([w[:, :2 * qk], v, g, w[:, -XA_WIDTH:]], axis=1).astype(BF16)


def _gla_in_weight(w):
    qk = GLA_HEADS * GLA_DK
    q = _pad_heads(w[:, :qk], GLA_DK, DK_PAD)
    k = _pad_heads(w[:, qk:2 * qk], GLA_DK, DK_PAD)
    v = _pad_heads(w[:, 2 * qk:2 * qk + MIX_WIDTH], GLA_DV, DV_PAD)
    r = _pad_heads(w[:, 2 * qk + MIX_WIDTH:2 * qk + 2 * MIX_WIDTH], GLA_DV, DV_PAD)
    low = jnp.pad(w[:, 2 * qk + 2 * MIX_WIDTH:2 * qk + 2 * MIX_WIDTH + GLA_RANK], [(0, 0), (0, 128 - GLA_RANK)])
    return jnp.concatenate([q, k, v, r, w[:, -XA_WIDTH:], low], axis=1).astype(BF16)


def kernel(x_prompt, x_sample, mem_prompt, state_s5_re, state_s5_im, state_ret, state_gla,
           cache_mem_k, cache_mem_v,
           w_in_s5, s5_lam_re, s5_lam_im, s5_log_dt, s5_b_re, s5_b_im, s5_c_re, s5_c_im,
           s5_d, s5_w_glu, s5_b_glu,
           w_in_ret, ret_gn,
           w_in_gla, gla_w_gate2, gla_b_gate2, gla_gn,
           w_mem_k, w_mem_v, w_out, ln_g, ln_b,
           moe_w_grp, moe_b_grp, moe_w_exp, moe_b_exp, moe_w_gate, moe_w_up, moe_w_down):
    bp, lp, _ = x_prompt.shape
    bs, ls, _ = x_sample.shape
    xp = x_prompt.reshape(bp * lp, D_MODEL)
    xs = x_sample.reshape(bs * ls, D_MODEL)
    pos_p = jnp.arange(lp, dtype=jnp.int32)
    pos_s = PAST_LEN + jnp.arange(ls, dtype=jnp.int32)

    to_t = lambda c: jnp.transpose(c, (0, 1, 3, 4, 2)).reshape(DEPTH, c.shape[1], XA_WIDTH, N_MEM)
    from_t = lambda c: jnp.transpose(c.reshape(DEPTH, c.shape[1], XA_HEADS, XA_HEAD_DIM, N_MEM), (0, 1, 4, 2, 3))
    mkt_all, mvt_all = _mem_kv(mem_prompt, jnp.swapaxes(w_mem_k, 1, 2).astype(BF16),
                               jnp.swapaxes(w_mem_v, 1, 2).astype(BF16))
    cache_kt = to_t(cache_mem_k)
    cache_vt = to_t(cache_mem_v)
    ret_t = jnp.swapaxes(state_ret, 3, 4)
    s5_t = lambda s: jnp.transpose(s, (0, 2, 3, 1)).reshape(s.shape[0], S5_GROUPS * S5_STATE, bs)
    s5_re_t, s5_im_t = s5_t(state_s5_re), s5_t(state_s5_im)
    moe_wg, moe_wu, moe_wd = moe_w_gate.astype(BF16), moe_w_up.astype(BF16), moe_w_down.astype(BF16)

    p_re, p_im, p_ret, p_gla = [], [], [], []
    s_re, s_im, s_ret, s_gla = [], [], [], []
    for i in range(DEPTH):
        kind, j = i % N_MIXERS, i // N_MIXERS
        ln1 = (ln_g[i, 0].reshape(1, D_MODEL), ln_b[i, 0].reshape(1, D_MODEL))
        ln2 = (ln_g[i, 1].reshape(1, D_MODEL), ln_b[i, 1].reshape(1, D_MODEL))
        wo = w_out[i]
        wo_att = wo[MIX_WIDTH:].astype(BF16)
        glu = {}
        if kind == 0:
            params = _s5_params(s5_lam_re[j], s5_lam_im[j], s5_log_dt[j], s5_b_re[j], s5_b_im[j],
                                s5_c_re[j], s5_c_im[j], s5_d[j])
            zp = _matmul(xp, w_in_s5[j].astype(BF16), 512)
            zs = _matmul(xs, w_in_s5[j], 512, precise=True)
            mix_p, a_re, a_im = _s5_mixer_prompt(zp, params, lp)
            mix_s, b_re, b_im = _s5_mixer_sample(zs, params, ls, s5_re_t, s5_im_t, j)
            p_re.append(a_re), p_im.append(a_im), s_re.append(b_re), s_im.append(b_im)
            xa_col = MIX_WIDTH
            wo_mix = wo[:MIX_WIDTH].astype(BF16)
            glu = dict(glu_w=s5_w_glu[j].astype(BF16), glu_b=s5_b_glu[j].reshape(1, MIX_WIDTH))
        elif kind == 1:
            w = _ret_in_weight(w_in_ret[j])
            gn = _pad_heads(ret_gn[j], RET_DV, DV_PAD).reshape(1, HEADS * DV_PAD)
            zp = _matmul(xp, w, 512)
            zs = _matmul(xs, w, 512)
            mix_p, a_s = _ret_mixer(zp, gn, lp, pos_p)
            mix_s, b_s = _ret_mixer(zs, gn, ls, pos_s, s0=ret_t[j])
            p_ret.append(a_s), s_ret.append(b_s)
            xa_col = COL_XA
            wo_mix = _pad_head_rows(wo[:MIX_WIDTH], RET_DV, DV_PAD).astype(BF16)
        else:
            w = _gla_in_weight(w_in_gla[j])
            gn = _pad_heads(gla_gn[j], GLA_DV, DV_PAD).reshape(1, HEADS * DV_PAD)
            wg = jnp.pad(_pad_heads(gla_w_gate2[j], GLA_DK, DK_PAD), [(0, 128 - GLA_RANK), (0, 0)])
            bg = _pad_heads(gla_b_gate2[j], GLA_DK, DK_PAD).reshape(1, HEADS * DK_PAD)
            zp = _matmul(xp, w, 512)
            zs = _matmul(xs, w, 512)
            mix_p, a_s = _gla_mixer(zp, wg, bg, gn, lp)
            mix_s, b_s = _gla_mixer(zs, wg, bg, gn, ls, s0=state_gla[j])
            p_gla.append(a_s), s_gla.append(b_s)
            xa_col = COL_XA
            wo_mix = _pad_head_rows(wo[:MIX_WIDTH], GLA_DV, DV_PAD).astype(BF16)
        att_p = _mem_attention(zp, xa_col, mkt_all, mvt_all, i, lp, rl=min(512, lp), nseq=1)
        att_s = _mem_attention(zs, xa_col, cache_kt, cache_vt, i, ls, rl=ls, nseq=128 // ls)
        xp = _out_proj_ln(mix_p, att_p, xp, wo_mix, wo_att, *ln1, **glu)
        xs = _out_proj_ln(mix_s, att_s, xs, wo_mix, wo_att, *ln1, **glu)

        wr = jnp.pad(jnp.concatenate([moe_w_exp[i], moe_w_grp[i]], axis=1),
                     [(0, 0), (0, 128 - MOE_EXPERTS - MOE_GROUPS)]).astype(F32)
        wr_hi = wr.astype(BF16)
        wr = jnp.concatenate([wr_hi, (wr - wr_hi.astype(F32)).astype(BF16)], axis=1)
        br = jnp.pad(jnp.concatenate([moe_b_exp[i], moe_b_grp[i]]),
                     [(0, 128 - MOE_EXPERTS - MOE_GROUPS)]).reshape(1, 128).astype(F32)
        xp = _moe_ln(xp, wr, br, moe_wg, moe_wu, moe_wd, i, *ln2)
        xs = _moe_ln(xs, wr, br, moe_wg, moe_wu, moe_wd, i, *ln2)

    st = lambda xs_: jnp.stack(xs_)
    s5_p = lambda xs_: st(xs_).reshape(len(xs_), bp, S5_GROUPS, S5_STATE)
    s5_s = lambda xs_: jnp.transpose(st(xs_).reshape(len(xs_), S5_GROUPS, S5_STATE, bs), (0, 3, 1, 2))
    ret_out = lambda xs_: jnp.swapaxes(st(xs_), 3, 4)
    return (xp.reshape(bp, lp, D_MODEL), xs.reshape(bs, ls, D_MODEL),
            s5_p(p_re), s5_p(p_im), ret_out(p_ret), st(p_gla),
            from_t(mkt_all), from_t(mvt_all),
            s5_s(s_re), s5_s(s_im), ret_out(s_ret), st(s_gla))
```

```python
import functools
import math

import jax
import jax.numpy as jnp
from jax import lax
from jax.experimental import pallas as pl
from jax.experimental.pallas import tpu as pltpu

F32 = jnp.float32
BF16 = jnp.bfloat16
HI = lax.Precision.HIGHEST

D_MODEL = 1024
DEPTH = 4
PAST_LEN = 16384
N_MIXERS = 3
MIX_WIDTH = 768
XA_HEADS = 4
XA_HEAD_DIM = 64
XA_WIDTH = 256
N_MEM = 256
S5_GROUP = 16
S5_GROUPS = 48
S5_STATE = 64
S5_SLABS = 6
S5_SLAB_STATE = 512
RET_HEADS = 4
RET_DK = 128
RET_DV = 192
RET_CHUNK = 128
ROPE_BASE = 10000.0
GLA_HEADS = 4
GLA_DK = 96
GLA_DV = 192
GLA_RANK = 16
GLA_TAU = 16.0
GLA_CHUNK = 64
MOE_GROUPS = 4
MOE_PER_GROUP = 4
MOE_EXPERTS = 16
MOE_HIDDEN = 256
DN_ALPHA = (2 * DEPTH) ** 0.25
NORM_EPS = 1e-5

DK_PAD = 128
DV_PAD = 256
HEADS = 4
COL_Q, COL_K, COL_V, COL_G, COL_XA, COL_LOW = 0, 512, 1024, 2048, 3072, 3328
RET_IN_PAD = 3328
GLA_IN_PAD = 3456
VMEM_LIMIT = 52 * 1024 * 1024


def _cparams(*sem):
    return pltpu.CompilerParams(dimension_semantics=sem, vmem_limit_bytes=VMEM_LIMIT)


def _dot(a, b):
    return jnp.dot(a, b, preferred_element_type=F32)


def _dot_nt(a, b):
    return lax.dot_general(a, b, (((1,), (1,)), ((), ())), preferred_element_type=F32)


def _dot_hi(a, b):
    return jnp.dot(a, b, preferred_element_type=F32, precision=HI)


def _layer_norm(y, g, b):
    mu = jnp.mean(y, axis=-1, keepdims=True)
    d = y - mu
    var = jnp.mean(d * d, axis=-1, keepdims=True)
    return d * lax.rsqrt(var + NORM_EPS) * g + b


def _sigmoid(x):
    return 1.0 / (1.0 + jnp.exp(-x))


def _silu(x):
    return x * _sigmoid(x)


def _gelu_tanh(x):
    c = math.sqrt(2.0 / math.pi)
    return 0.5 * x * (1.0 + jnp.tanh(c * (x + 0.044715 * (x * x * x))))


def _matmul_kernel(x_ref, w_ref, o_ref, *, precise):
    if precise:
        o_ref[...] = _dot_hi(x_ref[...], w_ref[...])
    else:
        o_ref[...] = _dot(x_ref[...].astype(BF16), w_ref[...])


def _matmul(x, w, tm, precise=False):
    t, k = x.shape
    tm = min(tm, t)
    n = w.shape[1]
    return pl.pallas_call(
        functools.partial(_matmul_kernel, precise=precise),
        grid=(t // tm,),
        in_specs=[pl.BlockSpec((tm, k), lambda i: (i, 0)),
                  pl.BlockSpec((k, n), lambda i: (0, 0))],
        out_specs=pl.BlockSpec((tm, n), lambda i: (i, 0)),
        out_shape=jax.ShapeDtypeStruct((t, n), F32),
        compiler_params=_cparams("parallel"),
        name="in_proj",
    )(x, w)


def _memkv_kernel(x_ref, wk_ref, wv_ref, ok_ref, ov_ref):
    xb = x_ref[0].astype(BF16)
    ok_ref[0, 0] = _dot_nt(wk_ref[0], xb)
    ov_ref[0, 0] = _dot_nt(wv_ref[0], xb)


def _mem_kv(mem, wkt, wvt):
    n = mem.shape[0]
    spec_w = pl.BlockSpec((1, XA_WIDTH, D_MODEL), lambda i, b: (i, 0, 0))
    spec_o = pl.BlockSpec((1, 1, XA_WIDTH, N_MEM), lambda i, b: (i, b, 0, 0))
    return pl.pallas_call(
        _memkv_kernel,
        grid=(DEPTH, n),
        in_specs=[pl.BlockSpec((1, N_MEM, D_MODEL), lambda i, b: (b, 0, 0)), spec_w, spec_w],
        out_specs=[spec_o, spec_o],
        out_shape=[jax.ShapeDtypeStruct((DEPTH, n, XA_WIDTH, N_MEM), F32)] * 2,
        compiler_params=_cparams("parallel", "parallel"),
        name="mem_kv",
    )(mem, wkt, wvt)


def _attn_kernel(q_ref, kt_ref, vt_ref, o_ref, *, nseq, rl):
    head = lax.broadcasted_iota(jnp.int32, (rl, XA_WIDTH), 1) // XA_HEAD_DIM
    outs = []
    for s in range(nseq):
        q = q_ref[s * rl:(s + 1) * rl, :] * (XA_HEAD_DIM ** -0.5)
        qs = jnp.concatenate([jnp.where(head == h, q, 0.0) for h in range(XA_HEADS)], axis=0)
        sc = _dot(qs.astype(BF16), kt_ref[0, s].astype(BF16))
        sc = sc - jnp.max(sc, axis=-1, keepdims=True)
        p = jnp.exp(sc)
        p = p / jnp.sum(p, axis=-1, keepdims=True)
        pv = _dot_nt(p.astype(BF16), vt_ref[0, s].astype(BF16))
        o = jnp.where(head == 0, pv[0:rl], 0.0)
        for h in range(1, XA_HEADS):
            o = o + jnp.where(head == h, pv[h * rl:(h + 1) * rl], 0.0)
        outs.append(o)
    o_ref[...] = jnp.concatenate(outs, axis=0).astype(o_ref.dtype)


def _mem_attention(z, xa_col, mkt, mvt, layer, seq_len, rl, nseq):
    t = z.shape[0]
    rows = rl * nseq
    cb = xa_col // XA_WIDTH
    if nseq == 1:
        per_seq = seq_len // rl
        kv_map = lambda i: (layer, i // per_seq, 0, 0)
    else:
        kv_map = lambda i: (layer, i, 0, 0)
    return pl.pallas_call(
        functools.partial(_attn_kernel, nseq=nseq, rl=rl),
        grid=(t // rows,),
        in_specs=[pl.BlockSpec((rows, XA_WIDTH), lambda i: (i, cb)),
                  pl.BlockSpec((1, nseq, XA_WIDTH, N_MEM), kv_map),
                  pl.BlockSpec((1, nseq, XA_WIDTH, N_MEM), kv_map)],
        out_specs=pl.BlockSpec((rows, XA_WIDTH), lambda i: (i, 0)),
        out_shape=jax.ShapeDtypeStruct((t, XA_WIDTH), BF16),
        compiler_params=_cparams("parallel"),
        name="mem_attention",
    )(z, mkt, mvt)


def _router_gates(lg):
    tm = lg.shape[0]
    lgt = jnp.transpose(lg)
    ninf = jnp.float32(-jnp.inf)
    grow = lax.broadcasted_iota(jnp.int32, (8, tm), 0).astype(F32)
    gm = grow < MOE_GROUPS
    gl = jnp.where(gm, lgt[MOE_EXPERTS:MOE_EXPERTS + 8], ninf)
    ge = jnp.exp(gl - jnp.max(gl, axis=0, keepdims=True))
    gp = ge / jnp.sum(ge, axis=0, keepdims=True)
    pmax = jnp.max(gp, axis=0, keepdims=True)
    gi = jnp.min(jnp.where((gp == pmax) & gm, grow, 1e4), axis=0, keepdims=True)
    erow = lax.broadcasted_iota(jnp.int32, (MOE_EXPERTS, tm), 0).astype(F32)
    lge = lgt[0:MOE_EXPERTS]
    lo = gi * MOE_PER_GROUP
    em = (erow >= lo) & (erow < lo + MOE_PER_GROUP)
    el = jnp.where(em, lge, ninf)
    v1 = jnp.max(el, axis=0, keepdims=True)
    i1 = jnp.min(jnp.where((el == v1) & em, erow, 1e4), axis=0, keepdims=True)
    em2 = em & (erow != i1)
    el2 = jnp.where(em2, lge, ninf)
    v2 = jnp.max(el2, axis=0, keepdims=True)
    i2 = jnp.min(jnp.where((el2 == v2) & em2, erow, 1e4), axis=0, keepdims=True)
    t = jnp.exp(v2 - v1)
    w1 = 1.0 / (1.0 + t)
    w2 = t / (1.0 + t)
    gates = jnp.where(erow == i1, w1 * pmax, 0.0) + jnp.where(erow == i2, w2 * pmax, 0.0)
    onehot = jnp.where(grow == gi, 1.0, 0.0)
    rows = jnp.concatenate([gates, onehot, jnp.zeros((128 - MOE_EXPERTS - 8, tm), F32)], axis=0)
    return jnp.transpose(rows)


MOE_CAP = 160
MOE_OV = 128


def _tail_kernel(*refs, glu):
    (mix_ref, att_ref, xin_ref, wm_ref, wa_ref, g1_ref, b1_ref, wr_ref, br_ref, ltri_ref,
     wg_ref, wu_ref, wd_ref, g_ref, b_ref) = refs[:15]
    if glu:
        wglu_ref, bglu_ref = refs[15:17]
    o_ref, acc_ref, xb_ref, gs_ref, x_ref = refs[-5:]
    if glu:
        y = _gelu_tanh(mix_ref[...])
        mix = (y * _sigmoid(_dot(y.astype(BF16), wglu_ref[...]) + bglu_ref[...])).astype(BF16)
    else:
        mix = mix_ref[...]
    h = _dot(mix, wm_ref[...]) + _dot(att_ref[...], wa_ref[...])
    x_ref[...] = _layer_norm(DN_ALPHA * xin_ref[...] + h, g1_ref[...], b1_ref[...])
    tm = x_ref.shape[0]
    x = x_ref[...]
    xh = x.astype(BF16)
    xl = (x - xh.astype(F32)).astype(BF16)
    lg = _dot(xh, wr_ref[...])
    lg = lg[:, 0:128] + lg[:, 128:256] + _dot(xl, wr_ref[:, 0:128]) + br_ref[...]
    routed = _router_gates(lg)
    lane = lax.broadcasted_iota(jnp.int32, (tm, 128), 1)
    gates = jnp.where(lane < MOE_EXPERTS, routed, 0.0)
    in_grp = (lane >= MOE_EXPERTS) & (lane < MOE_EXPERTS + MOE_GROUPS) & (routed > 0.0)
    g_hi = gates.astype(BF16).astype(F32)
    g_mid = (gates - g_hi).astype(BF16).astype(F32)
    g_lo = gates - g_hi - g_mid
    gs_ref[...] = (g_hi + pltpu.roll(g_mid, MOE_EXPERTS, axis=1) + pltpu.roll(g_lo, 2 * MOE_EXPERTS, axis=1)).astype(BF16)
    xb_ref[...] = xh
    rkc = _dot(ltri_ref[...], routed.astype(BF16))
    rkc = jnp.where(in_grp, rkc, -1.0)
    rkr = jnp.transpose(rkc)
    cnt = jnp.sum(jnp.where(in_grp, 1.0, 0.0), axis=0, keepdims=True)
    rk_cols = [jnp.sum(jnp.where(lane == MOE_EXPERTS + g, rkc, 0.0), axis=-1, keepdims=True)
               for g in range(MOE_GROUPS)]
    rk_rows = [rkr[MOE_EXPERTS + g:MOE_EXPERTS + g + 1, :] for g in range(MOE_GROUPS)]

    def gather_mat(g, base, rows):
        r_io = lax.broadcasted_iota(jnp.int32, (rows, tm), 0).astype(F32) + base
        return jnp.where(rk_rows[g] == r_io, 1.0, 0.0).astype(BF16)

    def scatter_mat(g, base, kpad):
        c_io = lax.broadcasted_iota(jnp.int32, (tm, kpad), 1).astype(F32) + base
        return jnp.where(rk_cols[g] == c_io, 1.0, 0.0).astype(BF16)

    def experts(g, xg, gg, kpad):
        rows = xg.shape[0]
        glane = lax.broadcasted_iota(jnp.int32, (rows, 128), 1)
        hs = []
        for j in range(MOE_PER_GROUP):
            e = g * MOE_PER_GROUP + j
            ge = jnp.sum(jnp.where((glane % MOE_EXPERTS == e) & (glane < 3 * MOE_EXPERTS), gg, 0.0),
                         axis=-1, keepdims=True)
            hs.append((_silu(_dot(xg, wg_ref[0, e])) * _dot(xg, wu_ref[0, e]) * ge).astype(BF16))
        wd = wd_ref[0, g * MOE_PER_GROUP:(g + 1) * MOE_PER_GROUP].reshape(MOE_PER_GROUP * MOE_HIDDEN, D_MODEL)
        out = _dot(jnp.concatenate(hs, axis=-1), wd).astype(BF16)
        if kpad > rows:
            out = jnp.concatenate([out, jnp.zeros((kpad - rows, D_MODEL), BF16)], axis=0)
        return out

    in_cap = lambda rk: (rk >= 0.0) & (rk < MOE_CAP)
    slot_col = sum(jnp.where(in_cap(rk_cols[g]), rk_cols[g] + (g * MOE_CAP + 1.0), 0.0) for g in range(MOE_GROUPS)) - 1.0
    slot_row = sum(jnp.where(in_cap(rk_rows[g]), rk_rows[g] + (g * MOE_CAP + 1.0), 0.0) for g in range(MOE_GROUPS)) - 1.0
    slots = MOE_GROUPS * MOE_CAP
    r_io = lax.broadcasted_iota(jnp.int32, (slots, tm), 0).astype(F32)
    p_all = jnp.where(slot_row == r_io, 1.0, 0.0).astype(BF16)
    xg_all = _dot(p_all, xh).astype(BF16)
    gg_all = _dot(p_all, gs_ref[...])
    outs = [experts(g, xg_all[g * MOE_CAP:(g + 1) * MOE_CAP], gg_all[g * MOE_CAP:(g + 1) * MOE_CAP], MOE_CAP)
            for g in range(MOE_GROUPS)]
    c_io = lax.broadcasted_iota(jnp.int32, (tm, slots), 1).astype(F32)
    pt_all = jnp.where(slot_col == c_io, 1.0, 0.0).astype(BF16)
    acc_ref[...] = _dot(pt_all, jnp.concatenate(outs, axis=0))

    for g in range(MOE_GROUPS):
        n_ov = jnp.maximum(cnt[0, MOE_EXPERTS + g].astype(jnp.int32) - MOE_CAP + MOE_OV - 1, 0) // MOE_OV

        def overflow(i, _, g=g):
            base = (MOE_CAP + i * MOE_OV).astype(F32)
            p = gather_mat(g, base, MOE_OV)
            out = experts(g, _dot(p, xb_ref[...]).astype(BF16), _dot(p, gs_ref[...]), MOE_OV)
            acc_ref[...] += _dot(scatter_mat(g, base, MOE_OV), out)
            return 0

        lax.fori_loop(0, n_ov, overflow, 0)

    o_ref[...] = _layer_norm(DN_ALPHA * x_ref[...] + acc_ref[...], g_ref[...], b_ref[...])


def _layer_tail(mix, att, x, wm, wa, ln1, wr, br, wg, wu, wd, layer, ln2, glu_w=None, glu_b=None, tm=512):
    t = x.shape[0]
    tm = min(tm, t)
    idx = jnp.arange(tm)
    ltri = (idx[:, None] > idx[None, :]).astype(BF16)
    row = lambda w: pl.BlockSpec((tm, w), lambda i: (i, 0))
    full = lambda a: pl.BlockSpec(a.shape, lambda i: (0,) * a.ndim, pipeline_mode=pl.Buffered(1))
    layer_w = lambda a: pl.BlockSpec((1,) + a.shape[1:], lambda i: (layer, 0, 0, 0),
                                     pipeline_mode=pl.Buffered(1))
    args = [mix, att, x, wm, wa, ln1[0], ln1[1], wr, br, ltri, wg, wu, wd, ln2[0], ln2[1]]
    specs = [row(mix.shape[1]), row(XA_WIDTH), row(D_MODEL), full(wm), full(wa), full(ln1[0]), full(ln1[1]),
             full(wr), full(br), full(ltri), layer_w(wg), layer_w(wu), layer_w(wd), full(ln2[0]), full(ln2[1])]
    if glu_w is not None:
        args += [glu_w, glu_b]
        specs += [full(glu_w), full(glu_b)]
    return pl.pallas_call(
        functools.partial(_tail_kernel, glu=glu_w is not None),
        grid=(t // tm,),
        in_specs=specs,
        out_specs=row(D_MODEL),
        out_shape=jax.ShapeDtypeStruct((t, D_MODEL), F32),
        scratch_shapes=[pltpu.VMEM((tm, D_MODEL), F32), pltpu.VMEM((tm, D_MODEL), BF16),
                        pltpu.VMEM((tm, 128), BF16), pltpu.VMEM((tm, D_MODEL), F32)],
        compiler_params=_cparams("arbitrary"),
        name="layer_tail",
    )(*args)


def _s5_kernel(*refs, tlen, ngrp, chained, precise):
    if chained:
        u_ref, wb_ref, wc_ref, lam_ref, d_ref, y_ref, ore_ref, oim_ref, bu_ref, h_ref, cr_ref, ci_ref = refs
        u = jnp.swapaxes(u_ref[...], 0, 1).reshape(8 * tlen, 128)
    else:
        (u_ref, h0r_ref, h0i_ref, wb_ref, wc_ref, lam_ref, d_ref,
         y_ref, ore_ref, oim_ref, bu_ref, h_ref, cr_ref, ci_ref) = refs
        gr = 8 * tlen
        u = jnp.concatenate(
            [jnp.swapaxes(u_ref[g * gr:(g + 1) * gr, :].reshape(8, tlen, 128), 0, 1).reshape(gr, 128)
             for g in range(ngrp)], axis=0)
    if precise:
        bu_ref[...] = _dot_hi(u, wb_ref[0])
    else:
        bu_ref[...] = _dot(u.astype(BF16), wb_ref[0].astype(BF16))
    ns = S5_SLAB_STATE
    lr = jnp.broadcast_to(lam_ref[0, 0:1, :], (8, ns))
    li = jnp.broadcast_to(lam_ref[0, 1:2, :], (8, ns))

    def step(tile, hr, hi):
        rows = pl.ds(pl.multiple_of(tile * 8, 8), 8)
        nr = lr * hr - li * hi + bu_ref[rows, 0:ns]
        ni = lr * hi + li * hr + bu_ref[rows, ns:2 * ns]
        h_ref[rows, 0:ns] = nr
        h_ref[rows, ns:2 * ns] = ni
        return nr, ni

    if chained:
        @pl.when(pl.program_id(2) == 0)
        def _():
            cr_ref[...] = jnp.zeros_like(cr_ref)
            ci_ref[...] = jnp.zeros_like(ci_ref)

        hr, hi = lax.fori_loop(0, tlen, lambda t, c: step(t, *c), (cr_ref[...], ci_ref[...]), unroll=8)
        cr_ref[...] = hr
        ci_ref[...] = hi
        ore_ref[...] = hr
        oim_ref[...] = hi
    else:
        cr_ref[...] = jnp.transpose(h0r_ref[0])
        ci_ref[...] = jnp.transpose(h0i_ref[0])

        def group(g, _):
            s0 = pl.multiple_of(g * 8, 8)
            hr, hi = cr_ref[pl.ds(s0, 8), :], ci_ref[pl.ds(s0, 8), :]
            for t in range(tlen):
                hr, hi = step(g * tlen + t, hr, hi)
            cr_ref[pl.ds(s0, 8), :] = hr
            ci_ref[pl.ds(s0, 8), :] = hi
            return 0

        lax.fori_loop(0, ngrp, group, 0)
        ore_ref[...] = jnp.transpose(cr_ref[...])
        oim_ref[...] = jnp.transpose(ci_ref[...])
    y = _dot(h_ref[...].astype(BF16), wc_ref[0].astype(BF16)) + d_ref[0] * u
    if chained:
        y_ref[...] = jnp.swapaxes(y.reshape(tlen, 8, 128), 0, 1)
    else:
        gr = 8 * tlen
        for g in range(ngrp):
            y_ref[g * gr:(g + 1) * gr, :] = jnp.swapaxes(
                y[g * gr:(g + 1) * gr, :].reshape(tlen, 8, 128), 0, 1).reshape(gr, 128)


def _s5_mixer_prompt(z, params, seq_len, tlen=256):
    t = z.shape[0]
    nseq = t // seq_len
    ns = S5_SLAB_STATE
    z3 = z.reshape(nseq, seq_len, z.shape[1])
    slab = lambda a: pl.BlockSpec((1,) + a.shape[1:], lambda j, b, i: (j,) + (0,) * (a.ndim - 1))
    u_spec = pl.BlockSpec((8, tlen, 128), lambda j, b, i: (b, i, j))
    st_spec = pl.BlockSpec((8, ns), lambda j, b, i: (b, j))
    st_shape = jax.ShapeDtypeStruct((nseq, S5_SLABS * ns), F32)
    rows = 8 * tlen
    y, hre, him = pl.pallas_call(
        functools.partial(_s5_kernel, tlen=tlen, ngrp=1, chained=True, precise=False),
        grid=(S5_SLABS, nseq // 8, seq_len // tlen),
        in_specs=[u_spec] + [slab(a) for a in params],
        out_specs=[u_spec, st_spec, st_spec],
        out_shape=[jax.ShapeDtypeStruct((nseq, seq_len, MIX_WIDTH), F32), st_shape, st_shape],
        scratch_shapes=[pltpu.VMEM((rows, 2 * ns), F32), pltpu.VMEM((rows, 2 * ns), F32),
                        pltpu.VMEM((8, ns), F32), pltpu.VMEM((8, ns), F32)],
        compiler_params=_cparams("parallel", "parallel", "arbitrary"),
        name="s5_mixer",
    )(z3, *params)
    return y.reshape(t, MIX_WIDTH), hre, him


def _s5_mixer_sample(z, params, seq_len, h0r, h0i, layer):
    t = z.shape[0]
    nseq = t // seq_len
    ns = S5_SLAB_STATE
    seqs = 128
    rows = seqs * seq_len
    slab = lambda a: pl.BlockSpec((1,) + a.shape[1:], lambda j, i: (j,) + (0,) * (a.ndim - 1))
    u_spec = pl.BlockSpec((rows, 128), lambda j, i: (i, j))
    h0_spec = pl.BlockSpec((1, ns, seqs), lambda j, i: (layer, j, i))
    st_spec = pl.BlockSpec((ns, seqs), lambda j, i: (j, i))
    st_shape = jax.ShapeDtypeStruct((S5_SLABS * ns, nseq), F32)
    return pl.pallas_call(
        functools.partial(_s5_kernel, tlen=seq_len, ngrp=seqs // 8, chained=False, precise=True),
        grid=(S5_SLABS, nseq // seqs),
        in_specs=[u_spec, h0_spec, h0_spec] + [slab(a) for a in params],
        out_specs=[u_spec, st_spec, st_spec],
        out_shape=[jax.ShapeDtypeStruct((t, MIX_WIDTH), F32), st_shape, st_shape],
        scratch_shapes=[pltpu.VMEM((rows, 2 * ns), F32), pltpu.VMEM((rows, 2 * ns), F32),
                        pltpu.VMEM((seqs, ns), F32), pltpu.VMEM((seqs, ns), F32)],
        compiler_params=_cparams("parallel", "parallel"),
        name="s5_mixer",
    )(z, h0r, h0i, *params)


def _s5_params(lam_re, lam_im, log_dt, b_re, b_im, c_re, c_im, d_skip):
    lre, lim = lam_re.astype(F32), lam_im.astype(F32)
    dt = jnp.exp(log_dt.astype(F32))[:, None]
    mag = jnp.exp(lre * dt)
    bar_re, bar_im = mag * jnp.cos(lim * dt), mag * jnp.sin(lim * dt)
    nre, nim = bar_re - 1.0, bar_im
    den = lre * lre + lim * lim
    fre, fim = (nre * lre + nim * lim) / den, (nim * lre - nre * lim) / den
    bre, bim = b_re.astype(F32), b_im.astype(F32)
    bb_re = fre[..., None] * bre - fim[..., None] * bim
    bb_im = fre[..., None] * bim + fim[..., None] * bre
    eye = jnp.eye(8, dtype=F32)
    gps = 8

    def blockdiag_in(m):
        m = m.reshape(S5_SLABS, gps, S5_STATE, S5_GROUP).transpose(0, 1, 3, 2)
        return jnp.einsum('jgcp,gh->jgchp', m, eye).reshape(S5_SLABS, 128, S5_SLAB_STATE)

    def blockdiag_out(m):
        m = m.reshape(S5_SLABS, gps, S5_GROUP, S5_STATE).transpose(0, 1, 3, 2)
        return jnp.einsum('jgpc,gh->jgphc', m, eye).reshape(S5_SLABS, S5_SLAB_STATE, 128)

    wb = jnp.concatenate([blockdiag_in(bb_re), blockdiag_in(bb_im)], axis=2)
    wc = jnp.concatenate([blockdiag_out(c_re.astype(F32)), blockdiag_out(-c_im.astype(F32))], axis=1)
    lam = jnp.stack([bar_re.reshape(S5_SLABS, S5_SLAB_STATE), bar_im.reshape(S5_SLABS, S5_SLAB_STATE)], axis=1)
    dsk = d_skip.astype(F32).reshape(S5_SLABS, 1, 128)
    return wb, wc, lam, dsk


def _head_norm_gate(o, gn, gate):
    col = lax.broadcasted_iota(jnp.int32, o.shape, 1)
    valid = col < RET_DV
    mu = jnp.sum(o, axis=-1, keepdims=True) * (1.0 / RET_DV)
    d = jnp.where(valid, o - mu, 0.0)
    var = jnp.sum(d * d, axis=-1, keepdims=True) * (1.0 / RET_DV)
    return d * lax.rsqrt(var + NORM_EPS) * gn * _silu(gate)


def _load_states(s_in_ref, spad_ref, nseg, dk):
    spad_ref[...] = jnp.zeros_like(spad_ref)
    for s in range(nseg):
        for h in range(HEADS):
            spad_ref[h, s * DK_PAD:s * DK_PAD + dk, 0:RET_DV] = s_in_ref[s, h]


def _ret_kernel(*refs, nseg, gammas):
    if nseg == 1:
        (q_ref, k_ref, v_ref, g_ref, cos_ref, sin_ref, dmat_ref, inner_ref, zeta_ref, gn_ref,
         o_ref, so_ref, st_ref) = refs

        @pl.when(pl.program_id(1) == 0)
        def _():
            st_ref[...] = jnp.zeros_like(st_ref)
    else:
        (q_ref, k_ref, v_ref, g_ref, cos_ref, sin_ref, dmat_ref, inner_ref, zeta_ref, gn_ref, s_in_ref,
         o_ref, so_ref, st_ref) = refs
        st_ref[:, RET_DV:DV_PAD, :] = jnp.zeros((HEADS, DV_PAD - RET_DV, nseg * RET_DK), F32)
        for s in range(nseg):
            for h in range(HEADS):
                st_ref[h, 0:RET_DV, s * RET_DK:(s + 1) * RET_DK] = s_in_ref[s, h]
    r = q_ref.shape[0]
    seg_len = r // nseg
    cos = cos_ref[...]
    sin = sin_ref[...]
    rot = lambda x: x * cos + pltpu.roll(x, RET_DK // 2, axis=1) * sin
    rowseg = lax.broadcasted_iota(jnp.int32, (r, DK_PAD), 0) // seg_len
    colseg = lax.broadcasted_iota(jnp.int32, (DV_PAD, r), 1) // seg_len
    for h in range(HEADS):
        q = rot(q_ref[:, h * DK_PAD:(h + 1) * DK_PAD])
        k = rot(k_ref[:, h * DK_PAD:(h + 1) * DK_PAD]) * (RET_DK ** -0.5)
        v = v_ref[:, h * DV_PAD:(h + 1) * DV_PAD]
        vb = v.astype(BF16)
        qb = q.astype(BF16)
        att = _dot_nt(qb, k.astype(BF16)) * dmat_ref[h]
        o = _dot(att.astype(BF16), vb)
        st_all = st_ref[h]
        if nseg == 1:
            qbd = qb
        else:
            qbd = jnp.concatenate([jnp.where(rowseg == s, q, 0.0) for s in range(nseg)], axis=1).astype(BF16)
        o = o + _dot_nt(qbd, st_all.astype(BF16)) * inner_ref[h]
        kzb = (k * zeta_ref[h]).astype(BF16)
        vt = jnp.transpose(v)
        for s in range(nseg):
            vts = vt if nseg == 1 else jnp.where(colseg == s, vt, 0.0)
            st_new = st_all[:, s * RET_DK:(s + 1) * RET_DK] * gammas[h] + _dot(vts.astype(BF16), kzb)
            if nseg == 1:
                st_ref[h] = st_new
                so_ref[0, h] = st_new[0:RET_DV, :]
            else:
                so_ref[s, h] = st_new[0:RET_DV, :]
        o_ref[:, h * DV_PAD:(h + 1) * DV_PAD] = _head_norm_gate(
            o, gn_ref[:, h * DV_PAD:(h + 1) * DV_PAD], g_ref[:, h * DV_PAD:(h + 1) * DV_PAD]).astype(o_ref.dtype)


def _ret_consts(seg_len, nseg, pos):
    hh = jnp.arange(RET_HEADS, dtype=F32)
    log_g = jnp.log1p(-jnp.exp2(-5.0 - hh))
    idx = jnp.arange(seg_len, dtype=F32)
    rel = idx[:, None] - idx[None, :]
    dmat = jnp.where(rel >= 0, jnp.exp(log_g[:, None, None] * jnp.maximum(rel, 0.0)), 0.0)
    inner = jnp.exp(log_g[:, None] * (idx + 1.0))
    zeta = jnp.exp(log_g[:, None] * (seg_len - 1.0 - idx))
    r = seg_len * nseg
    seg = jnp.arange(r) // seg_len
    same = (seg[:, None] == seg[None, :]).astype(F32)
    dmat = jnp.tile(dmat, (1, nseg, nseg)) * same[None]
    inner = jnp.broadcast_to(jnp.tile(inner, (1, nseg))[:, :, None], (RET_HEADS, r, DV_PAD))
    zeta = jnp.broadcast_to(jnp.tile(zeta, (1, nseg))[:, :, None], (RET_HEADS, r, DK_PAD))
    half = RET_DK // 2
    inv_freq = ROPE_BASE ** (-jnp.arange(half, dtype=F32) / half)
    ang = pos.astype(F32)[:, None] * inv_freq
    cos2 = jnp.concatenate([jnp.cos(ang), jnp.cos(ang)], axis=-1)
    sin2 = jnp.concatenate([-jnp.sin(ang), jnp.sin(ang)], axis=-1)
    return dmat, inner, zeta, cos2, sin2


def _ret_gammas(seg_len):
    return tuple(float((1.0 - 2.0 ** (-5.0 - h)) ** seg_len) for h in range(RET_HEADS))


def _ret_mixer(z, gn, seq_len, pos, s0=None):
    t = z.shape[0]
    nseq = t // seq_len
    if s0 is None:
        r, nseg, seg_len = RET_CHUNK, 1, RET_CHUNK
        steps = seq_len // r
        grid = (nseq, steps)
        rowmap = lambda c: (lambda n, i: (n * steps + i, c))
        tabmap = lambda n, i: (i, 0)
        stmap = lambda n, i: (n, 0, 0, 0)
        st_rows = 1
        sem = ("parallel", "arbitrary")
    else:
        seg_len = seq_len
        nseg = 128 // seg_len
        r = 128
        grid = (t // r,)
        rowmap = lambda c: (lambda i: (i, c))
        tabmap = lambda i: (0, 0)
        stmap = lambda i: (i, 0, 0, 0)
        st_rows = nseg
        sem = ("arbitrary",)
    dmat, inner, zeta, cos2, sin2 = _ret_consts(seg_len, nseg, pos)
    if s0 is not None:
        cos2 = jnp.tile(cos2, (nseg, 1))
        sin2 = jnp.tile(sin2, (nseg, 1))
    full = lambda a: pl.BlockSpec(a.shape, lambda *_: (0,) * a.ndim)
    st_spec = pl.BlockSpec((st_rows, RET_HEADS, RET_DV, RET_DK), stmap)
    in_specs = [pl.BlockSpec((r, 512), rowmap(0)), pl.BlockSpec((r, 512), rowmap(1)),
                pl.BlockSpec((r, 1024), rowmap(1)), pl.BlockSpec((r, 1024), rowmap(2)),
                pl.BlockSpec((r, DK_PAD), tabmap), pl.BlockSpec((r, DK_PAD), tabmap),
                full(dmat), full(inner), full(zeta), full(gn)]
    args = [z, z, z, z, cos2, sin2, dmat, inner, zeta, gn]
    if s0 is not None:
        in_specs.append(st_spec)
        args.append(s0)
    mix, s_new = pl.pallas_call(
        functools.partial(_ret_kernel, nseg=nseg, gammas=_ret_gammas(seg_len)),
        grid=grid,
        in_specs=in_specs,
        out_specs=[pl.BlockSpec((r, HEADS * DV_PAD), rowmap(0)), st_spec],
        out_shape=[jax.ShapeDtypeStruct((t, HEADS * DV_PAD), BF16),
                   jax.ShapeDtypeStruct((nseq, RET_HEADS, RET_DV, RET_DK), F32)],
        scratch_shapes=[pltpu.VMEM((HEADS, DV_PAD, nseg * RET_DK), F32)],
        compiler_params=_cparams(*sem),
        name="retention",
    )(*args)
    return mix, s_new


def _log_sigmoid(x):
    return jnp.minimum(x, 0.0) - jnp.log1p(jnp.exp(-jnp.abs(x)))


def _gla_kernel(*refs, nseg, chained):
    if chained:
        (q_ref, k_ref, v_ref, r_ref, low_ref, tril_ref, same_ref, wg_ref, bg_ref, gn_ref,
         o_ref, so_ref, spad_ref) = refs

        @pl.when(pl.program_id(1) == 0)
        def _():
            spad_ref[...] = jnp.zeros_like(spad_ref)
    else:
        (q_ref, k_ref, v_ref, r_ref, low_ref, tril_ref, same_ref, wg_ref, bg_ref, gn_ref, s_in_ref,
         o_ref, so_ref, spad_ref) = refs
        _load_states(s_in_ref, spad_ref, nseg, GLA_DK)
    r = q_ref.shape[0]
    seg_len = r // nseg
    tril = tril_ref[...]
    causal = tril > 0.0
    la = _log_sigmoid(_dot_hi(low_ref[...], wg_ref[...]) + bg_ref[...]) * (1.0 / GLA_TAU)
    b = _dot_hi(tril, la)
    b_tot = _dot_hi(same_ref[...], la)
    rowseg = lax.broadcasted_iota(jnp.int32, (r, DK_PAD), 0) // seg_len
    colseg = lax.broadcasted_iota(jnp.int32, (DK_PAD, r), 1) // seg_len

    for h in range(HEADS):
        sl = slice(h * DK_PAD, (h + 1) * DK_PAD)
        bh = b[:, sl]
        q = q_ref[:, sl] * (GLA_DK ** -0.5)
        k = k_ref[:, sl]
        vb = v_ref[:, h * DV_PAD:(h + 1) * DV_PAD].astype(BF16)
        q_t = q * jnp.exp(bh)
        k_t = k * jnp.exp(-bh)
        att = jnp.where(causal, _dot_nt(q_t.astype(BF16), k_t.astype(BF16)), 0.0)
        o = _dot(att.astype(BF16), vb)
        kdt = jnp.transpose(k * jnp.exp(b_tot[:, sl] - bh))
        lat = jnp.transpose(la[:, sl])

        def seg_update(s, s_old):
            kds = jnp.where(colseg == s, kdt, 0.0) if nseg > 1 else kdt
            las = jnp.where(colseg == s, lat, 0.0) if nseg > 1 else lat
            decay = jnp.exp(jnp.sum(las, axis=-1, keepdims=True))
            return s_old * decay + _dot(kds.astype(BF16), vb)

        if chained:
            qb = q_t.astype(BF16)
            s_cur = spad_ref[h]
            inter = []
            for s in range(nseg):
                inter.append(_dot(qb[s * seg_len:(s + 1) * seg_len], s_cur.astype(BF16)))
                s_cur = seg_update(s, s_cur)
            o = o + jnp.concatenate(inter, axis=0)
            spad_ref[h] = s_cur
            so_ref[0, h] = s_cur[0:GLA_DK, 0:GLA_DV]
        else:
            s_all = spad_ref[h]
            qbd = jnp.concatenate([jnp.where(rowseg == s, q_t, 0.0) for s in range(nseg)], axis=1).astype(BF16)
            o = o + _dot(qbd, s_all.astype(BF16))
            for s in range(nseg):
                so_ref[s, h] = seg_update(s, s_all[s * DK_PAD:(s + 1) * DK_PAD])[0:GLA_DK, 0:GLA_DV]
        o_ref[:, h * DV_PAD:(h + 1) * DV_PAD] = _head_norm_gate(
            o, gn_ref[:, h * DV_PAD:(h + 1) * DV_PAD], r_ref[:, h * DV_PAD:(h + 1) * DV_PAD]).astype(o_ref.dtype)


def _gla_mixer(z, wg, bg, gn, seq_len, s0=None):
    t = z.shape[0]
    nseq = t // seq_len
    r = 128
    if s0 is None:
        seg_len = GLA_CHUNK
        nseg = r // seg_len
        steps = seq_len // r
        grid = (nseq, steps)
        rowmap = lambda c: (lambda n, i: (n * steps + i, c))
        stmap = lambda n, i: (n, 0, 0, 0)
        st_rows = 1
        scratch_rows = DK_PAD
        sem = ("parallel", "arbitrary")
    else:
        seg_len = seq_len
        nseg = r // seg_len
        grid = (t // r,)
        rowmap = lambda c: (lambda i: (i, c))
        stmap = lambda i: (i, 0, 0, 0)
        st_rows = nseg
        scratch_rows = nseg * DK_PAD
        sem = ("arbitrary",)
    seg = jnp.arange(r) // seg_len
    idx = jnp.arange(r)
    same = (seg[:, None] == seg[None, :]).astype(F32)
    tril = same * (idx[:, None] >= idx[None, :]).astype(F32)
    full = lambda a: pl.BlockSpec(a.shape, lambda *_: (0,) * a.ndim)
    st_spec = pl.BlockSpec((st_rows, GLA_HEADS, GLA_DK, GLA_DV), stmap)
    in_specs = [pl.BlockSpec((r, 512), rowmap(0)), pl.BlockSpec((r, 512), rowmap(1)),
                pl.BlockSpec((r, 1024), rowmap(1)), pl.BlockSpec((r, 1024), rowmap(2)),
                pl.BlockSpec((r, 128), rowmap(COL_LOW // 128)),
                full(tril), full(same), full(wg), full(bg), full(gn)]
    args = [z, z, z, z, z, tril, same, wg, bg, gn]
    if s0 is not None:
        in_specs.append(st_spec)
        args.append(s0)
    mix, s_new = pl.pallas_call(
        functools.partial(_gla_kernel, nseg=nseg, chained=s0 is None),
        grid=grid,
        in_specs=in_specs,
        out_specs=[pl.BlockSpec((r, HEADS * DV_PAD), rowmap(0)), st_spec],
        out_shape=[jax.ShapeDtypeStruct((t, HEADS * DV_PAD), BF16),
                   jax.ShapeDtypeStruct((nseq, GLA_HEADS, GLA_DK, GLA_DV), F32)],
        scratch_shapes=[pltpu.VMEM((HEADS, scratch_rows, DV_PAD), F32)],
        compiler_params=_cparams(*sem),
        name="gla",
    )(*args)
    return mix, s_new


def _pad_heads(w, width, pad):
    lead = w.shape[:-1]
    w = w.reshape(lead + (HEADS, width))
    w = jnp.pad(w, [(0, 0)] * len(lead) + [(0, 0), (0, pad - width)])
    return w.reshape(lead + (HEADS * pad,))


def _pad_head_rows(w, width, pad):
    return jnp.swapaxes(_pad_heads(jnp.swapaxes(w, 0, 1), width, pad), 0, 1)


def _ret_in_weight(w):
    qk = RET_HEADS * RET_DK
    v = _pad_heads(w[:, 2 * qk:2 * qk + MIX_WIDTH], RET_DV, DV_PAD)
    g = _pad_heads(w[:, 2 * qk + MIX_WIDTH:2 * qk + 2 * MIX_WIDTH], RET_DV, DV_PAD)
    return jnp.concatenate([w[:, :2 * qk], v, g, w[:, -XA_WIDTH:]], axis=1).astype(BF16)


def _gla_in_weight(w):
    qk = GLA_HEADS * GLA_DK
    q = _pad_heads(w[:, :qk], GLA_DK, DK_PAD)
    k = _pad_heads(w[:, qk:2 * qk], GLA_DK, DK_PAD)
    v = _pad_heads(w[:, 2 * qk:2 * qk + MIX_WIDTH], GLA_DV, DV_PAD)
    r = _pad_heads(w[:, 2 * qk + MIX_WIDTH:2 * qk + 2 * MIX_WIDTH], GLA_DV, DV_PAD)
    low = jnp.pad(w[:, 2 * qk + 2 * MIX_WIDTH:2 * qk + 2 * MIX_WIDTH + GLA_RANK], [(0, 0), (0, 128 - GLA_RANK)])
    return jnp.concatenate([q, k, v, r, w[:, -XA_WIDTH:], low], axis=1).astype(BF16)


def kernel(x_prompt, x_sample, mem_prompt, state_s5_re, state_s5_im, state_ret, state_gla,
           cache_mem_k, cache_mem_v,
           w_in_s5, s5_lam_re, s5_lam_im, s5_log_dt, s5_b_re, s5_b_im, s5_c_re, s5_c_im,
           s5_d, s5_w_glu, s5_b_glu,
           w_in_ret, ret_gn,
           w_in_gla, gla_w_gate2, gla_b_gate2, gla_gn,
           w_mem_k, w_mem_v, w_out, ln_g, ln_b,
           moe_w_grp, moe_b_grp, moe_w_exp, moe_b_exp, moe_w_gate, moe_w_up, moe_w_down):
    bp, lp, _ = x_prompt.shape
    bs, ls, _ = x_sample.shape
    xp = x_prompt.reshape(bp * lp, D_MODEL)
    xs = x_sample.reshape(bs * ls, D_MODEL)
    pos_p = jnp.arange(lp, dtype=jnp.int32)
    pos_s = PAST_LEN + jnp.arange(ls, dtype=jnp.int32)

    to_t = lambda c: jnp.transpose(c, (0, 1, 3, 4, 2)).reshape(DEPTH, c.shape[1], XA_WIDTH, N_MEM)
    from_t = lambda c: jnp.transpose(c.reshape(DEPTH, c.shape[1], XA_HEADS, XA_HEAD_DIM, N_MEM), (0, 1, 4, 2, 3))
    mkt_all, mvt_all = _mem_kv(mem_prompt, jnp.swapaxes(w_mem_k, 1, 2).astype(BF16),
                               jnp.swapaxes(w_mem_v, 1, 2).astype(BF16))
    cache_kt = to_t(cache_mem_k)
    cache_vt = to_t(cache_mem_v)
    ret_t = jnp.swapaxes(state_ret, 3, 4)
    s5_t = lambda s: jnp.transpose(s, (0, 2, 3, 1)).reshape(s.shape[0], S5_GROUPS * S5_STATE, bs)
    s5_re_t, s5_im_t = s5_t(state_s5_re), s5_t(state_s5_im)
    moe_wg, moe_wu, moe_wd = moe_w_gate.astype(BF16), moe_w_up.astype(BF16), moe_w_down.astype(BF16)

    p_re, p_im, p_ret, p_gla = [], [], [], []
    s_re, s_im, s_ret, s_gla = [], [], [], []
    for i in range(DEPTH):
        kind, j = i % N_MIXERS, i // N_MIXERS
        ln1 = (ln_g[i, 0].reshape(1, D_MODEL), ln_b[i, 0].reshape(1, D_MODEL))
        ln2 = (ln_g[i, 1].reshape(1, D_MODEL), ln_b[i, 1].reshape(1, D_MODEL))
        wo = w_out[i]
        wo_att = wo[MIX_WIDTH:].astype(BF16)
        glu = {}
        if kind == 0:
            params = _s5_params(s5_lam_re[j], s5_lam_im[j], s5_log_dt[j], s5_b_re[j], s5_b_im[j],
                                s5_c_re[j], s5_c_im[j], s5_d[j])
            zp = _matmul(xp, w_in_s5[j].astype(BF16), 512)
            zs = _matmul(xs, w_in_s5[j], 512, precise=True)
            mix_p, a_re, a_im = _s5_mixer_prompt(zp, params, lp)
            mix_s, b_re, b_im = _s5_mixer_sample(zs, params, ls, s5_re_t, s5_im_t, j)
            p_re.append(a_re), p_im.append(a_im), s_re.append(b_re), s_im.append(b_im)
            xa_col = MIX_WIDTH
            wo_mix = wo[:MIX_WIDTH].astype(BF16)
            glu = dict(glu_w=s5_w_glu[j].astype(BF16), glu_b=s5_b_glu[j].reshape(1, MIX_WIDTH))
        elif kind == 1:
            w = _ret_in_weight(w_in_ret[j])
            gn = _pad_heads(ret_gn[j], RET_DV, DV_PAD).reshape(1, HEADS * DV_PAD)
            zp = _matmul(xp, w, 512)
            zs = _matmul(xs, w, 512)
            mix_p, a_s = _ret_mixer(zp, gn, lp, pos_p)
            mix_s, b_s = _ret_mixer(zs, gn, ls, pos_s, s0=ret_t[j])
            p_ret.append(a_s), s_ret.append(b_s)
            xa_col = COL_XA
            wo_mix = _pad_head_rows(wo[:MIX_WIDTH], RET_DV, DV_PAD).astype(BF16)
        else:
            w = _gla_in_weight(w_in_gla[j])
            gn = _pad_heads(gla_gn[j], GLA_DV, DV_PAD).reshape(1, HEADS * DV_PAD)
            wg = jnp.pad(_pad_heads(gla_w_gate2[j], GLA_DK, DK_PAD), [(0, 128 - GLA_RANK), (0, 0)])
            bg = _pad_heads(gla_b_gate2[j], GLA_DK, DK_PAD).reshape(1, HEADS * DK_PAD)
            zp = _matmul(xp, w, 512)
            zs = _matmul(xs, w, 512)
            mix_p, a_s = _gla_mixer(zp, wg, bg, gn, lp)
            mix_s, b_s = _gla_mixer(zs, wg, bg, gn, ls, s0=state_gla[j])
            p_gla.append(a_s), s_gla.append(b_s)
            xa_col = COL_XA
            wo_mix = _pad_head_rows(wo[:MIX_WIDTH], GLA_DV, DV_PAD).astype(BF16)
        att_p = _mem_attention(zp, xa_col, mkt_all, mvt_all, i, lp, rl=min(512, lp), nseq=1)
        att_s = _mem_attention(zs, xa_col, cache_kt, cache_vt, i, ls, rl=ls, nseq=128 // ls)
        wr = jnp.pad(jnp.concatenate([moe_w_exp[i], moe_w_grp[i]], axis=1),
                     [(0, 0), (0, 128 - MOE_EXPERTS - MOE_GROUPS)]).astype(F32)
        wr_hi = wr.astype(BF16)
        wr = jnp.concatenate([wr_hi, (wr - wr_hi.astype(F32)).astype(BF16)], axis=1)
        br = jnp.pad(jnp.concatenate([moe_b_exp[i], moe_b_grp[i]]),
                     [(0, 128 - MOE_EXPERTS - MOE_GROUPS)]).reshape(1, 128).astype(F32)
        xp = _layer_tail(mix_p, att_p, xp, wo_mix, wo_att, ln1, wr, br, moe_wg, moe_wu, moe_wd, i, ln2, **glu)
        xs = _layer_tail(mix_s, att_s, xs, wo_mix, wo_att, ln1, wr, br, moe_wg, moe_wu, moe_wd, i, ln2, **glu)

    st = lambda xs_: jnp.stack(xs_)
    s5_p = lambda xs_: st(xs_).reshape(len(xs_), bp, S5_GROUPS, S5_STATE)
    s5_s = lambda xs_: jnp.transpose(st(xs_).reshape(len(xs_), S5_GROUPS, S5_STATE, bs), (0, 3, 1, 2))
    ret_out = lambda xs_: jnp.swapaxes(st(xs_), 3, 4)
    return (xp.reshape(bp, lp, D_MODEL), xs.reshape(bs, ls, D_MODEL),
            s5_p(p_re), s5_p(p_im), ret_out(p_ret), st(p_gla),
            from_t(mkt_all), from_t(mvt_all),
            s5_s(s_re), s5_s(s_im), ret_out(s_ret), st(s_gla))
```

```python
import functools
import math

import jax
import jax.numpy as jnp
from jax import lax
from jax.experimental import pallas as pl
from jax.experimental.pallas import tpu as pltpu

F32 = jnp.float32
BF16 = jnp.bfloat16
HI = lax.Precision.HIGHEST

D_MODEL = 1024
DEPTH = 4
PAST_LEN = 16384
N_MIXERS = 3
MIX_WIDTH = 768
XA_HEADS = 4
XA_HEAD_DIM = 64
XA_WIDTH = 256
N_MEM = 256
S5_GROUP = 16
S5_GROUPS = 48
S5_STATE = 64
S5_SLABS = 6
S5_SLAB_STATE = 512
RET_HEADS = 4
RET_DK = 128
RET_DV = 192
RET_CHUNK = 128
ROPE_BASE = 10000.0
GLA_HEADS = 4
GLA_DK = 96
GLA_DV = 192
GLA_RANK = 16
GLA_TAU = 16.0
GLA_CHUNK = 64
MOE_GROUPS = 4
MOE_PER_GROUP = 4
MOE_EXPERTS = 16
MOE_HIDDEN = 256
DN_ALPHA = (2 * DEPTH) ** 0.25
NORM_EPS = 1e-5

DK_PAD = 128
DV_PAD = 256
HEADS = 4
COL_Q, COL_K, COL_V, COL_G, COL_XA, COL_LOW = 0, 512, 1024, 2048, 3072, 3328
RET_IN_PAD = 3328
GLA_IN_PAD = 3456
VMEM_LIMIT = 52 * 1024 * 1024
TAIL_VMEM_LIMIT = 58 * 1024 * 1024


def _cparams(*sem):
    return pltpu.CompilerParams(dimension_semantics=sem, vmem_limit_bytes=VMEM_LIMIT)


def _dot(a, b):
    return jnp.dot(a, b, preferred_element_type=F32)


def _dot_nt(a, b):
    return lax.dot_general(a, b, (((1,), (1,)), ((), ())), preferred_element_type=F32)


def _dot_hi(a, b):
    return jnp.dot(a, b, preferred_element_type=F32, precision=HI)


def _layer_norm(y, g, b):
    mu = jnp.mean(y, axis=-1, keepdims=True)
    d = y - mu
    var = jnp.mean(d * d, axis=-1, keepdims=True)
    return d * lax.rsqrt(var + NORM_EPS) * g + b


def _sigmoid(x):
    return 1.0 / (1.0 + jnp.exp(-x))


def _silu(x):
    return x * _sigmoid(x)


def _gelu_tanh(x):
    c = math.sqrt(2.0 / math.pi)
    return 0.5 * x * (1.0 + jnp.tanh(c * (x + 0.044715 * (x * x * x))))


def _matmul_kernel(x_ref, w_ref, o_ref, *, precise):
    if precise:
        o_ref[...] = _dot_hi(x_ref[...], w_ref[...])
    else:
        o_ref[...] = _dot(x_ref[...].astype(BF16), w_ref[...])


def _matmul(x, w, tm, precise=False):
    t, k = x.shape
    tm = min(tm, t)
    n = w.shape[1]
    return pl.pallas_call(
        functools.partial(_matmul_kernel, precise=precise),
        grid=(t // tm,),
        in_specs=[pl.BlockSpec((tm, k), lambda i: (i, 0)),
                  pl.BlockSpec((k, n), lambda i: (0, 0))],
        out_specs=pl.BlockSpec((tm, n), lambda i: (i, 0)),
        out_shape=jax.ShapeDtypeStruct((t, n), F32),
        compiler_params=_cparams("parallel"),
        name="in_proj",
    )(x, w)


def _memkv_kernel(x_ref, wk_ref, wv_ref, ok_ref, ov_ref):
    xb = x_ref[0].astype(BF16)
    ok_ref[0, 0] = _dot_nt(wk_ref[0], xb)
    ov_ref[0, 0] = _dot_nt(wv_ref[0], xb)


def _mem_kv(mem, wkt, wvt):
    n = mem.shape[0]
    spec_w = pl.BlockSpec((1, XA_WIDTH, D_MODEL), lambda i, b: (i, 0, 0))
    spec_o = pl.BlockSpec((1, 1, XA_WIDTH, N_MEM), lambda i, b: (i, b, 0, 0))
    return pl.pallas_call(
        _memkv_kernel,
        grid=(DEPTH, n),
        in_specs=[pl.BlockSpec((1, N_MEM, D_MODEL), lambda i, b: (b, 0, 0)), spec_w, spec_w],
        out_specs=[spec_o, spec_o],
        out_shape=[jax.ShapeDtypeStruct((DEPTH, n, XA_WIDTH, N_MEM), F32)] * 2,
        compiler_params=_cparams("parallel", "parallel"),
        name="mem_kv",
    )(mem, wkt, wvt)


def _attn_kernel(q_ref, kt_ref, vt_ref, o_ref, *, nseq, rl):
    head = lax.broadcasted_iota(jnp.int32, (rl, XA_WIDTH), 1) // XA_HEAD_DIM
    outs = []
    for s in range(nseq):
        q = q_ref[s * rl:(s + 1) * rl, :] * (XA_HEAD_DIM ** -0.5)
        qs = jnp.concatenate([jnp.where(head == h, q, 0.0) for h in range(XA_HEADS)], axis=0)
        sc = _dot(qs.astype(BF16), kt_ref[0, s].astype(BF16))
        sc = sc - jnp.max(sc, axis=-1, keepdims=True)
        p = jnp.exp(sc)
        p = p / jnp.sum(p, axis=-1, keepdims=True)
        pv = _dot_nt(p.astype(BF16), vt_ref[0, s].astype(BF16))
        o = jnp.where(head == 0, pv[0:rl], 0.0)
        for h in range(1, XA_HEADS):
            o = o + jnp.where(head == h, pv[h * rl:(h + 1) * rl], 0.0)
        outs.append(o)
    o_ref[...] = jnp.concatenate(outs, axis=0).astype(o_ref.dtype)


def _mem_attention(z, xa_col, mkt, mvt, layer, seq_len, rl, nseq):
    t = z.shape[0]
    rows = rl * nseq
    cb = xa_col // XA_WIDTH
    if nseq == 1:
        per_seq = seq_len // rl
        kv_map = lambda i: (layer, i // per_seq, 0, 0)
    else:
        kv_map = lambda i: (layer, i, 0, 0)
    return pl.pallas_call(
        functools.partial(_attn_kernel, nseq=nseq, rl=rl),
        grid=(t // rows,),
        in_specs=[pl.BlockSpec((rows, XA_WIDTH), lambda i: (i, cb)),
                  pl.BlockSpec((1, nseq, XA_WIDTH, N_MEM), kv_map),
                  pl.BlockSpec((1, nseq, XA_WIDTH, N_MEM), kv_map)],
        out_specs=pl.BlockSpec((rows, XA_WIDTH), lambda i: (i, 0)),
        out_shape=jax.ShapeDtypeStruct((t, XA_WIDTH), BF16),
        compiler_params=_cparams("parallel"),
        name="mem_attention",
    )(z, mkt, mvt)


def _router_gates(lg):
    tm = lg.shape[0]
    lgt = jnp.transpose(lg)
    ninf = jnp.float32(-jnp.inf)
    grow = lax.broadcasted_iota(jnp.int32, (8, tm), 0).astype(F32)
    gm = grow < MOE_GROUPS
    gl = jnp.where(gm, lgt[MOE_EXPERTS:MOE_EXPERTS + 8], ninf)
    ge = jnp.exp(gl - jnp.max(gl, axis=0, keepdims=True))
    gp = ge / jnp.sum(ge, axis=0, keepdims=True)
    pmax = jnp.max(gp, axis=0, keepdims=True)
    gi = jnp.min(jnp.where((gp == pmax) & gm, grow, 1e4), axis=0, keepdims=True)
    erow = lax.broadcasted_iota(jnp.int32, (MOE_EXPERTS, tm), 0).astype(F32)
    lge = lgt[0:MOE_EXPERTS]
    lo = gi * MOE_PER_GROUP
    em = (erow >= lo) & (erow < lo + MOE_PER_GROUP)
    el = jnp.where(em, lge, ninf)
    v1 = jnp.max(el, axis=0, keepdims=True)
    i1 = jnp.min(jnp.where((el == v1) & em, erow, 1e4), axis=0, keepdims=True)
    em2 = em & (erow != i1)
    el2 = jnp.where(em2, lge, ninf)
    v2 = jnp.max(el2, axis=0, keepdims=True)
    i2 = jnp.min(jnp.where((el2 == v2) & em2, erow, 1e4), axis=0, keepdims=True)
    t = jnp.exp(v2 - v1)
    w1 = 1.0 / (1.0 + t)
    w2 = t / (1.0 + t)
    gates = jnp.where(erow == i1, w1 * pmax, 0.0) + jnp.where(erow == i2, w2 * pmax, 0.0)
    onehot = jnp.where(grow == gi, 1.0, 0.0)
    rows = jnp.concatenate([gates, onehot, jnp.zeros((128 - MOE_EXPERTS - 8, tm), F32)], axis=0)
    return jnp.transpose(rows)


MOE_CAP = 160
MOE_OV = 128


def _tail_kernel(*refs, glu):
    (mix_ref, att_ref, xin_ref, wm_ref, wa_ref, g1_ref, b1_ref, wr_ref, br_ref, ltri_ref,
     wg_ref, wu_ref, wd_ref, g_ref, b_ref) = refs[:15]
    if glu:
        wglu_ref, bglu_ref = refs[15:17]
    (o_ref, acc_ref, xb_ref, gs_ref, x_ref, rkc_ref, rkr_ref,
     xb_nx, gs_nx, x_nx, rkc_nx, rkr_nx) = refs[-12:]
    tm = x_ref.shape[0]
    lane = lax.broadcasted_iota(jnp.int32, (tm, 128), 1)

    @pl.when(pl.program_id(0) == 0)
    def _():
        xb_ref[...] = jnp.zeros_like(xb_ref)
        gs_ref[...] = jnp.zeros_like(gs_ref)
        x_ref[...] = jnp.zeros_like(x_ref)
        rkc_ref[...] = jnp.full(rkc_ref.shape, -1.0, F32)
        rkr_ref[...] = jnp.full(rkr_ref.shape, -1.0, F32)

    front = {}

    def front_project():
        if glu:
            y = _gelu_tanh(mix_ref[...])
            mix = (y * _sigmoid(_dot(y.astype(BF16), wglu_ref[...]) + bglu_ref[...])).astype(BF16)
        else:
            mix = mix_ref[...]
        front["h"] = _dot(mix, wm_ref[...]) + _dot(att_ref[...], wa_ref[...])

    def front_norm():
        xn = _layer_norm(DN_ALPHA * xin_ref[...] + front["h"], g1_ref[...], b1_ref[...])
        x_nx[...] = xn
        xh = xn.astype(BF16)
        xb_nx[...] = xh
        front["xh"], front["xl"] = xh, (xn - xh.astype(F32)).astype(BF16)

    def front_route():
        lg = _dot(front["xh"], wr_ref[...])
        lg = lg[:, 0:128] + lg[:, 128:256] + _dot(front["xl"], wr_ref[:, 0:128]) + br_ref[...]
        front["routed"] = _router_gates(lg)

    def front_rank():
        routed = front["routed"]
        gates = jnp.where(lane < MOE_EXPERTS, routed, 0.0)
        in_grp = (lane >= MOE_EXPERTS) & (lane < MOE_EXPERTS + MOE_GROUPS) & (routed > 0.0)
        g_hi = gates.astype(BF16).astype(F32)
        g_mid = (gates - g_hi).astype(BF16).astype(F32)
        g_lo = gates - g_hi - g_mid
        gs_nx[...] = (g_hi + pltpu.roll(g_mid, MOE_EXPERTS, axis=1)
                      + pltpu.roll(g_lo, 2 * MOE_EXPERTS, axis=1)).astype(BF16)
        rkn = _dot(ltri_ref[...], routed.astype(BF16))
        rkn = jnp.where(in_grp, rkn, -1.0)
        rkc_nx[...] = rkn
        rkr_nx[...] = jnp.transpose(rkn)

    front_stages = [front_project, front_norm, front_route, front_rank]

    rkc = rkc_ref[...]
    rkr = rkr_ref[...]
    cnt = jnp.sum(jnp.where(rkc >= 0.0, 1.0, 0.0), axis=0, keepdims=True)
    rk_cols = [jnp.sum(jnp.where(lane == MOE_EXPERTS + g, rkc, 0.0), axis=-1, keepdims=True)
               for g in range(MOE_GROUPS)]
    rk_rows = [rkr[MOE_EXPERTS + g:MOE_EXPERTS + g + 1, :] for g in range(MOE_GROUPS)]

    def gather_mat(g, base, rows):
        r_io = lax.broadcasted_iota(jnp.int32, (rows, tm), 0).astype(F32) + base
        return jnp.where(rk_rows[g] == r_io, 1.0, 0.0).astype(BF16)

    def scatter_mat(g, base, kpad):
        c_io = lax.broadcasted_iota(jnp.int32, (tm, kpad), 1).astype(F32) + base
        return jnp.where(rk_cols[g] == c_io, 1.0, 0.0).astype(BF16)

    def experts(g, xg, gg, kpad):
        rows = xg.shape[0]
        glane = lax.broadcasted_iota(jnp.int32, (rows, 128), 1)
        hs = []
        for j in range(MOE_PER_GROUP):
            e = g * MOE_PER_GROUP + j
            ge = jnp.sum(jnp.where((glane % MOE_EXPERTS == e) & (glane < 3 * MOE_EXPERTS), gg, 0.0),
                         axis=-1, keepdims=True)
            hs.append((_silu(_dot(xg, wg_ref[0, e])) * _dot(xg, wu_ref[0, e]) * ge).astype(BF16))
        wd = wd_ref[0, g * MOE_PER_GROUP:(g + 1) * MOE_PER_GROUP].reshape(MOE_PER_GROUP * MOE_HIDDEN, D_MODEL)
        out = _dot(jnp.concatenate(hs, axis=-1), wd).astype(BF16)
        if kpad > rows:
            out = jnp.concatenate([out, jnp.zeros((kpad - rows, D_MODEL), BF16)], axis=0)
        return out

    in_cap = lambda rk: (rk >= 0.0) & (rk < MOE_CAP)
    slot_col = sum(jnp.where(in_cap(rk_cols[g]), rk_cols[g] + (g * MOE_CAP + 1.0), 0.0) for g in range(MOE_GROUPS)) - 1.0
    slot_row = sum(jnp.where(in_cap(rk_rows[g]), rk_rows[g] + (g * MOE_CAP + 1.0), 0.0) for g in range(MOE_GROUPS)) - 1.0
    slots = MOE_GROUPS * MOE_CAP
    r_io = lax.broadcasted_iota(jnp.int32, (slots, tm), 0).astype(F32)
    p_all = jnp.where(slot_row == r_io, 1.0, 0.0).astype(BF16)
    xg_all = _dot(p_all, xb_ref[...]).astype(BF16)
    gg_all = _dot(p_all, gs_ref[...])
    outs = []
    for g in range(MOE_GROUPS):
        front_stages[g]()
        outs.append(experts(g, xg_all[g * MOE_CAP:(g + 1) * MOE_CAP], gg_all[g * MOE_CAP:(g + 1) * MOE_CAP], MOE_CAP))
    c_io = lax.broadcasted_iota(jnp.int32, (tm, slots), 1).astype(F32)
    pt_all = jnp.where(slot_col == c_io, 1.0, 0.0).astype(BF16)
    acc_ref[...] = _dot(pt_all, jnp.concatenate(outs, axis=0))

    for g in range(MOE_GROUPS):
        n_ov = jnp.maximum(cnt[0, MOE_EXPERTS + g].astype(jnp.int32) - MOE_CAP + MOE_OV - 1, 0) // MOE_OV

        def overflow(i, _, g=g):
            base = (MOE_CAP + i * MOE_OV).astype(F32)
            p = gather_mat(g, base, MOE_OV)
            out = experts(g, _dot(p, xb_ref[...]).astype(BF16), _dot(p, gs_ref[...]), MOE_OV)
            acc_ref[...] += _dot(scatter_mat(g, base, MOE_OV), out)
            return 0

        lax.fori_loop(0, n_ov, overflow, 0)

    xb_ref[...] = xb_nx[...]
    gs_ref[...] = gs_nx[...]
    rkc_ref[...] = rkc_nx[...]
    rkr_ref[...] = rkr_nx[...]
    o_ref[...] = _layer_norm(DN_ALPHA * x_ref[...] + acc_ref[...], g_ref[...], b_ref[...])
    x_ref[...] = x_nx[...]


def _layer_tail(mix, att, x, wm, wa, ln1, wr, br, wg, wu, wd, layer, ln2, glu_w=None, glu_b=None, tm=512):
    t = x.shape[0]
    tm = min(tm, t)
    nt = t // tm
    idx = jnp.arange(tm)
    ltri = (idx[:, None] > idx[None, :]).astype(BF16)
    row_in = lambda w: pl.BlockSpec((tm, w), lambda s: (jnp.minimum(s, nt - 1), 0))
    row_out = pl.BlockSpec((tm, D_MODEL), lambda s: (jnp.maximum(s - 1, 0), 0))
    full = lambda a: pl.BlockSpec(a.shape, lambda s: (0,) * a.ndim, pipeline_mode=pl.Buffered(1))
    layer_w = lambda a: pl.BlockSpec((1,) + a.shape[1:], lambda s: (layer, 0, 0, 0),
                                     pipeline_mode=pl.Buffered(1))
    args = [mix, att, x, wm, wa, ln1[0], ln1[1], wr, br, ltri, wg, wu, wd, ln2[0], ln2[1]]
    specs = [row_in(mix.shape[1]), row_in(XA_WIDTH), row_in(D_MODEL), full(wm), full(wa), full(ln1[0]),
             full(ln1[1]), full(wr), full(br), full(ltri), layer_w(wg), layer_w(wu), layer_w(wd),
             full(ln2[0]), full(ln2[1])]
    if glu_w is not None:
        args += [glu_w, glu_b]
        specs += [full(glu_w), full(glu_b)]
    parked = [pltpu.VMEM((tm, D_MODEL), BF16), pltpu.VMEM((tm, 128), BF16), pltpu.VMEM((tm, D_MODEL), F32),
              pltpu.VMEM((tm, 128), F32), pltpu.VMEM((128, tm), F32)]
    return pl.pallas_call(
        functools.partial(_tail_kernel, glu=glu_w is not None),
        grid=(nt + 1,),
        in_specs=specs,
        out_specs=row_out,
        out_shape=jax.ShapeDtypeStruct((t, D_MODEL), F32),
        scratch_shapes=[pltpu.VMEM((tm, D_MODEL), F32)] + parked + parked,
        compiler_params=pltpu.CompilerParams(dimension_semantics=("arbitrary",),
                                             vmem_limit_bytes=TAIL_VMEM_LIMIT),
        name="layer_tail",
    )(*args)


def _s5_kernel(*refs, tlen, ngrp, chained, precise):
    if chained:
        u_ref, wb_ref, wc_ref, lam_ref, d_ref, y_ref, ore_ref, oim_ref, bu_ref, h_ref, cr_ref, ci_ref = refs
        u = jnp.swapaxes(u_ref[...], 0, 1).reshape(8 * tlen, 128)
    else:
        (u_ref, h0r_ref, h0i_ref, wb_ref, wc_ref, lam_ref, d_ref,
         y_ref, ore_ref, oim_ref, bu_ref, h_ref, cr_ref, ci_ref) = refs
        gr = 8 * tlen
        u = jnp.concatenate(
            [jnp.swapaxes(u_ref[g * gr:(g + 1) * gr, :].reshape(8, tlen, 128), 0, 1).reshape(gr, 128)
             for g in range(ngrp)], axis=0)
    if precise:
        bu_ref[...] = _dot_hi(u, wb_ref[0])
    else:
        bu_ref[...] = _dot(u.astype(BF16), wb_ref[0].astype(BF16))
    ns = S5_SLAB_STATE
    lr = jnp.broadcast_to(lam_ref[0, 0:1, :], (8, ns))
    li = jnp.broadcast_to(lam_ref[0, 1:2, :], (8, ns))

    def step(tile, hr, hi):
        rows = pl.ds(pl.multiple_of(tile * 8, 8), 8)
        nr = lr * hr - li * hi + bu_ref[rows, 0:ns]
        ni = lr * hi + li * hr + bu_ref[rows, ns:2 * ns]
        h_ref[rows, 0:ns] = nr
        h_ref[rows, ns:2 * ns] = ni
        return nr, ni

    if chained:
        @pl.when(pl.program_id(2) == 0)
        def _():
            cr_ref[...] = jnp.zeros_like(cr_ref)
            ci_ref[...] = jnp.zeros_like(ci_ref)

        hr, hi = lax.fori_loop(0, tlen, lambda t, c: step(t, *c), (cr_ref[...], ci_ref[...]), unroll=8)
        cr_ref[...] = hr
        ci_ref[...] = hi
        ore_ref[...] = hr
        oim_ref[...] = hi
    else:
        cr_ref[...] = jnp.transpose(h0r_ref[0])
        ci_ref[...] = jnp.transpose(h0i_ref[0])

        def group(g, _):
            s0 = pl.multiple_of(g * 8, 8)
            hr, hi = cr_ref[pl.ds(s0, 8), :], ci_ref[pl.ds(s0, 8), :]
            for t in range(tlen):
                hr, hi = step(g * tlen + t, hr, hi)
            cr_ref[pl.ds(s0, 8), :] = hr
            ci_ref[pl.ds(s0, 8), :] = hi
            return 0

        lax.fori_loop(0, ngrp, group, 0)
        ore_ref[...] = jnp.transpose(cr_ref[...])
        oim_ref[...] = jnp.transpose(ci_ref[...])
    y = _dot(h_ref[...].astype(BF16), wc_ref[0].astype(BF16)) + d_ref[0] * u
    if chained:
        y_ref[...] = jnp.swapaxes(y.reshape(tlen, 8, 128), 0, 1)
    else:
        gr = 8 * tlen
        for g in range(ngrp):
            y_ref[g * gr:(g + 1) * gr, :] = jnp.swapaxes(
                y[g * gr:(g + 1) * gr, :].reshape(tlen, 8, 128), 0, 1).reshape(gr, 128)


def _s5_mixer_prompt(z, params, seq_len, tlen=256):
    t = z.shape[0]
    nseq = t // seq_len
    ns = S5_SLAB_STATE
    z3 = z.reshape(nseq, seq_len, z.shape[1])
    slab = lambda a: pl.BlockSpec((1,) + a.shape[1:], lambda j, b, i: (j,) + (0,) * (a.ndim - 1))
    u_spec = pl.BlockSpec((8, tlen, 128), lambda j, b, i: (b, i, j))
    st_spec = pl.BlockSpec((8, ns), lambda j, b, i: (b, j))
    st_shape = jax.ShapeDtypeStruct((nseq, S5_SLABS * ns), F32)
    rows = 8 * tlen
    y, hre, him = pl.pallas_call(
        functools.partial(_s5_kernel, tlen=tlen, ngrp=1, chained=True, precise=False),
        grid=(S5_SLABS, nseq // 8, seq_len // tlen),
        in_specs=[u_spec] + [slab(a) for a in params],
        out_specs=[u_spec, st_spec, st_spec],
        out_shape=[jax.ShapeDtypeStruct((nseq, seq_len, MIX_WIDTH), F32), st_shape, st_shape],
        scratch_shapes=[pltpu.VMEM((rows, 2 * ns), F32), pltpu.VMEM((rows, 2 * ns), F32),
                        pltpu.VMEM((8, ns), F32), pltpu.VMEM((8, ns), F32)],
        compiler_params=_cparams("parallel", "parallel", "arbitrary"),
        name="s5_mixer",
    )(z3, *params)
    return y.reshape(t, MIX_WIDTH), hre, him


def _s5_mixer_sample(z, params, seq_len, h0r, h0i, layer):
    t = z.shape[0]
    nseq = t // seq_len
    ns = S5_SLAB_STATE
    seqs = 128
    rows = seqs * seq_len
    slab = lambda a: pl.BlockSpec((1,) + a.shape[1:], lambda j, i: (j,) + (0,) * (a.ndim - 1))
    u_spec = pl.BlockSpec((rows, 128), lambda j, i: (i, j))
    h0_spec = pl.BlockSpec((1, ns, seqs), lambda j, i: (layer, j, i))
    st_spec = pl.BlockSpec((ns, seqs), lambda j, i: (j, i))
    st_shape = jax.ShapeDtypeStruct((S5_SLABS * ns, nseq), F32)
    return pl.pallas_call(
        functools.partial(_s5_kernel, tlen=seq_len, ngrp=seqs // 8, chained=False, precise=True),
        grid=(S5_SLABS, nseq // seqs),
        in_specs=[u_spec, h0_spec, h0_spec] + [slab(a) for a in params],
        out_specs=[u_spec, st_spec, st_spec],
        out_shape=[jax.ShapeDtypeStruct((t, MIX_WIDTH), F32), st_shape, st_shape],
        scratch_shapes=[pltpu.VMEM((rows, 2 * ns), F32), pltpu.VMEM((rows, 2 * ns), F32),
                        pltpu.VMEM((seqs, ns), F32), pltpu.VMEM((seqs, ns), F32)],
        compiler_params=_cparams("parallel", "parallel"),
        name="s5_mixer",
    )(z, h0r, h0i, *params)


def _s5_params(lam_re, lam_im, log_dt, b_re, b_im, c_re, c_im, d_skip):
    lre, lim = lam_re.astype(F32), lam_im.astype(F32)
    dt = jnp.exp(log_dt.astype(F32))[:, None]
    mag = jnp.exp(lre * dt)
    bar_re, bar_im = mag * jnp.cos(lim * dt), mag * jnp.sin(lim * dt)
    nre, nim = bar_re - 1.0, bar_im
    den = lre * lre + lim * lim
    fre, fim = (nre * lre + nim * lim) / den, (nim * lre - nre * lim) / den
    bre, bim = b_re.astype(F32), b_im.astype(F32)
    bb_re = fre[..., None] * bre - fim[..., None] * bim
    bb_im = fre[..., None] * bim + fim[..., None] * bre
    eye = jnp.eye(8, dtype=F32)
    gps = 8

    def blockdiag_in(m):
        m = m.reshape(S5_SLABS, gps, S5_STATE, S5_GROUP).transpose(0, 1, 3, 2)
        return jnp.einsum('jgcp,gh->jgchp', m, eye).reshape(S5_SLABS, 128, S5_SLAB_STATE)

    def blockdiag_out(m):
        m = m.reshape(S5_SLABS, gps, S5_GROUP, S5_STATE).transpose(0, 1, 3, 2)
        return jnp.einsum('jgpc,gh->jgphc', m, eye).reshape(S5_SLABS, S5_SLAB_STATE, 128)

    wb = jnp.concatenate([blockdiag_in(bb_re), blockdiag_in(bb_im)], axis=2)
    wc = jnp.concatenate([blockdiag_out(c_re.astype(F32)), blockdiag_out(-c_im.astype(F32))], axis=1)
    lam = jnp.stack([bar_re.reshape(S5_SLABS, S5_SLAB_STATE), bar_im.reshape(S5_SLABS, S5_SLAB_STATE)], axis=1)
    dsk = d_skip.astype(F32).reshape(S5_SLABS, 1, 128)
    return wb, wc, lam, dsk


def _head_norm_gate(o, gn, gate):
    col = lax.broadcasted_iota(jnp.int32, o.shape, 1)
    valid = col < RET_DV
    mu = jnp.sum(o, axis=-1, keepdims=True) * (1.0 / RET_DV)
    d = jnp.where(valid, o - mu, 0.0)
    var = jnp.sum(d * d, axis=-1, keepdims=True) * (1.0 / RET_DV)
    return d * lax.rsqrt(var + NORM_EPS) * gn * _silu(gate)


def _load_states(s_in_ref, spad_ref, nseg, dk):
    spad_ref[...] = jnp.zeros_like(spad_ref)
    for s in range(nseg):
        for h in range(HEADS):
            spad_ref[h, s * DK_PAD:s * DK_PAD + dk, 0:RET_DV] = s_in_ref[s, h]


def _ret_kernel(*refs, nseg, gammas):
    if nseg == 1:
        (q_ref, k_ref, v_ref, g_ref, cos_ref, sin_ref, dmat_ref, inner_ref, zeta_ref, gn_ref,
         o_ref, so_ref, st_ref) = refs

        @pl.when(pl.program_id(1) == 0)
        def _():
            st_ref[...] = jnp.zeros_like(st_ref)
    else:
        (q_ref, k_ref, v_ref, g_ref, cos_ref, sin_ref, dmat_ref, inner_ref, zeta_ref, gn_ref, s_in_ref,
         o_ref, so_ref, st_ref) = refs
        st_ref[:, RET_DV:DV_PAD, :] = jnp.zeros((HEADS, DV_PAD - RET_DV, nseg * RET_DK), F32)
        for s in range(nseg):
            for h in range(HEADS):
                st_ref[h, 0:RET_DV, s * RET_DK:(s + 1) * RET_DK] = s_in_ref[s, h]
    r = q_ref.shape[0]
    seg_len = r // nseg
    cos = cos_ref[...]
    sin = sin_ref[...]
    rot = lambda x: x * cos + pltpu.roll(x, RET_DK // 2, axis=1) * sin
    rowseg = lax.broadcasted_iota(jnp.int32, (r, DK_PAD), 0) // seg_len
    colseg = lax.broadcasted_iota(jnp.int32, (DV_PAD, r), 1) // seg_len
    for h in range(HEADS):
        q = rot(q_ref[:, h * DK_PAD:(h + 1) * DK_PAD])
        k = rot(k_ref[:, h * DK_PAD:(h + 1) * DK_PAD]) * (RET_DK ** -0.5)
        v = v_ref[:, h * DV_PAD:(h + 1) * DV_PAD]
        vb = v.astype(BF16)
        qb = q.astype(BF16)
        att = _dot_nt(qb, k.astype(BF16)) * dmat_ref[h]
        o = _dot(att.astype(BF16), vb)
        st_all = st_ref[h]
        if nseg == 1:
            qbd = qb
        else:
            qbd = jnp.concatenate([jnp.where(rowseg == s, q, 0.0) for s in range(nseg)], axis=1).astype(BF16)
        o = o + _dot_nt(qbd, st_all.astype(BF16)) * inner_ref[h]
        kzb = (k * zeta_ref[h]).astype(BF16)
        vt = jnp.transpose(v)
        for s in range(nseg):
            vts = vt if nseg == 1 else jnp.where(colseg == s, vt, 0.0)
            st_new = st_all[:, s * RET_DK:(s + 1) * RET_DK] * gammas[h] + _dot(vts.astype(BF16), kzb)
            if nseg == 1:
                st_ref[h] = st_new
                so_ref[0, h] = st_new[0:RET_DV, :]
            else:
                so_ref[s, h] = st_new[0:RET_DV, :]
        o_ref[:, h * DV_PAD:(h + 1) * DV_PAD] = _head_norm_gate(
            o, gn_ref[:, h * DV_PAD:(h + 1) * DV_PAD], g_ref[:, h * DV_PAD:(h + 1) * DV_PAD]).astype(o_ref.dtype)


def _ret_consts(seg_len, nseg, pos):
    hh = jnp.arange(RET_HEADS, dtype=F32)
    log_g = jnp.log1p(-jnp.exp2(-5.0 - hh))
    idx = jnp.arange(seg_len, dtype=F32)
    rel = idx[:, None] - idx[None, :]
    dmat = jnp.where(rel >= 0, jnp.exp(log_g[:, None, None] * jnp.maximum(rel, 0.0)), 0.0)
    inner = jnp.exp(log_g[:, None] * (idx + 1.0))
    zeta = jnp.exp(log_g[:, None] * (seg_len - 1.0 - idx))
    r = seg_len * nseg
    seg = jnp.arange(r) // seg_len
    same = (seg[:, None] == seg[None, :]).astype(F32)
    dmat = jnp.tile(dmat, (1, nseg, nseg)) * same[None]
    inner = jnp.broadcast_to(jnp.tile(inner, (1, nseg))[:, :, None], (RET_HEADS, r, DV_PAD))
    zeta = jnp.broadcast_to(jnp.tile(zeta, (1, nseg))[:, :, None], (RET_HEADS, r, DK_PAD))
    half = RET_DK // 2
    inv_freq = ROPE_BASE ** (-jnp.arange(half, dtype=F32) / half)
    ang = pos.astype(F32)[:, None] * inv_freq
    cos2 = jnp.concatenate([jnp.cos(ang), jnp.cos(ang)], axis=-1)
    sin2 = jnp.concatenate([-jnp.sin(ang), jnp.sin(ang)], axis=-1)
    return dmat, inner, zeta, cos2, sin2


def _ret_gammas(seg_len):
    return tuple(float((1.0 - 2.0 ** (-5.0 - h)) ** seg_len) for h in range(RET_HEADS))


def _ret_mixer(z, gn, seq_len, pos, s0=None):
    t = z.shape[0]
    nseq = t // seq_len
    if s0 is None:
        r, nseg, seg_len = RET_CHUNK, 1, RET_CHUNK
        steps = seq_len // r
        grid = (nseq, steps)
        rowmap = lambda c: (lambda n, i: (n * steps + i, c))
        tabmap = lambda n, i: (i, 0)
        stmap = lambda n, i: (n, 0, 0, 0)
        st_rows = 1
        sem = ("parallel", "arbitrary")
    else:
        seg_len = seq_len
        nseg = 128 // seg_len
        r = 128
        grid = (t // r,)
        rowmap = lambda c: (lambda i: (i, c))
        tabmap = lambda i: (0, 0)
        stmap = lambda i: (i, 0, 0, 0)
        st_rows = nseg
        sem = ("arbitrary",)
    dmat, inner, zeta, cos2, sin2 = _ret_consts(seg_len, nseg, pos)
    if s0 is not None:
        cos2 = jnp.tile(cos2, (nseg, 1))
        sin2 = jnp.tile(sin2, (nseg, 1))
    full = lambda a: pl.BlockSpec(a.shape, lambda *_: (0,) * a.ndim)
    st_spec = pl.BlockSpec((st_rows, RET_HEADS, RET_DV, RET_DK), stmap)
    in_specs = [pl.BlockSpec((r, 512), rowmap(0)), pl.BlockSpec((r, 512), rowmap(1)),
                pl.BlockSpec((r, 1024), rowmap(1)), pl.BlockSpec((r, 1024), rowmap(2)),
                pl.BlockSpec((r, DK_PAD), tabmap), pl.BlockSpec((r, DK_PAD), tabmap),
                full(dmat), full(inner), full(zeta), full(gn)]
    args = [z, z, z, z, cos2, sin2, dmat, inner, zeta, gn]
    if s0 is not None:
        in_specs.append(st_spec)
        args.append(s0)
    mix, s_new = pl.pallas_call(
        functools.partial(_ret_kernel, nseg=nseg, gammas=_ret_gammas(seg_len)),
        grid=grid,
        in_specs=in_specs,
        out_specs=[pl.BlockSpec((r, HEADS * DV_PAD), rowmap(0)), st_spec],
        out_shape=[jax.ShapeDtypeStruct((t, HEADS * DV_PAD), BF16),
                   jax.ShapeDtypeStruct((nseq, RET_HEADS, RET_DV, RET_DK), F32)],
        scratch_shapes=[pltpu.VMEM((HEADS, DV_PAD, nseg * RET_DK), F32)],
        compiler_params=_cparams(*sem),
        name="retention",
    )(*args)
    return mix, s_new


def _log_sigmoid(x):
    return jnp.minimum(x, 0.0) - jnp.log1p(jnp.exp(-jnp.abs(x)))


def _gla_kernel(*refs, nseg, chained):
    if chained:
        (q_ref, k_ref, v_ref, r_ref, low_ref, tril_ref, same_ref, wg_ref, bg_ref, gn_ref,
         o_ref, so_ref, spad_ref) = refs

        @pl.when(pl.program_id(1) == 0)
        def _():
            spad_ref[...] = jnp.zeros_like(spad_ref)
    else:
        (q_ref, k_ref, v_ref, r_ref, low_ref, tril_ref, same_ref, wg_ref, bg_ref, gn_ref, s_in_ref,
         o_ref, so_ref, spad_ref) = refs
        _load_states(s_in_ref, spad_ref, nseg, GLA_DK)
    r = q_ref.shape[0]
    seg_len = r // nseg
    tril = tril_ref[...]
    causal = tril > 0.0
    la = _log_sigmoid(_dot_hi(low_ref[...], wg_ref[...]) + bg_ref[...]) * (1.0 / GLA_TAU)
    b = _dot_hi(tril, la)
    b_tot = _dot_hi(same_ref[...], la)
    rowseg = lax.broadcasted_iota(jnp.int32, (r, DK_PAD), 0) // seg_len
    colseg = lax.broadcasted_iota(jnp.int32, (DK_PAD, r), 1) // seg_len

    for h in range(HEADS):
        sl = slice(h * DK_PAD, (h + 1) * DK_PAD)
        bh = b[:, sl]
        q = q_ref[:, sl] * (GLA_DK ** -0.5)
        k = k_ref[:, sl]
        vb = v_ref[:, h * DV_PAD:(h + 1) * DV_PAD].astype(BF16)
        q_t = q * jnp.exp(bh)
        k_t = k * jnp.exp(-bh)
        att = jnp.where(causal, _dot_nt(q_t.astype(BF16), k_t.astype(BF16)), 0.0)
        o = _dot(att.astype(BF16), vb)
        kdt = jnp.transpose(k * jnp.exp(b_tot[:, sl] - bh))
        lat = jnp.transpose(la[:, sl])

        def seg_update(s, s_old):
            kds = jnp.where(colseg == s, kdt, 0.0) if nseg > 1 else kdt
            las = jnp.where(colseg == s, lat, 0.0) if nseg > 1 else lat
            decay = jnp.exp(jnp.sum(las, axis=-1, keepdims=True))
            return s_old * decay + _dot(kds.astype(BF16), vb)

        if chained:
            qb = q_t.astype(BF16)
            s_cur = spad_ref[h]
            inter = []
            for s in range(nseg):
                inter.append(_dot(qb[s * seg_len:(s + 1) * seg_len], s_cur.astype(BF16)))
                s_cur = seg_update(s, s_cur)
            o = o + jnp.concatenate(inter, axis=0)
            spad_ref[h] = s_cur
            so_ref[0, h] = s_cur[0:GLA_DK, 0:GLA_DV]
        else:
            s_all = spad_ref[h]
            qbd = jnp.concatenate([jnp.where(rowseg == s, q_t, 0.0) for s in range(nseg)], axis=1).astype(BF16)
            o = o + _dot(qbd, s_all.astype(BF16))
            for s in range(nseg):
                so_ref[s, h] = seg_update(s, s_all[s * DK_PAD:(s + 1) * DK_PAD])[0:GLA_DK, 0:GLA_DV]
        o_ref[:, h * DV_PAD:(h + 1) * DV_PAD] = _head_norm_gate(
            o, gn_ref[:, h * DV_PAD:(h + 1) * DV_PAD], r_ref[:, h * DV_PAD:(h + 1) * DV_PAD]).astype(o_ref.dtype)


def _gla_mixer(z, wg, bg, gn, seq_len, s0=None):
    t = z.shape[0]
    nseq = t // seq_len
    r = 128
    if s0 is None:
        seg_len = GLA_CHUNK
        nseg = r // seg_len
        steps = seq_len // r
        grid = (nseq, steps)
        rowmap = lambda c: (lambda n, i: (n * steps + i, c))
        stmap = lambda n, i: (n, 0, 0, 0)
        st_rows = 1
        scratch_rows = DK_PAD
        sem = ("parallel", "arbitrary")
    else:
        seg_len = seq_len
        nseg = r // seg_len
        grid = (t // r,)
        rowmap = lambda c: (lambda i: (i, c))
        stmap = lambda i: (i, 0, 0, 0)
        st_rows = nseg
        scratch_rows = nseg * DK_PAD
        sem = ("arbitrary",)
    seg = jnp.arange(r) // seg_len
    idx = jnp.arange(r)
    same = (seg[:, None] == seg[None, :]).astype(F32)
    tril = same * (idx[:, None] >= idx[None, :]).astype(F32)
    full = lambda a: pl.BlockSpec(a.shape, lambda *_: (0,) * a.ndim)
    st_spec = pl.BlockSpec((st_rows, GLA_HEADS, GLA_DK, GLA_DV), stmap)
    in_specs = [pl.BlockSpec((r, 512), rowmap(0)), pl.BlockSpec((r, 512), rowmap(1)),
                pl.BlockSpec((r, 1024), rowmap(1)), pl.BlockSpec((r, 1024), rowmap(2)),
                pl.BlockSpec((r, 128), rowmap(COL_LOW // 128)),
                full(tril), full(same), full(wg), full(bg), full(gn)]
    args = [z, z, z, z, z, tril, same, wg, bg, gn]
    if s0 is not None:
        in_specs.append(st_spec)
        args.append(s0)
    mix, s_new = pl.pallas_call(
        functools.partial(_gla_kernel, nseg=nseg, chained=s0 is None),
        grid=grid,
        in_specs=in_specs,
        out_specs=[pl.BlockSpec((r, HEADS * DV_PAD), rowmap(0)), st_spec],
        out_shape=[jax.ShapeDtypeStruct((t, HEADS * DV_PAD), BF16),
                   jax.ShapeDtypeStruct((nseq, GLA_HEADS, GLA_DK, GLA_DV), F32)],
        scratch_shapes=[pltpu.VMEM((HEADS, scratch_rows, DV_PAD), F32)],
        compiler_params=_cparams(*sem),
        name="gla",
    )(*args)
    return mix, s_new


def _pad_heads(w, width, pad):
    lead = w.shape[:-1]
    w = w.reshape(lead + (HEADS, width))
    w = jnp.pad(w, [(0, 0)] * len(lead) + [(0, 0), (0, pad - width)])
    return w.reshape(lead + (HEADS * pad,))


def _pad_head_rows(w, width, pad):
    return jnp.swapaxes(_pad_heads(jnp.swapaxes(w, 0, 1), width, pad), 0, 1)


def _ret_in_weight(w):
    qk = RET_HEADS * RET_DK
    v = _pad_heads(w[:, 2 * qk:2 * qk + MIX_WIDTH], RET_DV, DV_PAD)
    g = _pad_heads(w[:, 2 * qk + MIX_WIDTH:2 * qk + 2 * MIX_WIDTH], RET_DV, DV_PAD)
    return jnp.concatenate([w[:, :2 * qk], v, g, w[:, -XA_WIDTH:]], axis=1).astype(BF16)


def _gla_in_weight(w):
    qk = GLA_HEADS * GLA_DK
    q = _pad_heads(w[:, :qk], GLA_DK, DK_PAD)
    k = _pad_heads(w[:, qk:2 * qk], GLA_DK, DK_PAD)
    v = _pad_heads(w[:, 2 * qk:2 * qk + MIX_WIDTH], GLA_DV, DV_PAD)
    r = _pad_heads(w[:, 2 * qk + MIX_WIDTH:2 * qk + 2 * MIX_WIDTH], GLA_DV, DV_PAD)
    low = jnp.pad(w[:, 2 * qk + 2 * MIX_WIDTH:2 * qk + 2 * MIX_WIDTH + GLA_RANK], [(0, 0), (0, 128 - GLA_RANK)])
    return jnp.concatenate([q, k, v, r, w[:, -XA_WIDTH:], low], axis=1).astype(BF16)


def kernel(x_prompt, x_sample, mem_prompt, state_s5_re, state_s5_im, state_ret, state_gla,
           cache_mem_k, cache_mem_v,
           w_in_s5, s5_lam_re, s5_lam_im, s5_log_dt, s5_b_re, s5_b_im, s5_c_re, s5_c_im,
           s5_d, s5_w_glu, s5_b_glu,
           w_in_ret, ret_gn,
           w_in_gla, gla_w_gate2, gla_b_gate2, gla_gn,
           w_mem_k, w_mem_v, w_out, ln_g, ln_b,
           moe_w_grp, moe_b_grp, moe_w_exp, moe_b_exp, moe_w_gate, moe_w_up, moe_w_down):
    bp, lp, _ = x_prompt.shape
    bs, ls, _ = x_sample.shape
    xp = x_prompt.reshape(bp * lp, D_MODEL)
    xs = x_sample.reshape(bs * ls, D_MODEL)
    pos_p = jnp.arange(lp, dtype=jnp.int32)
    pos_s = PAST_LEN + jnp.arange(ls, dtype=jnp.int32)

    to_t = lambda c: jnp.transpose(c, (0, 1, 3, 4, 2)).reshape(DEPTH, c.shape[1], XA_WIDTH, N_MEM)
    from_t = lambda c: jnp.transpose(c.reshape(DEPTH, c.shape[1], XA_HEADS, XA_HEAD_DIM, N_MEM), (0, 1, 4, 2, 3))
    mkt_all, mvt_all = _mem_kv(mem_prompt, jnp.swapaxes(w_mem_k, 1, 2).astype(BF16),
                               jnp.swapaxes(w_mem_v, 1, 2).astype(BF16))
    cache_kt = to_t(cache_mem_k)
    cache_vt = to_t(cache_mem_v)
    ret_t = jnp.swapaxes(state_ret, 3, 4)
    s5_t = lambda s: jnp.transpose(s, (0, 2, 3, 1)).reshape(s.shape[0], S5_GROUPS * S5_STATE, bs)
    s5_re_t, s5_im_t = s5_t(state_s5_re), s5_t(state_s5_im)
    moe_wg, moe_wu, moe_wd = moe_w_gate.astype(BF16), moe_w_up.astype(BF16), moe_w_down.astype(BF16)

    p_re, p_im, p_ret, p_gla = [], [], [], []
    s_re, s_im, s_ret, s_gla = [], [], [], []
    for i in range(DEPTH):
        kind, j = i % N_MIXERS, i // N_MIXERS
        ln1 = (ln_g[i, 0].reshape(1, D_MODEL), ln_b[i, 0].reshape(1, D_MODEL))
        ln2 = (ln_g[i, 1].reshape(1, D_MODEL), ln_b[i, 1].reshape(1, D_MODEL))
        wo = w_out[i]
        wo_att = wo[MIX_WIDTH:].astype(BF16)
        glu = {}
        if kind == 0:
            params = _s5_params(s5_lam_re[j], s5_lam_im[j], s5_log_dt[j], s5_b_re[j], s5_b_im[j],
                                s5_c_re[j], s5_c_im[j], s5_d[j])
            zp = _matmul(xp, w_in_s5[j].astype(BF16), 512)
            zs = _matmul(xs, w_in_s5[j], 512, precise=True)
            mix_p, a_re, a_im = _s5_mixer_prompt(zp, params, lp)
            mix_s, b_re, b_im = _s5_mixer_sample(zs, params, ls, s5_re_t, s5_im_t, j)
            p_re.append(a_re), p_im.append(a_im), s_re.append(b_re), s_im.append(b_im)
            xa_col = MIX_WIDTH
            wo_mix = wo[:MIX_WIDTH].astype(BF16)
            glu = dict(glu_w=s5_w_glu[j].astype(BF16), glu_b=s5_b_glu[j].reshape(1, MIX_WIDTH))
        elif kind == 1:
            w = _ret_in_weight(w_in_ret[j])
            gn = _pad_heads(ret_gn[j], RET_DV, DV_PAD).reshape(1, HEADS * DV_PAD)
            zp = _matmul(xp, w, 512)
            zs = _matmul(xs, w, 512)
            mix_p, a_s = _ret_mixer(zp, gn, lp, pos_p)
            mix_s, b_s = _ret_mixer(zs, gn, ls, pos_s, s0=ret_t[j])
            p_ret.append(a_s), s_ret.append(b_s)
            xa_col = COL_XA
            wo_mix = _pad_head_rows(wo[:MIX_WIDTH], RET_DV, DV_PAD).astype(BF16)
        else:
            w = _gla_in_weight(w_in_gla[j])
            gn = _pad_heads(gla_gn[j], GLA_DV, DV_PAD).reshape(1, HEADS * DV_PAD)
            wg = jnp.pad(_pad_heads(gla_w_gate2[j], GLA_DK, DK_PAD), [(0, 128 - GLA_RANK), (0, 0)])
            bg = _pad_heads(gla_b_gate2[j], GLA_DK, DK_PAD).reshape(1, HEADS * DK_PAD)
            zp = _matmul(xp, w, 512)
            zs = _matmul(xs, w, 512)
            mix_p, a_s = _gla_mixer(zp, wg, bg, gn, lp)
            mix_s, b_s = _gla_mixer(zs, wg, bg, gn, ls, s0=state_gla[j])
            p_gla.append(a_s), s_gla.append(b_s)
            xa_col = COL_XA
            wo_mix = _pad_head_rows(wo[:MIX_WIDTH], GLA_DV, DV_PAD).astype(BF16)
        att_p = _mem_attention(zp, xa_col, mkt_all, mvt_all, i, lp, rl=min(512, lp), nseq=1)
        att_s = _mem_attention(zs, xa_col, cache_kt, cache_vt, i, ls, rl=ls, nseq=128 // ls)
        wr = jnp.pad(jnp.concatenate([moe_w_exp[i], moe_w_grp[i]], axis=1),
                     [(0, 0), (0, 128 - MOE_EXPERTS - MOE_GROUPS)]).astype(F32)
        wr_hi = wr.astype(BF16)
        wr = jnp.concatenate([wr_hi, (wr - wr_hi.astype(F32)).astype(BF16)], axis=1)
        br = jnp.pad(jnp.concatenate([moe_b_exp[i], moe_b_grp[i]]),
                     [(0, 128 - MOE_EXPERTS - MOE_GROUPS)]).reshape(1, 128).astype(F32)
        xp = _layer_tail(mix_p, att_p, xp, wo_mix, wo_att, ln1, wr, br, moe_wg, moe_wu, moe_wd, i, ln2, **glu)
        xs = _layer_tail(mix_s, att_s, xs, wo_mix, wo_att, ln1, wr, br, moe_wg, moe_wu, moe_wd, i, ln2, **glu)

    st = lambda xs_: jnp.stack(xs_)
    s5_p = lambda xs_: st(xs_).reshape(len(xs_), bp, S5_GROUPS, S5_STATE)
    s5_s = lambda xs_: jnp.transpose(st(xs_).reshape(len(xs_), S5_GROUPS, S5_STATE, bs), (0, 3, 1, 2))
    ret_out = lambda xs_: jnp.swapaxes(st(xs_), 3, 4)
    return (xp.reshape(bp, lp, D_MODEL), xs.reshape(bs, ls, D_MODEL),
            s5_p(p_re), s5_p(p_im), ret_out(p_ret), st(p_gla),
            from_t(mkt_all), from_t(mvt_all),
            s5_s(s_re), s5_s(s_im), ret_out(s_ret), st(s_gla))
```

```python
import functools
import math

import jax
import jax.numpy as jnp
from jax import lax
from jax.experimental import pallas as pl
from jax.experimental.pallas import tpu as pltpu

F32 = jnp.float32
BF16 = jnp.bfloat16
HI = lax.Precision.HIGHEST

D_MODEL = 1024
DEPTH = 4
PAST_LEN = 16384
N_MIXERS = 3
MIX_WIDTH = 768
XA_HEADS = 4
XA_HEAD_DIM = 64
XA_WIDTH = 256
N_MEM = 256
S5_GROUP = 16
S5_GROUPS = 48
S5_STATE = 64
S5_SLABS = 6
S5_SLAB_STATE = 512
RET_HEADS = 4
RET_DK = 128
RET_DV = 192
RET_CHUNK = 128
ROPE_BASE = 10000.0
GLA_HEADS = 4
GLA_DK = 96
GLA_DV = 192
GLA_RANK = 16
GLA_TAU = 16.0
GLA_CHUNK = 64
MOE_GROUPS = 4
MOE_PER_GROUP = 4
MOE_EXPERTS = 16
MOE_HIDDEN = 256
DN_ALPHA = (2 * DEPTH) ** 0.25
NORM_EPS = 1e-5

DK_PAD = 128
DV_PAD = 256
HEADS = 4
COL_Q, COL_K, COL_V, COL_G, COL_XA, COL_LOW = 0, 512, 1024, 2048, 3072, 3328
RET_IN_PAD = 3328
GLA_IN_PAD = 3456
VMEM_LIMIT = 52 * 1024 * 1024
TAIL_VMEM_LIMIT = 58 * 1024 * 1024


def _cparams(*sem):
    return pltpu.CompilerParams(dimension_semantics=sem, vmem_limit_bytes=VMEM_LIMIT)


def _dot(a, b):
    return jnp.dot(a, b, preferred_element_type=F32)


def _dot_nt(a, b):
    return lax.dot_general(a, b, (((1,), (1,)), ((), ())), preferred_element_type=F32)


def _dot_hi(a, b):
    return jnp.dot(a, b, preferred_element_type=F32, precision=HI)


def _layer_norm(y, g, b):
    mu = jnp.mean(y, axis=-1, keepdims=True)
    d = y - mu
    var = jnp.mean(d * d, axis=-1, keepdims=True)
    return d * lax.rsqrt(var + NORM_EPS) * g + b


def _sigmoid(x):
    return 1.0 / (1.0 + jnp.exp(-x))


def _silu(x):
    return x * _sigmoid(x)


def _gelu_tanh(x):
    c = math.sqrt(2.0 / math.pi)
    return 0.5 * x * (1.0 + jnp.tanh(c * (x + 0.044715 * (x * x * x))))


def _matmul_kernel(x_ref, w_ref, o_ref, *, precise):
    if precise:
        o_ref[...] = _dot_hi(x_ref[...], w_ref[...])
    else:
        o_ref[...] = _dot(x_ref[...].astype(BF16), w_ref[...])


def _matmul(x, w, tm, precise=False):
    t, k = x.shape
    tm = min(tm, t)
    n = w.shape[1]
    return pl.pallas_call(
        functools.partial(_matmul_kernel, precise=precise),
        grid=(t // tm,),
        in_specs=[pl.BlockSpec((tm, k), lambda i: (i, 0)),
                  pl.BlockSpec((k, n), lambda i: (0, 0))],
        out_specs=pl.BlockSpec((tm, n), lambda i: (i, 0)),
        out_shape=jax.ShapeDtypeStruct((t, n), F32),
        compiler_params=_cparams("parallel"),
        name="in_proj",
    )(x, w)


def _memkv_kernel(x_ref, w_ref, ok_ref, ov_ref):
    kv = _dot_nt(w_ref[...], x_ref[0].astype(BF16))
    for l in range(DEPTH):
        ok_ref[l, 0] = kv[l * XA_WIDTH:(l + 1) * XA_WIDTH]
        ov_ref[l, 0] = kv[(DEPTH + l) * XA_WIDTH:(DEPTH + l + 1) * XA_WIDTH]


def _mem_kv(mem, wkt, wvt):
    n = mem.shape[0]
    w = jnp.concatenate([wkt, wvt], axis=0).reshape(2 * DEPTH * XA_WIDTH, D_MODEL)
    spec_o = pl.BlockSpec((DEPTH, 1, XA_WIDTH, N_MEM), lambda b: (0, b, 0, 0))
    return pl.pallas_call(
        _memkv_kernel,
        grid=(n,),
        in_specs=[pl.BlockSpec((1, N_MEM, D_MODEL), lambda b: (b, 0, 0)),
                  pl.BlockSpec(w.shape, lambda b: (0, 0))],
        out_specs=[spec_o, spec_o],
        out_shape=[jax.ShapeDtypeStruct((DEPTH, n, XA_WIDTH, N_MEM), F32)] * 2,
        compiler_params=_cparams("parallel"),
        name="mem_kv",
    )(mem, w)


def _attn_kernel(q_ref, kt_ref, vt_ref, o_ref, *, nseq, rl):
    head = lax.broadcasted_iota(jnp.int32, (rl, XA_WIDTH), 1) // XA_HEAD_DIM
    outs = []
    for s in range(nseq):
        q = q_ref[s * rl:(s + 1) * rl, :] * (XA_HEAD_DIM ** -0.5)
        qs = jnp.concatenate([jnp.where(head == h, q, 0.0) for h in range(XA_HEADS)], axis=0)
        sc = _dot(qs.astype(BF16), kt_ref[0, s].astype(BF16))
        sc = sc - jnp.max(sc, axis=-1, keepdims=True)
        p = jnp.exp(sc)
        p = p / jnp.sum(p, axis=-1, keepdims=True)
        pv = _dot_nt(p.astype(BF16), vt_ref[0, s].astype(BF16))
        o = jnp.where(head == 0, pv[0:rl], 0.0)
        for h in range(1, XA_HEADS):
            o = o + jnp.where(head == h, pv[h * rl:(h + 1) * rl], 0.0)
        outs.append(o)
    o_ref[...] = jnp.concatenate(outs, axis=0).astype(o_ref.dtype)


def _mem_attention(z, xa_col, mkt, mvt, layer, seq_len, rl, nseq):
    t = z.shape[0]
    rows = rl * nseq
    cb = xa_col // XA_WIDTH
    if nseq == 1:
        per_seq = seq_len // rl
        kv_map = lambda i: (layer, i // per_seq, 0, 0)
    else:
        kv_map = lambda i: (layer, i, 0, 0)
    return pl.pallas_call(
        functools.partial(_attn_kernel, nseq=nseq, rl=rl),
        grid=(t // rows,),
        in_specs=[pl.BlockSpec((rows, XA_WIDTH), lambda i: (i, cb)),
                  pl.BlockSpec((1, nseq, XA_WIDTH, N_MEM), kv_map),
                  pl.BlockSpec((1, nseq, XA_WIDTH, N_MEM), kv_map)],
        out_specs=pl.BlockSpec((rows, XA_WIDTH), lambda i: (i, 0)),
        out_shape=jax.ShapeDtypeStruct((t, XA_WIDTH), BF16),
        compiler_params=_cparams("parallel"),
        name="mem_attention",
    )(z, mkt, mvt)


def _router_gates(lg):
    tm = lg.shape[0]
    lgt = jnp.transpose(lg)
    ninf = jnp.float32(-jnp.inf)
    grow = lax.broadcasted_iota(jnp.int32, (8, tm), 0).astype(F32)
    gm = grow < MOE_GROUPS
    gl = jnp.where(gm, lgt[MOE_EXPERTS:MOE_EXPERTS + 8], ninf)
    ge = jnp.exp(gl - jnp.max(gl, axis=0, keepdims=True))
    gp = ge / jnp.sum(ge, axis=0, keepdims=True)
    pmax = jnp.max(gp, axis=0, keepdims=True)
    gi = jnp.min(jnp.where((gp == pmax) & gm, grow, 1e4), axis=0, keepdims=True)
    erow = lax.broadcasted_iota(jnp.int32, (MOE_EXPERTS, tm), 0).astype(F32)
    lge = lgt[0:MOE_EXPERTS]
    lo = gi * MOE_PER_GROUP
    em = (erow >= lo) & (erow < lo + MOE_PER_GROUP)
    el = jnp.where(em, lge, ninf)
    v1 = jnp.max(el, axis=0, keepdims=True)
    i1 = jnp.min(jnp.where((el == v1) & em, erow, 1e4), axis=0, keepdims=True)
    em2 = em & (erow != i1)
    el2 = jnp.where(em2, lge, ninf)
    v2 = jnp.max(el2, axis=0, keepdims=True)
    i2 = jnp.min(jnp.where((el2 == v2) & em2, erow, 1e4), axis=0, keepdims=True)
    t = jnp.exp(v2 - v1)
    w1 = 1.0 / (1.0 + t)
    w2 = t / (1.0 + t)
    gates = jnp.where(erow == i1, w1 * pmax, 0.0) + jnp.where(erow == i2, w2 * pmax, 0.0)
    onehot = jnp.where(grow == gi, 1.0, 0.0)
    rows = jnp.concatenate([gates, onehot, jnp.zeros((128 - MOE_EXPERTS - 8, tm), F32)], axis=0)
    return jnp.transpose(rows)


MOE_CAP = 160
MOE_OV = 128
TAIL_PIPELINE_MIN_TILES = 8


def _tail_kernel(*refs, glu, pipelined):
    (mix_ref, att_ref, xin_ref, wm_ref, wa_ref, g1_ref, b1_ref, wr_ref, br_ref, ltri_ref,
     wg_ref, wu_ref, wd_ref, g_ref, b_ref) = refs[:15]
    if glu:
        wglu_ref, bglu_ref = refs[15:17]
    nscr = 11 if pipelined else 6
    o_ref, acc_ref, xb_nx, gs_nx, x_nx, rkc_nx, rkr_nx = refs[-nscr - 1:][:7]
    if pipelined:
        xb_ref, gs_ref, x_ref, rkc_ref, rkr_ref = refs[-5:]
    else:
        xb_ref, gs_ref, x_ref, rkc_ref, rkr_ref = xb_nx, gs_nx, x_nx, rkc_nx, rkr_nx
    tm = x_ref.shape[0]
    lane = lax.broadcasted_iota(jnp.int32, (tm, 128), 1)

    if pipelined:
        @pl.when(pl.program_id(0) == 0)
        def _():
            xb_ref[...] = jnp.zeros_like(xb_ref)
            gs_ref[...] = jnp.zeros_like(gs_ref)
            x_ref[...] = jnp.zeros_like(x_ref)
            rkc_ref[...] = jnp.full(rkc_ref.shape, -1.0, F32)
            rkr_ref[...] = jnp.full(rkr_ref.shape, -1.0, F32)

    front = {}

    def front_project():
        if glu:
            y = _gelu_tanh(mix_ref[...])
            mix = (y * _sigmoid(_dot(y.astype(BF16), wglu_ref[...]) + bglu_ref[...])).astype(BF16)
        else:
            mix = mix_ref[...]
        front["h"] = _dot(mix, wm_ref[...]) + _dot(att_ref[...], wa_ref[...])

    def front_norm():
        xn = _layer_norm(DN_ALPHA * xin_ref[...] + front["h"], g1_ref[...], b1_ref[...])
        x_nx[...] = xn
        xh = xn.astype(BF16)
        xb_nx[...] = xh
        front["xh"], front["xl"] = xh, (xn - xh.astype(F32)).astype(BF16)

    def front_route():
        lg = _dot(front["xh"], wr_ref[...])
        lg = lg[:, 0:128] + lg[:, 128:256] + _dot(front["xl"], wr_ref[:, 0:128]) + br_ref[...]
        front["routed"] = _router_gates(lg)

    def front_rank():
        routed = front["routed"]
        gates = jnp.where(lane < MOE_EXPERTS, routed, 0.0)
        in_grp = (lane >= MOE_EXPERTS) & (lane < MOE_EXPERTS + MOE_GROUPS) & (routed > 0.0)
        g_hi = gates.astype(BF16).astype(F32)
        g_mid = (gates - g_hi).astype(BF16).astype(F32)
        g_lo = gates - g_hi - g_mid
        gs_nx[...] = (g_hi + pltpu.roll(g_mid, MOE_EXPERTS, axis=1)
                      + pltpu.roll(g_lo, 2 * MOE_EXPERTS, axis=1)).astype(BF16)
        rkn = _dot(ltri_ref[...], routed.astype(BF16))
        rkn = jnp.where(in_grp, rkn, -1.0)
        rkc_nx[...] = rkn
        rkr_nx[...] = jnp.transpose(rkn)

    front_stages = [front_project, front_norm, front_route, front_rank]
    if not pipelined:
        for stage in front_stages:
            stage()

    rkc = rkc_ref[...]
    rkr = rkr_ref[...]
    cnt = jnp.sum(jnp.where(rkc >= 0.0, 1.0, 0.0), axis=0, keepdims=True)
    rk_cols = [jnp.sum(jnp.where(lane == MOE_EXPERTS + g, rkc, 0.0), axis=-1, keepdims=True)
               for g in range(MOE_GROUPS)]
    rk_rows = [rkr[MOE_EXPERTS + g:MOE_EXPERTS + g + 1, :] for g in range(MOE_GROUPS)]

    def gather_mat(g, base, rows):
        r_io = lax.broadcasted_iota(jnp.int32, (rows, tm), 0).astype(F32) + base
        return jnp.where(rk_rows[g] == r_io, 1.0, 0.0).astype(BF16)

    def scatter_mat(g, base, kpad):
        c_io = lax.broadcasted_iota(jnp.int32, (tm, kpad), 1).astype(F32) + base
        return jnp.where(rk_cols[g] == c_io, 1.0, 0.0).astype(BF16)

    def experts(g, xg, gg, kpad):
        rows = xg.shape[0]
        glane = lax.broadcasted_iota(jnp.int32, (rows, 128), 1)
        hs = []
        for j in range(MOE_PER_GROUP):
            e = g * MOE_PER_GROUP + j
            ge = jnp.sum(jnp.where((glane % MOE_EXPERTS == e) & (glane < 3 * MOE_EXPERTS), gg, 0.0),
                         axis=-1, keepdims=True)
            hs.append((_silu(_dot(xg, wg_ref[0, e])) * _dot(xg, wu_ref[0, e]) * ge).astype(BF16))
        wd = wd_ref[0, g * MOE_PER_GROUP:(g + 1) * MOE_PER_GROUP].reshape(MOE_PER_GROUP * MOE_HIDDEN, D_MODEL)
        out = _dot(jnp.concatenate(hs, axis=-1), wd).astype(BF16)
        if kpad > rows:
            out = jnp.concatenate([out, jnp.zeros((kpad - rows, D_MODEL), BF16)], axis=0)
        return out

    in_cap = lambda rk: (rk >= 0.0) & (rk < MOE_CAP)
    slot_col = sum(jnp.where(in_cap(rk_cols[g]), rk_cols[g] + (g * MOE_CAP + 1.0), 0.0) for g in range(MOE_GROUPS)) - 1.0
    slot_row = sum(jnp.where(in_cap(rk_rows[g]), rk_rows[g] + (g * MOE_CAP + 1.0), 0.0) for g in range(MOE_GROUPS)) - 1.0
    slots = MOE_GROUPS * MOE_CAP
    r_io = lax.broadcasted_iota(jnp.int32, (slots, tm), 0).astype(F32)
    p_all = jnp.where(slot_row == r_io, 1.0, 0.0).astype(BF16)
    xg_all = _dot(p_all, xb_ref[...]).astype(BF16)
    gg_all = _dot(p_all, gs_ref[...])
    outs = []
    for g in range(MOE_GROUPS):
        if pipelined:
            front_stages[g]()
        outs.append(experts(g, xg_all[g * MOE_CAP:(g + 1) * MOE_CAP], gg_all[g * MOE_CAP:(g + 1) * MOE_CAP], MOE_CAP))
    c_io = lax.broadcasted_iota(jnp.int32, (tm, slots), 1).astype(F32)
    pt_all = jnp.where(slot_col == c_io, 1.0, 0.0).astype(BF16)
    acc_ref[...] = _dot(pt_all, jnp.concatenate(outs, axis=0))

    for g in range(MOE_GROUPS):
        n_ov = jnp.maximum(cnt[0, MOE_EXPERTS + g].astype(jnp.int32) - MOE_CAP + MOE_OV - 1, 0) // MOE_OV

        def overflow(i, _, g=g):
            base = (MOE_CAP + i * MOE_OV).astype(F32)
            p = gather_mat(g, base, MOE_OV)
            out = experts(g, _dot(p, xb_ref[...]).astype(BF16), _dot(p, gs_ref[...]), MOE_OV)
            acc_ref[...] += _dot(scatter_mat(g, base, MOE_OV), out)
            return 0

        lax.fori_loop(0, n_ov, overflow, 0)

    if pipelined:
        xb_ref[...] = xb_nx[...]
        gs_ref[...] = gs_nx[...]
        rkc_ref[...] = rkc_nx[...]
        rkr_ref[...] = rkr_nx[...]
    o_ref[...] = _layer_norm(DN_ALPHA * x_ref[...] + acc_ref[...], g_ref[...], b_ref[...])
    if pipelined:
        x_ref[...] = x_nx[...]


def _layer_tail(mix, att, x, wm, wa, ln1, wr, br, wg, wu, wd, layer, ln2, glu_w=None, glu_b=None, tm=512):
    t = x.shape[0]
    tm = min(tm, t)
    nt = t // tm
    idx = jnp.arange(tm)
    ltri = (idx[:, None] > idx[None, :]).astype(BF16)
    pipelined = nt >= TAIL_PIPELINE_MIN_TILES
    if pipelined:
        row_in = lambda w: pl.BlockSpec((tm, w), lambda s: (jnp.minimum(s, nt - 1), 0))
        row_out = pl.BlockSpec((tm, D_MODEL), lambda s: (jnp.maximum(s - 1, 0), 0))
    else:
        row_in = lambda w: pl.BlockSpec((tm, w), lambda s: (s, 0))
        row_out = pl.BlockSpec((tm, D_MODEL), lambda s: (s, 0))
    full = lambda a: pl.BlockSpec(a.shape, lambda s: (0,) * a.ndim, pipeline_mode=pl.Buffered(1))
    layer_w = lambda a: pl.BlockSpec((1,) + a.shape[1:], lambda s: (layer, 0, 0, 0),
                                     pipeline_mode=pl.Buffered(1))
    args = [mix, att, x, wm, wa, ln1[0], ln1[1], wr, br, ltri, wg, wu, wd, ln2[0], ln2[1]]
    specs = [row_in(mix.shape[1]), row_in(XA_WIDTH), row_in(D_MODEL), full(wm), full(wa), full(ln1[0]),
             full(ln1[1]), full(wr), full(br), full(ltri), layer_w(wg), layer_w(wu), layer_w(wd),
             full(ln2[0]), full(ln2[1])]
    if glu_w is not None:
        args += [glu_w, glu_b]
        specs += [full(glu_w), full(glu_b)]
    parked = [pltpu.VMEM((tm, D_MODEL), BF16), pltpu.VMEM((tm, 128), BF16), pltpu.VMEM((tm, D_MODEL), F32),
              pltpu.VMEM((tm, 128), F32), pltpu.VMEM((128, tm), F32)]
    return pl.pallas_call(
        functools.partial(_tail_kernel, glu=glu_w is not None, pipelined=pipelined),
        grid=(nt + 1 if pipelined else nt,),
        in_specs=specs,
        out_specs=row_out,
        out_shape=jax.ShapeDtypeStruct((t, D_MODEL), F32),
        scratch_shapes=[pltpu.VMEM((tm, D_MODEL), F32)] + parked + (parked if pipelined else []),
        compiler_params=pltpu.CompilerParams(dimension_semantics=("arbitrary",),
                                             vmem_limit_bytes=TAIL_VMEM_LIMIT),
        name="layer_tail",
    )(*args)


def _s5_kernel(*refs, tlen, ngrp, chained, precise):
    if chained:
        u_ref, wb_ref, wc_ref, lam_ref, d_ref, y_ref, ore_ref, oim_ref, bu_ref, h_ref, cr_ref, ci_ref = refs
        u = jnp.swapaxes(u_ref[...], 0, 1).reshape(8 * tlen, 128)
    else:
        (u_ref, h0r_ref, h0i_ref, wb_ref, wc_ref, lam_ref, d_ref,
         y_ref, ore_ref, oim_ref, bu_ref, h_ref, cr_ref, ci_ref) = refs
        gr = 8 * tlen
        u = jnp.concatenate(
            [jnp.swapaxes(u_ref[g * gr:(g + 1) * gr, :].reshape(8, tlen, 128), 0, 1).reshape(gr, 128)
             for g in range(ngrp)], axis=0)
    if precise:
        bu_ref[...] = _dot_hi(u, wb_ref[0])
    else:
        bu_ref[...] = _dot(u.astype(BF16), wb_ref[0].astype(BF16))
    ns = S5_SLAB_STATE
    lr = jnp.broadcast_to(lam_ref[0, 0:1, :], (8, ns))
    li = jnp.broadcast_to(lam_ref[0, 1:2, :], (8, ns))

    def step(tile, hr, hi):
        rows = pl.ds(pl.multiple_of(tile * 8, 8), 8)
        nr = lr * hr - li * hi + bu_ref[rows, 0:ns]
        ni = lr * hi + li * hr + bu_ref[rows, ns:2 * ns]
        h_ref[rows, 0:ns] = nr
        h_ref[rows, ns:2 * ns] = ni
        return nr, ni

    if chained:
        @pl.when(pl.program_id(2) == 0)
        def _():
            cr_ref[...] = jnp.zeros_like(cr_ref)
            ci_ref[...] = jnp.zeros_like(ci_ref)

        hr, hi = lax.fori_loop(0, tlen, lambda t, c: step(t, *c), (cr_ref[...], ci_ref[...]), unroll=8)
        cr_ref[...] = hr
        ci_ref[...] = hi
        ore_ref[...] = hr
        oim_ref[...] = hi
    else:
        cr_ref[...] = jnp.transpose(h0r_ref[0])
        ci_ref[...] = jnp.transpose(h0i_ref[0])

        def group(g, _):
            s0 = pl.multiple_of(g * 8, 8)
            hr, hi = cr_ref[pl.ds(s0, 8), :], ci_ref[pl.ds(s0, 8), :]
            for t in range(tlen):
                hr, hi = step(g * tlen + t, hr, hi)
            cr_ref[pl.ds(s0, 8), :] = hr
            ci_ref[pl.ds(s0, 8), :] = hi
            return 0

        lax.fori_loop(0, ngrp, group, 0)
        ore_ref[...] = jnp.transpose(cr_ref[...])
        oim_ref[...] = jnp.transpose(ci_ref[...])
    y = _dot(h_ref[...].astype(BF16), wc_ref[0].astype(BF16)) + d_ref[0] * u
    if chained:
        y_ref[...] = jnp.swapaxes(y.reshape(tlen, 8, 128), 0, 1)
    else:
        gr = 8 * tlen
        for g in range(ngrp):
            y_ref[g * gr:(g + 1) * gr, :] = jnp.swapaxes(
                y[g * gr:(g + 1) * gr, :].reshape(tlen, 8, 128), 0, 1).reshape(gr, 128)


def _s5_mixer_prompt(z, params, seq_len, tlen=256):
    t = z.shape[0]
    nseq = t // seq_len
    ns = S5_SLAB_STATE
    z3 = z.reshape(nseq, seq_len, z.shape[1])
    slab = lambda a: pl.BlockSpec((1,) + a.shape[1:], lambda j, b, i: (j,) + (0,) * (a.ndim - 1))
    u_spec = pl.BlockSpec((8, tlen, 128), lambda j, b, i: (b, i, j))
    st_spec = pl.BlockSpec((8, ns), lambda j, b, i: (b, j))
    st_shape = jax.ShapeDtypeStruct((nseq, S5_SLABS * ns), F32)
    rows = 8 * tlen
    y, hre, him = pl.pallas_call(
        functools.partial(_s5_kernel, tlen=tlen, ngrp=1, chained=True, precise=False),
        grid=(S5_SLABS, nseq // 8, seq_len // tlen),
        in_specs=[u_spec] + [slab(a) for a in params],
        out_specs=[u_spec, st_spec, st_spec],
        out_shape=[jax.ShapeDtypeStruct((nseq, seq_len, MIX_WIDTH), F32), st_shape, st_shape],
        scratch_shapes=[pltpu.VMEM((rows, 2 * ns), F32), pltpu.VMEM((rows, 2 * ns), F32),
                        pltpu.VMEM((8, ns), F32), pltpu.VMEM((8, ns), F32)],
        compiler_params=_cparams("parallel", "parallel", "arbitrary"),
        name="s5_mixer",
    )(z3, *params)
    return y.reshape(t, MIX_WIDTH), hre, him


def _s5_mixer_sample(z, params, seq_len, h0r, h0i, layer):
    t = z.shape[0]
    nseq = t // seq_len
    ns = S5_SLAB_STATE
    seqs = 128
    rows = seqs * seq_len
    slab = lambda a: pl.BlockSpec((1,) + a.shape[1:], lambda j, i: (j,) + (0,) * (a.ndim - 1))
    u_spec = pl.BlockSpec((rows, 128), lambda j, i: (i, j))
    h0_spec = pl.BlockSpec((1, ns, seqs), lambda j, i: (layer, j, i))
    st_spec = pl.BlockSpec((ns, seqs), lambda j, i: (j, i))
    st_shape = jax.ShapeDtypeStruct((S5_SLABS * ns, nseq), F32)
    return pl.pallas_call(
        functools.partial(_s5_kernel, tlen=seq_len, ngrp=seqs // 8, chained=False, precise=True),
        grid=(S5_SLABS, nseq // seqs),
        in_specs=[u_spec, h0_spec, h0_spec] + [slab(a) for a in params],
        out_specs=[u_spec, st_spec, st_spec],
        out_shape=[jax.ShapeDtypeStruct((t, MIX_WIDTH), F32), st_shape, st_shape],
        scratch_shapes=[pltpu.VMEM((rows, 2 * ns), F32), pltpu.VMEM((rows, 2 * ns), F32),
                        pltpu.VMEM((seqs, ns), F32), pltpu.VMEM((seqs, ns), F32)],
        compiler_params=_cparams("parallel", "parallel"),
        name="s5_mixer",
    )(z, h0r, h0i, *params)


def _s5_params(lam_re, lam_im, log_dt, b_re, b_im, c_re, c_im, d_skip):
    lre, lim = lam_re.astype(F32), lam_im.astype(F32)
    dt = jnp.exp(log_dt.astype(F32))[:, None]
    mag = jnp.exp(lre * dt)
    bar_re, bar_im = mag * jnp.cos(lim * dt), mag * jnp.sin(lim * dt)
    nre, nim = bar_re - 1.0, bar_im
    den = lre * lre + lim * lim
    fre, fim = (nre * lre + nim * lim) / den, (nim * lre - nre * lim) / den
    bre, bim = b_re.astype(F32), b_im.astype(F32)
    bb_re = fre[..., None] * bre - fim[..., None] * bim
    bb_im = fre[..., None] * bim + fim[..., None] * bre
    eye = jnp.eye(8, dtype=F32)
    gps = 8

    def blockdiag_in(m):
        m = m.reshape(S5_SLABS, gps, S5_STATE, S5_GROUP).transpose(0, 1, 3, 2)
        return jnp.einsum('jgcp,gh->jgchp', m, eye).reshape(S5_SLABS, 128, S5_SLAB_STATE)

    def blockdiag_out(m):
        m = m.reshape(S5_SLABS, gps, S5_GROUP, S5_STATE).transpose(0, 1, 3, 2)
        return jnp.einsum('jgpc,gh->jgphc', m, eye).reshape(S5_SLABS, S5_SLAB_STATE, 128)

    wb = jnp.concatenate([blockdiag_in(bb_re), blockdiag_in(bb_im)], axis=2)
    wc = jnp.concatenate([blockdiag_out(c_re.astype(F32)), blockdiag_out(-c_im.astype(F32))], axis=1)
    lam = jnp.stack([bar_re.reshape(S5_SLABS, S5_SLAB_STATE), bar_im.reshape(S5_SLABS, S5_SLAB_STATE)], axis=1)
    dsk = d_skip.astype(F32).reshape(S5_SLABS, 1, 128)
    return wb, wc, lam, dsk


def _head_norm_gate(o, gn, gate):
    col = lax.broadcasted_iota(jnp.int32, o.shape, 1)
    valid = col < RET_DV
    mu = jnp.sum(o, axis=-1, keepdims=True) * (1.0 / RET_DV)
    d = jnp.where(valid, o - mu, 0.0)
    var = jnp.sum(d * d, axis=-1, keepdims=True) * (1.0 / RET_DV)
    return d * lax.rsqrt(var + NORM_EPS) * gn * _silu(gate)


def _load_states(s_in_ref, spad_ref, nseg, dk):
    spad_ref[...] = jnp.zeros_like(spad_ref)
    for s in range(nseg):
        for h in range(HEADS):
            spad_ref[h, s * DK_PAD:s * DK_PAD + dk, 0:RET_DV] = s_in_ref[s, h]


def _ret_kernel(*refs, nseg, gammas):
    if nseg == 1:
        (q_ref, k_ref, v_ref, g_ref, cos_ref, sin_ref, dmat_ref, inner_ref, zeta_ref, gn_ref,
         o_ref, so_ref, st_ref) = refs

        @pl.when(pl.program_id(1) == 0)
        def _():
            st_ref[...] = jnp.zeros_like(st_ref)
    else:
        (q_ref, k_ref, v_ref, g_ref, cos_ref, sin_ref, dmat_ref, inner_ref, zeta_ref, gn_ref, s_in_ref,
         o_ref, so_ref, st_ref) = refs
        st_ref[:, RET_DV:DV_PAD, :] = jnp.zeros((HEADS, DV_PAD - RET_DV, nseg * RET_DK), F32)
        for s in range(nseg):
            for h in range(HEADS):
                st_ref[h, 0:RET_DV, s * RET_DK:(s + 1) * RET_DK] = s_in_ref[s, h]
    r = q_ref.shape[0]
    seg_len = r // nseg
    cos = cos_ref[...]
    sin = sin_ref[...]
    rot = lambda x: x * cos + pltpu.roll(x, RET_DK // 2, axis=1) * sin
    rowseg = lax.broadcasted_iota(jnp.int32, (r, DK_PAD), 0) // seg_len
    colseg = lax.broadcasted_iota(jnp.int32, (DV_PAD, r), 1) // seg_len
    for h in range(HEADS):
        q = rot(q_ref[:, h * DK_PAD:(h + 1) * DK_PAD])
        k = rot(k_ref[:, h * DK_PAD:(h + 1) * DK_PAD]) * (RET_DK ** -0.5)
        v = v_ref[:, h * DV_PAD:(h + 1) * DV_PAD]
        vb = v.astype(BF16)
        qb = q.astype(BF16)
        att = _dot_nt(qb, k.astype(BF16)) * dmat_ref[h]
        o = _dot(att.astype(BF16), vb)
        st_all = st_ref[h]
        if nseg == 1:
            qbd = qb
        else:
            qbd = jnp.concatenate([jnp.where(rowseg == s, q, 0.0) for s in range(nseg)], axis=1).astype(BF16)
        o = o + _dot_nt(qbd, st_all.astype(BF16)) * inner_ref[h]
        kzb = (k * zeta_ref[h]).astype(BF16)
        vt = jnp.transpose(v)
        for s in range(nseg):
            vts = vt if nseg == 1 else jnp.where(colseg == s, vt, 0.0)
            st_new = st_all[:, s * RET_DK:(s + 1) * RET_DK] * gammas[h] + _dot(vts.astype(BF16), kzb)
            if nseg == 1:
                st_ref[h] = st_new
                so_ref[0, h] = st_new[0:RET_DV, :]
            else:
                so_ref[s, h] = st_new[0:RET_DV, :]
        o_ref[:, h * DV_PAD:(h + 1) * DV_PAD] = _head_norm_gate(
            o, gn_ref[:, h * DV_PAD:(h + 1) * DV_PAD], g_ref[:, h * DV_PAD:(h + 1) * DV_PAD]).astype(o_ref.dtype)


def _ret_consts(seg_len, nseg, pos):
    hh = jnp.arange(RET_HEADS, dtype=F32)
    log_g = jnp.log1p(-jnp.exp2(-5.0 - hh))
    idx = jnp.arange(seg_len, dtype=F32)
    rel = idx[:, None] - idx[None, :]
    dmat = jnp.where(rel >= 0, jnp.exp(log_g[:, None, None] * jnp.maximum(rel, 0.0)), 0.0)
    inner = jnp.exp(log_g[:, None] * (idx + 1.0))
    zeta = jnp.exp(log_g[:, None] * (seg_len - 1.0 - idx))
    r = seg_len * nseg
    seg = jnp.arange(r) // seg_len
    same = (seg[:, None] == seg[None, :]).astype(F32)
    dmat = jnp.tile(dmat, (1, nseg, nseg)) * same[None]
    inner = jnp.broadcast_to(jnp.tile(inner, (1, nseg))[:, :, None], (RET_HEADS, r, DV_PAD))
    zeta = jnp.broadcast_to(jnp.tile(zeta, (1, nseg))[:, :, None], (RET_HEADS, r, DK_PAD))
    half = RET_DK // 2
    inv_freq = ROPE_BASE ** (-jnp.arange(half, dtype=F32) / half)
    ang = pos.astype(F32)[:, None] * inv_freq
    cos2 = jnp.concatenate([jnp.cos(ang), jnp.cos(ang)], axis=-1)
    sin2 = jnp.concatenate([-jnp.sin(ang), jnp.sin(ang)], axis=-1)
    return dmat, inner, zeta, cos2, sin2


def _ret_gammas(seg_len):
    return tuple(float((1.0 - 2.0 ** (-5.0 - h)) ** seg_len) for h in range(RET_HEADS))


def _ret_mixer(z, gn, seq_len, pos, s0=None):
    t = z.shape[0]
    nseq = t // seq_len
    if s0 is None:
        r, nseg, seg_len = RET_CHUNK, 1, RET_CHUNK
        steps = seq_len // r
        grid = (nseq, steps)
        rowmap = lambda c: (lambda n, i: (n * steps + i, c))
        tabmap = lambda n, i: (i, 0)
        stmap = lambda n, i: (n, 0, 0, 0)
        st_rows = 1
        sem = ("parallel", "arbitrary")
    else:
        seg_len = seq_len
        nseg = 128 // seg_len
        r = 128
        grid = (t // r,)
        rowmap = lambda c: (lambda i: (i, c))
        tabmap = lambda i: (0, 0)
        stmap = lambda i: (i, 0, 0, 0)
        st_rows = nseg
        sem = ("arbitrary",)
    dmat, inner, zeta, cos2, sin2 = _ret_consts(seg_len, nseg, pos)
    if s0 is not None:
        cos2 = jnp.tile(cos2, (nseg, 1))
        sin2 = jnp.tile(sin2, (nseg, 1))
    full = lambda a: pl.BlockSpec(a.shape, lambda *_: (0,) * a.ndim)
    st_spec = pl.BlockSpec((st_rows, RET_HEADS, RET_DV, RET_DK), stmap)
    in_specs = [pl.BlockSpec((r, 512), rowmap(0)), pl.BlockSpec((r, 512), rowmap(1)),
                pl.BlockSpec((r, 1024), rowmap(1)), pl.BlockSpec((r, 1024), rowmap(2)),
                pl.BlockSpec((r, DK_PAD), tabmap), pl.BlockSpec((r, DK_PAD), tabmap),
                full(dmat), full(inner), full(zeta), full(gn)]
    args = [z, z, z, z, cos2, sin2, dmat, inner, zeta, gn]
    if s0 is not None:
        in_specs.append(st_spec)
        args.append(s0)
    mix, s_new = pl.pallas_call(
        functools.partial(_ret_kernel, nseg=nseg, gammas=_ret_gammas(seg_len)),
        grid=grid,
        in_specs=in_specs,
        out_specs=[pl.BlockSpec((r, HEADS * DV_PAD), rowmap(0)), st_spec],
        out_shape=[jax.ShapeDtypeStruct((t, HEADS * DV_PAD), BF16),
                   jax.ShapeDtypeStruct((nseq, RET_HEADS, RET_DV, RET_DK), F32)],
        scratch_shapes=[pltpu.VMEM((HEADS, DV_PAD, nseg * RET_DK), F32)],
        compiler_params=_cparams(*sem),
        name="retention",
    )(*args)
    return mix, s_new


def _log_sigmoid(x):
    return jnp.minimum(x, 0.0) - jnp.log1p(jnp.exp(-jnp.abs(x)))


def _gla_kernel(*refs, nseg, chained):
    if chained:
        (q_ref, k_ref, v_ref, r_ref, low_ref, tril_ref, same_ref, wg_ref, bg_ref, gn_ref,
         o_ref, so_ref, spad_ref) = refs

        @pl.when(pl.program_id(1) == 0)
        def _():
            spad_ref[...] = jnp.zeros_like(spad_ref)
    else:
        (q_ref, k_ref, v_ref, r_ref, low_ref, tril_ref, same_ref, wg_ref, bg_ref, gn_ref, s_in_ref,
         o_ref, so_ref, spad_ref) = refs
        _load_states(s_in_ref, spad_ref, nseg, GLA_DK)
    r = q_ref.shape[0]
    seg_len = r // nseg
    tril = tril_ref[...]
    causal = tril > 0.0
    la = _log_sigmoid(_dot_hi(low_ref[...], wg_ref[...]) + bg_ref[...]) * (1.0 / GLA_TAU)
    b = _dot_hi(tril, la)
    b_tot = _dot_hi(same_ref[...], la)
    rowseg = lax.broadcasted_iota(jnp.int32, (r, DK_PAD), 0) // seg_len
    colseg = lax.broadcasted_iota(jnp.int32, (DK_PAD, r), 1) // seg_len

    for h in range(HEADS):
        sl = slice(h * DK_PAD, (h + 1) * DK_PAD)
        bh = b[:, sl]
        q = q_ref[:, sl] * (GLA_DK ** -0.5)
        k = k_ref[:, sl]
        vb = v_ref[:, h * DV_PAD:(h + 1) * DV_PAD].astype(BF16)
        q_t = q * jnp.exp(bh)
        k_t = k * jnp.exp(-bh)
        att = jnp.where(causal, _dot_nt(q_t.astype(BF16), k_t.astype(BF16)), 0.0)
        o = _dot(att.astype(BF16), vb)
        kdt = jnp.transpose(k * jnp.exp(b_tot[:, sl] - bh))
        lat = jnp.transpose(la[:, sl])

        def seg_update(s, s_old):
            kds = jnp.where(colseg == s, kdt, 0.0) if nseg > 1 else kdt
            las = jnp.where(colseg == s, lat, 0.0) if nseg > 1 else lat
            decay = jnp.exp(jnp.sum(las, axis=-1, keepdims=True))
            return s_old * decay + _dot(kds.astype(BF16), vb)

        if chained:
            qb = q_t.astype(BF16)
            s_cur = spad_ref[h]
            inter = []
            for s in range(nseg):
                inter.append(_dot(qb[s * seg_len:(s + 1) * seg_len], s_cur.astype(BF16)))
                s_cur = seg_update(s, s_cur)
            o = o + jnp.concatenate(inter, axis=0)
            spad_ref[h] = s_cur
            so_ref[0, h] = s_cur[0:GLA_DK, 0:GLA_DV]
        else:
            s_all = spad_ref[h]
            qbd = jnp.concatenate([jnp.where(rowseg == s, q_t, 0.0) for s in range(nseg)], axis=1).astype(BF16)
            o = o + _dot(qbd, s_all.astype(BF16))
            for s in range(nseg):
                so_ref[s, h] = seg_update(s, s_all[s * DK_PAD:(s + 1) * DK_PAD])[0:GLA_DK, 0:GLA_DV]
        o_ref[:, h * DV_PAD:(h + 1) * DV_PAD] = _head_norm_gate(
            o, gn_ref[:, h * DV_PAD:(h + 1) * DV_PAD], r_ref[:, h * DV_PAD:(h + 1) * DV_PAD]).astype(o_ref.dtype)


def _gla_mixer(z, wg, bg, gn, seq_len, s0=None):
    t = z.shape[0]
    nseq = t // seq_len
    r = 128
    if s0 is None:
        seg_len = GLA_CHUNK
        nseg = r // seg_len
        steps = seq_len // r
        grid = (nseq, steps)
        rowmap = lambda c: (lambda n, i: (n * steps + i, c))
        stmap = lambda n, i: (n, 0, 0, 0)
        st_rows = 1
        scratch_rows = DK_PAD
        sem = ("parallel", "arbitrary")
    else:
        seg_len = seq_len
        nseg = r // seg_len
        grid = (t // r,)
        rowmap = lambda c: (lambda i: (i, c))
        stmap = lambda i: (i, 0, 0, 0)
        st_rows = nseg
        scratch_rows = nseg * DK_PAD
        sem = ("arbitrary",)
    seg = jnp.arange(r) // seg_len
    idx = jnp.arange(r)
    same = (seg[:, None] == seg[None, :]).astype(F32)
    tril = same * (idx[:, None] >= idx[None, :]).astype(F32)
    full = lambda a: pl.BlockSpec(a.shape, lambda *_: (0,) * a.ndim)
    st_spec = pl.BlockSpec((st_rows, GLA_HEADS, GLA_DK, GLA_DV), stmap)
    in_specs = [pl.BlockSpec((r, 512), rowmap(0)), pl.BlockSpec((r, 512), rowmap(1)),
                pl.BlockSpec((r, 1024), rowmap(1)), pl.BlockSpec((r, 1024), rowmap(2)),
                pl.BlockSpec((r, 128), rowmap(COL_LOW // 128)),
                full(tril), full(same), full(wg), full(bg), full(gn)]
    args = [z, z, z, z, z, tril, same, wg, bg, gn]
    if s0 is not None:
        in_specs.append(st_spec)
        args.append(s0)
    mix, s_new = pl.pallas_call(
        functools.partial(_gla_kernel, nseg=nseg, chained=s0 is None),
        grid=grid,
        in_specs=in_specs,
        out_specs=[pl.BlockSpec((r, HEADS * DV_PAD), rowmap(0)), st_spec],
        out_shape=[jax.ShapeDtypeStruct((t, HEADS * DV_PAD), BF16),
                   jax.ShapeDtypeStruct((nseq, GLA_HEADS, GLA_DK, GLA_DV), F32)],
        scratch_shapes=[pltpu.VMEM((HEADS, scratch_rows, DV_PAD), F32)],
        compiler_params=_cparams(*sem),
        name="gla",
    )(*args)
    return mix, s_new


def _pad_heads(w, width, pad):
    lead = w.shape[:-1]
    w = w.reshape(lead + (HEADS, width))
    w = jnp.pad(w, [(0, 0)] * len(lead) + [(0, 0), (0, pad - width)])
    return w.reshape(lead + (HEADS * pad,))


def _pad_head_rows(w, width, pad):
    return jnp.swapaxes(_pad_heads(jnp.swapaxes(w, 0, 1), width, pad), 0, 1)


def _ret_in_weight(w):
    qk = RET_HEADS * RET_DK
    v = _pad_heads(w[:, 2 * qk:2 * qk + MIX_WIDTH], RET_DV, DV_PAD)
    g = _pad_heads(w[:, 2 * qk + MIX_WIDTH:2 * qk + 2 * MIX_WIDTH], RET_DV, DV_PAD)
    return jnp.concatenate([w[:, :2 * qk], v, g, w[:, -XA_WIDTH:]], axis=1).astype(BF16)


def _gla_in_weight(w):
    qk = GLA_HEADS * GLA_DK
    q = _pad_heads(w[:, :qk], GLA_DK, DK_PAD)
    k = _pad_heads(w[:, qk:2 * qk], GLA_DK, DK_PAD)
    v = _pad_heads(w[:, 2 * qk:2 * qk + MIX_WIDTH], GLA_DV, DV_PAD)
    r = _pad_heads(w[:, 2 * qk + MIX_WIDTH:2 * qk + 2 * MIX_WIDTH], GLA_DV, DV_PAD)
    low = jnp.pad(w[:, 2 * qk + 2 * MIX_WIDTH:2 * qk + 2 * MIX_WIDTH + GLA_RANK], [(0, 0), (0, 128 - GLA_RANK)])
    return jnp.concatenate([q, k, v, r, w[:, -XA_WIDTH:], low], axis=1).astype(BF16)


def kernel(x_prompt, x_sample, mem_prompt, state_s5_re, state_s5_im, state_ret, state_gla,
           cache_mem_k, cache_mem_v,
           w_in_s5, s5_lam_re, s5_lam_im, s5_log_dt, s5_b_re, s5_b_im, s5_c_re, s5_c_im,
           s5_d, s5_w_glu, s5_b_glu,
           w_in_ret, ret_gn,
           w_in_gla, gla_w_gate2, gla_b_gate2, gla_gn,
           w_mem_k, w_mem_v, w_out, ln_g, ln_b,
           moe_w_grp, moe_b_grp, moe_w_exp, moe_b_exp, moe_w_gate, moe_w_up, moe_w_down):
    bp, lp, _ = x_prompt.shape
    bs, ls, _ = x_sample.shape
    xp = x_prompt.reshape(bp * lp, D_MODEL)
    xs = x_sample.reshape(bs * ls, D_MODEL)
    pos_p = jnp.arange(lp, dtype=jnp.int32)
    pos_s = PAST_LEN + jnp.arange(ls, dtype=jnp.int32)

    to_t = lambda c: jnp.transpose(c, (0, 1, 3, 4, 2)).reshape(DEPTH, c.shape[1], XA_WIDTH, N_MEM)
    from_t = lambda c: jnp.transpose(c.reshape(DEPTH, c.shape[1], XA_HEADS, XA_HEAD_DIM, N_MEM), (0, 1, 4, 2, 3))
    mkt_all, mvt_all = _mem_kv(mem_prompt, jnp.swapaxes(w_mem_k, 1, 2).astype(BF16),
                               jnp.swapaxes(w_mem_v, 1, 2).astype(BF16))
    cache_kt = to_t(cache_mem_k)
    cache_vt = to_t(cache_mem_v)
    ret_t = jnp.swapaxes(state_ret, 3, 4)
    s5_t = lambda s: jnp.transpose(s, (0, 2, 3, 1)).reshape(s.shape[0], S5_GROUPS * S5_STATE, bs)
    s5_re_t, s5_im_t = s5_t(state_s5_re), s5_t(state_s5_im)
    moe_wg, moe_wu, moe_wd = moe_w_gate.astype(BF16), moe_w_up.astype(BF16), moe_w_down.astype(BF16)

    p_re, p_im, p_ret, p_gla = [], [], [], []
    s_re, s_im, s_ret, s_gla = [], [], [], []
    for i in range(DEPTH):
        kind, j = i % N_MIXERS, i // N_MIXERS
        ln1 = (ln_g[i, 0].reshape(1, D_MODEL), ln_b[i, 0].reshape(1, D_MODEL))
        ln2 = (ln_g[i, 1].reshape(1, D_MODEL), ln_b[i, 1].reshape(1, D_MODEL))
        wo = w_out[i]
        wo_att = wo[MIX_WIDTH:].astype(BF16)
        glu = {}
        if kind == 0:
            params = _s5_params(s5_lam_re[j], s5_lam_im[j], s5_log_dt[j], s5_b_re[j], s5_b_im[j],
                                s5_c_re[j], s5_c_im[j], s5_d[j])
            zp = _matmul(xp, w_in_s5[j].astype(BF16), 512)
            zs = _matmul(xs, w_in_s5[j], 512, precise=True)
            mix_p, a_re, a_im = _s5_mixer_prompt(zp, params, lp)
            mix_s, b_re, b_im = _s5_mixer_sample(zs, params, ls, s5_re_t, s5_im_t, j)
            p_re.append(a_re), p_im.append(a_im), s_re.append(b_re), s_im.append(b_im)
            xa_col = MIX_WIDTH
            wo_mix = wo[:MIX_WIDTH].astype(BF16)
            glu = dict(glu_w=s5_w_glu[j].astype(BF16), glu_b=s5_b_glu[j].reshape(1, MIX_WIDTH))
        elif kind == 1:
            w = _ret_in_weight(w_in_ret[j])
            gn = _pad_heads(ret_gn[j], RET_DV, DV_PAD).reshape(1, HEADS * DV_PAD)
            zp = _matmul(xp, w, 512)
            zs = _matmul(xs, w, 512)
            mix_p, a_s = _ret_mixer(zp, gn, lp, pos_p)
            mix_s, b_s = _ret_mixer(zs, gn, ls, pos_s, s0=ret_t[j])
            p_ret.append(a_s), s_ret.append(b_s)
            xa_col = COL_XA
            wo_mix = _pad_head_rows(wo[:MIX_WIDTH], RET_DV, DV_PAD).astype(BF16)
        else:
            w = _gla_in_weight(w_in_gla[j])
            gn = _pad_heads(gla_gn[j], GLA_DV, DV_PAD).reshape(1, HEADS * DV_PAD)
            wg = jnp.pad(_pad_heads(gla_w_gate2[j], GLA_DK, DK_PAD), [(0, 128 - GLA_RANK), (0, 0)])
            bg = _pad_heads(gla_b_gate2[j], GLA_DK, DK_PAD).reshape(1, HEADS * DK_PAD)
            zp = _matmul(xp, w, 512)
            zs = _matmul(xs, w, 512)
            mix_p, a_s = _gla_mixer(zp, wg, bg, gn, lp)
            mix_s, b_s = _gla_mixer(zs, wg, bg, gn, ls, s0=state_gla[j])
            p_gla.append(a_s), s_gla.append(b_s)
            xa_col = COL_XA
            wo_mix = _pad_head_rows(wo[:MIX_WIDTH], GLA_DV, DV_PAD).astype(BF16)
        att_p = _mem_attention(zp, xa_col, mkt_all, mvt_all, i, lp, rl=min(512, lp), nseq=1)
        att_s = _mem_attention(zs, xa_col, cache_kt, cache_vt, i, ls, rl=ls, nseq=128 // ls)
        wr = jnp.pad(jnp.concatenate([moe_w_exp[i], moe_w_grp[i]], axis=1),
                     [(0, 0), (0, 128 - MOE_EXPERTS - MOE_GROUPS)]).astype(F32)
        wr_hi = wr.astype(BF16)
        wr = jnp.concatenate([wr_hi, (wr - wr_hi.astype(F32)).astype(BF16)], axis=1)
        br = jnp.pad(jnp.concatenate([moe_b_exp[i], moe_b_grp[i]]),
                     [(0, 128 - MOE_EXPERTS - MOE_GROUPS)]).reshape(1, 128).astype(F32)
        xp = _layer_tail(mix_p, att_p, xp, wo_mix, wo_att, ln1, wr, br, moe_wg, moe_wu, moe_wd, i, ln2, **glu)
        xs = _layer_tail(mix_s, att_s, xs, wo_mix, wo_att, ln1, wr, br, moe_wg, moe_wu, moe_wd, i, ln2, **glu)

    st = lambda xs_: jnp.stack(xs_)
    s5_p = lambda xs_: st(xs_).reshape(len(xs_), bp, S5_GROUPS, S5_STATE)
    s5_s = lambda xs_: jnp.transpose(st(xs_).reshape(len(xs_), S5_GROUPS, S5_STATE, bs), (0, 3, 1, 2))
    ret_out = lambda xs_: jnp.swapaxes(st(xs_), 3, 4)
    return (xp.reshape(bp, lp, D_MODEL), xs.reshape(bs, ls, D_MODEL),
            s5_p(p_re), s5_p(p_im), ret_out(p_ret), st(p_gla),
            from_t(mkt_all), from_t(mvt_all),
            s5_s(s_re), s5_s(s_im), ret_out(s_ret), st(s_gla))
```

```python
import functools
import math

import jax
import jax.numpy as jnp
from jax import lax
from jax.experimental import pallas as pl
from jax.experimental.pallas import tpu as pltpu

F32 = jnp.float32
BF16 = jnp.bfloat16
HI = lax.Precision.HIGHEST

D_MODEL = 1024
DEPTH = 4
PAST_LEN = 16384
N_MIXERS = 3
MIX_WIDTH = 768
XA_HEADS = 4
XA_HEAD_DIM = 64
XA_WIDTH = 256
N_MEM = 256
S5_GROUP = 16
S5_GROUPS = 48
S5_STATE = 64
S5_SLABS = 6
S5_SLAB_STATE = 512
RET_HEADS = 4
RET_DK = 128
RET_DV = 192
RET_CHUNK = 128
ROPE_BASE = 10000.0
GLA_HEADS = 4
GLA_DK = 96
GLA_DV = 192
GLA_RANK = 16
GLA_TAU = 16.0
GLA_CHUNK = 64
MOE_GROUPS = 4
MOE_PER_GROUP = 4
MOE_EXPERTS = 16
MOE_HIDDEN = 256
DN_ALPHA = (2 * DEPTH) ** 0.25
NORM_EPS = 1e-5

DK_PAD = 128
DV_PAD = 256
HEADS = 4
COL_Q, COL_K, COL_V, COL_G, COL_XA, COL_LOW = 0, 512, 1024, 2048, 3072, 3328
RET_IN_PAD = 3328
GLA_IN_PAD = 3456
VMEM_LIMIT = 52 * 1024 * 1024
TAIL_VMEM_LIMIT = 58 * 1024 * 1024


def _cparams(*sem):
    return pltpu.CompilerParams(dimension_semantics=sem, vmem_limit_bytes=VMEM_LIMIT)


def _dot(a, b):
    return jnp.dot(a, b, preferred_element_type=F32)


def _dot_nt(a, b):
    return lax.dot_general(a, b, (((1,), (1,)), ((), ())), preferred_element_type=F32)


def _dot_hi(a, b):
    return jnp.dot(a, b, preferred_element_type=F32, precision=HI)


def _layer_norm(y, g, b):
    mu = jnp.mean(y, axis=-1, keepdims=True)
    d = y - mu
    var = jnp.mean(d * d, axis=-1, keepdims=True)
    return d * lax.rsqrt(var + NORM_EPS) * g + b


def _sigmoid(x):
    return 1.0 / (1.0 + jnp.exp(-x))


def _silu(x):
    return x * _sigmoid(x)


def _gelu_tanh(x):
    c = math.sqrt(2.0 / math.pi)
    return 0.5 * x * (1.0 + jnp.tanh(c * (x + 0.044715 * (x * x * x))))


def _matmul_kernel(x_ref, w_ref, o_ref, *, precise):
    if precise:
        o_ref[...] = _dot_hi(x_ref[...], w_ref[...])
    else:
        o_ref[...] = _dot(x_ref[...].astype(BF16), w_ref[...])


def _matmul(x, w, tm, precise=False):
    t, k = x.shape
    tm = min(tm, t)
    n = w.shape[1]
    return pl.pallas_call(
        functools.partial(_matmul_kernel, precise=precise),
        grid=(t // tm,),
        in_specs=[pl.BlockSpec((tm, k), lambda i: (i, 0)),
                  pl.BlockSpec((k, n), lambda i: (0, 0))],
        out_specs=pl.BlockSpec((tm, n), lambda i: (i, 0)),
        out_shape=jax.ShapeDtypeStruct((t, n), F32),
        compiler_params=_cparams("parallel"),
        name="in_proj",
    )(x, w)


def _memkv_kernel(x_ref, w_ref, ok_ref, ov_ref):
    kv = _dot_nt(w_ref[...], x_ref[0].astype(BF16))
    for l in range(DEPTH):
        ok_ref[l, 0] = kv[l * XA_WIDTH:(l + 1) * XA_WIDTH]
        ov_ref[l, 0] = kv[(DEPTH + l) * XA_WIDTH:(DEPTH + l + 1) * XA_WIDTH]


def _mem_kv(mem, wkt, wvt):
    n = mem.shape[0]
    w = jnp.concatenate([wkt, wvt], axis=0).reshape(2 * DEPTH * XA_WIDTH, D_MODEL)
    spec_o = pl.BlockSpec((DEPTH, 1, XA_WIDTH, N_MEM), lambda b: (0, b, 0, 0))
    return pl.pallas_call(
        _memkv_kernel,
        grid=(n,),
        in_specs=[pl.BlockSpec((1, N_MEM, D_MODEL), lambda b: (b, 0, 0)),
                  pl.BlockSpec(w.shape, lambda b: (0, 0))],
        out_specs=[spec_o, spec_o],
        out_shape=[jax.ShapeDtypeStruct((DEPTH, n, XA_WIDTH, N_MEM), F32)] * 2,
        compiler_params=_cparams("parallel"),
        name="mem_kv",
    )(mem, w)


def _attn_kernel(q_ref, kt_ref, vt_ref, o_ref, *, nseq, rl):
    head = lax.broadcasted_iota(jnp.int32, (rl, XA_WIDTH), 1) // XA_HEAD_DIM
    outs = []
    for s in range(nseq):
        q = q_ref[s * rl:(s + 1) * rl, :] * (XA_HEAD_DIM ** -0.5)
        qs = jnp.concatenate([jnp.where(head == h, q, 0.0) for h in range(XA_HEADS)], axis=0)
        sc = _dot(qs.astype(BF16), kt_ref[0, s].astype(BF16))
        sc = sc - jnp.max(sc, axis=-1, keepdims=True)
        p = jnp.exp(sc)
        p = p / jnp.sum(p, axis=-1, keepdims=True)
        pv = _dot_nt(p.astype(BF16), vt_ref[0, s].astype(BF16))
        o = jnp.where(head == 0, pv[0:rl], 0.0)
        for h in range(1, XA_HEADS):
            o = o + jnp.where(head == h, pv[h * rl:(h + 1) * rl], 0.0)
        outs.append(o)
    o_ref[...] = jnp.concatenate(outs, axis=0).astype(o_ref.dtype)


def _mem_attention(z, xa_col, mkt, mvt, layer, seq_len, rl, nseq):
    t = z.shape[0]
    rows = rl * nseq
    cb = xa_col // XA_WIDTH
    if nseq == 1:
        per_seq = seq_len // rl
        kv_map = lambda i: (layer, i // per_seq, 0, 0)
    else:
        kv_map = lambda i: (layer, i, 0, 0)
    return pl.pallas_call(
        functools.partial(_attn_kernel, nseq=nseq, rl=rl),
        grid=(t // rows,),
        in_specs=[pl.BlockSpec((rows, XA_WIDTH), lambda i: (i, cb)),
                  pl.BlockSpec((1, nseq, XA_WIDTH, N_MEM), kv_map),
                  pl.BlockSpec((1, nseq, XA_WIDTH, N_MEM), kv_map)],
        out_specs=pl.BlockSpec((rows, XA_WIDTH), lambda i: (i, 0)),
        out_shape=jax.ShapeDtypeStruct((t, XA_WIDTH), BF16),
        compiler_params=_cparams("parallel"),
        name="mem_attention",
    )(z, mkt, mvt)


def _router_gates(lg):
    tm = lg.shape[0]
    lgt = jnp.transpose(lg)
    ninf = jnp.float32(-jnp.inf)
    grow = lax.broadcasted_iota(jnp.int32, (8, tm), 0).astype(F32)
    gm = grow < MOE_GROUPS
    gl = jnp.where(gm, lgt[MOE_EXPERTS:MOE_EXPERTS + 8], ninf)
    ge = jnp.exp(gl - jnp.max(gl, axis=0, keepdims=True))
    gp = ge / jnp.sum(ge, axis=0, keepdims=True)
    pmax = jnp.max(gp, axis=0, keepdims=True)
    gi = jnp.min(jnp.where((gp == pmax) & gm, grow, 1e4), axis=0, keepdims=True)
    erow = lax.broadcasted_iota(jnp.int32, (MOE_EXPERTS, tm), 0).astype(F32)
    lge = lgt[0:MOE_EXPERTS]
    lo = gi * MOE_PER_GROUP
    em = (erow >= lo) & (erow < lo + MOE_PER_GROUP)
    el = jnp.where(em, lge, ninf)
    v1 = jnp.max(el, axis=0, keepdims=True)
    i1 = jnp.min(jnp.where((el == v1) & em, erow, 1e4), axis=0, keepdims=True)
    em2 = em & (erow != i1)
    el2 = jnp.where(em2, lge, ninf)
    v2 = jnp.max(el2, axis=0, keepdims=True)
    i2 = jnp.min(jnp.where((el2 == v2) & em2, erow, 1e4), axis=0, keepdims=True)
    t = jnp.exp(v2 - v1)
    w1 = 1.0 / (1.0 + t)
    w2 = t / (1.0 + t)
    gates = jnp.where(erow == i1, w1 * pmax, 0.0) + jnp.where(erow == i2, w2 * pmax, 0.0)
    onehot = jnp.where(grow == gi, 1.0, 0.0)
    rows = jnp.concatenate([gates, onehot, jnp.zeros((128 - MOE_EXPERTS - 8, tm), F32)], axis=0)
    return jnp.transpose(rows)


MOE_CAP = 160
MOE_OV = 128
TAIL_PIPELINE_MIN_TILES = 8


def _tail_kernel(*refs, glu, pipelined):
    (mix_ref, att_ref, xin_ref, wm_ref, wa_ref, g1_ref, b1_ref, wr_ref, br_ref, ltri_ref,
     wg_ref, wu_ref, wd_ref, g_ref, b_ref) = refs[:15]
    if glu:
        wglu_ref, bglu_ref = refs[15:17]
    nscr = 11 if pipelined else 6
    o_ref, acc_ref, xb_nx, gs_nx, x_nx, rkc_nx, rkr_nx = refs[-nscr - 1:][:7]
    if pipelined:
        xb_ref, gs_ref, x_ref, rkc_ref, rkr_ref = refs[-5:]
    else:
        xb_ref, gs_ref, x_ref, rkc_ref, rkr_ref = xb_nx, gs_nx, x_nx, rkc_nx, rkr_nx
    tm = x_ref.shape[0]
    lane = lax.broadcasted_iota(jnp.int32, (tm, 128), 1)

    if pipelined:
        @pl.when(pl.program_id(0) == 0)
        def _():
            xb_ref[...] = jnp.zeros_like(xb_ref)
            gs_ref[...] = jnp.zeros_like(gs_ref)
            x_ref[...] = jnp.zeros_like(x_ref)
            rkc_ref[...] = jnp.full(rkc_ref.shape, -1.0, F32)
            rkr_ref[...] = jnp.full(rkr_ref.shape, -1.0, F32)

    front = {}

    def front_project():
        if glu:
            y = _gelu_tanh(mix_ref[...])
            mix = (y * _sigmoid(_dot(y.astype(BF16), wglu_ref[...]) + bglu_ref[...])).astype(BF16)
        else:
            mix = mix_ref[...]
        front["h"] = _dot(mix, wm_ref[...]) + _dot(att_ref[...], wa_ref[...])

    def front_norm():
        xn = _layer_norm(DN_ALPHA * xin_ref[...] + front["h"], g1_ref[...], b1_ref[...])
        x_nx[...] = xn
        xh = xn.astype(BF16)
        xb_nx[...] = xh
        front["xh"], front["xl"] = xh, (xn - xh.astype(F32)).astype(BF16)

    def front_route():
        lg = _dot(front["xh"], wr_ref[...])
        lg = lg[:, 0:128] + lg[:, 128:256] + _dot(front["xl"], wr_ref[:, 0:128]) + br_ref[...]
        front["routed"] = _router_gates(lg)

    def front_rank():
        routed = front["routed"]
        gates = jnp.where(lane < MOE_EXPERTS, routed, 0.0)
        in_grp = (lane >= MOE_EXPERTS) & (lane < MOE_EXPERTS + MOE_GROUPS) & (routed > 0.0)
        g_hi = gates.astype(BF16).astype(F32)
        g_mid = (gates - g_hi).astype(BF16).astype(F32)
        g_lo = gates - g_hi - g_mid
        gs_nx[...] = (g_hi + pltpu.roll(g_mid, MOE_EXPERTS, axis=1)
                      + pltpu.roll(g_lo, 2 * MOE_EXPERTS, axis=1)).astype(BF16)
        rkn = _dot(ltri_ref[...], routed.astype(BF16))
        rkn = jnp.where(in_grp, rkn, -1.0)
        rkc_nx[...] = rkn
        rkr_nx[...] = jnp.transpose(rkn)

    front_stages = [front_project, front_norm, front_route, front_rank]
    if not pipelined:
        for stage in front_stages:
            stage()

    rkc = rkc_ref[...]
    rkr = rkr_ref[...]
    cnt = jnp.sum(jnp.where(rkc >= 0.0, 1.0, 0.0), axis=0, keepdims=True)
    rk_cols = [jnp.sum(jnp.where(lane == MOE_EXPERTS + g, rkc, 0.0), axis=-1, keepdims=True)
               for g in range(MOE_GROUPS)]
    rk_rows = [rkr[MOE_EXPERTS + g:MOE_EXPERTS + g + 1, :] for g in range(MOE_GROUPS)]

    def gather_mat(g, base, rows):
        r_io = lax.broadcasted_iota(jnp.int32, (rows, tm), 0).astype(F32) + base
        return jnp.where(rk_rows[g] == r_io, 1.0, 0.0).astype(BF16)

    def scatter_mat(g, base, kpad):
        c_io = lax.broadcasted_iota(jnp.int32, (tm, kpad), 1).astype(F32) + base
        return jnp.where(rk_cols[g] == c_io, 1.0, 0.0).astype(BF16)

    def experts(g, xg, gg, kpad):
        rows = xg.shape[0]
        glane = lax.broadcasted_iota(jnp.int32, (rows, 128), 1)
        hs = []
        for j in range(MOE_PER_GROUP):
            e = g * MOE_PER_GROUP + j
            ge = jnp.sum(jnp.where((glane % MOE_EXPERTS == e) & (glane < 3 * MOE_EXPERTS), gg, 0.0),
                         axis=-1, keepdims=True)
            hs.append((_silu(_dot(xg, wg_ref[0, e])) * _dot(xg, wu_ref[0, e]) * ge).astype(BF16))
        wd = wd_ref[0, g * MOE_PER_GROUP:(g + 1) * MOE_PER_GROUP].reshape(MOE_PER_GROUP * MOE_HIDDEN, D_MODEL)
        out = _dot(jnp.concatenate(hs, axis=-1), wd).astype(BF16)
        if kpad > rows:
            out = jnp.concatenate([out, jnp.zeros((kpad - rows, D_MODEL), BF16)], axis=0)
        return out

    in_cap = lambda rk: (rk >= 0.0) & (rk < MOE_CAP)
    slot_col = sum(jnp.where(in_cap(rk_cols[g]), rk_cols[g] + (g * MOE_CAP + 1.0), 0.0) for g in range(MOE_GROUPS)) - 1.0
    slot_row = sum(jnp.where(in_cap(rk_rows[g]), rk_rows[g] + (g * MOE_CAP + 1.0), 0.0) for g in range(MOE_GROUPS)) - 1.0
    slots = MOE_GROUPS * MOE_CAP
    r_io = lax.broadcasted_iota(jnp.int32, (slots, tm), 0).astype(F32)
    p_all = jnp.where(slot_row == r_io, 1.0, 0.0).astype(BF16)
    xg_all = _dot(p_all, xb_ref[...]).astype(BF16)
    gg_all = _dot(p_all, gs_ref[...])
    outs = []
    for g in range(MOE_GROUPS):
        if pipelined:
            front_stages[g]()
        outs.append(experts(g, xg_all[g * MOE_CAP:(g + 1) * MOE_CAP], gg_all[g * MOE_CAP:(g + 1) * MOE_CAP], MOE_CAP))
    c_io = lax.broadcasted_iota(jnp.int32, (tm, slots), 1).astype(F32)
    pt_all = jnp.where(slot_col == c_io, 1.0, 0.0).astype(BF16)
    acc_ref[...] = _dot(pt_all, jnp.concatenate(outs, axis=0))

    for g in range(MOE_GROUPS):
        n_ov = jnp.maximum(cnt[0, MOE_EXPERTS + g].astype(jnp.int32) - MOE_CAP + MOE_OV - 1, 0) // MOE_OV

        def overflow(i, _, g=g):
            base = (MOE_CAP + i * MOE_OV).astype(F32)
            p = gather_mat(g, base, MOE_OV)
            out = experts(g, _dot(p, xb_ref[...]).astype(BF16), _dot(p, gs_ref[...]), MOE_OV)
            acc_ref[...] += _dot(scatter_mat(g, base, MOE_OV), out)
            return 0

        lax.fori_loop(0, n_ov, overflow, 0)

    if pipelined:
        xb_ref[...] = xb_nx[...]
        gs_ref[...] = gs_nx[...]
        rkc_ref[...] = rkc_nx[...]
        rkr_ref[...] = rkr_nx[...]
    o_ref[...] = _layer_norm(DN_ALPHA * x_ref[...] + acc_ref[...], g_ref[...], b_ref[...])
    if pipelined:
        x_ref[...] = x_nx[...]


def _layer_tail(mix, att, x, wm, wa, ln1, wr, br, wg, wu, wd, layer, ln2, glu_w=None, glu_b=None, tm=512):
    t = x.shape[0]
    tm = min(tm, t)
    nt = t // tm
    idx = jnp.arange(tm)
    ltri = (idx[:, None] > idx[None, :]).astype(BF16)
    pipelined = nt >= TAIL_PIPELINE_MIN_TILES
    if pipelined:
        row_in = lambda w: pl.BlockSpec((tm, w), lambda s: (jnp.minimum(s, nt - 1), 0))
        row_out = pl.BlockSpec((tm, D_MODEL), lambda s: (jnp.maximum(s - 1, 0), 0))
    else:
        row_in = lambda w: pl.BlockSpec((tm, w), lambda s: (s, 0))
        row_out = pl.BlockSpec((tm, D_MODEL), lambda s: (s, 0))
    full = lambda a: pl.BlockSpec(a.shape, lambda s: (0,) * a.ndim, pipeline_mode=pl.Buffered(1))
    layer_w = lambda a: pl.BlockSpec((1,) + a.shape[1:], lambda s: (layer, 0, 0, 0),
                                     pipeline_mode=pl.Buffered(1))
    args = [mix, att, x, wm, wa, ln1[0], ln1[1], wr, br, ltri, wg, wu, wd, ln2[0], ln2[1]]
    specs = [row_in(mix.shape[1]), row_in(XA_WIDTH), row_in(D_MODEL), full(wm), full(wa), full(ln1[0]),
             full(ln1[1]), full(wr), full(br), full(ltri), layer_w(wg), layer_w(wu), layer_w(wd),
             full(ln2[0]), full(ln2[1])]
    if glu_w is not None:
        args += [glu_w, glu_b]
        specs += [full(glu_w), full(glu_b)]
    parked = [pltpu.VMEM((tm, D_MODEL), BF16), pltpu.VMEM((tm, 128), BF16), pltpu.VMEM((tm, D_MODEL), F32),
              pltpu.VMEM((tm, 128), F32), pltpu.VMEM((128, tm), F32)]
    return pl.pallas_call(
        functools.partial(_tail_kernel, glu=glu_w is not None, pipelined=pipelined),
        grid=(nt + 1 if pipelined else nt,),
        in_specs=specs,
        out_specs=row_out,
        out_shape=jax.ShapeDtypeStruct((t, D_MODEL), F32),
        scratch_shapes=[pltpu.VMEM((tm, D_MODEL), F32)] + parked + (parked if pipelined else []),
        compiler_params=pltpu.CompilerParams(dimension_semantics=("arbitrary",),
                                             vmem_limit_bytes=TAIL_VMEM_LIMIT),
        name="layer_tail",
    )(*args)


def _s5_kernel(*refs, tlen, ngrp, chained, precise):
    if chained:
        u_ref, wb_ref, wc_ref, lam_ref, d_ref, y_ref, ore_ref, oim_ref, bu_ref, h_ref, cr_ref, ci_ref = refs
        u = jnp.swapaxes(u_ref[...], 0, 1).reshape(8 * tlen, 128)
    else:
        (u_ref, h0r_ref, h0i_ref, wb_ref, wc_ref, lam_ref, d_ref,
         y_ref, ore_ref, oim_ref, bu_ref, h_ref, cr_ref, ci_ref) = refs
        gr = 8 * tlen
        u = jnp.concatenate(
            [jnp.swapaxes(u_ref[g * gr:(g + 1) * gr, :].reshape(8, tlen, 128), 0, 1).reshape(gr, 128)
             for g in range(ngrp)], axis=0)
    if precise:
        bu_ref[...] = _dot_hi(u, wb_ref[0])
    else:
        bu_ref[...] = _dot(u.astype(BF16), wb_ref[0].astype(BF16))
    ns = S5_SLAB_STATE
    lr = jnp.broadcast_to(lam_ref[0, 0:1, :], (8, ns))
    li = jnp.broadcast_to(lam_ref[0, 1:2, :], (8, ns))

    def step(tile, hr, hi):
        rows = pl.ds(pl.multiple_of(tile * 8, 8), 8)
        nr = lr * hr - li * hi + bu_ref[rows, 0:ns]
        ni = lr * hi + li * hr + bu_ref[rows, ns:2 * ns]
        h_ref[rows, 0:ns] = nr
        h_ref[rows, ns:2 * ns] = ni
        return nr, ni

    if chained:
        @pl.when(pl.program_id(2) == 0)
        def _():
            cr_ref[...] = jnp.zeros_like(cr_ref)
            ci_ref[...] = jnp.zeros_like(ci_ref)

        hr, hi = lax.fori_loop(0, tlen, lambda t, c: step(t, *c), (cr_ref[...], ci_ref[...]), unroll=8)
        cr_ref[...] = hr
        ci_ref[...] = hi
        ore_ref[...] = hr
        oim_ref[...] = hi
    else:
        cr_ref[...] = jnp.transpose(h0r_ref[0])
        ci_ref[...] = jnp.transpose(h0i_ref[0])

        def group(g, _):
            s0 = pl.multiple_of(g * 8, 8)
            hr, hi = cr_ref[pl.ds(s0, 8), :], ci_ref[pl.ds(s0, 8), :]
            for t in range(tlen):
                hr, hi = step(g * tlen + t, hr, hi)
            cr_ref[pl.ds(s0, 8), :] = hr
            ci_ref[pl.ds(s0, 8), :] = hi
            return 0

        lax.fori_loop(0, ngrp, group, 0)
        ore_ref[...] = jnp.transpose(cr_ref[...])
        oim_ref[...] = jnp.transpose(ci_ref[...])
    y = _dot(h_ref[...].astype(BF16), wc_ref[0].astype(BF16)) + d_ref[0] * u
    if chained:
        y_ref[...] = jnp.swapaxes(y.reshape(tlen, 8, 128), 0, 1)
    else:
        gr = 8 * tlen
        for g in range(ngrp):
            y_ref[g * gr:(g + 1) * gr, :] = jnp.swapaxes(
                y[g * gr:(g + 1) * gr, :].reshape(tlen, 8, 128), 0, 1).reshape(gr, 128)


def _s5_mixer_prompt(z, params, seq_len, tlen=256):
    t = z.shape[0]
    nseq = t // seq_len
    ns = S5_SLAB_STATE
    z3 = z.reshape(nseq, seq_len, z.shape[1])
    slab = lambda a: pl.BlockSpec((1,) + a.shape[1:], lambda j, b, i: (j,) + (0,) * (a.ndim - 1))
    u_spec = pl.BlockSpec((8, tlen, 128), lambda j, b, i: (b, i, j))
    st_spec = pl.BlockSpec((8, ns), lambda j, b, i: (b, j))
    st_shape = jax.ShapeDtypeStruct((nseq, S5_SLABS * ns), F32)
    rows = 8 * tlen
    y, hre, him = pl.pallas_call(
        functools.partial(_s5_kernel, tlen=tlen, ngrp=1, chained=True, precise=False),
        grid=(S5_SLABS, nseq // 8, seq_len // tlen),
        in_specs=[u_spec] + [slab(a) for a in params],
        out_specs=[u_spec, st_spec, st_spec],
        out_shape=[jax.ShapeDtypeStruct((nseq, seq_len, MIX_WIDTH), F32), st_shape, st_shape],
        scratch_shapes=[pltpu.VMEM((rows, 2 * ns), F32), pltpu.VMEM((rows, 2 * ns), F32),
                        pltpu.VMEM((8, ns), F32), pltpu.VMEM((8, ns), F32)],
        compiler_params=_cparams("parallel", "parallel", "arbitrary"),
        name="s5_mixer",
    )(z3, *params)
    return y.reshape(t, MIX_WIDTH), hre, him


def _s5_mixer_sample(z, params, seq_len, h0r, h0i, layer):
    t = z.shape[0]
    nseq = t // seq_len
    ns = S5_SLAB_STATE
    seqs = 128
    rows = seqs * seq_len
    slab = lambda a: pl.BlockSpec((1,) + a.shape[1:], lambda j, i: (j,) + (0,) * (a.ndim - 1))
    u_spec = pl.BlockSpec((rows, 128), lambda j, i: (i, j))
    h0_spec = pl.BlockSpec((1, ns, seqs), lambda j, i: (layer, j, i))
    st_spec = pl.BlockSpec((ns, seqs), lambda j, i: (j, i))
    st_shape = jax.ShapeDtypeStruct((S5_SLABS * ns, nseq), F32)
    return pl.pallas_call(
        functools.partial(_s5_kernel, tlen=seq_len, ngrp=seqs // 8, chained=False, precise=True),
        grid=(S5_SLABS, nseq // seqs),
        in_specs=[u_spec, h0_spec, h0_spec] + [slab(a) for a in params],
        out_specs=[u_spec, st_spec, st_spec],
        out_shape=[jax.ShapeDtypeStruct((t, MIX_WIDTH), F32), st_shape, st_shape],
        scratch_shapes=[pltpu.VMEM((rows, 2 * ns), F32), pltpu.VMEM((rows, 2 * ns), F32),
                        pltpu.VMEM((seqs, ns), F32), pltpu.VMEM((seqs, ns), F32)],
        compiler_params=_cparams("parallel", "parallel"),
        name="s5_mixer",
    )(z, h0r, h0i, *params)


def _s5_params(lam_re, lam_im, log_dt, b_re, b_im, c_re, c_im, d_skip):
    lre, lim = lam_re.astype(F32), lam_im.astype(F32)
    dt = jnp.exp(log_dt.astype(F32))[:, None]
    mag = jnp.exp(lre * dt)
    bar_re, bar_im = mag * jnp.cos(lim * dt), mag * jnp.sin(lim * dt)
    nre, nim = bar_re - 1.0, bar_im
    den = lre * lre + lim * lim
    fre, fim = (nre * lre + nim * lim) / den, (nim * lre - nre * lim) / den
    bre, bim = b_re.astype(F32), b_im.astype(F32)
    bb_re = fre[..., None] * bre - fim[..., None] * bim
    bb_im = fre[..., None] * bim + fim[..., None] * bre
    eye = jnp.eye(8, dtype=F32)
    gps = 8

    def blockdiag_in(m):
        m = m.reshape(S5_SLABS, gps, S5_STATE, S5_GROUP).transpose(0, 1, 3, 2)
        return jnp.einsum('jgcp,gh->jgchp', m, eye).reshape(S5_SLABS, 128, S5_SLAB_STATE)

    def blockdiag_out(m):
        m = m.reshape(S5_SLABS, gps, S5_GROUP, S5_STATE).transpose(0, 1, 3, 2)
        return jnp.einsum('jgpc,gh->jgphc', m, eye).reshape(S5_SLABS, S5_SLAB_STATE, 128)

    wb = jnp.concatenate([blockdiag_in(bb_re), blockdiag_in(bb_im)], axis=2)
    wc = jnp.concatenate([blockdiag_out(c_re.astype(F32)), blockdiag_out(-c_im.astype(F32))], axis=1)
    lam = jnp.stack([bar_re.reshape(S5_SLABS, S5_SLAB_STATE), bar_im.reshape(S5_SLABS, S5_SLAB_STATE)], axis=1)
    dsk = d_skip.astype(F32).reshape(S5_SLABS, 1, 128)
    return wb, wc, lam, dsk


def _head_norm_gate(o, gn, gate):
    col = lax.broadcasted_iota(jnp.int32, o.shape, 1)
    valid = col < RET_DV
    mu = jnp.sum(o, axis=-1, keepdims=True) * (1.0 / RET_DV)
    d = jnp.where(valid, o - mu, 0.0)
    var = jnp.sum(d * d, axis=-1, keepdims=True) * (1.0 / RET_DV)
    return d * lax.rsqrt(var + NORM_EPS) * gn * _silu(gate)


def _load_states(s_in_ref, spad_ref, nseg, dk):
    spad_ref[...] = jnp.zeros_like(spad_ref)
    for s in range(nseg):
        for h in range(HEADS):
            spad_ref[h, s * DK_PAD:s * DK_PAD + dk, 0:RET_DV] = s_in_ref[s, h]


def _ret_kernel(*refs, nseg, gammas):
    if nseg == 1:
        (q_ref, k_ref, v_ref, g_ref, cos_ref, sin_ref, dmat_ref, inner_ref, zeta_ref, gn_ref,
         o_ref, so_ref, st_ref) = refs

        @pl.when(pl.program_id(1) == 0)
        def _():
            st_ref[...] = jnp.zeros_like(st_ref)
    else:
        (q_ref, k_ref, v_ref, g_ref, cos_ref, sin_ref, dmat_ref, inner_ref, zeta_ref, gn_ref, s_in_ref,
         o_ref, so_ref, st_ref) = refs
        st_ref[:, RET_DV:DV_PAD, :] = jnp.zeros((HEADS, DV_PAD - RET_DV, nseg * RET_DK), F32)
        for s in range(nseg):
            for h in range(HEADS):
                st_ref[h, 0:RET_DV, s * RET_DK:(s + 1) * RET_DK] = s_in_ref[s, h]
    chained = nseg == 1
    nb = q_ref.shape[0] if chained else 1
    rd = (lambda ref, sq, cs: ref[sq, :, cs]) if chained else (lambda ref, sq, cs: ref[:, cs])
    r = q_ref.shape[-2]
    seg_len = r // nseg
    cos = cos_ref[...]
    sin = sin_ref[...]
    rot = lambda x: x * cos + pltpu.roll(x, RET_DK // 2, axis=1) * sin
    rowseg = lax.broadcasted_iota(jnp.int32, (r, DK_PAD), 0) // seg_len
    colseg = lax.broadcasted_iota(jnp.int32, (DV_PAD, r), 1) // seg_len

    def head(sq, h):
        sl = slice(h * DK_PAD, (h + 1) * DK_PAD)
        sv = slice(h * DV_PAD, (h + 1) * DV_PAD)
        q = rot(rd(q_ref, sq, sl))
        k = rot(rd(k_ref, sq, sl)) * (RET_DK ** -0.5)
        v = rd(v_ref, sq, sv)
        vb = v.astype(BF16)
        qb = q.astype(BF16)
        att = _dot_nt(qb, k.astype(BF16)) * dmat_ref[h]
        o = _dot(att.astype(BF16), vb)
        st_all = st_ref[sq, h] if chained else st_ref[h]
        if chained:
            qbd = qb
        else:
            qbd = jnp.concatenate([jnp.where(rowseg == s, q, 0.0) for s in range(nseg)], axis=1).astype(BF16)
        o = o + _dot_nt(qbd, st_all.astype(BF16)) * inner_ref[h]
        kzb = (k * zeta_ref[h]).astype(BF16)
        vt = jnp.transpose(v)
        for s in range(nseg):
            vts = vt if chained else jnp.where(colseg == s, vt, 0.0)
            st_new = st_all[:, s * RET_DK:(s + 1) * RET_DK] * gammas[h] + _dot(vts.astype(BF16), kzb)
            if chained:
                st_ref[sq, h] = st_new
                so_ref[sq, h] = st_new[0:RET_DV, :]
            else:
                so_ref[s, h] = st_new[0:RET_DV, :]
        out = _head_norm_gate(o, gn_ref[:, sv], rd(g_ref, sq, sv)).astype(o_ref.dtype)
        if chained:
            o_ref[sq, :, sv] = out
        else:
            o_ref[:, sv] = out

    for h in range(HEADS):
        for sq in range(nb):
            head(sq, h)


def _ret_consts(seg_len, nseg, pos):
    hh = jnp.arange(RET_HEADS, dtype=F32)
    log_g = jnp.log1p(-jnp.exp2(-5.0 - hh))
    idx = jnp.arange(seg_len, dtype=F32)
    rel = idx[:, None] - idx[None, :]
    dmat = jnp.where(rel >= 0, jnp.exp(log_g[:, None, None] * jnp.maximum(rel, 0.0)), 0.0)
    inner = jnp.exp(log_g[:, None] * (idx + 1.0))
    zeta = jnp.exp(log_g[:, None] * (seg_len - 1.0 - idx))
    r = seg_len * nseg
    seg = jnp.arange(r) // seg_len
    same = (seg[:, None] == seg[None, :]).astype(F32)
    dmat = jnp.tile(dmat, (1, nseg, nseg)) * same[None]
    inner = jnp.broadcast_to(jnp.tile(inner, (1, nseg))[:, :, None], (RET_HEADS, r, DV_PAD))
    zeta = jnp.broadcast_to(jnp.tile(zeta, (1, nseg))[:, :, None], (RET_HEADS, r, DK_PAD))
    half = RET_DK // 2
    inv_freq = ROPE_BASE ** (-jnp.arange(half, dtype=F32) / half)
    ang = pos.astype(F32)[:, None] * inv_freq
    cos2 = jnp.concatenate([jnp.cos(ang), jnp.cos(ang)], axis=-1)
    sin2 = jnp.concatenate([-jnp.sin(ang), jnp.sin(ang)], axis=-1)
    return dmat, inner, zeta, cos2, sin2


def _ret_gammas(seg_len):
    return tuple(float((1.0 - 2.0 ** (-5.0 - h)) ** seg_len) for h in range(RET_HEADS))


def _ret_mixer(z, gn, seq_len, pos, s0=None):
    t = z.shape[0]
    nseq = t // seq_len
    r = 128
    if s0 is None:
        nb = 2 if nseq % 2 == 0 else 1
        nseg, seg_len = 1, RET_CHUNK
        grid = (nseq // nb, seq_len // r)
        z = z.reshape(nseq, seq_len, z.shape[1])
        blk = lambda w, c: pl.BlockSpec((nb, r, w), lambda n, i: (n, i, c))
        mix_shape = (nseq, seq_len, HEADS * DV_PAD)
        tabmap = lambda n, i: (i, 0)
        stmap = lambda n, i: (n, 0, 0, 0)
        st_rows = nb
        scratch = pltpu.VMEM((nb, HEADS, DV_PAD, RET_DK), F32)
        sem = ("parallel", "arbitrary")
    else:
        seg_len = seq_len
        nseg = r // seg_len
        grid = (t // r,)
        blk = lambda w, c: pl.BlockSpec((r, w), lambda i: (i, c))
        mix_shape = (t, HEADS * DV_PAD)
        tabmap = lambda i: (0, 0)
        stmap = lambda i: (i, 0, 0, 0)
        st_rows = nseg
        scratch = pltpu.VMEM((HEADS, DV_PAD, nseg * RET_DK), F32)
        sem = ("arbitrary",)
    dmat, inner, zeta, cos2, sin2 = _ret_consts(seg_len, nseg, pos)
    if s0 is not None:
        cos2 = jnp.tile(cos2, (nseg, 1))
        sin2 = jnp.tile(sin2, (nseg, 1))
    full = lambda a: pl.BlockSpec(a.shape, lambda *_: (0,) * a.ndim)
    st_spec = pl.BlockSpec((st_rows, RET_HEADS, RET_DV, RET_DK), stmap)
    in_specs = [blk(512, 0), blk(512, 1), blk(1024, 1), blk(1024, 2),
                pl.BlockSpec((r, DK_PAD), tabmap), pl.BlockSpec((r, DK_PAD), tabmap),
                full(dmat), full(inner), full(zeta), full(gn)]
    args = [z, z, z, z, cos2, sin2, dmat, inner, zeta, gn]
    if s0 is not None:
        in_specs.append(st_spec)
        args.append(s0)
    mix, s_new = pl.pallas_call(
        functools.partial(_ret_kernel, nseg=nseg, gammas=_ret_gammas(seg_len)),
        grid=grid,
        in_specs=in_specs,
        out_specs=[blk(HEADS * DV_PAD, 0), st_spec],
        out_shape=[jax.ShapeDtypeStruct(mix_shape, BF16),
                   jax.ShapeDtypeStruct((nseq, RET_HEADS, RET_DV, RET_DK), F32)],
        scratch_shapes=[scratch],
        compiler_params=_cparams(*sem),
        name="retention",
    )(*args)
    return mix.reshape(t, HEADS * DV_PAD), s_new


def _log_sigmoid(x):
    return jnp.minimum(x, 0.0) - jnp.log1p(jnp.exp(-jnp.abs(x)))


def _gla_kernel(*refs, nseg, chained):
    if chained:
        (q_ref, k_ref, v_ref, r_ref, low_ref, tril_ref, same_ref, wg_ref, bg_ref, gn_ref,
         o_ref, so_ref, spad_ref) = refs

        @pl.when(pl.program_id(1) == 0)
        def _():
            spad_ref[...] = jnp.zeros_like(spad_ref)
    else:
        (q_ref, k_ref, v_ref, r_ref, low_ref, tril_ref, same_ref, wg_ref, bg_ref, gn_ref, s_in_ref,
         o_ref, so_ref, spad_ref) = refs
        _load_states(s_in_ref, spad_ref, nseg, GLA_DK)
    nb = q_ref.shape[0] if chained else 1
    rd = (lambda ref, sq, cs: ref[sq, :, cs]) if chained else (lambda ref, sq, cs: ref[:, cs])
    r = q_ref.shape[-2]
    seg_len = r // nseg
    tril = tril_ref[...]
    causal = tril > 0.0
    every = slice(None)
    la = [_log_sigmoid(_dot_hi(rd(low_ref, sq, every), wg_ref[...]) + bg_ref[...]) * (1.0 / GLA_TAU)
          for sq in range(nb)]
    b = [_dot_hi(tril, la[sq]) for sq in range(nb)]
    b_tot = [_dot_hi(same_ref[...], la[sq]) for sq in range(nb)]
    rowseg = lax.broadcasted_iota(jnp.int32, (r, DK_PAD), 0) // seg_len
    colseg = lax.broadcasted_iota(jnp.int32, (DK_PAD, r), 1) // seg_len

    def head(sq, h):
        sl = slice(h * DK_PAD, (h + 1) * DK_PAD)
        sv = slice(h * DV_PAD, (h + 1) * DV_PAD)
        bh = b[sq][:, sl]
        q = rd(q_ref, sq, sl) * (GLA_DK ** -0.5)
        k = rd(k_ref, sq, sl)
        vb = rd(v_ref, sq, sv).astype(BF16)
        q_t = q * jnp.exp(bh)
        k_t = k * jnp.exp(-bh)
        att = jnp.where(causal, _dot_nt(q_t.astype(BF16), k_t.astype(BF16)), 0.0)
        o = _dot(att.astype(BF16), vb)
        kdt = jnp.transpose(k * jnp.exp(b_tot[sq][:, sl] - bh))
        lat = jnp.transpose(la[sq][:, sl])

        def seg_update(s, s_old):
            kds = jnp.where(colseg == s, kdt, 0.0) if nseg > 1 else kdt
            las = jnp.where(colseg == s, lat, 0.0) if nseg > 1 else lat
            decay = jnp.exp(jnp.sum(las, axis=-1, keepdims=True))
            return s_old * decay + _dot(kds.astype(BF16), vb)

        if chained:
            qb = q_t.astype(BF16)
            s_cur = spad_ref[sq, h]
            inter = []
            for s in range(nseg):
                inter.append(_dot(qb[s * seg_len:(s + 1) * seg_len], s_cur.astype(BF16)))
                s_cur = seg_update(s, s_cur)
            o = o + jnp.concatenate(inter, axis=0)
            spad_ref[sq, h] = s_cur
            so_ref[sq, h] = s_cur[0:GLA_DK, 0:GLA_DV]
        else:
            s_all = spad_ref[h]
            qbd = jnp.concatenate([jnp.where(rowseg == s, q_t, 0.0) for s in range(nseg)], axis=1).astype(BF16)
            o = o + _dot(qbd, s_all.astype(BF16))
            for s in range(nseg):
                so_ref[s, h] = seg_update(s, s_all[s * DK_PAD:(s + 1) * DK_PAD])[0:GLA_DK, 0:GLA_DV]
        out = _head_norm_gate(o, gn_ref[:, sv], rd(r_ref, sq, sv)).astype(o_ref.dtype)
        if chained:
            o_ref[sq, :, sv] = out
        else:
            o_ref[:, sv] = out

    for h in range(HEADS):
        for sq in range(nb):
            head(sq, h)


def _gla_mixer(z, wg, bg, gn, seq_len, s0=None):
    t = z.shape[0]
    nseq = t // seq_len
    r = 128
    if s0 is None:
        nb = 2 if nseq % 2 == 0 else 1
        seg_len = GLA_CHUNK
        nseg = r // seg_len
        grid = (nseq // nb, seq_len // r)
        z = z.reshape(nseq, seq_len, z.shape[1])
        blk = lambda w, c: pl.BlockSpec((nb, r, w), lambda n, i: (n, i, c))
        mix_shape = (nseq, seq_len, HEADS * DV_PAD)
        stmap = lambda n, i: (n, 0, 0, 0)
        st_rows = nb
        scratch = pltpu.VMEM((nb, HEADS, DK_PAD, DV_PAD), F32)
        sem = ("parallel", "arbitrary")
    else:
        seg_len = seq_len
        nseg = r // seg_len
        grid = (t // r,)
        blk = lambda w, c: pl.BlockSpec((r, w), lambda i: (i, c))
        mix_shape = (t, HEADS * DV_PAD)
        stmap = lambda i: (i, 0, 0, 0)
        st_rows = nseg
        scratch = pltpu.VMEM((HEADS, nseg * DK_PAD, DV_PAD), F32)
        sem = ("arbitrary",)
    seg = jnp.arange(r) // seg_len
    idx = jnp.arange(r)
    same = (seg[:, None] == seg[None, :]).astype(F32)
    tril = same * (idx[:, None] >= idx[None, :]).astype(F32)
    full = lambda a: pl.BlockSpec(a.shape, lambda *_: (0,) * a.ndim)
    st_spec = pl.BlockSpec((st_rows, GLA_HEADS, GLA_DK, GLA_DV), stmap)
    in_specs = [blk(512, 0), blk(512, 1), blk(1024, 1), blk(1024, 2), blk(128, COL_LOW // 128),
                full(tril), full(same), full(wg), full(bg), full(gn)]
    args = [z, z, z, z, z, tril, same, wg, bg, gn]
    if s0 is not None:
        in_specs.append(st_spec)
        args.append(s0)
    mix, s_new = pl.pallas_call(
        functools.partial(_gla_kernel, nseg=nseg, chained=s0 is None),
        grid=grid,
        in_specs=in_specs,
        out_specs=[blk(HEADS * DV_PAD, 0), st_spec],
        out_shape=[jax.ShapeDtypeStruct(mix_shape, BF16),
                   jax.ShapeDtypeStruct((nseq, GLA_HEADS, GLA_DK, GLA_DV), F32)],
        scratch_shapes=[scratch],
        compiler_params=_cparams(*sem),
        name="gla",
    )(*args)
    return mix.reshape(t, HEADS * DV_PAD), s_new


def _pad_heads(w, width, pad):
    lead = w.shape[:-1]
    w = w.reshape(lead + (HEADS, width))
    w = jnp.pad(w, [(0, 0)] * len(lead) + [(0, 0), (0, pad - width)])
    return w.reshape(lead + (HEADS * pad,))


def _pad_head_rows(w, width, pad):
    return jnp.swapaxes(_pad_heads(jnp.swapaxes(w, 0, 1), width, pad), 0, 1)


def _ret_in_weight(w):
    qk = RET_HEADS * RET_DK
    v = _pad_heads(w[:, 2 * qk:2 * qk + MIX_WIDTH], RET_DV, DV_PAD)
    g = _pad_heads(w[:, 2 * qk + MIX_WIDTH:2 * qk + 2 * MIX_WIDTH], RET_DV, DV_PAD)
    return jnp.concatenate([w[:, :2 * qk], v, g, w[:, -XA_WIDTH:]], axis=1).astype(BF16)


def _gla_in_weight(w):
    qk = GLA_HEADS * GLA_DK
    q = _pad_heads(w[:, :qk], GLA_DK, DK_PAD)
    k = _pad_heads(w[:, qk:2 * qk], GLA_DK, DK_PAD)
    v = _pad_heads(w[:, 2 * qk:2 * qk + MIX_WIDTH], GLA_DV, DV_PAD)
    r = _pad_heads(w[:, 2 * qk + MIX_WIDTH:2 * qk + 2 * MIX_WIDTH], GLA_DV, DV_PAD)
    low = jnp.pad(w[:, 2 * qk + 2 * MIX_WIDTH:2 * qk + 2 * MIX_WIDTH + GLA_RANK], [(0, 0), (0, 128 - GLA_RANK)])
    return jnp.concatenate([q, k, v, r, w[:, -XA_WIDTH:], low], axis=1).astype(BF16)


def kernel(x_prompt, x_sample, mem_prompt, state_s5_re, state_s5_im, state_ret, state_gla,
           cache_mem_k, cache_mem_v,
           w_in_s5, s5_lam_re, s5_lam_im, s5_log_dt, s5_b_re, s5_b_im, s5_c_re, s5_c_im,
           s5_d, s5_w_glu, s5_b_glu,
           w_in_ret, ret_gn,
           w_in_gla, gla_w_gate2, gla_b_gate2, gla_gn,
           w_mem_k, w_mem_v, w_out, ln_g, ln_b,
           moe_w_grp, moe_b_grp, moe_w_exp, moe_b_exp, moe_w_gate, moe_w_up, moe_w_down):
    bp, lp, _ = x_prompt.shape
    bs, ls, _ = x_sample.shape
    xp = x_prompt.reshape(bp * lp, D_MODEL)
    xs = x_sample.reshape(bs * ls, D_MODEL)
    pos_p = jnp.arange(lp, dtype=jnp.int32)
    pos_s = PAST_LEN + jnp.arange(ls, dtype=jnp.int32)

    to_t = lambda c: jnp.transpose(c, (0, 1, 3, 4, 2)).reshape(DEPTH, c.shape[1], XA_WIDTH, N_MEM)
    from_t = lambda c: jnp.transpose(c.reshape(DEPTH, c.shape[1], XA_HEADS, XA_HEAD_DIM, N_MEM), (0, 1, 4, 2, 3))
    mkt_all, mvt_all = _mem_kv(mem_prompt, jnp.swapaxes(w_mem_k, 1, 2).astype(BF16),
                               jnp.swapaxes(w_mem_v, 1, 2).astype(BF16))
    cache_kt = to_t(cache_mem_k)
    cache_vt = to_t(cache_mem_v)
    ret_t = jnp.swapaxes(state_ret, 3, 4)
    s5_t = lambda s: jnp.transpose(s, (0, 2, 3, 1)).reshape(s.shape[0], S5_GROUPS * S5_STATE, bs)
    s5_re_t, s5_im_t = s5_t(state_s5_re), s5_t(state_s5_im)
    moe_wg, moe_wu, moe_wd = moe_w_gate.astype(BF16), moe_w_up.astype(BF16), moe_w_down.astype(BF16)

    p_re, p_im, p_ret, p_gla = [], [], [], []
    s_re, s_im, s_ret, s_gla = [], [], [], []
    for i in range(DEPTH):
        kind, j = i % N_MIXERS, i // N_MIXERS
        ln1 = (ln_g[i, 0].reshape(1, D_MODEL), ln_b[i, 0].reshape(1, D_MODEL))
        ln2 = (ln_g[i, 1].reshape(1, D_MODEL), ln_b[i, 1].reshape(1, D_MODEL))
        wo = w_out[i]
        wo_att = wo[MIX_WIDTH:].astype(BF16)
        glu = {}
        if kind == 0:
            params = _s5_params(s5_lam_re[j], s5_lam_im[j], s5_log_dt[j], s5_b_re[j], s5_b_im[j],
                                s5_c_re[j], s5_c_im[j], s5_d[j])
            zp = _matmul(xp, w_in_s5[j].astype(BF16), 512)
            zs = _matmul(xs, w_in_s5[j], 512, precise=True)
            mix_p, a_re, a_im = _s5_mixer_prompt(zp, params, lp)
            mix_s, b_re, b_im = _s5_mixer_sample(zs, params, ls, s5_re_t, s5_im_t, j)
            p_re.append(a_re), p_im.append(a_im), s_re.append(b_re), s_im.append(b_im)
            xa_col = MIX_WIDTH
            wo_mix = wo[:MIX_WIDTH].astype(BF16)
            glu = dict(glu_w=s5_w_glu[j].astype(BF16), glu_b=s5_b_glu[j].reshape(1, MIX_WIDTH))
        elif kind == 1:
            w = _ret_in_weight(w_in_ret[j])
            gn = _pad_heads(ret_gn[j], RET_DV, DV_PAD).reshape(1, HEADS * DV_PAD)
            zp = _matmul(xp, w, 512)
            zs = _matmul(xs, w, 512)
            mix_p, a_s = _ret_mixer(zp, gn, lp, pos_p)
            mix_s, b_s = _ret_mixer(zs, gn, ls, pos_s, s0=ret_t[j])
            p_ret.append(a_s), s_ret.append(b_s)
            xa_col = COL_XA
            wo_mix = _pad_head_rows(wo[:MIX_WIDTH], RET_DV, DV_PAD).astype(BF16)
        else:
            w = _gla_in_weight(w_in_gla[j])
            gn = _pad_heads(gla_gn[j], GLA_DV, DV_PAD).reshape(1, HEADS * DV_PAD)
            wg = jnp.pad(_pad_heads(gla_w_gate2[j], GLA_DK, DK_PAD), [(0, 128 - GLA_RANK), (0, 0)])
            bg = _pad_heads(gla_b_gate2[j], GLA_DK, DK_PAD).reshape(1, HEADS * DK_PAD)
            zp = _matmul(xp, w, 512)
            zs = _matmul(xs, w, 512)
            mix_p, a_s = _gla_mixer(zp, wg, bg, gn, lp)
            mix_s, b_s = _gla_mixer(zs, wg, bg, gn, ls, s0=state_gla[j])
            p_gla.append(a_s), s_gla.append(b_s)
            xa_col = COL_XA
            wo_mix = _pad_head_rows(wo[:MIX_WIDTH], GLA_DV, DV_PAD).astype(BF16)
        att_p = _mem_attention(zp, xa_col, mkt_all, mvt_all, i, lp, rl=min(512, lp), nseq=1)
        att_s = _mem_attention(zs, xa_col, cache_kt, cache_vt, i, ls, rl=ls, nseq=128 // ls)
        wr = jnp.pad(jnp.concatenate([moe_w_exp[i], moe_w_grp[i]], axis=1),
                     [(0, 0), (0, 128 - MOE_EXPERTS - MOE_GROUPS)]).astype(F32)
        wr_hi = wr.astype(BF16)
        wr = jnp.concatenate([wr_hi, (wr - wr_hi.astype(F32)).astype(BF16)], axis=1)
        br = jnp.pad(jnp.concatenate([moe_b_exp[i], moe_b_grp[i]]),
                     [(0, 128 - MOE_EXPERTS - MOE_GROUPS)]).reshape(1, 128).astype(F32)
        xp = _layer_tail(mix_p, att_p, xp, wo_mix, wo_att, ln1, wr, br, moe_wg, moe_wu, moe_wd, i, ln2, **glu)
        xs = _layer_tail(mix_s, att_s, xs, wo_mix, wo_att, ln1, wr, br, moe_wg, moe_wu, moe_wd, i, ln2, **glu)

    st = lambda xs_: jnp.stack(xs_)
    s5_p = lambda xs_: st(xs_).reshape(len(xs_), bp, S5_GROUPS, S5_STATE)
    s5_s = lambda xs_: jnp.transpose(st(xs_).reshape(len(xs_), S5_GROUPS, S5_STATE, bs), (0, 3, 1, 2))
    ret_out = lambda xs_: jnp.swapaxes(st(xs_), 3, 4)
    return (xp.reshape(bp, lp, D_MODEL), xs.reshape(bs, ls, D_MODEL),
            s5_p(p_re), s5_p(p_im), ret_out(p_ret), st(p_gla),
            from_t(mkt_all), from_t(mvt_all),
            s5_s(s_re), s5_s(s_im), ret_out(s_ret), st(s_gla))
```

```python
import functools
import math

import jax
import jax.numpy as jnp
from jax import lax
from jax.experimental import pallas as pl
from jax.experimental.pallas import tpu as pltpu

F32 = jnp.float32
BF16 = jnp.bfloat16
HI = lax.Precision.HIGHEST

D_MODEL = 1024
DEPTH = 4
PAST_LEN = 16384
N_MIXERS = 3
MIX_WIDTH = 768
XA_HEADS = 4
XA_HEAD_DIM = 64
XA_WIDTH = 256
N_MEM = 256
S5_GROUP = 16
S5_GROUPS = 48
S5_STATE = 64
S5_SLABS = 6
S5_SLAB_STATE = 512
RET_HEADS = 4
RET_DK = 128
RET_DV = 192
RET_CHUNK = 128
ROPE_BASE = 10000.0
GLA_HEADS = 4
GLA_DK = 96
GLA_DV = 192
GLA_RANK = 16
GLA_TAU = 16.0
GLA_CHUNK = 64
MOE_GROUPS = 4
MOE_PER_GROUP = 4
MOE_EXPERTS = 16
MOE_HIDDEN = 256
DN_ALPHA = (2 * DEPTH) ** 0.25
NORM_EPS = 1e-5

DK_PAD = 128
DV_PAD = 256
HEADS = 4
COL_Q, COL_K, COL_V, COL_G, COL_XA, COL_LOW = 0, 512, 1024, 2048, 3072, 3328
RET_IN_PAD = 3328
GLA_IN_PAD = 3456
VMEM_LIMIT = 52 * 1024 * 1024
TAIL_VMEM_LIMIT = 58 * 1024 * 1024


def _cparams(*sem):
    return pltpu.CompilerParams(dimension_semantics=sem, vmem_limit_bytes=VMEM_LIMIT)


def _dot(a, b):
    return jnp.dot(a, b, preferred_element_type=F32)


def _dot_nt(a, b):
    return lax.dot_general(a, b, (((1,), (1,)), ((), ())), preferred_element_type=F32)


def _dot_hi(a, b):
    return jnp.dot(a, b, preferred_element_type=F32, precision=HI)


def _layer_norm(y, g, b):
    mu = jnp.mean(y, axis=-1, keepdims=True)
    d = y - mu
    var = jnp.mean(d * d, axis=-1, keepdims=True)
    return d * lax.rsqrt(var + NORM_EPS) * g + b


def _sigmoid(x):
    return 1.0 / (1.0 + jnp.exp(-x))


def _silu(x):
    return x * _sigmoid(x)


def _gelu_tanh(x):
    c = math.sqrt(2.0 / math.pi)
    return 0.5 * x * (1.0 + jnp.tanh(c * (x + 0.044715 * (x * x * x))))


def _matmul_kernel(x_ref, w_ref, o_ref, *, precise):
    if precise:
        o_ref[...] = _dot_hi(x_ref[...], w_ref[...])
    else:
        o_ref[...] = _dot(x_ref[...].astype(BF16), w_ref[...])


def _matmul(x, w, tm, precise=False):
    t, k = x.shape
    tm = min(tm, t)
    n = w.shape[1]
    return pl.pallas_call(
        functools.partial(_matmul_kernel, precise=precise),
        grid=(t // tm,),
        in_specs=[pl.BlockSpec((tm, k), lambda i: (i, 0)),
                  pl.BlockSpec((k, n), lambda i: (0, 0))],
        out_specs=pl.BlockSpec((tm, n), lambda i: (i, 0)),
        out_shape=jax.ShapeDtypeStruct((t, n), F32),
        compiler_params=_cparams("parallel"),
        name="in_proj",
    )(x, w)


def _memkv_kernel(x_ref, w_ref, ok_ref, ov_ref):
    kv = _dot_nt(w_ref[...], x_ref[0].astype(BF16))
    for l in range(DEPTH):
        ok_ref[l, 0] = kv[l * XA_WIDTH:(l + 1) * XA_WIDTH]
        ov_ref[l, 0] = kv[(DEPTH + l) * XA_WIDTH:(DEPTH + l + 1) * XA_WIDTH]


def _mem_kv(mem, wkt, wvt):
    n = mem.shape[0]
    w = jnp.concatenate([wkt, wvt], axis=0).reshape(2 * DEPTH * XA_WIDTH, D_MODEL)
    spec_o = pl.BlockSpec((DEPTH, 1, XA_WIDTH, N_MEM), lambda b: (0, b, 0, 0))
    return pl.pallas_call(
        _memkv_kernel,
        grid=(n,),
        in_specs=[pl.BlockSpec((1, N_MEM, D_MODEL), lambda b: (b, 0, 0)),
                  pl.BlockSpec(w.shape, lambda b: (0, 0))],
        out_specs=[spec_o, spec_o],
        out_shape=[jax.ShapeDtypeStruct((DEPTH, n, XA_WIDTH, N_MEM), F32)] * 2,
        compiler_params=_cparams("parallel"),
        name="mem_kv",
    )(mem, w)


def _attn_kernel(q_ref, kt_ref, vt_ref, o_ref, *, nseq, rl):
    head = lax.broadcasted_iota(jnp.int32, (rl, XA_WIDTH), 1) // XA_HEAD_DIM
    outs = []
    for s in range(nseq):
        q = q_ref[s * rl:(s + 1) * rl, :] * (XA_HEAD_DIM ** -0.5)
        qs = jnp.concatenate([jnp.where(head == h, q, 0.0) for h in range(XA_HEADS)], axis=0)
        sc = _dot(qs.astype(BF16), kt_ref[0, s].astype(BF16))
        sc = sc - jnp.max(sc, axis=-1, keepdims=True)
        p = jnp.exp(sc)
        p = p / jnp.sum(p, axis=-1, keepdims=True)
        pv = _dot_nt(p.astype(BF16), vt_ref[0, s].astype(BF16))
        o = jnp.where(head == 0, pv[0:rl], 0.0)
        for h in range(1, XA_HEADS):
            o = o + jnp.where(head == h, pv[h * rl:(h + 1) * rl], 0.0)
        outs.append(o)
    o_ref[...] = jnp.concatenate(outs, axis=0).astype(o_ref.dtype)


def _mem_attention(z, xa_col, mkt, mvt, layer, seq_len, rl, nseq):
    t = z.shape[0]
    rows = rl * nseq
    cb = xa_col // XA_WIDTH
    if nseq == 1:
        per_seq = seq_len // rl
        kv_map = lambda i: (layer, i // per_seq, 0, 0)
    else:
        kv_map = lambda i: (layer, i, 0, 0)
    return pl.pallas_call(
        functools.partial(_attn_kernel, nseq=nseq, rl=rl),
        grid=(t // rows,),
        in_specs=[pl.BlockSpec((rows, XA_WIDTH), lambda i: (i, cb)),
                  pl.BlockSpec((1, nseq, XA_WIDTH, N_MEM), kv_map),
                  pl.BlockSpec((1, nseq, XA_WIDTH, N_MEM), kv_map)],
        out_specs=pl.BlockSpec((rows, XA_WIDTH), lambda i: (i, 0)),
        out_shape=jax.ShapeDtypeStruct((t, XA_WIDTH), BF16),
        compiler_params=_cparams("parallel"),
        name="mem_attention",
    )(z, mkt, mvt)


def _router_gates(lg):
    tm = lg.shape[0]
    lgt = jnp.transpose(lg)
    ninf = jnp.float32(-jnp.inf)
    grow = lax.broadcasted_iota(jnp.int32, (8, tm), 0).astype(F32)
    gm = grow < MOE_GROUPS
    gl = jnp.where(gm, lgt[MOE_EXPERTS:MOE_EXPERTS + 8], ninf)
    ge = jnp.exp(gl - jnp.max(gl, axis=0, keepdims=True))
    gp = ge / jnp.sum(ge, axis=0, keepdims=True)
    pmax = jnp.max(gp, axis=0, keepdims=True)
    gi = jnp.min(jnp.where((gp == pmax) & gm, grow, 1e4), axis=0, keepdims=True)
    erow = lax.broadcasted_iota(jnp.int32, (MOE_EXPERTS, tm), 0).astype(F32)
    lge = lgt[0:MOE_EXPERTS]
    lo = gi * MOE_PER_GROUP
    em = (erow >= lo) & (erow < lo + MOE_PER_GROUP)
    el = jnp.where(em, lge, ninf)
    v1 = jnp.max(el, axis=0, keepdims=True)
    i1 = jnp.min(jnp.where((el == v1) & em, erow, 1e4), axis=0, keepdims=True)
    em2 = em & (erow != i1)
    el2 = jnp.where(em2, lge, ninf)
    v2 = jnp.max(el2, axis=0, keepdims=True)
    i2 = jnp.min(jnp.where((el2 == v2) & em2, erow, 1e4), axis=0, keepdims=True)
    t = jnp.exp(v2 - v1)
    w1 = 1.0 / (1.0 + t)
    w2 = t / (1.0 + t)
    gates = jnp.where(erow == i1, w1 * pmax, 0.0) + jnp.where(erow == i2, w2 * pmax, 0.0)
    onehot = jnp.where(grow == gi, 1.0, 0.0)
    rows = jnp.concatenate([gates, onehot, jnp.zeros((128 - MOE_EXPERTS - 8, tm), F32)], axis=0)
    return jnp.transpose(rows)


MOE_CAP = 160
MOE_OV = 128
TAIL_PIPELINE_MIN_TILES = 8


def _tail_kernel(*refs, glu, pipelined):
    (mix_ref, att_ref, xin_ref, wm_ref, wa_ref, g1_ref, b1_ref, wr_ref, br_ref, ltri_ref,
     wg_ref, wu_ref, wd_ref, g_ref, b_ref) = refs[:15]
    if glu:
        wglu_ref, bglu_ref = refs[15:17]
    nscr = 11 if pipelined else 6
    o_ref, acc_ref, xb_nx, gs_nx, x_nx, rkc_nx, rkr_nx = refs[-nscr - 1:][:7]
    if pipelined:
        xb_ref, gs_ref, x_ref, rkc_ref, rkr_ref = refs[-5:]
    else:
        xb_ref, gs_ref, x_ref, rkc_ref, rkr_ref = xb_nx, gs_nx, x_nx, rkc_nx, rkr_nx
    tm = x_ref.shape[0]
    lane = lax.broadcasted_iota(jnp.int32, (tm, 128), 1)

    if pipelined:
        @pl.when(pl.program_id(0) == 0)
        def _():
            xb_ref[...] = jnp.zeros_like(xb_ref)
            gs_ref[...] = jnp.zeros_like(gs_ref)
            x_ref[...] = jnp.zeros_like(x_ref)
            rkc_ref[...] = jnp.full(rkc_ref.shape, -1.0, F32)
            rkr_ref[...] = jnp.full(rkr_ref.shape, -1.0, F32)

    front = {}

    def front_project():
        if glu:
            y = _gelu_tanh(mix_ref[...])
            mix = (y * _sigmoid(_dot(y.astype(BF16), wglu_ref[...]) + bglu_ref[...])).astype(BF16)
        else:
            mix = mix_ref[...]
        front["h"] = _dot(mix, wm_ref[...]) + _dot(att_ref[...], wa_ref[...])

    def front_norm():
        xn = _layer_norm(DN_ALPHA * xin_ref[...] + front["h"], g1_ref[...], b1_ref[...])
        x_nx[...] = xn
        xh = xn.astype(BF16)
        xb_nx[...] = xh
        front["xh"], front["xl"] = xh, (xn - xh.astype(F32)).astype(BF16)

    def front_route():
        lg = _dot(front["xh"], wr_ref[...])
        lg = lg[:, 0:128] + lg[:, 128:256] + _dot(front["xl"], wr_ref[:, 0:128]) + br_ref[...]
        front["routed"] = _router_gates(lg)

    def front_rank():
        routed = front["routed"]
        gates = jnp.where(lane < MOE_EXPERTS, routed, 0.0)
        in_grp = (lane >= MOE_EXPERTS) & (lane < MOE_EXPERTS + MOE_GROUPS) & (routed > 0.0)
        g_hi = gates.astype(BF16).astype(F32)
        g_mid = (gates - g_hi).astype(BF16).astype(F32)
        g_lo = gates - g_hi - g_mid
        gs_nx[...] = (g_hi + pltpu.roll(g_mid, MOE_EXPERTS, axis=1)
                      + pltpu.roll(g_lo, 2 * MOE_EXPERTS, axis=1)).astype(BF16)
        rkn = _dot(ltri_ref[...], routed.astype(BF16))
        rkn = jnp.where(in_grp, rkn, -1.0)
        rkc_nx[...] = rkn
        rkr_nx[...] = jnp.transpose(rkn)

    front_stages = [front_project, front_norm, front_route, front_rank]
    if not pipelined:
        for stage in front_stages:
            stage()

    rkc = rkc_ref[...]
    rkr = rkr_ref[...]
    cnt = jnp.sum(jnp.where(rkc >= 0.0, 1.0, 0.0), axis=0, keepdims=True)
    rk_cols = [jnp.sum(jnp.where(lane == MOE_EXPERTS + g, rkc, 0.0), axis=-1, keepdims=True)
               for g in range(MOE_GROUPS)]
    rk_rows = [rkr[MOE_EXPERTS + g:MOE_EXPERTS + g + 1, :] for g in range(MOE_GROUPS)]

    def gather_mat(g, base, rows):
        r_io = lax.broadcasted_iota(jnp.int32, (rows, tm), 0).astype(F32) + base
        return jnp.where(rk_rows[g] == r_io, 1.0, 0.0).astype(BF16)

    def scatter_mat(g, base, kpad):
        c_io = lax.broadcasted_iota(jnp.int32, (tm, kpad), 1).astype(F32) + base
        return jnp.where(rk_cols[g] == c_io, 1.0, 0.0).astype(BF16)

    def experts(g, xg, gg, kpad):
        rows = xg.shape[0]
        glane = lax.broadcasted_iota(jnp.int32, (rows, 128), 1)
        hs = []
        for j in range(MOE_PER_GROUP):
            e = g * MOE_PER_GROUP + j
            ge = jnp.sum(jnp.where((glane % MOE_EXPERTS == e) & (glane < 3 * MOE_EXPERTS), gg, 0.0),
                         axis=-1, keepdims=True)
            hs.append((_silu(_dot(xg, wg_ref[0, e])) * _dot(xg, wu_ref[0, e]) * ge).astype(BF16))
        wd = wd_ref[0, g * MOE_PER_GROUP:(g + 1) * MOE_PER_GROUP].reshape(MOE_PER_GROUP * MOE_HIDDEN, D_MODEL)
        out = _dot(jnp.concatenate(hs, axis=-1), wd).astype(BF16)
        if kpad > rows:
            out = jnp.concatenate([out, jnp.zeros((kpad - rows, D_MODEL), BF16)], axis=0)
        return out

    in_cap = lambda rk: (rk >= 0.0) & (rk < MOE_CAP)
    slot_col = sum(jnp.where(in_cap(rk_cols[g]), rk_cols[g] + (g * MOE_CAP + 1.0), 0.0) for g in range(MOE_GROUPS)) - 1.0
    slot_row = sum(jnp.where(in_cap(rk_rows[g]), rk_rows[g] + (g * MOE_CAP + 1.0), 0.0) for g in range(MOE_GROUPS)) - 1.0
    slots = MOE_GROUPS * MOE_CAP
    r_io = lax.broadcasted_iota(jnp.int32, (slots, tm), 0).astype(F32)
    p_all = jnp.where(slot_row == r_io, 1.0, 0.0).astype(BF16)
    xg_all = _dot(p_all, xb_ref[...]).astype(BF16)
    gg_all = _dot(p_all, gs_ref[...])
    outs = []
    for g in range(MOE_GROUPS):
        if pipelined:
            front_stages[g]()
        outs.append(experts(g, xg_all[g * MOE_CAP:(g + 1) * MOE_CAP], gg_all[g * MOE_CAP:(g + 1) * MOE_CAP], MOE_CAP))
    c_io = lax.broadcasted_iota(jnp.int32, (tm, slots), 1).astype(F32)
    pt_all = jnp.where(slot_col == c_io, 1.0, 0.0).astype(BF16)
    acc_ref[...] = _dot(pt_all, jnp.concatenate(outs, axis=0))

    for g in range(MOE_GROUPS):
        n_ov = jnp.maximum(cnt[0, MOE_EXPERTS + g].astype(jnp.int32) - MOE_CAP + MOE_OV - 1, 0) // MOE_OV

        def overflow(i, _, g=g):
            base = (MOE_CAP + i * MOE_OV).astype(F32)
            p = gather_mat(g, base, MOE_OV)
            out = experts(g, _dot(p, xb_ref[...]).astype(BF16), _dot(p, gs_ref[...]), MOE_OV)
            acc_ref[...] += _dot(scatter_mat(g, base, MOE_OV), out)
            return 0

        lax.fori_loop(0, n_ov, overflow, 0)

    if pipelined:
        xb_ref[...] = xb_nx[...]
        gs_ref[...] = gs_nx[...]
        rkc_ref[...] = rkc_nx[...]
        rkr_ref[...] = rkr_nx[...]
    o_ref[...] = _layer_norm(DN_ALPHA * x_ref[...] + acc_ref[...], g_ref[...], b_ref[...])
    if pipelined:
        x_ref[...] = x_nx[...]


def _layer_tail(mix, att, x, wm, wa, ln1, wr, br, wg, wu, wd, layer, ln2, glu_w=None, glu_b=None, tm=512):
    t = x.shape[0]
    tm = min(tm, t)
    nt = t // tm
    idx = jnp.arange(tm)
    ltri = (idx[:, None] > idx[None, :]).astype(BF16)
    pipelined = nt >= TAIL_PIPELINE_MIN_TILES
    if pipelined:
        row_in = lambda w: pl.BlockSpec((tm, w), lambda s: (jnp.minimum(s, nt - 1), 0))
        row_out = pl.BlockSpec((tm, D_MODEL), lambda s: (jnp.maximum(s - 1, 0), 0))
    else:
        row_in = lambda w: pl.BlockSpec((tm, w), lambda s: (s, 0))
        row_out = pl.BlockSpec((tm, D_MODEL), lambda s: (s, 0))
    full = lambda a: pl.BlockSpec(a.shape, lambda s: (0,) * a.ndim, pipeline_mode=pl.Buffered(1))
    layer_w = lambda a: pl.BlockSpec((1,) + a.shape[1:], lambda s: (layer, 0, 0, 0),
                                     pipeline_mode=pl.Buffered(1))
    args = [mix, att, x, wm, wa, ln1[0], ln1[1], wr, br, ltri, wg, wu, wd, ln2[0], ln2[1]]
    specs = [row_in(mix.shape[1]), row_in(XA_WIDTH), row_in(D_MODEL), full(wm), full(wa), full(ln1[0]),
             full(ln1[1]), full(wr), full(br), full(ltri), layer_w(wg), layer_w(wu), layer_w(wd),
             full(ln2[0]), full(ln2[1])]
    if glu_w is not None:
        args += [glu_w, glu_b]
        specs += [full(glu_w), full(glu_b)]
    parked = [pltpu.VMEM((tm, D_MODEL), BF16), pltpu.VMEM((tm, 128), BF16), pltpu.VMEM((tm, D_MODEL), F32),
              pltpu.VMEM((tm, 128), F32), pltpu.VMEM((128, tm), F32)]
    return pl.pallas_call(
        functools.partial(_tail_kernel, glu=glu_w is not None, pipelined=pipelined),
        grid=(nt + 1 if pipelined else nt,),
        in_specs=specs,
        out_specs=row_out,
        out_shape=jax.ShapeDtypeStruct((t, D_MODEL), F32),
        scratch_shapes=[pltpu.VMEM((tm, D_MODEL), F32)] + parked + (parked if pipelined else []),
        compiler_params=pltpu.CompilerParams(dimension_semantics=("arbitrary",),
                                             vmem_limit_bytes=TAIL_VMEM_LIMIT),
        name="layer_tail",
    )(*args)


def _s5_kernel(*refs, tlen, ngrp, chained, precise):
    if chained:
        u_ref, wb_ref, wc_ref, lam_ref, d_ref, y_ref, ore_ref, oim_ref, bu_ref, h_ref, cr_ref, ci_ref = refs
        u = jnp.swapaxes(u_ref[...], 0, 1).reshape(8 * tlen, 128)
    else:
        (u_ref, h0r_ref, h0i_ref, wb_ref, wc_ref, lam_ref, d_ref,
         y_ref, ore_ref, oim_ref, bu_ref, h_ref, cr_ref, ci_ref) = refs
        gr = 8 * tlen
        u = jnp.concatenate(
            [jnp.swapaxes(u_ref[g * gr:(g + 1) * gr, :].reshape(8, tlen, 128), 0, 1).reshape(gr, 128)
             for g in range(ngrp)], axis=0)
    if precise:
        bu_ref[...] = _dot_hi(u, wb_ref[0])
    else:
        bu_ref[...] = _dot(u.astype(BF16), wb_ref[0].astype(BF16))
    ns = S5_SLAB_STATE
    lr = jnp.broadcast_to(lam_ref[0, 0:1, :], (8, ns))
    li = jnp.broadcast_to(lam_ref[0, 1:2, :], (8, ns))

    def step(tile, hr, hi):
        rows = pl.ds(pl.multiple_of(tile * 8, 8), 8)
        nr = lr * hr - li * hi + bu_ref[rows, 0:ns]
        ni = lr * hi + li * hr + bu_ref[rows, ns:2 * ns]
        h_ref[rows, 0:ns] = nr
        h_ref[rows, ns:2 * ns] = ni
        return nr, ni

    if chained:
        @pl.when(pl.program_id(2) == 0)
        def _():
            cr_ref[...] = jnp.zeros_like(cr_ref)
            ci_ref[...] = jnp.zeros_like(ci_ref)

        hr, hi = lax.fori_loop(0, tlen, lambda t, c: step(t, *c), (cr_ref[...], ci_ref[...]), unroll=8)
        cr_ref[...] = hr
        ci_ref[...] = hi
        ore_ref[...] = hr
        oim_ref[...] = hi
    else:
        cr_ref[...] = jnp.transpose(h0r_ref[0])
        ci_ref[...] = jnp.transpose(h0i_ref[0])

        def group(g, _):
            s0 = pl.multiple_of(g * 8, 8)
            hr, hi = cr_ref[pl.ds(s0, 8), :], ci_ref[pl.ds(s0, 8), :]
            for t in range(tlen):
                hr, hi = step(g * tlen + t, hr, hi)
            cr_ref[pl.ds(s0, 8), :] = hr
            ci_ref[pl.ds(s0, 8), :] = hi
            return 0

        lax.fori_loop(0, ngrp, group, 0)
        ore_ref[...] = jnp.transpose(cr_ref[...])
        oim_ref[...] = jnp.transpose(ci_ref[...])
    y = _dot(h_ref[...].astype(BF16), wc_ref[0].astype(BF16)) + d_ref[0] * u
    if chained:
        y_ref[...] = jnp.swapaxes(y.reshape(tlen, 8, 128), 0, 1)
    else:
        gr = 8 * tlen
        for g in range(ngrp):
            y_ref[g * gr:(g + 1) * gr, :] = jnp.swapaxes(
                y[g * gr:(g + 1) * gr, :].reshape(tlen, 8, 128), 0, 1).reshape(gr, 128)


def _s5_mixer_prompt(z, params, seq_len, tlen=256):
    t = z.shape[0]
    nseq = t // seq_len
    ns = S5_SLAB_STATE
    z3 = z.reshape(nseq, seq_len, z.shape[1])
    slab = lambda a: pl.BlockSpec((1,) + a.shape[1:], lambda j, b, i: (j,) + (0,) * (a.ndim - 1))
    u_spec = pl.BlockSpec((8, tlen, 128), lambda j, b, i: (b, i, j))
    st_spec = pl.BlockSpec((8, ns), lambda j, b, i: (b, j))
    st_shape = jax.ShapeDtypeStruct((nseq, S5_SLABS * ns), F32)
    rows = 8 * tlen
    y, hre, him = pl.pallas_call(
        functools.partial(_s5_kernel, tlen=tlen, ngrp=1, chained=True, precise=False),
        grid=(S5_SLABS, nseq // 8, seq_len // tlen),
        in_specs=[u_spec] + [slab(a) for a in params],
        out_specs=[u_spec, st_spec, st_spec],
        out_shape=[jax.ShapeDtypeStruct((nseq, seq_len, MIX_WIDTH), F32), st_shape, st_shape],
        scratch_shapes=[pltpu.VMEM((rows, 2 * ns), F32), pltpu.VMEM((rows, 2 * ns), F32),
                        pltpu.VMEM((8, ns), F32), pltpu.VMEM((8, ns), F32)],
        compiler_params=_cparams("parallel", "parallel", "arbitrary"),
        name="s5_mixer",
    )(z3, *params)
    return y.reshape(t, MIX_WIDTH), hre, him


def _s5_mixer_sample(z, params, seq_len, h0r, h0i, layer):
    t = z.shape[0]
    nseq = t // seq_len
    ns = S5_SLAB_STATE
    seqs = 128
    rows = seqs * seq_len
    slab = lambda a: pl.BlockSpec((1,) + a.shape[1:], lambda j, i: (j,) + (0,) * (a.ndim - 1))
    u_spec = pl.BlockSpec((rows, 128), lambda j, i: (i, j))
    h0_spec = pl.BlockSpec((1, ns, seqs), lambda j, i: (layer, j, i))
    st_spec = pl.BlockSpec((ns, seqs), lambda j, i: (j, i))
    st_shape = jax.ShapeDtypeStruct((S5_SLABS * ns, nseq), F32)
    return pl.pallas_call(
        functools.partial(_s5_kernel, tlen=seq_len, ngrp=seqs // 8, chained=False, precise=True),
        grid=(S5_SLABS, nseq // seqs),
        in_specs=[u_spec, h0_spec, h0_spec] + [slab(a) for a in params],
        out_specs=[u_spec, st_spec, st_spec],
        out_shape=[jax.ShapeDtypeStruct((t, MIX_WIDTH), F32), st_shape, st_shape],
        scratch_shapes=[pltpu.VMEM((rows, 2 * ns), F32), pltpu.VMEM((rows, 2 * ns), F32),
                        pltpu.VMEM((seqs, ns), F32), pltpu.VMEM((seqs, ns), F32)],
        compiler_params=_cparams("parallel", "parallel"),
        name="s5_mixer",
    )(z, h0r, h0i, *params)


def _s5_params(lam_re, lam_im, log_dt, b_re, b_im, c_re, c_im, d_skip):
    lre, lim = lam_re.astype(F32), lam_im.astype(F32)
    dt = jnp.exp(log_dt.astype(F32))[:, None]
    mag = jnp.exp(lre * dt)
    bar_re, bar_im = mag * jnp.cos(lim * dt), mag * jnp.sin(lim * dt)
    nre, nim = bar_re - 1.0, bar_im
    den = lre * lre + lim * lim
    fre, fim = (nre * lre + nim * lim) / den, (nim * lre - nre * lim) / den
    bre, bim = b_re.astype(F32), b_im.astype(F32)
    bb_re = fre[..., None] * bre - fim[..., None] * bim
    bb_im = fre[..., None] * bim + fim[..., None] * bre
    eye = jnp.eye(8, dtype=F32)
    gps = 8

    def blockdiag_in(m):
        m = m.reshape(S5_SLABS, gps, S5_STATE, S5_GROUP).transpose(0, 1, 3, 2)
        return jnp.einsum('jgcp,gh->jgchp', m, eye).reshape(S5_SLABS, 128, S5_SLAB_STATE)

    def blockdiag_out(m):
        m = m.reshape(S5_SLABS, gps, S5_GROUP, S5_STATE).transpose(0, 1, 3, 2)
        return jnp.einsum('jgpc,gh->jgphc', m, eye).reshape(S5_SLABS, S5_SLAB_STATE, 128)

    wb = jnp.concatenate([blockdiag_in(bb_re), blockdiag_in(bb_im)], axis=2)
    wc = jnp.concatenate([blockdiag_out(c_re.astype(F32)), blockdiag_out(-c_im.astype(F32))], axis=1)
    lam = jnp.stack([bar_re.reshape(S5_SLABS, S5_SLAB_STATE), bar_im.reshape(S5_SLABS, S5_SLAB_STATE)], axis=1)
    dsk = d_skip.astype(F32).reshape(S5_SLABS, 1, 128)
    return wb, wc, lam, dsk


def _head_norm_gate(o, gn, gate):
    col = lax.broadcasted_iota(jnp.int32, o.shape, 1)
    valid = col < RET_DV
    mu = jnp.sum(o, axis=-1, keepdims=True) * (1.0 / RET_DV)
    d = jnp.where(valid, o - mu, 0.0)
    var = jnp.sum(d * d, axis=-1, keepdims=True) * (1.0 / RET_DV)
    return d * lax.rsqrt(var + NORM_EPS) * gn * _silu(gate)


def _load_states(s_in_ref, spad_ref, nseg, dk):
    spad_ref[...] = jnp.zeros_like(spad_ref)
    for s in range(nseg):
        for h in range(HEADS):
            spad_ref[h, s * DK_PAD:s * DK_PAD + dk, 0:RET_DV] = s_in_ref[s, h]


def _ret_kernel(*refs, nseg, gammas):
    if nseg == 1:
        (q_ref, k_ref, v_ref, g_ref, cos_ref, sin_ref, dmat_ref, inner_ref, zeta_ref, gn_ref,
         o_ref, so_ref, st_ref) = refs

        @pl.when(pl.program_id(1) == 0)
        def _():
            st_ref[...] = jnp.zeros_like(st_ref)
    else:
        (q_ref, k_ref, v_ref, g_ref, cos_ref, sin_ref, dmat_ref, inner_ref, zeta_ref, gn_ref, s_in_ref,
         o_ref, so_ref, st_ref) = refs
        st_ref[:, RET_DV:DV_PAD, :] = jnp.zeros((HEADS, DV_PAD - RET_DV, nseg * RET_DK), F32)
        for s in range(nseg):
            for h in range(HEADS):
                st_ref[h, 0:RET_DV, s * RET_DK:(s + 1) * RET_DK] = s_in_ref[s, h]
    chained = nseg == 1
    nb = q_ref.shape[0] if chained else 1
    rd = (lambda ref, sq, cs: ref[sq, :, cs]) if chained else (lambda ref, sq, cs: ref[:, cs])
    r = q_ref.shape[-2]
    seg_len = r // nseg
    cos = cos_ref[...]
    sin = sin_ref[...]
    rot = lambda x: x * cos + pltpu.roll(x, RET_DK // 2, axis=1) * sin
    rowseg = lax.broadcasted_iota(jnp.int32, (r, DK_PAD), 0) // seg_len
    colseg = lax.broadcasted_iota(jnp.int32, (DV_PAD, r), 1) // seg_len

    def head(sq, h):
        sl = slice(h * DK_PAD, (h + 1) * DK_PAD)
        sv = slice(h * DV_PAD, (h + 1) * DV_PAD)
        q = rot(rd(q_ref, sq, sl))
        k = rot(rd(k_ref, sq, sl)) * (RET_DK ** -0.5)
        v = rd(v_ref, sq, sv)
        vb = v.astype(BF16)
        qb = q.astype(BF16)
        att = _dot_nt(qb, k.astype(BF16)) * dmat_ref[h]
        o = _dot(att.astype(BF16), vb)
        st_all = st_ref[sq, h] if chained else st_ref[h]
        if chained:
            qbd = qb
        else:
            qbd = jnp.concatenate([jnp.where(rowseg == s, q, 0.0) for s in range(nseg)], axis=1).astype(BF16)
        o = o + _dot_nt(qbd, st_all.astype(BF16)) * inner_ref[h]
        kzb = (k * zeta_ref[h]).astype(BF16)
        vt = jnp.transpose(v)
        for s in range(nseg):
            vts = vt if chained else jnp.where(colseg == s, vt, 0.0)
            st_new = st_all[:, s * RET_DK:(s + 1) * RET_DK] * gammas[h] + _dot(vts.astype(BF16), kzb)
            if chained:
                st_ref[sq, h] = st_new
                so_ref[sq, h] = st_new[0:RET_DV, :]
            else:
                so_ref[s, h] = st_new[0:RET_DV, :]
        out = _head_norm_gate(o, gn_ref[:, sv], rd(g_ref, sq, sv)).astype(o_ref.dtype)
        if chained:
            o_ref[sq, :, sv] = out
        else:
            o_ref[:, sv] = out

    for h in range(HEADS):
        for sq in range(nb):
            head(sq, h)


def _ret_consts(seg_len, nseg, pos):
    hh = jnp.arange(RET_HEADS, dtype=F32)
    log_g = jnp.log1p(-jnp.exp2(-5.0 - hh))
    idx = jnp.arange(seg_len, dtype=F32)
    rel = idx[:, None] - idx[None, :]
    dmat = jnp.where(rel >= 0, jnp.exp(log_g[:, None, None] * jnp.maximum(rel, 0.0)), 0.0)
    inner = jnp.exp(log_g[:, None] * (idx + 1.0))
    zeta = jnp.exp(log_g[:, None] * (seg_len - 1.0 - idx))
    r = seg_len * nseg
    seg = jnp.arange(r) // seg_len
    same = (seg[:, None] == seg[None, :]).astype(F32)
    dmat = jnp.tile(dmat, (1, nseg, nseg)) * same[None]
    inner = jnp.broadcast_to(jnp.tile(inner, (1, nseg))[:, :, None], (RET_HEADS, r, DV_PAD))
    zeta = jnp.broadcast_to(jnp.tile(zeta, (1, nseg))[:, :, None], (RET_HEADS, r, DK_PAD))
    half = RET_DK // 2
    inv_freq = ROPE_BASE ** (-jnp.arange(half, dtype=F32) / half)
    ang = pos.astype(F32)[:, None] * inv_freq
    cos2 = jnp.concatenate([jnp.cos(ang), jnp.cos(ang)], axis=-1)
    sin2 = jnp.concatenate([-jnp.sin(ang), jnp.sin(ang)], axis=-1)
    return dmat, inner, zeta, cos2, sin2


def _ret_gammas(seg_len):
    return tuple(float((1.0 - 2.0 ** (-5.0 - h)) ** seg_len) for h in range(RET_HEADS))


def _ret_mixer(z, gn, seq_len, pos, s0=None):
    t = z.shape[0]
    nseq = t // seq_len
    r = 128
    if s0 is None:
        nb = max(n for n in (1, 2, 4) if nseq % n == 0)
        nseg, seg_len = 1, RET_CHUNK
        grid = (nseq // nb, seq_len // r)
        z = z.reshape(nseq, seq_len, z.shape[1])
        blk = lambda w, c: pl.BlockSpec((nb, r, w), lambda n, i: (n, i, c))
        mix_shape = (nseq, seq_len, HEADS * DV_PAD)
        tabmap = lambda n, i: (i, 0)
        stmap = lambda n, i: (n, 0, 0, 0)
        st_rows = nb
        scratch = pltpu.VMEM((nb, HEADS, DV_PAD, RET_DK), F32)
        sem = ("parallel", "arbitrary")
    else:
        seg_len = seq_len
        nseg = r // seg_len
        grid = (t // r,)
        blk = lambda w, c: pl.BlockSpec((r, w), lambda i: (i, c))
        mix_shape = (t, HEADS * DV_PAD)
        tabmap = lambda i: (0, 0)
        stmap = lambda i: (i, 0, 0, 0)
        st_rows = nseg
        scratch = pltpu.VMEM((HEADS, DV_PAD, nseg * RET_DK), F32)
        sem = ("arbitrary",)
    dmat, inner, zeta, cos2, sin2 = _ret_consts(seg_len, nseg, pos)
    if s0 is not None:
        cos2 = jnp.tile(cos2, (nseg, 1))
        sin2 = jnp.tile(sin2, (nseg, 1))
    full = lambda a: pl.BlockSpec(a.shape, lambda *_: (0,) * a.ndim)
    st_spec = pl.BlockSpec((st_rows, RET_HEADS, RET_DV, RET_DK), stmap)
    in_specs = [blk(512, 0), blk(512, 1), blk(1024, 1), blk(1024, 2),
                pl.BlockSpec((r, DK_PAD), tabmap), pl.BlockSpec((r, DK_PAD), tabmap),
                full(dmat), full(inner), full(zeta), full(gn)]
    args = [z, z, z, z, cos2, sin2, dmat, inner, zeta, gn]
    if s0 is not None:
        in_specs.append(st_spec)
        args.append(s0)
    mix, s_new = pl.pallas_call(
        functools.partial(_ret_kernel, nseg=nseg, gammas=_ret_gammas(seg_len)),
        grid=grid,
        in_specs=in_specs,
        out_specs=[blk(HEADS * DV_PAD, 0), st_spec],
        out_shape=[jax.ShapeDtypeStruct(mix_shape, BF16),
                   jax.ShapeDtypeStruct((nseq, RET_HEADS, RET_DV, RET_DK), F32)],
        scratch_shapes=[scratch],
        compiler_params=_cparams(*sem),
        name="retention",
    )(*args)
    return mix.reshape(t, HEADS * DV_PAD), s_new


def _log_sigmoid(x):
    return jnp.minimum(x, 0.0) - jnp.log1p(jnp.exp(-jnp.abs(x)))


def _gla_kernel(*refs, nseg, chained):
    if chained:
        (q_ref, k_ref, v_ref, r_ref, low_ref, tril_ref, same_ref, wg_ref, bg_ref, gn_ref,
         o_ref, so_ref, spad_ref) = refs

        @pl.when(pl.program_id(1) == 0)
        def _():
            spad_ref[...] = jnp.zeros_like(spad_ref)
    else:
        (q_ref, k_ref, v_ref, r_ref, low_ref, tril_ref, same_ref, wg_ref, bg_ref, gn_ref, s_in_ref,
         o_ref, so_ref, spad_ref) = refs
        _load_states(s_in_ref, spad_ref, nseg, GLA_DK)
    nb = q_ref.shape[0] if chained else 1
    rd = (lambda ref, sq, cs: ref[sq, :, cs]) if chained else (lambda ref, sq, cs: ref[:, cs])
    r = q_ref.shape[-2]
    seg_len = r // nseg
    tril = tril_ref[...]
    causal = tril > 0.0
    every = slice(None)
    la = [_log_sigmoid(_dot_hi(rd(low_ref, sq, every), wg_ref[...]) + bg_ref[...]) * (1.0 / GLA_TAU)
          for sq in range(nb)]
    b = [_dot_hi(tril, la[sq]) for sq in range(nb)]
    b_tot = [_dot_hi(same_ref[...], la[sq]) for sq in range(nb)]
    rowseg = lax.broadcasted_iota(jnp.int32, (r, DK_PAD), 0) // seg_len
    colseg = lax.broadcasted_iota(jnp.int32, (DK_PAD, r), 1) // seg_len

    def head(sq, h):
        sl = slice(h * DK_PAD, (h + 1) * DK_PAD)
        sv = slice(h * DV_PAD, (h + 1) * DV_PAD)
        bh = b[sq][:, sl]
        q = rd(q_ref, sq, sl) * (GLA_DK ** -0.5)
        k = rd(k_ref, sq, sl)
        vb = rd(v_ref, sq, sv).astype(BF16)
        q_t = q * jnp.exp(bh)
        k_t = k * jnp.exp(-bh)
        att = jnp.where(causal, _dot_nt(q_t.astype(BF16), k_t.astype(BF16)), 0.0)
        o = _dot(att.astype(BF16), vb)
        kdt = jnp.transpose(k * jnp.exp(b_tot[sq][:, sl] - bh))
        lat = jnp.transpose(la[sq][:, sl])

        def seg_update(s, s_old):
            kds = jnp.where(colseg == s, kdt, 0.0) if nseg > 1 else kdt
            las = jnp.where(colseg == s, lat, 0.0) if nseg > 1 else lat
            decay = jnp.exp(jnp.sum(las, axis=-1, keepdims=True))
            return s_old * decay + _dot(kds.astype(BF16), vb)

        if chained:
            qb = q_t.astype(BF16)
            s_cur = spad_ref[sq, h]
            inter = []
            for s in range(nseg):
                inter.append(_dot(qb[s * seg_len:(s + 1) * seg_len], s_cur.astype(BF16)))
                s_cur = seg_update(s, s_cur)
            o = o + jnp.concatenate(inter, axis=0)
            spad_ref[sq, h] = s_cur
            so_ref[sq, h] = s_cur[0:GLA_DK, 0:GLA_DV]
        else:
            s_all = spad_ref[h]
            qbd = jnp.concatenate([jnp.where(rowseg == s, q_t, 0.0) for s in range(nseg)], axis=1).astype(BF16)
            o = o + _dot(qbd, s_all.astype(BF16))
            for s in range(nseg):
                so_ref[s, h] = seg_update(s, s_all[s * DK_PAD:(s + 1) * DK_PAD])[0:GLA_DK, 0:GLA_DV]
        out = _head_norm_gate(o, gn_ref[:, sv], rd(r_ref, sq, sv)).astype(o_ref.dtype)
        if chained:
            o_ref[sq, :, sv] = out
        else:
            o_ref[:, sv] = out

    for h in range(HEADS):
        for sq in range(nb):
            head(sq, h)


def _gla_mixer(z, wg, bg, gn, seq_len, s0=None):
    t = z.shape[0]
    nseq = t // seq_len
    r = 128
    if s0 is None:
        nb = max(n for n in (1, 2, 4) if nseq % n == 0)
        seg_len = GLA_CHUNK
        nseg = r // seg_len
        grid = (nseq // nb, seq_len // r)
        z = z.reshape(nseq, seq_len, z.shape[1])
        blk = lambda w, c: pl.BlockSpec((nb, r, w), lambda n, i: (n, i, c))
        mix_shape = (nseq, seq_len, HEADS * DV_PAD)
        stmap = lambda n, i: (n, 0, 0, 0)
        st_rows = nb
        scratch = pltpu.VMEM((nb, HEADS, DK_PAD, DV_PAD), F32)
        sem = ("parallel", "arbitrary")
    else:
        seg_len = seq_len
        nseg = r // seg_len
        grid = (t // r,)
        blk = lambda w, c: pl.BlockSpec((r, w), lambda i: (i, c))
        mix_shape = (t, HEADS * DV_PAD)
        stmap = lambda i: (i, 0, 0, 0)
        st_rows = nseg
        scratch = pltpu.VMEM((HEADS, nseg * DK_PAD, DV_PAD), F32)
        sem = ("arbitrary",)
    seg = jnp.arange(r) // seg_len
    idx = jnp.arange(r)
    same = (seg[:, None] == seg[None, :]).astype(F32)
    tril = same * (idx[:, None] >= idx[None, :]).astype(F32)
    full = lambda a: pl.BlockSpec(a.shape, lambda *_: (0,) * a.ndim)
    st_spec = pl.BlockSpec((st_rows, GLA_HEADS, GLA_DK, GLA_DV), stmap)
    in_specs = [blk(512, 0), blk(512, 1), blk(1024, 1), blk(1024, 2), blk(128, COL_LOW // 128),
                full(tril), full(same), full(wg), full(bg), full(gn)]
    args = [z, z, z, z, z, tril, same, wg, bg, gn]
    if s0 is not None:
        in_specs.append(st_spec)
        args.append(s0)
    mix, s_new = pl.pallas_call(
        functools.partial(_gla_kernel, nseg=nseg, chained=s0 is None),
        grid=grid,
        in_specs=in_specs,
        out_specs=[blk(HEADS * DV_PAD, 0), st_spec],
        out_shape=[jax.ShapeDtypeStruct(mix_shape, BF16),
                   jax.ShapeDtypeStruct((nseq, GLA_HEADS, GLA_DK, GLA_DV), F32)],
        scratch_shapes=[scratch],
        compiler_params=_cparams(*sem),
        name="gla",
    )(*args)
    return mix.reshape(t, HEADS * DV_PAD), s_new


def _pad_heads(w, width, pad):
    lead = w.shape[:-1]
    w = w.reshape(lead + (HEADS, width))
    w = jnp.pad(w, [(0, 0)] * len(lead) + [(0, 0), (0, pad - width)])
    return w.reshape(lead + (HEADS * pad,))


def _pad_head_rows(w, width, pad):
    return jnp.swapaxes(_pad_heads(jnp.swapaxes(w, 0, 1), width, pad), 0, 1)


def _ret_in_weight(w):
    qk = RET_HEADS * RET_DK
    v = _pad_heads(w[:, 2 * qk:2 * qk + MIX_WIDTH], RET_DV, DV_PAD)
    g = _pad_heads(w[:, 2 * qk + MIX_WIDTH:2 * qk + 2 * MIX_WIDTH], RET_DV, DV_PAD)
    return jnp.concatenate([w[:, :2 * qk], v, g, w[:, -XA_WIDTH:]], axis=1).astype(BF16)


def _gla_in_weight(w):
    qk = GLA_HEADS * GLA_DK
    q = _pad_heads(w[:, :qk], GLA_DK, DK_PAD)
    k = _pad_heads(w[:, qk:2 * qk], GLA_DK, DK_PAD)
    v = _pad_heads(w[:, 2 * qk:2 * qk + MIX_WIDTH], GLA_DV, DV_PAD)
    r = _pad_heads(w[:, 2 * qk + MIX_WIDTH:2 * qk + 2 * MIX_WIDTH], GLA_DV, DV_PAD)
    low = jnp.pad(w[:, 2 * qk + 2 * MIX_WIDTH:2 * qk + 2 * MIX_WIDTH + GLA_RANK], [(0, 0), (0, 128 - GLA_RANK)])
    return jnp.concatenate([q, k, v, r, w[:, -XA_WIDTH:], low], axis=1).astype(BF16)


def kernel(x_prompt, x_sample, mem_prompt, state_s5_re, state_s5_im, state_ret, state_gla,
           cache_mem_k, cache_mem_v,
           w_in_s5, s5_lam_re, s5_lam_im, s5_log_dt, s5_b_re, s5_b_im, s5_c_re, s5_c_im,
           s5_d, s5_w_glu, s5_b_glu,
           w_in_ret, ret_gn,
           w_in_gla, gla_w_gate2, gla_b_gate2, gla_gn,
           w_mem_k, w_mem_v, w_out, ln_g, ln_b,
           moe_w_grp, moe_b_grp, moe_w_exp, moe_b_exp, moe_w_gate, moe_w_up, moe_w_down):
    bp, lp, _ = x_prompt.shape
    bs, ls, _ = x_sample.shape
    xp = x_prompt.reshape(bp * lp, D_MODEL)
    xs = x_sample.reshape(bs * ls, D_MODEL)
    pos_p = jnp.arange(lp, dtype=jnp.int32)
    pos_s = PAST_LEN + jnp.arange(ls, dtype=jnp.int32)

    to_t = lambda c: jnp.transpose(c, (0, 1, 3, 4, 2)).reshape(DEPTH, c.shape[1], XA_WIDTH, N_MEM)
    from_t = lambda c: jnp.transpose(c.reshape(DEPTH, c.shape[1], XA_HEADS, XA_HEAD_DIM, N_MEM), (0, 1, 4, 2, 3))
    mkt_all, mvt_all = _mem_kv(mem_prompt, jnp.swapaxes(w_mem_k, 1, 2).astype(BF16),
                               jnp.swapaxes(w_mem_v, 1, 2).astype(BF16))
    cache_kt = to_t(cache_mem_k)
    cache_vt = to_t(cache_mem_v)
    ret_t = jnp.swapaxes(state_ret, 3, 4)
    s5_t = lambda s: jnp.transpose(s, (0, 2, 3, 1)).reshape(s.shape[0], S5_GROUPS * S5_STATE, bs)
    s5_re_t, s5_im_t = s5_t(state_s5_re), s5_t(state_s5_im)
    moe_wg, moe_wu, moe_wd = moe_w_gate.astype(BF16), moe_w_up.astype(BF16), moe_w_down.astype(BF16)

    p_re, p_im, p_ret, p_gla = [], [], [], []
    s_re, s_im, s_ret, s_gla = [], [], [], []
    for i in range(DEPTH):
        kind, j = i % N_MIXERS, i // N_MIXERS
        ln1 = (ln_g[i, 0].reshape(1, D_MODEL), ln_b[i, 0].reshape(1, D_MODEL))
        ln2 = (ln_g[i, 1].reshape(1, D_MODEL), ln_b[i, 1].reshape(1, D_MODEL))
        wo = w_out[i]
        wo_att = wo[MIX_WIDTH:].astype(BF16)
        glu = {}
        if kind == 0:
            params = _s5_params(s5_lam_re[j], s5_lam_im[j], s5_log_dt[j], s5_b_re[j], s5_b_im[j],
                                s5_c_re[j], s5_c_im[j], s5_d[j])
            zp = _matmul(xp, w_in_s5[j].astype(BF16), 512)
            zs = _matmul(xs, w_in_s5[j], 512, precise=True)
            mix_p, a_re, a_im = _s5_mixer_prompt(zp, params, lp)
            mix_s, b_re, b_im = _s5_mixer_sample(zs, params, ls, s5_re_t, s5_im_t, j)
            p_re.append(a_re), p_im.append(a_im), s_re.append(b_re), s_im.append(b_im)
            xa_col = MIX_WIDTH
            wo_mix = wo[:MIX_WIDTH].astype(BF16)
            glu = dict(glu_w=s5_w_glu[j].astype(BF16), glu_b=s5_b_glu[j].reshape(1, MIX_WIDTH))
        elif kind == 1:
            w = _ret_in_weight(w_in_ret[j])
            gn = _pad_heads(ret_gn[j], RET_DV, DV_PAD).reshape(1, HEADS * DV_PAD)
            zp = _matmul(xp, w, 512)
            zs = _matmul(xs, w, 512)
            mix_p, a_s = _ret_mixer(zp, gn, lp, pos_p)
            mix_s, b_s = _ret_mixer(zs, gn, ls, pos_s, s0=ret_t[j])
            p_ret.append(a_s), s_ret.append(b_s)
            xa_col = COL_XA
            wo_mix = _pad_head_rows(wo[:MIX_WIDTH], RET_DV, DV_PAD).astype(BF16)
        else:
            w = _gla_in_weight(w_in_gla[j])
            gn = _pad_heads(gla_gn[j], GLA_DV, DV_PAD).reshape(1, HEADS * DV_PAD)
            wg = jnp.pad(_pad_heads(gla_w_gate2[j], GLA_DK, DK_PAD), [(0, 128 - GLA_RANK), (0, 0)])
            bg = _pad_heads(gla_b_gate2[j], GLA_DK, DK_PAD).reshape(1, HEADS * DK_PAD)
            zp = _matmul(xp, w, 512)
            zs = _matmul(xs, w, 512)
            mix_p, a_s = _gla_mixer(zp, wg, bg, gn, lp)
            mix_s, b_s = _gla_mixer(zs, wg, bg, gn, ls, s0=state_gla[j])
            p_gla.append(a_s), s_gla.append(b_s)
            xa_col = COL_XA
            wo_mix = _pad_head_rows(wo[:MIX_WIDTH], GLA_DV, DV_PAD).astype(BF16)
        att_p = _mem_attention(zp, xa_col, mkt_all, mvt_all, i, lp, rl=min(512, lp), nseq=1)
        att_s = _mem_attention(zs, xa_col, cache_kt, cache_vt, i, ls, rl=ls, nseq=128 // ls)
        wr = jnp.pad(jnp.concatenate([moe_w_exp[i], moe_w_grp[i]], axis=1),
                     [(0, 0), (0, 128 - MOE_EXPERTS - MOE_GROUPS)]).astype(F32)
        wr_hi = wr.astype(BF16)
        wr = jnp.concatenate([wr_hi, (wr - wr_hi.astype(F32)).astype(BF16)], axis=1)
        br = jnp.pad(jnp.concatenate([moe_b_exp[i], moe_b_grp[i]]),
                     [(0, 128 - MOE_EXPERTS - MOE_GROUPS)]).reshape(1, 128).astype(F32)
        xp = _layer_tail(mix_p, att_p, xp, wo_mix, wo_att, ln1, wr, br, moe_wg, moe_wu, moe_wd, i, ln2, **glu)
        xs = _layer_tail(mix_s, att_s, xs, wo_mix, wo_att, ln1, wr, br, moe_wg, moe_wu, moe_wd, i, ln2, **glu)

    st = lambda xs_: jnp.stack(xs_)
    s5_p = lambda xs_: st(xs_).reshape(len(xs_), bp, S5_GROUPS, S5_STATE)
    s5_s = lambda xs_: jnp.transpose(st(xs_).reshape(len(xs_), S5_GROUPS, S5_STATE, bs), (0, 3, 1, 2))
    ret_out = lambda xs_: jnp.swapaxes(st(xs_), 3, 4)
    return (xp.reshape(bp, lp, D_MODEL), xs.reshape(bs, ls, D_MODEL),
            s5_p(p_re), s5_p(p_im), ret_out(p_ret), st(p_gla),
            from_t(mkt_all), from_t(mvt_all),
            s5_s(s_re), s5_s(s_im), ret_out(s_ret), st(s_gla))
```

```python
import functools
import math

import jax
import jax.numpy as jnp
from jax import lax
from jax.experimental import pallas as pl
from jax.experimental.pallas import tpu as pltpu

F32 = jnp.float32
BF16 = jnp.bfloat16

D_MODEL = 1024
DEPTH = 4
PAST_LEN = 16384
N_MIXERS = 3
MIX_WIDTH = 768
XA_HEADS = 4
XA_HEAD_DIM = 64
XA_WIDTH = 256
N_MEM = 256
S5_GROUP = 16
S5_GROUPS = 48
S5_STATE = 64
S5_SLABS = 6
S5_SLAB_STATE = 512
RET_HEADS = 4
RET_DK = 128
RET_DV = 192
RET_CHUNK = 128
ROPE_BASE = 10000.0
GLA_HEADS = 4
GLA_DK = 96
GLA_DV = 192
GLA_RANK = 16
GLA_TAU = 16.0
GLA_CHUNK = 64
MOE_GROUPS = 4
MOE_PER_GROUP = 4
MOE_EXPERTS = 16
MOE_HIDDEN = 256
DN_ALPHA = (2 * DEPTH) ** 0.25
NORM_EPS = 1e-5

DK_PAD = 128
DV_PAD = 256
HEADS = 4
COL_Q, COL_K, COL_V, COL_G, COL_XA, COL_LOW = 0, 512, 1024, 2048, 3072, 3328
TOKEN_ROWS = 512
MIXER_ROWS = 128
S5_STEPS = 256
VMEM_LIMIT = 52 * 1024 * 1024
TAIL_VMEM_LIMIT = 58 * 1024 * 1024


def _cparams(*sem):
    return pltpu.CompilerParams(dimension_semantics=sem, vmem_limit_bytes=VMEM_LIMIT)


def _dot(a, b):
    return jnp.dot(a, b, preferred_element_type=F32)


def _dot_nt(a, b):
    return lax.dot_general(a, b, (((1,), (1,)), ((), ())), preferred_element_type=F32)


def _split2(a):
    hi = a.astype(BF16)
    return hi, (a - hi.astype(F32)).astype(BF16)


def _dot_hi(a, b):
    ah, al = _split2(a)
    bh, bl = _split2(b)
    return _dot(ah, bh) + _dot(ah, bl) + _dot(al, bh)


def _dot_sel(sel, b):
    bh = b.astype(BF16)
    r1 = b - bh.astype(F32)
    bm = r1.astype(BF16)
    bl = (r1 - bm.astype(F32)).astype(BF16)
    sb = sel.astype(BF16)
    return _dot(sb, bh) + _dot(sb, bm) + _dot(sb, bl)


def _layer_norm(y, g, b):
    mu = jnp.mean(y, axis=-1, keepdims=True)
    d = y - mu
    var = jnp.mean(d * d, axis=-1, keepdims=True)
    return d * lax.rsqrt(var + NORM_EPS) * g + b


def _sigmoid(x):
    return 1.0 / (1.0 + jnp.exp(-x))


def _silu(x):
    return x * _sigmoid(x)


def _gelu_tanh(x):
    c = math.sqrt(2.0 / math.pi)
    return 0.5 * x * (1.0 + jnp.tanh(c * (x + 0.044715 * (x * x * x))))


def _matmul_kernel(x_ref, w_ref, o_ref, *, precise):
    if precise:
        o_ref[...] = _dot_hi(x_ref[...], w_ref[...])
    else:
        o_ref[...] = _dot(x_ref[...].astype(BF16), w_ref[...])


def _matmul(x, w, tm, precise=False):
    t, k = x.shape
    tm = min(tm, t)
    n = w.shape[1]
    return pl.pallas_call(
        functools.partial(_matmul_kernel, precise=precise),
        grid=(t // tm,),
        in_specs=[pl.BlockSpec((tm, k), lambda i: (i, 0)),
                  pl.BlockSpec((k, n), lambda i: (0, 0))],
        out_specs=pl.BlockSpec((tm, n), lambda i: (i, 0)),
        out_shape=jax.ShapeDtypeStruct((t, n), F32),
        compiler_params=_cparams("parallel"),
        name="in_proj",
    )(x, w)


def _memkv_kernel(x_ref, w_ref, ok_ref, ov_ref):
    kv = _dot_nt(w_ref[...], x_ref[0].astype(BF16))
    for l in range(DEPTH):
        ok_ref[l, 0] = kv[l * XA_WIDTH:(l + 1) * XA_WIDTH]
        ov_ref[l, 0] = kv[(DEPTH + l) * XA_WIDTH:(DEPTH + l + 1) * XA_WIDTH]


def _mem_kv(mem, wkt, wvt):
    n = mem.shape[0]
    w = jnp.concatenate([wkt, wvt], axis=0).reshape(2 * DEPTH * XA_WIDTH, D_MODEL)
    spec_o = pl.BlockSpec((DEPTH, 1, XA_WIDTH, N_MEM), lambda b: (0, b, 0, 0))
    return pl.pallas_call(
        _memkv_kernel,
        grid=(n,),
        in_specs=[pl.BlockSpec((1, N_MEM, D_MODEL), lambda b: (b, 0, 0)),
                  pl.BlockSpec(w.shape, lambda b: (0, 0))],
        out_specs=[spec_o, spec_o],
        out_shape=[jax.ShapeDtypeStruct((DEPTH, n, XA_WIDTH, N_MEM), F32)] * 2,
        compiler_params=_cparams("parallel"),
        name="mem_kv",
    )(mem, w)


def _attn_kernel(q_ref, kt_ref, vt_ref, o_ref, *, nseq, rl, nsub):
    rs = rl // nsub
    head = lax.broadcasted_iota(jnp.int32, (rs, XA_WIDTH), 1) // XA_HEAD_DIM
    pieces = [(s, s * rl + j * rs) for s in range(nseq) for j in range(nsub)]
    sc = []
    for s, r0 in pieces:
        q = q_ref[r0:r0 + rs, :] * (XA_HEAD_DIM ** -0.5)
        qs = jnp.concatenate([jnp.where(head == h, q, 0.0) for h in range(XA_HEADS)], axis=0)
        sc.append(_dot(qs.astype(BF16), kt_ref[0, s].astype(BF16)))
    p = []
    for x in sc:
        e = jnp.exp(x - jnp.max(x, axis=-1, keepdims=True))
        p.append((e / jnp.sum(e, axis=-1, keepdims=True)).astype(BF16))
    outs = []
    for (s, _), pb in zip(pieces, p):
        pv = _dot_nt(pb, vt_ref[0, s].astype(BF16))
        o = jnp.where(head == 0, pv[0:rs], 0.0)
        for h in range(1, XA_HEADS):
            o = o + jnp.where(head == h, pv[h * rs:(h + 1) * rs], 0.0)
        outs.append(o)
    o_ref[...] = jnp.concatenate(outs, axis=0).astype(o_ref.dtype)


def _mem_attention(z, xa_col, mkt, mvt, layer, seq_len, rl, nseq):
    t = z.shape[0]
    rows = rl * nseq
    cb = xa_col // XA_WIDTH
    if nseq == 1:
        per_seq = seq_len // rl
        kv_map = lambda i: (layer, i // per_seq, 0, 0)
    else:
        kv_map = lambda i: (layer, i, 0, 0)
    return pl.pallas_call(
        functools.partial(_attn_kernel, nseq=nseq, rl=rl, nsub=2 if rl % 256 == 0 else 1),
        grid=(t // rows,),
        in_specs=[pl.BlockSpec((rows, XA_WIDTH), lambda i: (i, cb)),
                  pl.BlockSpec((1, nseq, XA_WIDTH, N_MEM), kv_map),
                  pl.BlockSpec((1, nseq, XA_WIDTH, N_MEM), kv_map)],
        out_specs=pl.BlockSpec((rows, XA_WIDTH), lambda i: (i, 0)),
        out_shape=jax.ShapeDtypeStruct((t, XA_WIDTH), BF16),
        compiler_params=_cparams("parallel"),
        name="mem_attention",
    )(z, mkt, mvt)


def _router_gates(lg):
    tm = lg.shape[0]
    lgt = jnp.transpose(lg)
    ninf = jnp.float32(-jnp.inf)
    grow = lax.broadcasted_iota(jnp.int32, (8, tm), 0).astype(F32)
    gm = grow < MOE_GROUPS
    gl = jnp.where(gm, lgt[MOE_EXPERTS:MOE_EXPERTS + 8], ninf)
    ge = jnp.exp(gl - jnp.max(gl, axis=0, keepdims=True))
    gp = ge / jnp.sum(ge, axis=0, keepdims=True)
    pmax = jnp.max(gp, axis=0, keepdims=True)
    gi = jnp.min(jnp.where((gp == pmax) & gm, grow, 1e4), axis=0, keepdims=True)
    erow = lax.broadcasted_iota(jnp.int32, (MOE_EXPERTS, tm), 0).astype(F32)
    lge = lgt[0:MOE_EXPERTS]
    lo = gi * MOE_PER_GROUP
    em = (erow >= lo) & (erow < lo + MOE_PER_GROUP)
    el = jnp.where(em, lge, ninf)
    v1 = jnp.max(el, axis=0, keepdims=True)
    i1 = jnp.min(jnp.where((el == v1) & em, erow, 1e4), axis=0, keepdims=True)
    em2 = em & (erow != i1)
    el2 = jnp.where(em2, lge, ninf)
    v2 = jnp.max(el2, axis=0, keepdims=True)
    i2 = jnp.min(jnp.where((el2 == v2) & em2, erow, 1e4), axis=0, keepdims=True)
    t = jnp.exp(v2 - v1)
    w1 = 1.0 / (1.0 + t)
    w2 = t / (1.0 + t)
    gates = jnp.where(erow == i1, w1 * pmax, 0.0) + jnp.where(erow == i2, w2 * pmax, 0.0)
    onehot = jnp.where(grow == gi, 1.0, 0.0)
    rows = jnp.concatenate([gates, onehot, jnp.zeros((128 - MOE_EXPERTS - 8, tm), F32)], axis=0)
    return jnp.transpose(rows)


MOE_CAP = 160
MOE_OV = 128
TAIL_PIPELINE_MIN_TILES = 8


def _tail_kernel(*refs, glu, pipelined):
    (mix_ref, att_ref, xin_ref, wm_ref, wa_ref, g1_ref, b1_ref, wr_ref, br_ref, ltri_ref,
     wg_ref, wu_ref, wd_ref, g_ref, b_ref) = refs[:15]
    if glu:
        wglu_ref, bglu_ref = refs[15:17]
    nscr = 11 if pipelined else 6
    o_ref, acc_ref, xb_nx, gs_nx, x_nx, rkc_nx, rkr_nx = refs[-nscr - 1:][:7]
    if pipelined:
        xb_ref, gs_ref, x_ref, rkc_ref, rkr_ref = refs[-5:]
    else:
        xb_ref, gs_ref, x_ref, rkc_ref, rkr_ref = xb_nx, gs_nx, x_nx, rkc_nx, rkr_nx
    tm = x_ref.shape[0]
    lane = lax.broadcasted_iota(jnp.int32, (tm, 128), 1)

    if pipelined:
        @pl.when(pl.program_id(0) == 0)
        def _():
            xb_ref[...] = jnp.zeros_like(xb_ref)
            gs_ref[...] = jnp.zeros_like(gs_ref)
            x_ref[...] = jnp.zeros_like(x_ref)
            rkc_ref[...] = jnp.full(rkc_ref.shape, -1.0, F32)
            rkr_ref[...] = jnp.full(rkr_ref.shape, -1.0, F32)

    front = {}

    def front_project():
        if glu:
            y = _gelu_tanh(mix_ref[...])
            mix = (y * _sigmoid(_dot(y.astype(BF16), wglu_ref[...]) + bglu_ref[...])).astype(BF16)
        else:
            mix = mix_ref[...]
        front["h"] = _dot(mix, wm_ref[...]) + _dot(att_ref[...], wa_ref[...])

    def front_norm():
        xn = _layer_norm(DN_ALPHA * xin_ref[...] + front["h"], g1_ref[...], b1_ref[...])
        x_nx[...] = xn
        xh = xn.astype(BF16)
        xb_nx[...] = xh
        front["xh"], front["xl"] = xh, (xn - xh.astype(F32)).astype(BF16)

    def front_route():
        lg = _dot(front["xh"], wr_ref[...])
        lg = lg[:, 0:128] + lg[:, 128:256] + _dot(front["xl"], wr_ref[:, 0:128]) + br_ref[...]
        front["routed"] = _router_gates(lg)

    def front_rank():
        routed = front["routed"]
        gates = jnp.where(lane < MOE_EXPERTS, routed, 0.0)
        in_grp = (lane >= MOE_EXPERTS) & (lane < MOE_EXPERTS + MOE_GROUPS) & (routed > 0.0)
        g_hi = gates.astype(BF16).astype(F32)
        g_mid = (gates - g_hi).astype(BF16).astype(F32)
        g_lo = gates - g_hi - g_mid
        gs_nx[...] = (g_hi + pltpu.roll(g_mid, MOE_EXPERTS, axis=1)
                      + pltpu.roll(g_lo, 2 * MOE_EXPERTS, axis=1)).astype(BF16)
        rkn = _dot(ltri_ref[...], routed.astype(BF16))
        rkn = jnp.where(in_grp, rkn, -1.0)
        rkc_nx[...] = rkn
        rkr_nx[...] = jnp.transpose(rkn)

    front_stages = [front_project, front_norm, front_route, front_rank]
    if not pipelined:
        for stage in front_stages:
            stage()

    rkc = rkc_ref[...]
    rkr = rkr_ref[...]
    cnt = jnp.sum(jnp.where(rkc >= 0.0, 1.0, 0.0), axis=0, keepdims=True)
    rk_cols = [jnp.sum(jnp.where(lane == MOE_EXPERTS + g, rkc, 0.0), axis=-1, keepdims=True)
               for g in range(MOE_GROUPS)]
    rk_rows = [rkr[MOE_EXPERTS + g:MOE_EXPERTS + g + 1, :] for g in range(MOE_GROUPS)]

    def gather_mat(g, base, rows):
        r_io = lax.broadcasted_iota(jnp.int32, (rows, tm), 0).astype(F32) + base
        return jnp.where(rk_rows[g] == r_io, 1.0, 0.0).astype(BF16)

    def scatter_mat(g, base, kpad):
        c_io = lax.broadcasted_iota(jnp.int32, (tm, kpad), 1).astype(F32) + base
        return jnp.where(rk_cols[g] == c_io, 1.0, 0.0).astype(BF16)

    def experts(g, xg, gg, kpad):
        rows = xg.shape[0]
        glane = lax.broadcasted_iota(jnp.int32, (rows, 128), 1)
        hs = []
        for j in range(MOE_PER_GROUP):
            e = g * MOE_PER_GROUP + j
            ge = jnp.sum(jnp.where((glane % MOE_EXPERTS == e) & (glane < 3 * MOE_EXPERTS), gg, 0.0),
                         axis=-1, keepdims=True)
            hs.append((_silu(_dot(xg, wg_ref[0, e])) * _dot(xg, wu_ref[0, e]) * ge).astype(BF16))
        wd = wd_ref[0, g * MOE_PER_GROUP:(g + 1) * MOE_PER_GROUP].reshape(MOE_PER_GROUP * MOE_HIDDEN, D_MODEL)
        out = _dot(jnp.concatenate(hs, axis=-1), wd).astype(BF16)
        if kpad > rows:
            out = jnp.concatenate([out, jnp.zeros((kpad - rows, D_MODEL), BF16)], axis=0)
        return out

    in_cap = lambda rk: (rk >= 0.0) & (rk < MOE_CAP)
    slot_col = sum(jnp.where(in_cap(rk_cols[g]), rk_cols[g] + (g * MOE_CAP + 1.0), 0.0) for g in range(MOE_GROUPS)) - 1.0
    slot_row = sum(jnp.where(in_cap(rk_rows[g]), rk_rows[g] + (g * MOE_CAP + 1.0), 0.0) for g in range(MOE_GROUPS)) - 1.0
    slots = MOE_GROUPS * MOE_CAP
    r_io = lax.broadcasted_iota(jnp.int32, (slots, tm), 0).astype(F32)
    p_all = jnp.where(slot_row == r_io, 1.0, 0.0).astype(BF16)
    xg_all = _dot(p_all, xb_ref[...]).astype(BF16)
    gg_all = _dot(p_all, gs_ref[...])
    outs = []
    for g in range(MOE_GROUPS):
        if pipelined:
            front_stages[g]()
        outs.append(experts(g, xg_all[g * MOE_CAP:(g + 1) * MOE_CAP], gg_all[g * MOE_CAP:(g + 1) * MOE_CAP], MOE_CAP))
    c_io = lax.broadcasted_iota(jnp.int32, (tm, slots), 1).astype(F32)
    pt_all = jnp.where(slot_col == c_io, 1.0, 0.0).astype(BF16)
    acc_ref[...] = _dot(pt_all, jnp.concatenate(outs, axis=0))

    for g in range(MOE_GROUPS):
        n_ov = jnp.maximum(cnt[0, MOE_EXPERTS + g].astype(jnp.int32) - MOE_CAP + MOE_OV - 1, 0) // MOE_OV

        def overflow(i, _, g=g):
            base = (MOE_CAP + i * MOE_OV).astype(F32)
            p = gather_mat(g, base, MOE_OV)
            out = experts(g, _dot(p, xb_ref[...]).astype(BF16), _dot(p, gs_ref[...]), MOE_OV)
            acc_ref[...] += _dot(scatter_mat(g, base, MOE_OV), out)
            return 0

        lax.fori_loop(0, n_ov, overflow, 0)

    if pipelined:
        xb_ref[...] = xb_nx[...]
        gs_ref[...] = gs_nx[...]
        rkc_ref[...] = rkc_nx[...]
        rkr_ref[...] = rkr_nx[...]
    o_ref[...] = _layer_norm(DN_ALPHA * x_ref[...] + acc_ref[...], g_ref[...], b_ref[...])
    if pipelined:
        x_ref[...] = x_nx[...]


def _layer_tail(mix, att, x, wm, wa, ln1, wr, br, wg, wu, wd, layer, ln2, glu_w=None, glu_b=None, tm=TOKEN_ROWS):
    t = x.shape[0]
    tm = min(tm, t)
    nt = t // tm
    idx = jnp.arange(tm)
    ltri = (idx[:, None] > idx[None, :]).astype(BF16)
    pipelined = nt >= TAIL_PIPELINE_MIN_TILES
    if pipelined:
        row_in = lambda w: pl.BlockSpec((tm, w), lambda s: (jnp.minimum(s, nt - 1), 0))
        row_out = pl.BlockSpec((tm, D_MODEL), lambda s: (jnp.maximum(s - 1, 0), 0))
    else:
        row_in = lambda w: pl.BlockSpec((tm, w), lambda s: (s, 0))
        row_out = pl.BlockSpec((tm, D_MODEL), lambda s: (s, 0))
    full = lambda a: pl.BlockSpec(a.shape, lambda s: (0,) * a.ndim, pipeline_mode=pl.Buffered(1))
    layer_w = lambda a: pl.BlockSpec((1,) + a.shape[1:], lambda s: (layer, 0, 0, 0),
                                     pipeline_mode=pl.Buffered(1))
    args = [mix, att, x, wm, wa, ln1[0], ln1[1], wr, br, ltri, wg, wu, wd, ln2[0], ln2[1]]
    specs = [row_in(mix.shape[1]), row_in(XA_WIDTH), row_in(D_MODEL), full(wm), full(wa), full(ln1[0]),
             full(ln1[1]), full(wr), full(br), full(ltri), layer_w(wg), layer_w(wu), layer_w(wd),
             full(ln2[0]), full(ln2[1])]
    if glu_w is not None:
        args += [glu_w, glu_b]
        specs += [full(glu_w), full(glu_b)]
    parked = [pltpu.VMEM((tm, D_MODEL), BF16), pltpu.VMEM((tm, 128), BF16), pltpu.VMEM((tm, D_MODEL), F32),
              pltpu.VMEM((tm, 128), F32), pltpu.VMEM((128, tm), F32)]
    return pl.pallas_call(
        functools.partial(_tail_kernel, glu=glu_w is not None, pipelined=pipelined),
        grid=(nt + 1 if pipelined else nt,),
        in_specs=specs,
        out_specs=row_out,
        out_shape=jax.ShapeDtypeStruct((t, D_MODEL), F32),
        scratch_shapes=[pltpu.VMEM((tm, D_MODEL), F32)] + parked + (parked if pipelined else []),
        compiler_params=pltpu.CompilerParams(dimension_semantics=("arbitrary",),
                                             vmem_limit_bytes=TAIL_VMEM_LIMIT),
        name="layer_tail",
    )(*args)


def _s5_kernel(*refs, tlen, ngrp, chained, precise):
    if chained:
        u_ref, wb_ref, wc_ref, lam_ref, d_ref, y_ref, ore_ref, oim_ref, bu_ref, h_ref, cr_ref, ci_ref = refs
        u = jnp.swapaxes(u_ref[...], 0, 1).reshape(8 * tlen, 128)
    else:
        (u_ref, h0r_ref, h0i_ref, wb_ref, wc_ref, lam_ref, d_ref,
         y_ref, ore_ref, oim_ref, bu_ref, h_ref, cr_ref, ci_ref) = refs
        gr = 8 * tlen
        u = jnp.concatenate(
            [jnp.swapaxes(u_ref[g * gr:(g + 1) * gr, :].reshape(8, tlen, 128), 0, 1).reshape(gr, 128)
             for g in range(ngrp)], axis=0)
    if precise:
        bu_ref[...] = _dot_hi(u, wb_ref[0])
    else:
        bu_ref[...] = _dot(u.astype(BF16), wb_ref[0].astype(BF16))
    ns = S5_SLAB_STATE
    lr = jnp.broadcast_to(lam_ref[0, 0:1, :], (8, ns))
    li = jnp.broadcast_to(lam_ref[0, 1:2, :], (8, ns))

    def step(tile, hr, hi):
        rows = pl.ds(pl.multiple_of(tile * 8, 8), 8)
        nr = lr * hr - li * hi + bu_ref[rows, 0:ns]
        ni = lr * hi + li * hr + bu_ref[rows, ns:2 * ns]
        h_ref[rows, 0:ns] = nr
        h_ref[rows, ns:2 * ns] = ni
        return nr, ni

    if chained:
        @pl.when(pl.program_id(2) == 0)
        def _():
            cr_ref[...] = jnp.zeros_like(cr_ref)
            ci_ref[...] = jnp.zeros_like(ci_ref)

        hr, hi = lax.fori_loop(0, tlen, lambda t, c: step(t, *c), (cr_ref[...], ci_ref[...]), unroll=8)
        cr_ref[...] = hr
        ci_ref[...] = hi
        ore_ref[...] = hr
        oim_ref[...] = hi
    else:
        cr_ref[...] = jnp.transpose(h0r_ref[0])
        ci_ref[...] = jnp.transpose(h0i_ref[0])

        def group(g, _):
            s0 = pl.multiple_of(g * 8, 8)
            hr, hi = cr_ref[pl.ds(s0, 8), :], ci_ref[pl.ds(s0, 8), :]
            for t in range(tlen):
                hr, hi = step(g * tlen + t, hr, hi)
            cr_ref[pl.ds(s0, 8), :] = hr
            ci_ref[pl.ds(s0, 8), :] = hi
            return 0

        lax.fori_loop(0, ngrp, group, 0)
        ore_ref[...] = jnp.transpose(cr_ref[...])
        oim_ref[...] = jnp.transpose(ci_ref[...])
    y = _dot(h_ref[...].astype(BF16), wc_ref[0].astype(BF16)) + d_ref[0] * u
    if chained:
        y_ref[...] = jnp.swapaxes(y.reshape(tlen, 8, 128), 0, 1)
    else:
        gr = 8 * tlen
        for g in range(ngrp):
            y_ref[g * gr:(g + 1) * gr, :] = jnp.swapaxes(
                y[g * gr:(g + 1) * gr, :].reshape(tlen, 8, 128), 0, 1).reshape(gr, 128)


def _s5_mixer_prompt(z, params, seq_len, tlen=S5_STEPS):
    t = z.shape[0]
    nseq = t // seq_len
    ns = S5_SLAB_STATE
    z3 = z.reshape(nseq, seq_len, z.shape[1])
    slab = lambda a: pl.BlockSpec((1,) + a.shape[1:], lambda j, b, i: (j,) + (0,) * (a.ndim - 1))
    u_spec = pl.BlockSpec((8, tlen, 128), lambda j, b, i: (b, i, j))
    st_spec = pl.BlockSpec((8, ns), lambda j, b, i: (b, j))
    st_shape = jax.ShapeDtypeStruct((nseq, S5_SLABS * ns), F32)
    rows = 8 * tlen
    y, hre, him = pl.pallas_call(
        functools.partial(_s5_kernel, tlen=tlen, ngrp=1, chained=True, precise=False),
        grid=(S5_SLABS, nseq // 8, seq_len // tlen),
        in_specs=[u_spec] + [slab(a) for a in params],
        out_specs=[u_spec, st_spec, st_spec],
        out_shape=[jax.ShapeDtypeStruct((nseq, seq_len, MIX_WIDTH), F32), st_shape, st_shape],
        scratch_shapes=[pltpu.VMEM((rows, 2 * ns), F32), pltpu.VMEM((rows, 2 * ns), F32),
                        pltpu.VMEM((8, ns), F32), pltpu.VMEM((8, ns), F32)],
        compiler_params=_cparams("parallel", "parallel", "arbitrary"),
        name="s5_mixer",
    )(z3, *params)
    return y.reshape(t, MIX_WIDTH), hre, him


def _s5_mixer_sample(z, params, seq_len, h0r, h0i, layer):
    t = z.shape[0]
    nseq = t // seq_len
    ns = S5_SLAB_STATE
    seqs = 128
    rows = seqs * seq_len
    slab = lambda a: pl.BlockSpec((1,) + a.shape[1:], lambda j, i: (j,) + (0,) * (a.ndim - 1))
    u_spec = pl.BlockSpec((rows, 128), lambda j, i: (i, j))
    h0_spec = pl.BlockSpec((1, ns, seqs), lambda j, i: (layer, j, i))
    st_spec = pl.BlockSpec((ns, seqs), lambda j, i: (j, i))
    st_shape = jax.ShapeDtypeStruct((S5_SLABS * ns, nseq), F32)
    return pl.pallas_call(
        functools.partial(_s5_kernel, tlen=seq_len, ngrp=seqs // 8, chained=False, precise=True),
        grid=(S5_SLABS, nseq // seqs),
        in_specs=[u_spec, h0_spec, h0_spec] + [slab(a) for a in params],
        out_specs=[u_spec, st_spec, st_spec],
        out_shape=[jax.ShapeDtypeStruct((t, MIX_WIDTH), F32), st_shape, st_shape],
        scratch_shapes=[pltpu.VMEM((rows, 2 * ns), F32), pltpu.VMEM((rows, 2 * ns), F32),
                        pltpu.VMEM((seqs, ns), F32), pltpu.VMEM((seqs, ns), F32)],
        compiler_params=_cparams("parallel", "parallel"),
        name="s5_mixer",
    )(z, h0r, h0i, *params)


def _s5_params(lam_re, lam_im, log_dt, b_re, b_im, c_re, c_im, d_skip):
    lre, lim = lam_re.astype(F32), lam_im.astype(F32)
    dt = jnp.exp(log_dt.astype(F32))[:, None]
    mag = jnp.exp(lre * dt)
    bar_re, bar_im = mag * jnp.cos(lim * dt), mag * jnp.sin(lim * dt)
    nre, nim = bar_re - 1.0, bar_im
    den = lre * lre + lim * lim
    fre, fim = (nre * lre + nim * lim) / den, (nim * lre - nre * lim) / den
    bre, bim = b_re.astype(F32), b_im.astype(F32)
    bb_re = fre[..., None] * bre - fim[..., None] * bim
    bb_im = fre[..., None] * bim + fim[..., None] * bre
    eye = jnp.eye(8, dtype=F32)
    gps = 8

    def blockdiag_in(m):
        m = m.reshape(S5_SLABS, gps, S5_STATE, S5_GROUP).transpose(0, 1, 3, 2)
        return jnp.einsum('jgcp,gh->jgchp', m, eye).reshape(S5_SLABS, 128, S5_SLAB_STATE)

    def blockdiag_out(m):
        m = m.reshape(S5_SLABS, gps, S5_GROUP, S5_STATE).transpose(0, 1, 3, 2)
        return jnp.einsum('jgpc,gh->jgphc', m, eye).reshape(S5_SLABS, S5_SLAB_STATE, 128)

    wb = jnp.concatenate([blockdiag_in(bb_re), blockdiag_in(bb_im)], axis=2)
    wc = jnp.concatenate([blockdiag_out(c_re.astype(F32)), blockdiag_out(-c_im.astype(F32))], axis=1)
    lam = jnp.stack([bar_re.reshape(S5_SLABS, S5_SLAB_STATE), bar_im.reshape(S5_SLABS, S5_SLAB_STATE)], axis=1)
    dsk = d_skip.astype(F32).reshape(S5_SLABS, 1, 128)
    return wb, wc, lam, dsk


def _head_norm_gate(o, gn, gate):
    col = lax.broadcasted_iota(jnp.int32, o.shape, 1)
    valid = col < RET_DV
    mu = jnp.sum(o, axis=-1, keepdims=True) * (1.0 / RET_DV)
    d = jnp.where(valid, o - mu, 0.0)
    var = jnp.sum(d * d, axis=-1, keepdims=True) * (1.0 / RET_DV)
    return d * lax.rsqrt(var + NORM_EPS) * gn * _silu(gate)


def _load_states(s_in_ref, spad_ref, nseg, dk):
    spad_ref[...] = jnp.zeros_like(spad_ref)
    for s in range(nseg):
        for h in range(HEADS):
            spad_ref[h, s * DK_PAD:s * DK_PAD + dk, 0:RET_DV] = s_in_ref[s, h]


def _ret_kernel(*refs, nseg, gammas):
    if nseg == 1:
        (q_ref, k_ref, v_ref, g_ref, cos_ref, sin_ref, dmat_ref, inner_ref, zeta_ref, gn_ref,
         o_ref, so_ref, st_ref) = refs

        @pl.when(pl.program_id(1) == 0)
        def _():
            st_ref[...] = jnp.zeros_like(st_ref)
    else:
        (q_ref, k_ref, v_ref, g_ref, cos_ref, sin_ref, dmat_ref, inner_ref, zeta_ref, gn_ref, s_in_ref,
         o_ref, so_ref, st_ref) = refs
        st_ref[:, RET_DV:DV_PAD, :] = jnp.zeros((HEADS, DV_PAD - RET_DV, nseg * RET_DK), F32)
        for s in range(nseg):
            for h in range(HEADS):
                st_ref[h, 0:RET_DV, s * RET_DK:(s + 1) * RET_DK] = s_in_ref[s, h]
    chained = nseg == 1
    nb = q_ref.shape[0] if chained else 1
    rd = (lambda ref, sq, cs: ref[sq, :, cs]) if chained else (lambda ref, sq, cs: ref[:, cs])
    r = q_ref.shape[-2]
    seg_len = r // nseg
    cos = cos_ref[...]
    sin = sin_ref[...]
    rot = lambda x: x * cos + pltpu.roll(x, RET_DK // 2, axis=1) * sin
    rowseg = lax.broadcasted_iota(jnp.int32, (r, DK_PAD), 0) // seg_len
    colseg = lax.broadcasted_iota(jnp.int32, (DV_PAD, r), 1) // seg_len

    def head(sq, h):
        sl = slice(h * DK_PAD, (h + 1) * DK_PAD)
        sv = slice(h * DV_PAD, (h + 1) * DV_PAD)
        q = rot(rd(q_ref, sq, sl))
        k = rot(rd(k_ref, sq, sl)) * (RET_DK ** -0.5)
        v = rd(v_ref, sq, sv)
        vb = v.astype(BF16)
        qb = q.astype(BF16)
        att = _dot_nt(qb, k.astype(BF16)) * dmat_ref[h]
        o = _dot(att.astype(BF16), vb)
        st_all = st_ref[sq, h] if chained else st_ref[h]
        if chained:
            qbd = qb
        else:
            qbd = jnp.concatenate([jnp.where(rowseg == s, q, 0.0) for s in range(nseg)], axis=1).astype(BF16)
        o = o + _dot_nt(qbd, st_all.astype(BF16)) * inner_ref[h]
        kzb = (k * zeta_ref[h]).astype(BF16)
        vt = jnp.transpose(v)
        for s in range(nseg):
            vts = vt if chained else jnp.where(colseg == s, vt, 0.0)
            st_new = st_all[:, s * RET_DK:(s + 1) * RET_DK] * gammas[h] + _dot(vts.astype(BF16), kzb)
            if chained:
                st_ref[sq, h] = st_new
                so_ref[sq, h] = st_new[0:RET_DV, :]
            else:
                so_ref[s, h] = st_new[0:RET_DV, :]
        out = _head_norm_gate(o, gn_ref[:, sv], rd(g_ref, sq, sv)).astype(o_ref.dtype)
        if chained:
            o_ref[sq, :, sv] = out
        else:
            o_ref[:, sv] = out

    for h in range(HEADS):
        for sq in range(nb):
            head(sq, h)


def _ret_consts(seg_len, nseg, pos):
    hh = jnp.arange(RET_HEADS, dtype=F32)
    log_g = jnp.log1p(-jnp.exp2(-5.0 - hh))
    idx = jnp.arange(seg_len, dtype=F32)
    rel = idx[:, None] - idx[None, :]
    dmat = jnp.where(rel >= 0, jnp.exp(log_g[:, None, None] * jnp.maximum(rel, 0.0)), 0.0)
    inner = jnp.exp(log_g[:, None] * (idx + 1.0))
    zeta = jnp.exp(log_g[:, None] * (seg_len - 1.0 - idx))
    r = seg_len * nseg
    seg = jnp.arange(r) // seg_len
    same = (seg[:, None] == seg[None, :]).astype(F32)
    dmat = jnp.tile(dmat, (1, nseg, nseg)) * same[None]
    inner = jnp.broadcast_to(jnp.tile(inner, (1, nseg))[:, :, None], (RET_HEADS, r, DV_PAD))
    zeta = jnp.broadcast_to(jnp.tile(zeta, (1, nseg))[:, :, None], (RET_HEADS, r, DK_PAD))
    half = RET_DK // 2
    inv_freq = ROPE_BASE ** (-jnp.arange(half, dtype=F32) / half)
    ang = pos.astype(F32)[:, None] * inv_freq
    cos2 = jnp.concatenate([jnp.cos(ang), jnp.cos(ang)], axis=-1)
    sin2 = jnp.concatenate([-jnp.sin(ang), jnp.sin(ang)], axis=-1)
    return dmat, inner, zeta, cos2, sin2


def _ret_gammas(seg_len):
    return tuple(float((1.0 - 2.0 ** (-5.0 - h)) ** seg_len) for h in range(RET_HEADS))


def _ret_mixer(z, gn, seq_len, pos, s0=None):
    t = z.shape[0]
    nseq = t // seq_len
    r = MIXER_ROWS
    if s0 is None:
        nb = max(n for n in (1, 2, 4) if nseq % n == 0)
        nseg, seg_len = 1, RET_CHUNK
        grid = (nseq // nb, seq_len // r)
        z = z.reshape(nseq, seq_len, z.shape[1])
        blk = lambda w, c: pl.BlockSpec((nb, r, w), lambda n, i: (n, i, c))
        mix_shape = (nseq, seq_len, HEADS * DV_PAD)
        tabmap = lambda n, i: (i, 0)
        stmap = lambda n, i: (n, 0, 0, 0)
        st_rows = nb
        scratch = pltpu.VMEM((nb, HEADS, DV_PAD, RET_DK), F32)
        sem = ("parallel", "arbitrary")
    else:
        seg_len = seq_len
        nseg = r // seg_len
        grid = (t // r,)
        blk = lambda w, c: pl.BlockSpec((r, w), lambda i: (i, c))
        mix_shape = (t, HEADS * DV_PAD)
        tabmap = lambda i: (0, 0)
        stmap = lambda i: (i, 0, 0, 0)
        st_rows = nseg
        scratch = pltpu.VMEM((HEADS, DV_PAD, nseg * RET_DK), F32)
        sem = ("arbitrary",)
    dmat, inner, zeta, cos2, sin2 = _ret_consts(seg_len, nseg, pos)
    if s0 is not None:
        cos2 = jnp.tile(cos2, (nseg, 1))
        sin2 = jnp.tile(sin2, (nseg, 1))
    full = lambda a: pl.BlockSpec(a.shape, lambda *_: (0,) * a.ndim)
    st_spec = pl.BlockSpec((st_rows, RET_HEADS, RET_DV, RET_DK), stmap)
    in_specs = [blk(512, 0), blk(512, 1), blk(1024, 1), blk(1024, 2),
                pl.BlockSpec((r, DK_PAD), tabmap), pl.BlockSpec((r, DK_PAD), tabmap),
                full(dmat), full(inner), full(zeta), full(gn)]
    args = [z, z, z, z, cos2, sin2, dmat, inner, zeta, gn]
    if s0 is not None:
        in_specs.append(st_spec)
        args.append(s0)
    mix, s_new = pl.pallas_call(
        functools.partial(_ret_kernel, nseg=nseg, gammas=_ret_gammas(seg_len)),
        grid=grid,
        in_specs=in_specs,
        out_specs=[blk(HEADS * DV_PAD, 0), st_spec],
        out_shape=[jax.ShapeDtypeStruct(mix_shape, BF16),
                   jax.ShapeDtypeStruct((nseq, RET_HEADS, RET_DV, RET_DK), F32)],
        scratch_shapes=[scratch],
        compiler_params=_cparams(*sem),
        name="retention",
    )(*args)
    return mix.reshape(t, HEADS * DV_PAD), s_new


def _log_sigmoid(x):
    return jnp.minimum(x, 0.0) - jnp.log1p(jnp.exp(-jnp.abs(x)))


def _gla_kernel(*refs, nseg, chained):
    if chained:
        (q_ref, k_ref, v_ref, r_ref, low_ref, tril_ref, same_ref, wg_ref, bg_ref, gn_ref,
         o_ref, so_ref, spad_ref) = refs

        @pl.when(pl.program_id(1) == 0)
        def _():
            spad_ref[...] = jnp.zeros_like(spad_ref)
    else:
        (q_ref, k_ref, v_ref, r_ref, low_ref, tril_ref, same_ref, wg_ref, bg_ref, gn_ref, s_in_ref,
         o_ref, so_ref, spad_ref) = refs
        _load_states(s_in_ref, spad_ref, nseg, GLA_DK)
    nb = q_ref.shape[0] if chained else 1
    rd = (lambda ref, sq, cs: ref[sq, :, cs]) if chained else (lambda ref, sq, cs: ref[:, cs])
    r = q_ref.shape[-2]
    seg_len = r // nseg
    tril = tril_ref[...]
    causal = tril > 0.0
    every = slice(None)
    la = [_log_sigmoid(_dot_hi(rd(low_ref, sq, every), wg_ref[...]) + bg_ref[...]) * (1.0 / GLA_TAU)
          for sq in range(nb)]
    b = [_dot_sel(tril, la[sq]) for sq in range(nb)]
    b_tot = [_dot_sel(same_ref[...], la[sq]) for sq in range(nb)]
    rowseg = lax.broadcasted_iota(jnp.int32, (r, DK_PAD), 0) // seg_len
    colseg = lax.broadcasted_iota(jnp.int32, (DK_PAD, r), 1) // seg_len

    def head(sq, h):
        sl = slice(h * DK_PAD, (h + 1) * DK_PAD)
        sv = slice(h * DV_PAD, (h + 1) * DV_PAD)
        bh = b[sq][:, sl]
        q = rd(q_ref, sq, sl) * (GLA_DK ** -0.5)
        k = rd(k_ref, sq, sl)
        vb = rd(v_ref, sq, sv).astype(BF16)
        q_t = q * jnp.exp(bh)
        k_t = k * jnp.exp(-bh)
        att = jnp.where(causal, _dot_nt(q_t.astype(BF16), k_t.astype(BF16)), 0.0)
        o = _dot(att.astype(BF16), vb)
        kdt = jnp.transpose(k * jnp.exp(b_tot[sq][:, sl] - bh))
        lat = jnp.transpose(la[sq][:, sl])

        def seg_update(s, s_old):
            kds = jnp.where(colseg == s, kdt, 0.0) if nseg > 1 else kdt
            las = jnp.where(colseg == s, lat, 0.0) if nseg > 1 else lat
            decay = jnp.exp(jnp.sum(las, axis=-1, keepdims=True))
            return s_old * decay + _dot(kds.astype(BF16), vb)

        if chained:
            qb = q_t.astype(BF16)
            s_cur = spad_ref[sq, h]
            inter = []
            for s in range(nseg):
                inter.append(_dot(qb[s * seg_len:(s + 1) * seg_len], s_cur.astype(BF16)))
                s_cur = seg_update(s, s_cur)
            o = o + jnp.concatenate(inter, axis=0)
            spad_ref[sq, h] = s_cur
            so_ref[sq, h] = s_cur[0:GLA_DK, 0:GLA_DV]
        else:
            s_all = spad_ref[h]
            qbd = jnp.concatenate([jnp.where(rowseg == s, q_t, 0.0) for s in range(nseg)], axis=1).astype(BF16)
            o = o + _dot(qbd, s_all.astype(BF16))
            for s in range(nseg):
                so_ref[s, h] = seg_update(s, s_all[s * DK_PAD:(s + 1) * DK_PAD])[0:GLA_DK, 0:GLA_DV]
        out = _head_norm_gate(o, gn_ref[:, sv], rd(r_ref, sq, sv)).astype(o_ref.dtype)
        if chained:
            o_ref[sq, :, sv] = out
        else:
            o_ref[:, sv] = out

    for h in range(HEADS):
        for sq in range(nb):
            head(sq, h)


def _gla_mixer(z, wg, bg, gn, seq_len, s0=None):
    t = z.shape[0]
    nseq = t // seq_len
    r = MIXER_ROWS
    if s0 is None:
        nb = max(n for n in (1, 2, 4) if nseq % n == 0)
        seg_len = GLA_CHUNK
        nseg = r // seg_len
        grid = (nseq // nb, seq_len // r)
        z = z.reshape(nseq, seq_len, z.shape[1])
        blk = lambda w, c: pl.BlockSpec((nb, r, w), lambda n, i: (n, i, c))
        mix_shape = (nseq, seq_len, HEADS * DV_PAD)
        stmap = lambda n, i: (n, 0, 0, 0)
        st_rows = nb
        scratch = pltpu.VMEM((nb, HEADS, DK_PAD, DV_PAD), F32)
        sem = ("parallel", "arbitrary")
    else:
        seg_len = seq_len
        nseg = r // seg_len
        grid = (t // r,)
        blk = lambda w, c: pl.BlockSpec((r, w), lambda i: (i, c))
        mix_shape = (t, HEADS * DV_PAD)
        stmap = lambda i: (i, 0, 0, 0)
        st_rows = nseg
        scratch = pltpu.VMEM((HEADS, nseg * DK_PAD, DV_PAD), F32)
        sem = ("arbitrary",)
    seg = jnp.arange(r) // seg_len
    idx = jnp.arange(r)
    same = (seg[:, None] == seg[None, :]).astype(F32)
    tril = same * (idx[:, None] >= idx[None, :]).astype(F32)
    full = lambda a: pl.BlockSpec(a.shape, lambda *_: (0,) * a.ndim)
    st_spec = pl.BlockSpec((st_rows, GLA_HEADS, GLA_DK, GLA_DV), stmap)
    in_specs = [blk(512, 0), blk(512, 1), blk(1024, 1), blk(1024, 2), blk(128, COL_LOW // 128),
                full(tril), full(same), full(wg), full(bg), full(gn)]
    args = [z, z, z, z, z, tril, same, wg, bg, gn]
    if s0 is not None:
        in_specs.append(st_spec)
        args.append(s0)
    mix, s_new = pl.pallas_call(
        functools.partial(_gla_kernel, nseg=nseg, chained=s0 is None),
        grid=grid,
        in_specs=in_specs,
        out_specs=[blk(HEADS * DV_PAD, 0), st_spec],
        out_shape=[jax.ShapeDtypeStruct(mix_shape, BF16),
                   jax.ShapeDtypeStruct((nseq, GLA_HEADS, GLA_DK, GLA_DV), F32)],
        scratch_shapes=[scratch],
        compiler_params=_cparams(*sem),
        name="gla",
    )(*args)
    return mix.reshape(t, HEADS * DV_PAD), s_new


def _pad_heads(w, width, pad):
    lead = w.shape[:-1]
    w = w.reshape(lead + (HEADS, width))
    w = jnp.pad(w, [(0, 0)] * len(lead) + [(0, 0), (0, pad - width)])
    return w.reshape(lead + (HEADS * pad,))


def _pad_head_rows(w, width, pad):
    return jnp.swapaxes(_pad_heads(jnp.swapaxes(w, 0, 1), width, pad), 0, 1)


def _ret_in_weight(w):
    qk = RET_HEADS * RET_DK
    v = _pad_heads(w[:, 2 * qk:2 * qk + MIX_WIDTH], RET_DV, DV_PAD)
    g = _pad_heads(w[:, 2 * qk + MIX_WIDTH:2 * qk + 2 * MIX_WIDTH], RET_DV, DV_PAD)
    return jnp.concatenate([w[:, :2 * qk], v, g, w[:, -XA_WIDTH:]], axis=1).astype(BF16)


def _gla_in_weight(w):
    qk = GLA_HEADS * GLA_DK
    q = _pad_heads(w[:, :qk], GLA_DK, DK_PAD)
    k = _pad_heads(w[:, qk:2 * qk], GLA_DK, DK_PAD)
    v = _pad_heads(w[:, 2 * qk:2 * qk + MIX_WIDTH], GLA_DV, DV_PAD)
    r = _pad_heads(w[:, 2 * qk + MIX_WIDTH:2 * qk + 2 * MIX_WIDTH], GLA_DV, DV_PAD)
    low = jnp.pad(w[:, 2 * qk + 2 * MIX_WIDTH:2 * qk + 2 * MIX_WIDTH + GLA_RANK], [(0, 0), (0, 128 - GLA_RANK)])
    return jnp.concatenate([q, k, v, r, w[:, -XA_WIDTH:], low], axis=1).astype(BF16)


def kernel(x_prompt, x_sample, mem_prompt, state_s5_re, state_s5_im, state_ret, state_gla,
           cache_mem_k, cache_mem_v,
           w_in_s5, s5_lam_re, s5_lam_im, s5_log_dt, s5_b_re, s5_b_im, s5_c_re, s5_c_im,
           s5_d, s5_w_glu, s5_b_glu,
           w_in_ret, ret_gn,
           w_in_gla, gla_w_gate2, gla_b_gate2, gla_gn,
           w_mem_k, w_mem_v, w_out, ln_g, ln_b,
           moe_w_grp, moe_b_grp, moe_w_exp, moe_b_exp, moe_w_gate, moe_w_up, moe_w_down):
    bp, lp, _ = x_prompt.shape
    bs, ls, _ = x_sample.shape
    xp = x_prompt.reshape(bp * lp, D_MODEL)
    xs = x_sample.reshape(bs * ls, D_MODEL)
    pos_p = jnp.arange(lp, dtype=jnp.int32)
    pos_s = PAST_LEN + jnp.arange(ls, dtype=jnp.int32)

    to_t = lambda c: jnp.transpose(c, (0, 1, 3, 4, 2)).reshape(DEPTH, c.shape[1], XA_WIDTH, N_MEM)
    from_t = lambda c: jnp.transpose(c.reshape(DEPTH, c.shape[1], XA_HEADS, XA_HEAD_DIM, N_MEM), (0, 1, 4, 2, 3))
    mkt_all, mvt_all = _mem_kv(mem_prompt, jnp.swapaxes(w_mem_k, 1, 2).astype(BF16),
                               jnp.swapaxes(w_mem_v, 1, 2).astype(BF16))
    cache_kt = to_t(cache_mem_k)
    cache_vt = to_t(cache_mem_v)
    ret_t = jnp.swapaxes(state_ret, 3, 4)
    s5_t = lambda s: jnp.transpose(s, (0, 2, 3, 1)).reshape(s.shape[0], S5_GROUPS * S5_STATE, bs)
    s5_re_t, s5_im_t = s5_t(state_s5_re), s5_t(state_s5_im)
    moe_wg, moe_wu, moe_wd = moe_w_gate.astype(BF16), moe_w_up.astype(BF16), moe_w_down.astype(BF16)

    p_re, p_im, p_ret, p_gla = [], [], [], []
    s_re, s_im, s_ret, s_gla = [], [], [], []
    for i in range(DEPTH):
        kind, j = i % N_MIXERS, i // N_MIXERS
        ln1 = (ln_g[i, 0].reshape(1, D_MODEL), ln_b[i, 0].reshape(1, D_MODEL))
        ln2 = (ln_g[i, 1].reshape(1, D_MODEL), ln_b[i, 1].reshape(1, D_MODEL))
        wo = w_out[i]
        wo_att = wo[MIX_WIDTH:].astype(BF16)
        glu = {}
        if kind == 0:
            params = _s5_params(s5_lam_re[j], s5_lam_im[j], s5_log_dt[j], s5_b_re[j], s5_b_im[j],
                                s5_c_re[j], s5_c_im[j], s5_d[j])
            zp = _matmul(xp, w_in_s5[j].astype(BF16), TOKEN_ROWS)
            zs = _matmul(xs, w_in_s5[j], TOKEN_ROWS, precise=True)
            mix_p, a_re, a_im = _s5_mixer_prompt(zp, params, lp)
            mix_s, b_re, b_im = _s5_mixer_sample(zs, params, ls, s5_re_t, s5_im_t, j)
            p_re.append(a_re), p_im.append(a_im), s_re.append(b_re), s_im.append(b_im)
            xa_col = MIX_WIDTH
            wo_mix = wo[:MIX_WIDTH].astype(BF16)
            glu = dict(glu_w=s5_w_glu[j].astype(BF16), glu_b=s5_b_glu[j].reshape(1, MIX_WIDTH))
        elif kind == 1:
            w = _ret_in_weight(w_in_ret[j])
            gn = _pad_heads(ret_gn[j], RET_DV, DV_PAD).reshape(1, HEADS * DV_PAD)
            zp = _matmul(xp, w, TOKEN_ROWS)
            zs = _matmul(xs, w, TOKEN_ROWS)
            mix_p, a_s = _ret_mixer(zp, gn, lp, pos_p)
            mix_s, b_s = _ret_mixer(zs, gn, ls, pos_s, s0=ret_t[j])
            p_ret.append(a_s), s_ret.append(b_s)
            xa_col = COL_XA
            wo_mix = _pad_head_rows(wo[:MIX_WIDTH], RET_DV, DV_PAD).astype(BF16)
        else:
            w = _gla_in_weight(w_in_gla[j])
            gn = _pad_heads(gla_gn[j], GLA_DV, DV_PAD).reshape(1, HEADS * DV_PAD)
            wg = jnp.pad(_pad_heads(gla_w_gate2[j], GLA_DK, DK_PAD), [(0, 128 - GLA_RANK), (0, 0)])
            bg = _pad_heads(gla_b_gate2[j], GLA_DK, DK_PAD).reshape(1, HEADS * DK_PAD)
            zp = _matmul(xp, w, TOKEN_ROWS)
            zs = _matmul(xs, w, TOKEN_ROWS)
            mix_p, a_s = _gla_mixer(zp, wg, bg, gn, lp)
            mix_s, b_s = _gla_mixer(zs, wg, bg, gn, ls, s0=state_gla[j])
            p_gla.append(a_s), s_gla.append(b_s)
            xa_col = COL_XA
            wo_mix = _pad_head_rows(wo[:MIX_WIDTH], GLA_DV, DV_PAD).astype(BF16)
        att_p = _mem_attention(zp, xa_col, mkt_all, mvt_all, i, lp, rl=min(TOKEN_ROWS, lp), nseq=1)
        att_s = _mem_attention(zs, xa_col, cache_kt, cache_vt, i, ls, rl=ls, nseq=MIXER_ROWS // ls)
        wr = jnp.pad(jnp.concatenate([moe_w_exp[i], moe_w_grp[i]], axis=1),
                     [(0, 0), (0, 128 - MOE_EXPERTS - MOE_GROUPS)]).astype(F32)
        wr_hi = wr.astype(BF16)
        wr = jnp.concatenate([wr_hi, (wr - wr_hi.astype(F32)).astype(BF16)], axis=1)
        br = jnp.pad(jnp.concatenate([moe_b_exp[i], moe_b_grp[i]]),
                     [(0, 128 - MOE_EXPERTS - MOE_GROUPS)]).reshape(1, 128).astype(F32)
        xp = _layer_tail(mix_p, att_p, xp, wo_mix, wo_att, ln1, wr, br, moe_wg, moe_wu, moe_wd, i, ln2, **glu)
        xs = _layer_tail(mix_s, att_s, xs, wo_mix, wo_att, ln1, wr, br, moe_wg, moe_wu, moe_wd, i, ln2, **glu)

    st = lambda xs_: jnp.stack(xs_)
    s5_p = lambda xs_: st(xs_).reshape(len(xs_), bp, S5_GROUPS, S5_STATE)
    s5_s = lambda xs_: jnp.transpose(st(xs_).reshape(len(xs_), S5_GROUPS, S5_STATE, bs), (0, 3, 1, 2))
    ret_out = lambda xs_: jnp.swapaxes(st(xs_), 3, 4)
    return (xp.reshape(bp, lp, D_MODEL), xs.reshape(bs, ls, D_MODEL),
            s5_p(p_re), s5_p(p_im), ret_out(p_ret), st(p_gla),
            from_t(mkt_all), from_t(mvt_all),
            s5_s(s_re), s5_s(s_im), ret_out(s_ret), st(s_gla))
```
